```python
import math
import jax, jax.numpy as jnp
from jax import lax
import numpy as np

D_MODEL = 1024
BATCH = 8
SEQ = 2048
DEPTH = 2
DEC_BATCH = 128
DEC_SEQ = 1
PAST_LEN = 16384
PAGE_SIZE = 128

SSD_EXPAND = 2
D_INNER = SSD_EXPAND * D_MODEL
SSD_HEADDIM = 64
SSD_HEADS = D_INNER // SSD_HEADDIM
SSD_GROUPS = 4
HEADS_PER_GROUP = SSD_HEADS // SSD_GROUPS
D_STATE = 128
CONV_W = 4
CONV_DIM = D_INNER + 2 * SSD_GROUPS * D_STATE
SSD_CHUNK = 128
D_SGU = 2 * D_MODEL
SGU_HEADS = 8
SGU_HEAD_DIM = D_SGU // SGU_HEADS
SGU_CHUNK = 128
D_FF_DENSE = 2816
N_EXPERTS = 8
TOP_K = 2
D_FF_EXPERT = 3584
N_SSD = (DEPTH + 1) // 2
N_SGU = DEPTH // 2
EPS = 1e-5

kernel_name = "hybrid_ssd_chunkmlp_moe_decode_step"


def rmsnorm(x, g):
    xf = x.astype(jnp.float32)
    y = xf * lax.rsqrt(jnp.mean(xf * xf, axis=-1, keepdims=True) + EPS)
    return (y * g.astype(jnp.float32)).astype(x.dtype)


def layernorm(x, g, b):
    xf = x.astype(jnp.float32)
    mu = jnp.mean(xf, axis=-1, keepdims=True)
    var = jnp.mean(jnp.square(xf - mu), axis=-1, keepdims=True)
    y = (xf - mu) * lax.rsqrt(var + EPS) * g.astype(jnp.float32) + b.astype(jnp.float32)
    return y.astype(x.dtype)


def causal_dwconv(xpad, w, b, length):
    out = b + sum(xpad[:, k:k + length] * w[k] for k in range(CONV_W))
    return out


def ssd_chunked(x, dt, a_neg, Bm, Cm):
    b, l = x.shape[:2]
    nc = l // SSD_CHUNK
    q = SSD_CHUNK
    xc = x.reshape(b, nc, q, SSD_GROUPS, HEADS_PER_GROUP, SSD_HEADDIM)
    dtc = dt.reshape(b, nc, q, SSD_GROUPS, HEADS_PER_GROUP)
    Bc = Bm.reshape(b, nc, q, SSD_GROUPS, D_STATE)
    Cc = Cm.reshape(b, nc, q, SSD_GROUPS, D_STATE)
    la_cs = jnp.cumsum(dtc * a_neg.reshape(SSD_GROUPS, HEADS_PER_GROUP), axis=2)
    xdt = xc * dtc[..., None]
    seg = la_cs[:, :, :, None] - la_cs[:, :, None, :]
    causal = jnp.tril(jnp.ones((q, q), dtype=bool))[:, :, None, None]
    decay = jnp.exp(jnp.where(causal, seg, -jnp.inf))
    cb = jnp.einsum('bcign,bcjgn->bcijg', Cc, Bc)
    y_diag = jnp.einsum('bcijgr,bcjgrp->bcigrp', cb[..., None] * decay, xdt)
    decay_to_end = jnp.exp(la_cs[:, :, -1:] - la_cs)
    chunk_states = jnp.einsum('bcjgn,bcjgrp->bcgrpn', Bc, xdt * decay_to_end[..., None])
    chunk_decay = jnp.exp(la_cs[:, :, -1])

    def step(hs, inp):
        st, dec = inp
        return dec[..., None, None] * hs + st, hs

    h0 = jnp.zeros((b, SSD_GROUPS, HEADS_PER_GROUP, SSD_HEADDIM, D_STATE), jnp.float32)
    h_final, h_prev = lax.scan(step, h0, (jnp.moveaxis(chunk_states, 1, 0).astype(jnp.float32),
                                          jnp.moveaxis(chunk_decay, 1, 0)))
    h_prev = jnp.moveaxis(h_prev, 0, 1)
    y_off = jnp.einsum('bcign,bcgrpn->bcigrp', Cc, h_prev) * jnp.exp(la_cs)[..., None]
    y = (y_diag + y_off).reshape(b, l, SSD_HEADS, SSD_HEADDIM)
    return y, h_final.reshape(b, SSD_HEADS, SSD_HEADDIM, D_STATE)


def ssd_recurrent(x, dt, a_neg, Bm, Cm, h0):
    b, l = x.shape[:2]
    xs = x.reshape(b, l, SSD_GROUPS, HEADS_PER_GROUP, SSD_HEADDIM)
    dts = dt.reshape(b, l, SSD_GROUPS, HEADS_PER_GROUP)
    a = a_neg.reshape(SSD_GROUPS, HEADS_PER_GROUP)
    hs0 = h0.reshape(b, SSD_GROUPS, HEADS_PER_GROUP, SSD_HEADDIM, D_STATE).astype(jnp.float32)

    def step(hs, inp):
        xt, dtt, bt, ct = inp
        upd = jnp.einsum('bgn,bgrp->bgrpn', bt, xt * dtt[..., None])
        hs = (jnp.exp(dtt * a)[..., None, None] * hs + upd).astype(jnp.float32)
        yt = jnp.einsum('bgn,bgrpn->bgrp', ct, hs)
        return hs, yt

    tmajor = lambda t: jnp.moveaxis(t, 1, 0)
    h_final, ys = lax.scan(step, hs0, (tmajor(xs), tmajor(dts), tmajor(Bm), tmajor(Cm)))
    y = jnp.moveaxis(ys, 0, 1).reshape(b, l, SSD_HEADS, SSD_HEADDIM)
    return y, h_final.reshape(b, SSD_HEADS, SSD_HEADDIM, D_STATE)


def ssd_mixer(h, conv_prev, ssm_prev, w_in, conv_w, conv_b, dt_bias, a_log, d_skip, gnorm_w, w_out):
    b, L, _ = h.shape
    proj = h @ w_in
    z = proj[..., :D_INNER]
    xbc = proj[..., D_INNER:D_INNER + CONV_DIM]
    dt_raw = proj[..., D_INNER + CONV_DIM:]
    if conv_prev is None:
        conv_prev = jnp.zeros((b, CONV_W - 1, CONV_DIM), xbc.dtype)
    xpad = jnp.concatenate([conv_prev.astype(xbc.dtype), xbc], axis=1)
    new_conv = xpad[:, -(CONV_W - 1):]
    xbc = jax.nn.silu(causal_dwconv(xpad, conv_w, conv_b, L))
    xs = xbc[..., :D_INNER].reshape(b, L, SSD_HEADS, SSD_HEADDIM)
    Bm = xbc[..., D_INNER:D_INNER + SSD_GROUPS * D_STATE].reshape(b, L, SSD_GROUPS, D_STATE)
    Cm = xbc[..., D_INNER + SSD_GROUPS * D_STATE:].reshape(b, L, SSD_GROUPS, D_STATE)
    dt = jax.nn.softplus(dt_raw.astype(jnp.float32) + dt_bias.astype(jnp.float32))
    a_neg = -jnp.exp(a_log.astype(jnp.float32))
    if ssm_prev is None:
        y, new_ssm = ssd_chunked(xs, dt, a_neg, Bm, Cm)
    else:
        y, new_ssm = ssd_recurrent(xs, dt, a_neg, Bm, Cm, ssm_prev)
    y = y + d_skip.astype(jnp.float32)[:, None] * xs.astype(jnp.float32)
    yg = y.reshape(b, L, D_INNER) * jax.nn.silu(z.astype(jnp.float32))
    yg = yg.reshape(b, L, SSD_GROUPS, D_INNER // SSD_GROUPS)
    yg = yg * lax.rsqrt(jnp.mean(yg * yg, axis=-1, keepdims=True) + EPS)
    yg = yg.reshape(b, L, D_INNER) * gnorm_w.astype(jnp.float32)
    return (yg.astype(h.dtype) @ w_out).astype(h.dtype), new_ssm, new_conv


def sgu_mixer(h, w_in, b_in, ln_g, ln_b, w_s, b_s, w_out):
    b, L, _ = h.shape
    uv = jax.nn.gelu(h @ w_in + b_in, approximate=False)
    u, v = uv[..., :D_SGU], uv[..., D_SGU:]
    v = layernorm(v, ln_g, ln_b)
    q = min(L, SGU_CHUNK)
    nc = L // q
    vc = v.reshape(b, nc, q, SGU_HEADS, SGU_HEAD_DIM)
    ws = jnp.tril(w_s[:, :q, :q])
    s = jnp.einsum('gij,bcjgd->bcigd', ws, vc) + jnp.transpose(b_s[:, :q])[None, None, :, :, None]
    y = u * s.reshape(b, L, D_SGU).astype(u.dtype)
    return (y @ w_out).astype(h.dtype), v


def swiglu(h, w_gu, w_down, d_ff):
    gu = h @ w_gu
    return (jax.nn.silu(gu[..., :d_ff]) * gu[..., d_ff:]) @ w_down


def moe_swiglu(h, w_router, w_gu, w_down):
    logits = (h @ w_router).astype(jnp.float32)
    top_v, top_i = lax.top_k(logits, TOP_K)
    gates = jax.nn.softmax(top_v, axis=-1)
    combine = jnp.sum(jax.nn.one_hot(top_i, N_EXPERTS, dtype=jnp.float32) * gates[..., None], axis=-2)
    out = jnp.zeros(h.shape, jnp.float32)
    for e in range(N_EXPERTS):
        out = out + combine[..., e:e + 1] * swiglu(h, w_gu[e], w_down[e], D_FF_EXPERT).astype(jnp.float32)
    return out.astype(h.dtype)


def setup_inputs(seed: int = 0) -> dict:
    key = jax.random.key(seed)
    ks = iter(jax.random.split(key, 40))
    nrm = lambda shape, scale: jax.random.normal(next(ks), shape, jnp.float32) * scale
    d_in_proj = 2 * D_INNER + 2 * SSD_GROUPS * D_STATE + SSD_HEADS
    dt0 = jnp.exp(jax.random.uniform(next(ks), (N_SSD, SSD_HEADS), jnp.float32,
                                     minval=math.log(1e-3), maxval=math.log(1e-1)))
    inp = {
        "x_prompt": nrm((BATCH, SEQ, D_MODEL), 1.0),
        "x_sample": nrm((DEC_BATCH, DEC_SEQ, D_MODEL), 1.0),
        "state_ssm": nrm((N_SSD, DEC_BATCH, SSD_HEADS, SSD_HEADDIM, D_STATE), 0.1),
        "state_conv": nrm((N_SSD, DEC_BATCH, CONV_W - 1, CONV_DIM), 1.0),
        "norm_mix": 1.0 + nrm((DEPTH, D_MODEL), 0.05),
        "norm_ffn": 1.0 + nrm((DEPTH, D_MODEL), 0.05),
        "norm_final": 1.0 + nrm((D_MODEL,), 0.05),
        "ssd_w_in": nrm((N_SSD, D_MODEL, d_in_proj), D_MODEL ** -0.5),
        "ssd_conv_w": nrm((N_SSD, CONV_W, CONV_DIM), CONV_W ** -0.5),
        "ssd_conv_b": nrm((N_SSD, CONV_DIM), 0.02),
        "ssd_dt_bias": dt0 + jnp.log(-jnp.expm1(-dt0)),
        "ssd_a_log": jnp.log(jax.random.uniform(next(ks), (N_SSD, SSD_HEADS), jnp.float32, minval=1.0, maxval=16.0)),
        "ssd_d": 1.0 + nrm((N_SSD, SSD_HEADS), 0.1),
        "ssd_gnorm": 1.0 + nrm((N_SSD, D_INNER), 0.05),
        "ssd_w_out": nrm((N_SSD, D_INNER, D_MODEL), D_INNER ** -0.5),
        "sgu_w_in": nrm((N_SGU, D_MODEL, 2 * D_SGU), D_MODEL ** -0.5),
        "sgu_b_in": nrm((N_SGU, 2 * D_SGU), 0.02),
        "sgu_ln_g": 1.0 + nrm((N_SGU, D_SGU), 0.05),
        "sgu_ln_b": nrm((N_SGU, D_SGU), 0.02),
        "sgu_w_s": nrm((N_SGU, SGU_HEADS, SGU_CHUNK, SGU_CHUNK), SGU_CHUNK ** -0.5),
        "sgu_b_s": 1.0 + nrm((N_SGU, SGU_HEADS, SGU_CHUNK), 0.1),
        "sgu_w_out": nrm((N_SGU, D_SGU, D_MODEL), D_SGU ** -0.5),
        "ffn_w_gu": nrm((N_SSD, D_MODEL, 2 * D_FF_DENSE), D_MODEL ** -0.5),
        "ffn_w_down": nrm((N_SSD, D_FF_DENSE, D_MODEL), D_FF_DENSE ** -0.5),
        "moe_w_router": nrm((N_SGU, D_MODEL, N_EXPERTS), D_MODEL ** -0.5),
        "moe_w_gu": nrm((N_SGU, N_EXPERTS, D_MODEL, 2 * D_FF_EXPERT), D_MODEL ** -0.5),
        "moe_w_down": nrm((N_SGU, N_EXPERTS, D_FF_EXPERT, D_MODEL), D_FF_EXPERT ** -0.5),
    }
    return inp


def reference(x_prompt, x_sample, state_ssm, state_conv, norm_mix, norm_ffn, norm_final,
              ssd_w_in, ssd_conv_w, ssd_conv_b, ssd_dt_bias, ssd_a_log, ssd_d, ssd_gnorm, ssd_w_out,
              sgu_w_in, sgu_b_in, sgu_ln_g, sgu_ln_b, sgu_w_s, sgu_b_s, sgu_w_out,
              ffn_w_gu, ffn_w_down, moe_w_router, moe_w_gu, moe_w_down):
    hp, hs = x_prompt, x_sample
    ssm_p, conv_p, ssm_s, conv_s, v_s = [], [], [], [], []
    for i in range(DEPTH):
        j = i // 2
        a_p, a_s = rmsnorm(hp, norm_mix[i]), rmsnorm(hs, norm_mix[i])
        if i % 2 == 0:
            prm = (ssd_w_in[j], ssd_conv_w[j], ssd_conv_b[j], ssd_dt_bias[j], ssd_a_log[j],
                   ssd_d[j], ssd_gnorm[j], ssd_w_out[j])
            yp, sp, cp = ssd_mixer(a_p, None, None, *prm)
            ys, ss, cs = ssd_mixer(a_s, state_conv[j], state_ssm[j], *prm)
            ssm_p.append(sp); conv_p.append(cp); ssm_s.append(ss); conv_s.append(cs)
        else:
            prm = (sgu_w_in[j], sgu_b_in[j], sgu_ln_g[j], sgu_ln_b[j], sgu_w_s[j], sgu_b_s[j], sgu_w_out[j])
            yp, _ = sgu_mixer(a_p, *prm)
            ys, vs = sgu_mixer(a_s, *prm)
            v_s.append(vs)
        hp = hp + yp
        hs = hs + ys
        a_p, a_s = rmsnorm(hp, norm_ffn[i]), rmsnorm(hs, norm_ffn[i])
        if i % 2 == 0:
            hp = hp + swiglu(a_p, ffn_w_gu[j], ffn_w_down[j], D_FF_DENSE)
            hs = hs + swiglu(a_s, ffn_w_gu[j], ffn_w_down[j], D_FF_DENSE)
        else:
            hp = hp + moe_swiglu(a_p, moe_w_router[j], moe_w_gu[j], moe_w_down[j])
            hs = hs + moe_swiglu(a_s, moe_w_router[j], moe_w_gu[j], moe_w_down[j])
    y_prompt = rmsnorm(hp, norm_final)
    y_sample = rmsnorm(hs, norm_final)
    return (y_prompt, y_sample, jnp.stack(ssm_p), jnp.stack(conv_p), jnp.stack(ssm_s), jnp.stack(conv_s), jnp.stack(v_s))
```

```python
import functools

import jax
import jax.numpy as jnp
from jax import lax
from jax.experimental import pallas as pl
from jax.experimental.pallas import tpu as pltpu

F32 = jnp.float32
BF16 = jnp.bfloat16
I32 = jnp.int32
HIGHEST = lax.Precision.HIGHEST

D_MODEL = 1024
D_INNER = 2048
SSD_HEADS = 32
SSD_HEADDIM = 64
SSD_GROUPS = 4
D_STATE = 128
CONV_W = 4
CONV_DIM = D_INNER + 2 * SSD_GROUPS * D_STATE
CHUNK = 128
D_SGU = 2048
SGU_HEADS = 8
SGU_HEAD_DIM = D_SGU // SGU_HEADS
D_FF_DENSE = 2816
N_EXPERTS = 8
D_FF_EXPERT = 3584
EPS = 1e-5

LANES = 128
VMEM_CAP = 60000 * 1024
VMEM_FLOOR = 32 * 1024 * 1024


def _vmem_limit(nbytes):
    return int(min(max(nbytes * 5 // 4 + (4 << 20), VMEM_FLOOR), VMEM_CAP))


def _params(sem, nbytes):
    return pltpu.CompilerParams(dimension_semantics=sem, vmem_limit_bytes=_vmem_limit(nbytes))


def _silu(x):
    return x / (1.0 + jnp.exp(-x))


def _rms(x, g):
    return x * lax.rsqrt(jnp.mean(x * x, axis=-1, keepdims=True) + EPS) * g


def _bdot(a, b):
    return jnp.dot(a.astype(BF16), b.astype(BF16), preferred_element_type=F32)


def _rmsnorm_kernel(x_ref, g_ref, o_ref):
    o_ref[...] = _rms(x_ref[...], g_ref[...]).astype(o_ref.dtype)


def _rmsnorm(x, g, tm, out_dtype):
    m, d = x.shape
    return pl.pallas_call(
        _rmsnorm_kernel,
        grid=(m // tm,),
        in_specs=[pl.BlockSpec((tm, d), lambda i: (i, 0)), pl.BlockSpec((1, d), lambda i: (0, 0))],
        out_specs=pl.BlockSpec((tm, d), lambda i: (i, 0)),
        out_shape=jax.ShapeDtypeStruct((m, d), out_dtype),
        compiler_params=_params(("arbitrary",), 4 * tm * d * 4),
        name="rmsnorm",
    )(x, g.reshape(1, d))


def _mm_plain_kernel(x_ref, w_ref, o_ref, wb_ref):
    @pl.when(pl.program_id(1) == 0)
    def _():
        wb_ref[...] = w_ref[...].astype(BF16)

    o_ref[...] = _bdot(x_ref[...], wb_ref[...]).astype(o_ref.dtype)


def _mm_bias_gelu_kernel(x_ref, w_ref, b_ref, o_ref, wb_ref):
    @pl.when(pl.program_id(1) == 0)
    def _():
        wb_ref[...] = w_ref[...].astype(BF16)

    y = _bdot(x_ref[...], wb_ref[...]) + b_ref[...]
    o_ref[...] = (0.5 * y * (1.0 + lax.erf(y * (2.0 ** -0.5)))).astype(o_ref.dtype)


def _mm_swiglu_kernel(x_ref, wg_ref, wu_ref, o_ref, wgb_ref, wub_ref):
    @pl.when(pl.program_id(1) == 0)
    def _():
        wgb_ref[...] = wg_ref[...].astype(BF16)
        wub_ref[...] = wu_ref[...].astype(BF16)

    x = x_ref[...].astype(BF16)
    g = jnp.dot(x, wgb_ref[...], preferred_element_type=F32)
    u = jnp.dot(x, wub_ref[...], preferred_element_type=F32)
    o_ref[...] = (_silu(g) * u).astype(o_ref.dtype)


def _wspec(k, tn, col_block0):
    return pl.BlockSpec((k, tn), lambda j, i: (0, j + col_block0), pipeline_mode=pl.Buffered(1))


def _mm_plain(x, w, col0, n, tm, tn, out_dtype, name):
    m, k = x.shape
    nbytes = k * tn * 6 + 2 * tm * k * x.dtype.itemsize + 2 * tm * tn * 4
    return pl.pallas_call(
        _mm_plain_kernel,
        grid=(n // tn, m // tm),
        in_specs=[pl.BlockSpec((tm, k), lambda j, i: (i, 0)), _wspec(k, tn, col0 // tn)],
        out_specs=pl.BlockSpec((tm, tn), lambda j, i: (i, j)),
        out_shape=jax.ShapeDtypeStruct((m, n), out_dtype),
        scratch_shapes=[pltpu.VMEM((k, tn), BF16)],
        compiler_params=_params(("arbitrary", "arbitrary"), nbytes),
        name=name,
    )(x, w)


def _mm_bias_gelu(x, w, b, tm, tn, out_dtype, name):
    m, k = x.shape
    n = w.shape[1]
    nbytes = k * tn * 6 + 2 * tm * k * x.dtype.itemsize + 2 * tm * tn * 4
    return pl.pallas_call(
        _mm_bias_gelu_kernel,
        grid=(n // tn, m // tm),
        in_specs=[pl.BlockSpec((tm, k), lambda j, i: (i, 0)), _wspec(k, tn, 0),
                  pl.BlockSpec((1, tn), lambda j, i: (0, j))],
        out_specs=pl.BlockSpec((tm, tn), lambda j, i: (i, j)),
        out_shape=jax.ShapeDtypeStruct((m, n), out_dtype),
        scratch_shapes=[pltpu.VMEM((k, tn), BF16)],
        compiler_params=_params(("arbitrary", "arbitrary"), nbytes),
        name=name,
    )(x, w, b.reshape(1, n))


def _mm_swiglu(x, w_gu, d_ff, tm, tn, name):
    m, k = x.shape
    nbytes = 2 * k * tn * 6 + 2 * tm * k * x.dtype.itemsize + 2 * tm * tn * 2 + 3 * tm * tn * 4
    return pl.pallas_call(
        _mm_swiglu_kernel,
        grid=(d_ff // tn, m // tm),
        in_specs=[pl.BlockSpec((tm, k), lambda j, i: (i, 0)), _wspec(k, tn, 0), _wspec(k, tn, d_ff // tn)],
        out_specs=pl.BlockSpec((tm, tn), lambda j, i: (i, j)),
        out_shape=jax.ShapeDtypeStruct((m, d_ff), BF16),
        scratch_shapes=[pltpu.VMEM((k, tn), BF16), pltpu.VMEM((k, tn), BF16)],
        compiler_params=_params(("arbitrary", "arbitrary"), nbytes),
        name=name,
    )(x, w_gu, w_gu)


def _mm_resnorm_kernel(x_ref, w_ref, r_ref, g_ref, *rest, with_router):
    if with_router:
        wr_ref, h_ref, a_ref, lg_ref, wb_ref = rest
    else:
        h_ref, a_ref, wb_ref = rest

    @pl.when(pl.program_id(0) == 0)
    def _():
        wb_ref[...] = w_ref[...].astype(BF16)

    h = r_ref[...] + _bdot(x_ref[...], wb_ref[...])
    h_ref[...] = h
    a = _rms(h, g_ref[...])
    a_ref[...] = a.astype(a_ref.dtype)
    if with_router:
        lg_ref[...] = jnp.dot(a, wr_ref[...], precision=HIGHEST, preferred_element_type=F32)


def _mm_resnorm(x, w, resid, g, tm, a_dtype, name, w_router=None):
    m, k = x.shape
    d = w.shape[1]
    with_router = w_router is not None
    in_specs = [pl.BlockSpec((tm, k), lambda i: (i, 0)),
                pl.BlockSpec((k, d), lambda i: (0, 0), pipeline_mode=pl.Buffered(1)),
                pl.BlockSpec((tm, d), lambda i: (i, 0)),
                pl.BlockSpec((1, d), lambda i: (0, 0))]
    out_specs = [pl.BlockSpec((tm, d), lambda i: (i, 0)), pl.BlockSpec((tm, d), lambda i: (i, 0))]
    out_shape = [jax.ShapeDtypeStruct((m, d), F32), jax.ShapeDtypeStruct((m, d), a_dtype)]
    args = [x, w, resid, g.reshape(1, d)]
    if with_router:
        in_specs.append(pl.BlockSpec((d, LANES), lambda i: (0, 0)))
        out_specs.append(pl.BlockSpec((tm, LANES), lambda i: (i, 0)))
        out_shape.append(jax.ShapeDtypeStruct((m, LANES), F32))
        args.append(w_router)
    nbytes = k * d * 6 + 2 * tm * k * 2 + 8 * tm * d * 4
    return pl.pallas_call(
        functools.partial(_mm_resnorm_kernel, with_router=with_router),
        grid=(m // tm,),
        in_specs=in_specs,
        out_specs=out_specs,
        out_shape=out_shape,
        scratch_shapes=[pltpu.VMEM((k, d), BF16)],
        compiler_params=_params(("arbitrary",), nbytes),
        name=name,
    )(*args)


def _gate_and_groupnorm(y, xs, z, d_rep, gn_w):
    yg = (y + d_rep * xs) * _silu(z)
    gw = D_INNER // SSD_GROUPS
    outs = []
    for g in range(SSD_GROUPS):
        blk = yg[:, g * gw:(g + 1) * gw]
        outs.append(blk * lax.rsqrt(jnp.mean(blk * blk, axis=-1, keepdims=True) + EPS))
    return jnp.concatenate(outs, axis=1) * gn_w


def _softplus(x):
    return jnp.maximum(x, 0.0) + jnp.log(1.0 + jnp.exp(-jnp.abs(x)))


def _ssd_prompt_kernel(xbc_ref, z_ref, dtr_ref, cw_ref, cb_ref, dtb_ref, alog_ref, drep_ref, gnw_ref,
                       yg_ref, ssm_ref, conv_ref, ext_ref, st_ref, y_ref, xa_ref):
    c = pl.program_id(1)
    nc = pl.num_programs(1)
    q = CHUNK

    @pl.when(c == 0)
    def _():
        ext_ref[0:8, :] = jnp.zeros((8, CONV_DIM), F32)
        st_ref[...] = jnp.zeros_like(st_ref)

    ext_ref[8:8 + q, :] = xbc_ref[...]
    acc = cb_ref[...] + cw_ref[3:4, :] * ext_ref[8:8 + q, :]
    for k in range(CONV_W - 1):
        acc = acc + cw_ref[k:k + 1, :] * ext_ref[5 + k:5 + k + q, :]
    xa_ref[...] = _silu(acc)
    tail = ext_ref[q:q + 8, :]
    ext_ref[0:8, :] = tail

    dt = _softplus(dtr_ref[...] + dtb_ref[...])
    a_neg = -jnp.exp(alog_ref[...])
    dta = dt * a_neg
    row_i = lax.broadcasted_iota(I32, (q, q), 0)
    col_i = lax.broadcasted_iota(I32, (q, q), 1)
    causal = row_i >= col_i
    tril = jnp.where(causal, 1.0, 0.0)
    cs = jnp.dot(tril, dta, precision=HIGHEST, preferred_element_type=F32)
    cs_t = cs.T
    dt_t = dt.T
    w_t = dt_t * jnp.exp(cs_t[:, q - 1:q] - cs_t)
    lane = lax.broadcasted_iota(I32, (q, 2 * SSD_HEADDIM), 1)
    first = lane < SSD_HEADDIM

    for g in range(SSD_GROUPS):
        b_g = xa_ref[:, D_INNER + g * D_STATE:D_INNER + (g + 1) * D_STATE]
        c_g = xa_ref[:, D_INNER + (SSD_GROUPS + g) * D_STATE:D_INNER + (SSD_GROUPS + g + 1) * D_STATE]
        b_gt = b_g.T
        cb = _bdot(c_g, b_gt)
        for pr in range(4):
            hp = g * 4 + pr
            cols = slice(hp * 2 * SSD_HEADDIM, (hp + 1) * 2 * SSD_HEADDIM)
            x_p = xa_ref[:, cols].astype(BF16)
            st_p = st_ref[:, cols]
            rhs = jnp.concatenate([x_p, st_p.astype(BF16)], axis=0)
            ys, ss, cds = [], [], []
            for s in range(2):
                h = 2 * hp + s
                colb = jnp.broadcast_to(cs[:, h:h + 1], (q, q))
                rowb = jnp.broadcast_to(cs_t[h:h + 1, :], (q, q))
                ecol = jnp.exp(colb)
                lmat = jnp.where(causal, jnp.exp(jnp.minimum(colb - rowb, 0.0)), 0.0)
                m_h = cb * lmat * dt_t[h:h + 1, :]
                lhs = jnp.concatenate([m_h.astype(BF16), (c_g * ecol).astype(BF16)], axis=1)
                ys.append(jnp.dot(lhs, rhs, preferred_element_type=F32))
                bw = (b_gt * w_t[h:h + 1, :]).astype(BF16)
                ss.append(jnp.dot(bw, x_p, preferred_element_type=F32))
                cds.append(ecol[q - 1:q, :])
            y_ref[:, cols] = jnp.where(first, ys[0], ys[1])
            cd = jnp.where(first[0:1, :], cds[0], cds[1])
            st_ref[:, cols] = st_p * cd + jnp.where(first, ss[0], ss[1])

    out = _gate_and_groupnorm(y_ref[...], xa_ref[:, :D_INNER], z_ref[...].astype(F32), drep_ref[...],
                              gnw_ref[...])
    yg_ref[...] = out.astype(yg_ref.dtype)

    @pl.when(c == nc - 1)
    def _():
        conv_ref[0] = tail[8 - (CONV_W - 1):, :]
        for hp in range(SSD_HEADS // 2):
            blk = st_ref[:, hp * 2 * SSD_HEADDIM:(hp + 1) * 2 * SSD_HEADDIM].T
            ssm_ref[0, 2 * hp:2 * hp + 2] = blk.reshape(2, SSD_HEADDIM, D_STATE)


def _ssd_prompt(xbc, z, dtr, conv_w, conv_b, dtb, alog, d_rep, gn_w, batch, seq):
    nc = seq // CHUNK
    row = lambda b, c: (b * nc + c, 0)
    fixed = lambda b, c: (0, 0)
    nbytes = 2 * CHUNK * (CONV_DIM * 4 + D_INNER * 4 + LANES * 4) + (CHUNK + 8) * CONV_DIM * 4 \
        + 4 * D_STATE * D_INNER * 4 + 16 * CHUNK * CONV_DIM * 4
    return pl.pallas_call(
        _ssd_prompt_kernel,
        grid=(batch, nc),
        in_specs=[pl.BlockSpec((CHUNK, CONV_DIM), row), pl.BlockSpec((CHUNK, D_INNER), row),
                  pl.BlockSpec((CHUNK, LANES), row),
                  pl.BlockSpec((CONV_W, CONV_DIM), fixed), pl.BlockSpec((1, CONV_DIM), fixed),
                  pl.BlockSpec((1, LANES), fixed), pl.BlockSpec((1, LANES), fixed),
                  pl.BlockSpec((1, D_INNER), fixed), pl.BlockSpec((1, D_INNER), fixed)],
        out_specs=[pl.BlockSpec((CHUNK, D_INNER), row),
                   pl.BlockSpec((1, SSD_HEADS, SSD_HEADDIM, D_STATE), lambda b, c: (b, 0, 0, 0)),
                   pl.BlockSpec((1, CONV_W - 1, CONV_DIM), lambda b, c: (b, 0, 0))],
        out_shape=[jax.ShapeDtypeStruct((batch * seq, D_INNER), BF16),
                   jax.ShapeDtypeStruct((batch, SSD_HEADS, SSD_HEADDIM, D_STATE), F32),
                   jax.ShapeDtypeStruct((batch, CONV_W - 1, CONV_DIM), F32)],
        scratch_shapes=[pltpu.VMEM((CHUNK + 8, CONV_DIM), F32), pltpu.VMEM((D_STATE, D_INNER), F32),
                        pltpu.VMEM((CHUNK, D_INNER), F32), pltpu.VMEM((CHUNK, CONV_DIM), F32)],
        compiler_params=_params(("arbitrary", "arbitrary"), nbytes),
        name="ssd_prompt",
    )(xbc, z, dtr, conv_w, conv_b, dtb, alog, d_rep, gn_w)


def _ssd_sample_pre_kernel(xbc_ref, conv_ref, dtr_ref, cw_ref, cb_ref, dtb_ref, alog_ref,
                           nconv_ref, xa_ref, xdt_ref, da_ref):
    xn = xbc_ref[...]
    acc = cb_ref[...] + cw_ref[3:4, :] * xn
    for k in range(CONV_W - 1):
        acc = acc + cw_ref[k:k + 1, :] * conv_ref[:, k * CONV_DIM:(k + 1) * CONV_DIM]
    xa = _silu(acc)
    xa_ref[...] = xa
    nconv_ref[:, 0:CONV_DIM] = conv_ref[:, CONV_DIM:2 * CONV_DIM]
    nconv_ref[:, CONV_DIM:2 * CONV_DIM] = conv_ref[:, 2 * CONV_DIM:3 * CONV_DIM]
    nconv_ref[:, 2 * CONV_DIM:3 * CONV_DIM] = xn

    dt = _softplus(dtr_ref[...] + dtb_ref[...])
    a_neg = -jnp.exp(alog_ref[...])
    hh = lax.broadcasted_iota(I32, (LANES, D_INNER), 0)
    cc = lax.broadcasted_iota(I32, (LANES, D_INNER), 1)
    expand = jnp.where(lax.shift_right_logical(cc, 6) == hh, 1.0, 0.0)
    dt_rep = jnp.dot(dt, expand, precision=HIGHEST, preferred_element_type=F32)
    dta_rep = jnp.dot(dt * a_neg, expand, precision=HIGHEST, preferred_element_type=F32)
    xdt_ref[...] = xa[:, :D_INNER] * dt_rep
    da_ref[...] = jnp.exp(dta_rep)


def _ssd_sample_pre(xbc, conv_flat, dtr, conv_w, conv_b, dtb, alog):
    n = xbc.shape[0]
    full = lambda shape: pl.BlockSpec(shape, lambda i: (0,) * len(shape))
    return pl.pallas_call(
        _ssd_sample_pre_kernel,
        grid=(1,),
        in_specs=[full((n, CONV_DIM)), full((n, 3 * CONV_DIM)), full((n, LANES)), full((CONV_W, CONV_DIM)),
                  full((1, CONV_DIM)), full((1, LANES)), full((1, LANES))],
        out_specs=[full((n, 3 * CONV_DIM)), full((n, CONV_DIM)), full((n, D_INNER)), full((n, D_INNER))],
        out_shape=[jax.ShapeDtypeStruct((n, 3 * CONV_DIM), F32), jax.ShapeDtypeStruct((n, CONV_DIM), F32),
                   jax.ShapeDtypeStruct((n, D_INNER), F32), jax.ShapeDtypeStruct((n, D_INNER), F32)],
        compiler_params=_params(("arbitrary",), 24 * n * CONV_DIM * 4),
        name="ssd_sample_pre",
    )(xbc, conv_flat, dtr, conv_w, conv_b, dtb, alog)


def _ssd_sample_state_kernel(st_ref, xdt_ref, da_ref, b_ref, c_ref, nst_ref, y_ref):
    nblk = D_INNER // LANES
    pad = jnp.zeros((LANES - nblk, LANES), F32)
    xdt_c = jnp.concatenate([xdt_ref[0], pad], axis=0).T
    da_c = jnp.concatenate([da_ref[0], pad], axis=0).T
    lane = lax.broadcasted_iota(I32, (LANES, LANES), 1)
    y_c = jnp.zeros((LANES, LANES), F32)
    for j in range(nblk):
        g = j // (nblk // SSD_GROUPS)
        h_old = st_ref[0, j * LANES:(j + 1) * LANES, :]
        h_new = da_c[:, j:j + 1] * h_old + xdt_c[:, j:j + 1] * b_ref[0, g:g + 1, :]
        nst_ref[0, j * LANES:(j + 1) * LANES, :] = h_new
        y_c = jnp.where(lane == j, jnp.sum(h_new * c_ref[0, g:g + 1, :], axis=1, keepdims=True), y_c)
    y_ref[0] = y_c.T[0:nblk, :]


def _ssd_sample_state(state, xdt, da, bm, cm):
    n = state.shape[0]
    nblk = D_INNER // LANES
    return pl.pallas_call(
        _ssd_sample_state_kernel,
        grid=(n,),
        in_specs=[pl.BlockSpec((1, D_INNER, D_STATE), lambda i: (i, 0, 0)),
                  pl.BlockSpec((1, nblk, LANES), lambda i: (i, 0, 0)),
                  pl.BlockSpec((1, nblk, LANES), lambda i: (i, 0, 0)),
                  pl.BlockSpec((1, SSD_GROUPS, D_STATE), lambda i: (i, 0, 0)),
                  pl.BlockSpec((1, SSD_GROUPS, D_STATE), lambda i: (i, 0, 0))],
        out_specs=[pl.BlockSpec((1, D_INNER, D_STATE), lambda i: (i, 0, 0)),
                   pl.BlockSpec((1, nblk, LANES), lambda i: (i, 0, 0))],
        out_shape=[jax.ShapeDtypeStruct((n, D_INNER, D_STATE), F32),
                   jax.ShapeDtypeStruct((n, nblk, LANES), F32)],
        compiler_params=_params(("arbitrary",), 6 * D_INNER * D_STATE * 4),
        name="ssd_sample_state",
    )(state, xdt, da, bm, cm)


def _ssd_sample_post_kernel(y_ref, xa_ref, z_ref, drep_ref, gnw_ref, o_ref):
    out = _gate_and_groupnorm(y_ref[...], xa_ref[:, :D_INNER], z_ref[...].astype(F32), drep_ref[...], gnw_ref[...])
    o_ref[...] = out.astype(o_ref.dtype)


def _ssd_sample_post(y, xa, z, d_rep, gn_w):
    n = y.shape[0]
    full = lambda shape: pl.BlockSpec(shape, lambda i: (0,) * len(shape))
    return pl.pallas_call(
        _ssd_sample_post_kernel,
        grid=(1,),
        in_specs=[full((n, D_INNER)), full((n, CONV_DIM)), full((n, D_INNER)), full((1, D_INNER)), full((1, D_INNER))],
        out_specs=full((n, D_INNER)),
        out_shape=jax.ShapeDtypeStruct((n, D_INNER), BF16),
        compiler_params=_params(("arbitrary",), 12 * n * CONV_DIM * 4),
        name="ssd_sample_post",
    )(y, xa, z, d_rep, gn_w)


def _layernorm(v, g, b):
    mu = jnp.mean(v, axis=-1, keepdims=True)
    d = v - mu
    var = jnp.mean(d * d, axis=-1, keepdims=True)
    return d * lax.rsqrt(var + EPS) * g + b


def _sgu_prompt_kernel(u_ref, v_ref, lng_ref, lnb_ref, ws_ref, bst_ref, o_ref):
    q = CHUNK
    vn = _layernorm(v_ref[...].astype(F32), lng_ref[...], lnb_ref[...]).astype(BF16)
    causal = lax.broadcasted_iota(I32, (q, q), 0) >= lax.broadcasted_iota(I32, (q, q), 1)
    for g in range(SGU_HEADS):
        cols = slice(g * SGU_HEAD_DIM, (g + 1) * SGU_HEAD_DIM)
        w = jnp.where(causal, ws_ref[g], 0.0).astype(BF16)
        s = jnp.dot(w, vn[:, cols], preferred_element_type=F32) + bst_ref[:, g:g + 1]
        o_ref[:, cols] = (u_ref[:, cols].astype(F32) * s).astype(o_ref.dtype)


def _sgu_prompt(uv, ln_g, ln_b, w_s, bs_t, batch, seq):
    nc = seq // CHUNK
    m = batch * seq
    fixed2 = lambda i: (0, 0)
    return pl.pallas_call(
        _sgu_prompt_kernel,
        grid=(m // CHUNK,),
        in_specs=[pl.BlockSpec((CHUNK, D_SGU), lambda i: (i, 0)), pl.BlockSpec((CHUNK, D_SGU), lambda i: (i, 1)),
                  pl.BlockSpec((1, D_SGU), fixed2), pl.BlockSpec((1, D_SGU), fixed2),
                  pl.BlockSpec((SGU_HEADS, CHUNK, CHUNK), lambda i: (0, 0, 0)),
                  pl.BlockSpec((CHUNK, LANES), fixed2)],
        out_specs=pl.BlockSpec((CHUNK, D_SGU), lambda i: (i, 0)),
        out_shape=jax.ShapeDtypeStruct((m, D_SGU), BF16),
        compiler_params=_params(("arbitrary",), 16 * CHUNK * D_SGU * 4),
        name="sgu_prompt",
    )(uv, uv, ln_g, ln_b, w_s, bs_t)


def _sgu_sample_kernel(u_ref, v_ref, lng_ref, lnb_ref, w0_ref, b0_ref, o_ref, vn_ref):
    vn = _layernorm(v_ref[...], lng_ref[...], lnb_ref[...])
    vn_ref[...] = vn
    o_ref[...] = (u_ref[...] * (w0_ref[...] * vn + b0_ref[...])).astype(o_ref.dtype)


def _sgu_sample(uv, ln_g, ln_b, w0_rep, b0_rep):
    n = uv.shape[0]
    fixed2 = lambda i: (0, 0)
    return pl.pallas_call(
        _sgu_sample_kernel,
        grid=(1,),
        in_specs=[pl.BlockSpec((n, D_SGU), lambda i: (0, 0)), pl.BlockSpec((n, D_SGU), lambda i: (0, 1)),
                  pl.BlockSpec((1, D_SGU), fixed2), pl.BlockSpec((1, D_SGU), fixed2),
                  pl.BlockSpec((1, D_SGU), fixed2), pl.BlockSpec((1, D_SGU), fixed2)],
        out_specs=[pl.BlockSpec((n, D_SGU), fixed2), pl.BlockSpec((n, D_SGU), fixed2)],
        out_shape=[jax.ShapeDtypeStruct((n, D_SGU), BF16), jax.ShapeDtypeStruct((n, D_SGU), F32)],
        compiler_params=_params(("arbitrary",), 12 * n * D_SGU * 4),
        name="sgu_sample",
    )(uv, uv, ln_g, ln_b, w0_rep, b0_rep)


def _route_kernel(lg_ref, pos_ref, gate_ref, meta_ref, rank_ref, *, tile):
    nb = lg_ref.shape[0]
    sub = lax.broadcasted_iota(I32, (N_EXPERTS, LANES), 0)
    lane = lax.broadcasted_iota(I32, (N_EXPERTS, LANES), 1)
    incl = jnp.where(lax.broadcasted_iota(I32, (LANES, LANES), 0) <= lax.broadcasted_iota(I32, (LANES, LANES), 1),
                     1.0, 0.0)
    neg = jnp.float32(-jnp.inf)

    subf = sub.astype(F32)
    none = jnp.float32(N_EXPERTS)

    def top2(l):
        m1 = jnp.max(l, axis=0, keepdims=True)
        i1 = jnp.min(jnp.where(l == m1, subf, none), axis=0, keepdims=True)
        l2 = jnp.where(subf == i1, neg, l)
        m2 = jnp.max(l2, axis=0, keepdims=True)
        i2 = jnp.min(jnp.where(l2 == m2, subf, none), axis=0, keepdims=True)
        return m1, i1, m2, i2

    def count_body(k, carry):
        m1, i1, m2, i2 = top2(lg_ref[k])
        sel = jnp.where(subf == i1, 1.0, jnp.where(subf == i2, 1.0, 0.0))
        run = jnp.dot(sel, incl, preferred_element_type=F32)
        rank_ref[k] = run - sel + carry
        return carry + run[:, LANES - 1:LANES]

    counts = lax.fori_loop(0, nb, count_body, jnp.zeros((N_EXPERTS, 1), F32))
    tiles = jnp.ceil(counts / tile)
    tiles_row = jnp.sum(jnp.where(sub == lane, tiles, 0.0), axis=0, keepdims=True)
    cum_incl = jnp.sum(jnp.where(lane <= sub, tiles_row, 0.0), axis=1, keepdims=True)
    offset = (cum_incl - tiles) * tile
    tile_expert = jnp.sum(jnp.where(lane.astype(F32) >= cum_incl, 1.0, 0.0), axis=0, keepdims=True)
    tile_expert = jnp.minimum(tile_expert, N_EXPERTS - 1.0)
    n_used = jnp.max(cum_incl, axis=0, keepdims=True)
    meta_ref[...] = jnp.where(sub == 0, tile_expert, jnp.where(sub == 1, n_used, 0.0)).astype(I32)

    def pos_body(k, carry):
        m1, i1, m2, i2 = top2(lg_ref[k])
        slot = rank_ref[k] + offset
        p1 = jnp.sum(jnp.where(subf == i1, slot, 0.0), axis=0, keepdims=True)
        p2 = jnp.sum(jnp.where(subf == i2, slot, 0.0), axis=0, keepdims=True)
        pos_ref[k] = jnp.where(sub == 0, p1, jnp.where(sub == 1, p2, 0.0)).astype(I32)
        e = jnp.exp(m2 - m1)
        g1 = 1.0 / (1.0 + e)
        g2 = e / (1.0 + e)
        gate_ref[k] = jnp.where(sub == 0, g1, jnp.where(sub == 1, g2, 0.0))
        return carry

    lax.fori_loop(0, nb, pos_body, 0)


def _route(logits3, tile):
    nb = logits3.shape[0]
    blk = pl.BlockSpec((nb, N_EXPERTS, LANES), lambda i: (0, 0, 0))
    return pl.pallas_call(
        functools.partial(_route_kernel, tile=tile),
        grid=(1,),
        in_specs=[blk],
        out_specs=[blk, blk, pl.BlockSpec((N_EXPERTS, LANES), lambda i: (0, 0))],
        out_shape=[jax.ShapeDtypeStruct((nb, N_EXPERTS, LANES), I32),
                   jax.ShapeDtypeStruct((nb, N_EXPERTS, LANES), F32),
                   jax.ShapeDtypeStruct((N_EXPERTS, LANES), I32)],
        scratch_shapes=[pltpu.VMEM((nb, N_EXPERTS, LANES), F32)],
        compiler_params=_params(("arbitrary",), 8 * nb * N_EXPERTS * LANES * 4),
        name="moe_route",
    )(logits3)


def _row_copy(src_ref, src_row, dst_ref, dst_row, sem):
    return pltpu.make_async_copy(src_ref.at[pl.ds(src_row, 1)], dst_ref.at[pl.ds(dst_row, 1)], sem)


def _moe_scatter_kernel(pos_ref, a_ref, xs_in_ref, xs_ref, sem, *, tb):
    del xs_in_ref

    def start(r, carry):
        _row_copy(a_ref, r, xs_ref, pos_ref[0, 0, r], sem).start()
        _row_copy(a_ref, r, xs_ref, pos_ref[0, 0, tb + r], sem).start()
        return carry

    lax.fori_loop(0, tb, start, 0)

    def wait(r, carry):
        _row_copy(a_ref, r, xs_ref, pos_ref[0, 0, r], sem).wait()
        _row_copy(a_ref, r, xs_ref, pos_ref[0, 0, tb + r], sem).wait()
        return carry

    lax.fori_loop(0, tb, wait, 0)


def _moe_scatter(a, pos3, n_slots, tb):
    m, d = a.shape
    xs0 = jnp.zeros((n_slots, d), a.dtype)
    return pl.pallas_call(
        functools.partial(_moe_scatter_kernel, tb=tb),
        grid=(m // tb,),
        in_specs=[pl.BlockSpec((1, 1, 2 * tb), lambda i: (i, 0, 0), memory_space=pltpu.SMEM),
                  pl.BlockSpec((tb, d), lambda i: (i, 0)),
                  pl.BlockSpec(memory_space=pl.ANY)],
        out_specs=pl.BlockSpec(memory_space=pl.ANY),
        out_shape=jax.ShapeDtypeStruct((n_slots, d), a.dtype),
        scratch_shapes=[pltpu.SemaphoreType.DMA(())],
        input_output_aliases={2: 0},
        compiler_params=_params(("arbitrary",), 4 * tb * d * 4),
        name="moe_scatter",
    )(pos3, a, xs0)


def _new_weights(te_ref):
    i = pl.program_id(1)
    return (i == 0) | (te_ref[i] != te_ref[jnp.maximum(i - 1, 0)])


def _moe_gu_kernel(te_ref, nu_ref, x_ref, wg_ref, wu_ref, o_ref, wgb_ref, wub_ref):
    i = pl.program_id(1)

    @pl.when(_new_weights(te_ref))
    def _():
        wgb_ref[...] = wg_ref[0].astype(BF16)
        wub_ref[...] = wu_ref[0].astype(BF16)

    @pl.when(i < nu_ref[0])
    def _():
        x = x_ref[...].astype(BF16)
        g = jnp.dot(x, wgb_ref[...], preferred_element_type=F32)
        u = jnp.dot(x, wub_ref[...], preferred_element_type=F32)
        o_ref[...] = (_silu(g) * u).astype(o_ref.dtype)

    @pl.when(i >= nu_ref[0])
    def _():
        o_ref[...] = jnp.zeros_like(o_ref)


def _moe_gu(xs, w_gu, te, nu, tm, tn):
    s, k = xs.shape
    f = D_FF_EXPERT
    nbytes = 2 * (2 * k * tn * 4 + k * tn * 2) + 2 * tm * k * 4 + 2 * tm * tn * 2 + 3 * tm * tn * 4
    grid_spec = pltpu.PrefetchScalarGridSpec(
        num_scalar_prefetch=2,
        grid=(f // tn, s // tm),
        in_specs=[pl.BlockSpec((tm, k), lambda j, i, te, nu: (i, 0)),
                  pl.BlockSpec((1, k, tn), lambda j, i, te, nu: (te[i], 0, j)),
                  pl.BlockSpec((1, k, tn), lambda j, i, te, nu: (te[i], 0, j + f // tn))],
        out_specs=pl.BlockSpec((tm, tn), lambda j, i, te, nu: (i, j)),
        scratch_shapes=[pltpu.VMEM((k, tn), BF16), pltpu.VMEM((k, tn), BF16)],
    )
    return pl.pallas_call(
        _moe_gu_kernel,
        grid_spec=grid_spec,
        out_shape=jax.ShapeDtypeStruct((s, f), BF16),
        compiler_params=_params(("arbitrary", "arbitrary"), nbytes),
        name="moe_gate_up",
    )(te, nu, xs, w_gu, w_gu)


def _moe_down_kernel(te_ref, nu_ref, x_ref, w_ref, o_ref, wb_ref):
    i = pl.program_id(1)

    @pl.when(_new_weights(te_ref))
    def _():
        wb_ref[...] = w_ref[0].astype(BF16)

    @pl.when(i < nu_ref[0])
    def _():
        o_ref[...] = jnp.dot(x_ref[...], wb_ref[...], preferred_element_type=F32)

    @pl.when(i >= nu_ref[0])
    def _():
        o_ref[...] = jnp.zeros_like(o_ref)


def _moe_down(act, w_down, te, nu, tm, tn):
    s, k = act.shape
    d = w_down.shape[2]
    nbytes = 2 * k * tn * 4 + k * tn * 2 + 2 * tm * k * 2 + 3 * tm * tn * 4
    grid_spec = pltpu.PrefetchScalarGridSpec(
        num_scalar_prefetch=2,
        grid=(d // tn, s // tm),
        in_specs=[pl.BlockSpec((tm, k), lambda j, i, te, nu: (i, 0)),
                  pl.BlockSpec((1, k, tn), lambda j, i, te, nu: (te[i], 0, j))],
        out_specs=pl.BlockSpec((tm, tn), lambda j, i, te, nu: (i, j)),
        scratch_shapes=[pltpu.VMEM((k, tn), BF16)],
    )
    return pl.pallas_call(
        _moe_down_kernel,
        grid_spec=grid_spec,
        out_shape=jax.ShapeDtypeStruct((s, d), F32),
        compiler_params=_params(("arbitrary", "arbitrary"), nbytes),
        name="moe_down",
    )(te, nu, act, w_down)


def _moe_combine_kernel(pos_ref, h_ref, gate_ref, g_ref, ys_ref, o_ref, buf_ref, sem, *, tb):
    def start(r, carry):
        _row_copy(ys_ref, pos_ref[0, 0, r], buf_ref.at[0], r, sem).start()
        _row_copy(ys_ref, pos_ref[0, 0, tb + r], buf_ref.at[1], r, sem).start()
        return carry

    lax.fori_loop(0, tb, start, 0)

    def wait(r, carry):
        _row_copy(ys_ref, pos_ref[0, 0, r], buf_ref.at[0], r, sem).wait()
        _row_copy(ys_ref, pos_ref[0, 0, tb + r], buf_ref.at[1], r, sem).wait()
        return carry

    lax.fori_loop(0, tb, wait, 0)
    h = h_ref[...] + gate_ref[:, 0:1] * buf_ref[0] + gate_ref[:, 1:2] * buf_ref[1]
    o_ref[...] = _rms(h, g_ref[...])


def _moe_combine(h, gates, pos3, ys, g, tb):
    m, d = h.shape
    return pl.pallas_call(
        functools.partial(_moe_combine_kernel, tb=tb),
        grid=(m // tb,),
        in_specs=[pl.BlockSpec((1, 1, 2 * tb), lambda i: (i, 0, 0), memory_space=pltpu.SMEM),
                  pl.BlockSpec((tb, d), lambda i: (i, 0)),
                  pl.BlockSpec((tb, N_EXPERTS), lambda i: (i, 0)),
                  pl.BlockSpec((1, d), lambda i: (0, 0)),
                  pl.BlockSpec(memory_space=pl.ANY)],
        out_specs=pl.BlockSpec((tb, d), lambda i: (i, 0)),
        out_shape=jax.ShapeDtypeStruct((m, d), F32),
        scratch_shapes=[pltpu.VMEM((2, tb, d), F32), pltpu.SemaphoreType.DMA(())],
        compiler_params=_params(("arbitrary",), 8 * tb * d * 4),
        name="moe_combine",
    )(pos3, h, gates, g.reshape(1, d), ys)


def _moe(h, a, logits, w_gu, w_down, g_final, tm, tb):
    m, d = h.shape
    nb = m // LANES
    lg3 = logits[:, :N_EXPERTS].reshape(nb, LANES, N_EXPERTS).transpose(0, 2, 1)
    n_tiles = (2 * m) // tm + N_EXPERTS
    pos, gate, meta = _route(lg3, tm)
    te = meta[0, :n_tiles]
    nu = meta[1, 0:1]
    pos3 = pos[:, 0:2, :].reshape(m // tb, tb // LANES, 2, LANES).transpose(0, 2, 1, 3).reshape(m // tb, 1, 2 * tb)
    gates = gate.transpose(0, 2, 1).reshape(m, N_EXPERTS)
    xs = _moe_scatter(a, pos3, n_tiles * tm, tb)
    act = _moe_gu(xs, w_gu, te, nu, tm, 896)
    ys = _moe_down(act, w_down, te, nu, tm, 512)
    return _moe_combine(h, gates, pos3, ys, g_final, tb)


def _pad_lanes(v):
    return jnp.pad(v.reshape(1, -1), ((0, 0), (0, LANES - v.shape[-1])))


def kernel(x_prompt, x_sample, state_ssm, state_conv, norm_mix, norm_ffn, norm_final, ssd_w_in, ssd_conv_w,
           ssd_conv_b, ssd_dt_bias, ssd_a_log, ssd_d, ssd_gnorm, ssd_w_out, sgu_w_in, sgu_b_in, sgu_ln_g,
           sgu_ln_b, sgu_w_s, sgu_b_s, sgu_w_out, ffn_w_gu, ffn_w_down, moe_w_router, moe_w_gu, moe_w_down):
    batch, seq, d = x_prompt.shape
    n_s = x_sample.shape[0]
    mp = batch * seq
    xp = x_prompt.reshape(mp, d)
    xs = x_sample.reshape(n_s, d)

    w_in = ssd_w_in[0]
    w_dt = jnp.pad(w_in[:, D_INNER + CONV_DIM:], ((0, 0), (0, LANES - SSD_HEADS)))
    dtb = _pad_lanes(ssd_dt_bias[0])
    alog = _pad_lanes(ssd_a_log[0])
    d_rep = jnp.repeat(ssd_d[0], SSD_HEADDIM).reshape(1, D_INNER)
    gn_w = ssd_gnorm[0].reshape(1, D_INNER)
    conv_w = ssd_conv_w[0]
    conv_b = ssd_conv_b[0].reshape(1, CONV_DIM)
    ln_g = sgu_ln_g[0].reshape(1, D_SGU)
    ln_b = sgu_ln_b[0].reshape(1, D_SGU)
    bs_t = jnp.pad(sgu_b_s[0].T, ((0, 0), (0, LANES - SGU_HEADS)))
    w0_rep = jnp.repeat(sgu_w_s[0][:, 0, 0], SGU_HEAD_DIM).reshape(1, D_SGU)
    b0_rep = jnp.repeat(sgu_b_s[0][:, 0], SGU_HEAD_DIM).reshape(1, D_SGU)
    w_router = jnp.pad(moe_w_router[0], ((0, 0), (0, LANES - N_EXPERTS)))

    def in_proj(a, tm):
        z = _mm_plain(a, w_in, 0, D_INNER, tm, 1024, BF16, "ssd_in_z")
        xbc = _mm_plain(a, w_in, D_INNER, CONV_DIM, tm, 1024, F32, "ssd_in_xbc")
        dtr = _mm_plain(a, w_dt, 0, LANES, tm, LANES, F32, "ssd_in_dt")
        return z, xbc, dtr

    def after_mixer0(yg, resid, tm):
        h1, a1 = _mm_resnorm(yg, ssd_w_out[0], resid, norm_ffn[0], tm, BF16, "ssd_out")
        act = _mm_swiglu(a1, ffn_w_gu[0], D_FF_DENSE, tm, 1408, "ffn_gate_up")
        return _mm_resnorm(act, ffn_w_down[0], h1, norm_mix[1], tm, BF16, "ffn_down")

    a0p = _rmsnorm(xp, norm_mix[0], 512, BF16)
    zp, xbcp, dtrp = in_proj(a0p, 1024)
    ygp, ssm_p, conv_p = _ssd_prompt(xbcp, zp, dtrp, conv_w, conv_b, dtb, alog, d_rep, gn_w, batch, seq)
    h2p, a2p = after_mixer0(ygp, xp, 512)

    a0s = _rmsnorm(xs, norm_mix[0], n_s, BF16)
    zs, xbcs, dtrs = in_proj(a0s, n_s)
    conv_s, xas, xdts, das = _ssd_sample_pre(xbcs, state_conv[0].reshape(n_s, 3 * CONV_DIM), dtrs, conv_w, conv_b,
                                             dtb, alog)
    nblk = D_INNER // LANES
    bms = xas[:, D_INNER:D_INNER + SSD_GROUPS * D_STATE].reshape(n_s, SSD_GROUPS, D_STATE)
    cms = xas[:, D_INNER + SSD_GROUPS * D_STATE:].reshape(n_s, SSD_GROUPS, D_STATE)
    ssm_s, ys3 = _ssd_sample_state(state_ssm[0].reshape(n_s, D_INNER, D_STATE), xdts.reshape(n_s, nblk, LANES),
                                   das.reshape(n_s, nblk, LANES), bms, cms)
    ygs = _ssd_sample_post(ys3.reshape(n_s, D_INNER), xas, zs, d_rep, gn_w)
    h2s, a2s = after_mixer0(ygs, xs, n_s)

    uvp = _mm_bias_gelu(a2p, sgu_w_in[0], sgu_b_in[0], 1024, 1024, BF16, "sgu_in")
    sp = _sgu_prompt(uvp, ln_g, ln_b, sgu_w_s[0], bs_t, batch, seq)
    h3p, a3p, lgp = _mm_resnorm(sp, sgu_w_out[0], h2p, norm_ffn[1], 512, F32, "sgu_out", w_router)
    y_prompt = _moe(h3p, a3p, lgp, moe_w_gu[0], moe_w_down[0], norm_final, 512, 256)

    uvs = _mm_bias_gelu(a2s, sgu_w_in[0], sgu_b_in[0], n_s, 1024, F32, "sgu_in")
    ss, v_s = _sgu_sample(uvs, ln_g, ln_b, w0_rep, b0_rep)
    h3s, a3s, lgs = _mm_resnorm(ss, sgu_w_out[0], h2s, norm_ffn[1], n_s, F32, "sgu_out", w_router)
    y_sample = _moe(h3s, a3s, lgs, moe_w_gu[0], moe_w_down[0], norm_final, 128, 128)

    return (y_prompt.reshape(batch, seq, d),
            y_sample.reshape(n_s, 1, d),
            ssm_p[None],
            conv_p[None],
            ssm_s.reshape(1, n_s, SSD_HEADS, SSD_HEADDIM, D_STATE),
            conv_s.reshape(1, n_s, CONV_W - 1, CONV_DIM),
            v_s.reshape(1, n_s, 1, D_SGU))
```

```python
import functools

import jax
import jax.numpy as jnp
from jax import lax
from jax.experimental import pallas as pl
from jax.experimental.pallas import tpu as pltpu

F32 = jnp.float32
BF16 = jnp.bfloat16
I32 = jnp.int32
HIGHEST = lax.Precision.HIGHEST

D_MODEL = 1024
D_INNER = 2048
SSD_HEADS = 32
SSD_HEADDIM = 64
SSD_GROUPS = 4
D_STATE = 128
CONV_W = 4
CONV_DIM = D_INNER + 2 * SSD_GROUPS * D_STATE
CHUNK = 128
D_SGU = 2048
SGU_HEADS = 8
SGU_HEAD_DIM = D_SGU // SGU_HEADS
D_FF_DENSE = 2816
N_EXPERTS = 8
D_FF_EXPERT = 3584
EPS = 1e-5

LANES = 128
SUBLANES = 8
VMEM_CAP = 60000 * 1024
VMEM_FLOOR = 32 * 1024 * 1024


def _vmem_limit(nbytes):
    return int(min(max(nbytes * 5 // 4 + (4 << 20), VMEM_FLOOR), VMEM_CAP))


def _params(sem, nbytes):
    return pltpu.CompilerParams(dimension_semantics=sem, vmem_limit_bytes=_vmem_limit(nbytes))


def _silu(x):
    return x / (1.0 + jnp.exp(-x))


def _rms(x, g):
    return x * lax.rsqrt(jnp.mean(x * x, axis=-1, keepdims=True) + EPS) * g


def _bdot(a, b):
    return jnp.dot(a.astype(BF16), b.astype(BF16), preferred_element_type=F32)


def _rmsnorm_kernel(x_ref, g_ref, o_ref):
    o_ref[...] = _rms(x_ref[...], g_ref[...]).astype(o_ref.dtype)


def _rmsnorm(x, g, tm, out_dtype):
    m, d = x.shape
    return pl.pallas_call(
        _rmsnorm_kernel,
        grid=(m // tm,),
        in_specs=[pl.BlockSpec((tm, d), lambda i: (i, 0)), pl.BlockSpec((1, d), lambda i: (0, 0))],
        out_specs=pl.BlockSpec((tm, d), lambda i: (i, 0)),
        out_shape=jax.ShapeDtypeStruct((m, d), out_dtype),
        compiler_params=_params(("arbitrary",), 4 * tm * d * 4),
        name="rmsnorm",
    )(x, g.reshape(1, d))


def _mm_plain_kernel(x_ref, w_ref, o_ref, wb_ref):
    @pl.when(pl.program_id(1) == 0)
    def _():
        wb_ref[...] = w_ref[...].astype(BF16)

    o_ref[...] = _bdot(x_ref[...], wb_ref[...]).astype(o_ref.dtype)


def _mm_bias_gelu_kernel(x_ref, w_ref, b_ref, o_ref, wb_ref):
    @pl.when(pl.program_id(1) == 0)
    def _():
        wb_ref[...] = w_ref[...].astype(BF16)

    y = _bdot(x_ref[...], wb_ref[...]) + b_ref[...]
    o_ref[...] = (0.5 * y * (1.0 + lax.erf(y * (2.0 ** -0.5)))).astype(o_ref.dtype)


def _mm_swiglu_kernel(x_ref, wg_ref, wu_ref, o_ref, wgb_ref, wub_ref):
    @pl.when(pl.program_id(1) == 0)
    def _():
        wgb_ref[...] = wg_ref[...].astype(BF16)
        wub_ref[...] = wu_ref[...].astype(BF16)

    x = x_ref[...].astype(BF16)
    g = jnp.dot(x, wgb_ref[...], preferred_element_type=F32)
    u = jnp.dot(x, wub_ref[...], preferred_element_type=F32)
    o_ref[...] = (_silu(g) * u).astype(o_ref.dtype)


def _wspec(k, tn, col_block0):
    return pl.BlockSpec((k, tn), lambda j, i: (0, j + col_block0), pipeline_mode=pl.Buffered(1))


def _mm_plain(x, w, col0, n, tm, tn, out_dtype, name):
    m, k = x.shape
    nbytes = k * tn * 6 + 2 * tm * k * x.dtype.itemsize + 2 * tm * tn * 4
    return pl.pallas_call(
        _mm_plain_kernel,
        grid=(n // tn, m // tm),
        in_specs=[pl.BlockSpec((tm, k), lambda j, i: (i, 0)), _wspec(k, tn, col0 // tn)],
        out_specs=pl.BlockSpec((tm, tn), lambda j, i: (i, j)),
        out_shape=jax.ShapeDtypeStruct((m, n), out_dtype),
        scratch_shapes=[pltpu.VMEM((k, tn), BF16)],
        compiler_params=_params(("arbitrary", "arbitrary"), nbytes),
        name=name,
    )(x, w)


def _mm_bias_gelu(x, w, b, tm, tn, out_dtype, name):
    m, k = x.shape
    n = w.shape[1]
    nbytes = k * tn * 6 + 2 * tm * k * x.dtype.itemsize + 2 * tm * tn * 4
    return pl.pallas_call(
        _mm_bias_gelu_kernel,
        grid=(n // tn, m // tm),
        in_specs=[pl.BlockSpec((tm, k), lambda j, i: (i, 0)), _wspec(k, tn, 0),
                  pl.BlockSpec((1, tn), lambda j, i: (0, j))],
        out_specs=pl.BlockSpec((tm, tn), lambda j, i: (i, j)),
        out_shape=jax.ShapeDtypeStruct((m, n), out_dtype),
        scratch_shapes=[pltpu.VMEM((k, tn), BF16)],
        compiler_params=_params(("arbitrary", "arbitrary"), nbytes),
        name=name,
    )(x, w, b.reshape(1, n))


def _mm_swiglu(x, w_gu, d_ff, tm, tn, name):
    m, k = x.shape
    nbytes = 2 * k * tn * 6 + 2 * tm * k * x.dtype.itemsize + 2 * tm * tn * 2 + 3 * tm * tn * 4
    return pl.pallas_call(
        _mm_swiglu_kernel,
        grid=(d_ff // tn, m // tm),
        in_specs=[pl.BlockSpec((tm, k), lambda j, i: (i, 0)), _wspec(k, tn, 0), _wspec(k, tn, d_ff // tn)],
        out_specs=pl.BlockSpec((tm, tn), lambda j, i: (i, j)),
        out_shape=jax.ShapeDtypeStruct((m, d_ff), BF16),
        scratch_shapes=[pltpu.VMEM((k, tn), BF16), pltpu.VMEM((k, tn), BF16)],
        compiler_params=_params(("arbitrary", "arbitrary"), nbytes),
        name=name,
    )(x, w_gu, w_gu)


def _mm_resnorm_kernel(x_ref, w_ref, r_ref, g_ref, h_ref, a_ref, wb_ref):
    @pl.when(pl.program_id(0) == 0)
    def _():
        wb_ref[...] = w_ref[...].astype(BF16)

    h = r_ref[...] + _bdot(x_ref[...], wb_ref[...])
    h_ref[...] = h
    a_ref[...] = _rms(h, g_ref[...]).astype(a_ref.dtype)


def _mm_resnorm(x, w, resid, g, tm, name):
    m, k = x.shape
    d = w.shape[1]
    nbytes = k * d * 6 + 2 * tm * k * 2 + 8 * tm * d * 4
    return pl.pallas_call(
        _mm_resnorm_kernel,
        grid=(m // tm,),
        in_specs=[pl.BlockSpec((tm, k), lambda i: (i, 0)),
                  pl.BlockSpec((k, d), lambda i: (0, 0), pipeline_mode=pl.Buffered(1)),
                  pl.BlockSpec((tm, d), lambda i: (i, 0)),
                  pl.BlockSpec((1, d), lambda i: (0, 0))],
        out_specs=[pl.BlockSpec((tm, d), lambda i: (i, 0)), pl.BlockSpec((tm, d), lambda i: (i, 0))],
        out_shape=[jax.ShapeDtypeStruct((m, d), F32), jax.ShapeDtypeStruct((m, d), BF16)],
        scratch_shapes=[pltpu.VMEM((k, d), BF16)],
        compiler_params=_params(("arbitrary",), nbytes),
        name=name,
    )(x, w, resid, g.reshape(1, d))


def _mm_resnorm_router_kernel(xp_ref, xt_ref, w_ref, rp_ref, rt_ref, g_ref, w3_ref, h_ref, a_ref, lg_ref, wb_ref,
                              *, prompt_tiles):
    i = pl.program_id(0)

    @pl.when(i == 0)
    def _():
        wb_ref[...] = w_ref[...].astype(BF16)

    def rows(x_ref, r_ref):
        h = r_ref[...] + _bdot(x_ref[...], wb_ref[...])
        h_ref[...] = h
        a = _rms(h, g_ref[...])
        a_hi = a.astype(BF16)
        a_lo = (a - a_hi.astype(F32)).astype(BF16)
        a_ref[...] = a_hi
        lg_ref[...] = (jnp.dot(a_hi, w3_ref[...], preferred_element_type=F32)
                       + jnp.dot(a_lo, w3_ref[...], preferred_element_type=F32))

    @pl.when(i < prompt_tiles)
    def _():
        rows(xp_ref, rp_ref)

    @pl.when(i == prompt_tiles)
    def _():
        rows(xt_ref, rt_ref)


def _mm_resnorm_router(xp, xt, w, rp, rt, g, w3, tm, name):
    mp, k = xp.shape
    d = w.shape[1]
    prompt_tiles = mp // tm
    last = prompt_tiles - 1
    rows = mp + tm
    nbytes = k * d * 6 + 4 * tm * k * 2 + 12 * tm * d * 4
    return pl.pallas_call(
        functools.partial(_mm_resnorm_router_kernel, prompt_tiles=prompt_tiles),
        grid=(prompt_tiles + 1,),
        in_specs=[pl.BlockSpec((tm, k), lambda i: (jnp.minimum(i, last), 0)),
                  pl.BlockSpec((tm, k), lambda i: (0, 0)),
                  pl.BlockSpec((k, d), lambda i: (0, 0), pipeline_mode=pl.Buffered(1)),
                  pl.BlockSpec((tm, d), lambda i: (jnp.minimum(i, last), 0)),
                  pl.BlockSpec((tm, d), lambda i: (0, 0)),
                  pl.BlockSpec((1, d), lambda i: (0, 0)),
                  pl.BlockSpec((d, LANES), lambda i: (0, 0))],
        out_specs=[pl.BlockSpec((tm, d), lambda i: (i, 0)), pl.BlockSpec((tm, d), lambda i: (i, 0)),
                   pl.BlockSpec((tm, LANES), lambda i: (i, 0))],
        out_shape=[jax.ShapeDtypeStruct((rows, d), F32), jax.ShapeDtypeStruct((rows, d), BF16),
                   jax.ShapeDtypeStruct((rows, LANES), F32)],
        scratch_shapes=[pltpu.VMEM((k, d), BF16)],
        compiler_params=_params(("arbitrary",), nbytes),
        name=name,
    )(xp, xt, w, rp, rt, g.reshape(1, d), w3)


def _gate_and_groupnorm(y, xs, z, d_rep, gn_w):
    yg = (y + d_rep * xs) * _silu(z)
    gw = D_INNER // SSD_GROUPS
    outs = []
    for g in range(SSD_GROUPS):
        blk = yg[:, g * gw:(g + 1) * gw]
        outs.append(blk * lax.rsqrt(jnp.mean(blk * blk, axis=-1, keepdims=True) + EPS))
    return jnp.concatenate(outs, axis=1) * gn_w


def _softplus(x):
    return jnp.maximum(x, 0.0) + jnp.log(1.0 + jnp.exp(-jnp.abs(x)))


def _ssd_prompt_kernel(xbc_ref, z_ref, dtr_ref, cw_ref, cb_ref, dtb_ref, alog_ref, drep_ref, gnw_ref,
                       yg_ref, ssm_ref, conv_ref, ext_ref, st_ref, y_ref, xa_ref):
    c = pl.program_id(1)
    nc = pl.num_programs(1)
    q = CHUNK

    @pl.when(c == 0)
    def _():
        ext_ref[0:8, :] = jnp.zeros((8, CONV_DIM), F32)
        st_ref[...] = jnp.zeros_like(st_ref)

    ext_ref[8:8 + q, :] = xbc_ref[...]
    acc = cb_ref[...] + cw_ref[3:4, :] * ext_ref[8:8 + q, :]
    for k in range(CONV_W - 1):
        acc = acc + cw_ref[k:k + 1, :] * ext_ref[5 + k:5 + k + q, :]
    xa_ref[...] = _silu(acc)
    tail = ext_ref[q:q + 8, :]
    ext_ref[0:8, :] = tail

    dt = _softplus(dtr_ref[...] + dtb_ref[...])
    a_neg = -jnp.exp(alog_ref[...])
    dta = dt * a_neg
    row_i = lax.broadcasted_iota(I32, (q, q), 0)
    col_i = lax.broadcasted_iota(I32, (q, q), 1)
    causal = row_i >= col_i
    tril = jnp.where(causal, 1.0, 0.0)
    cs = jnp.dot(tril, dta, precision=HIGHEST, preferred_element_type=F32)
    cs_t = cs.T
    dt_t = dt.T
    w_t = dt_t * jnp.exp(cs_t[:, q - 1:q] - cs_t)
    lane = lax.broadcasted_iota(I32, (q, 2 * SSD_HEADDIM), 1)
    first = lane < SSD_HEADDIM

    for g in range(SSD_GROUPS):
        b_g = xa_ref[:, D_INNER + g * D_STATE:D_INNER + (g + 1) * D_STATE]
        c_g = xa_ref[:, D_INNER + (SSD_GROUPS + g) * D_STATE:D_INNER + (SSD_GROUPS + g + 1) * D_STATE]
        b_gt = b_g.T
        cb = _bdot(c_g, b_gt)
        for pr in range(4):
            hp = g * 4 + pr
            cols = slice(hp * 2 * SSD_HEADDIM, (hp + 1) * 2 * SSD_HEADDIM)
            x_p = xa_ref[:, cols].astype(BF16)
            st_p = st_ref[:, cols]
            rhs = jnp.concatenate([x_p, st_p.astype(BF16)], axis=0)
            ys, ss, cds = [], [], []
            for s in range(2):
                h = 2 * hp + s
                colb = jnp.broadcast_to(cs[:, h:h + 1], (q, q))
                rowb = jnp.broadcast_to(cs_t[h:h + 1, :], (q, q))
                ecol = jnp.exp(colb)
                lmat = jnp.where(causal, jnp.exp(jnp.minimum(colb - rowb, 0.0)), 0.0)
                m_h = cb * lmat * dt_t[h:h + 1, :]
                lhs = jnp.concatenate([m_h.astype(BF16), (c_g * ecol).astype(BF16)], axis=1)
                ys.append(jnp.dot(lhs, rhs, preferred_element_type=F32))
                bw = (b_gt * w_t[h:h + 1, :]).astype(BF16)
                ss.append(jnp.dot(bw, x_p, preferred_element_type=F32))
                cds.append(ecol[q - 1:q, :])
            y_ref[:, cols] = jnp.where(first, ys[0], ys[1])
            cd = jnp.where(first[0:1, :], cds[0], cds[1])
            st_ref[:, cols] = st_p * cd + jnp.where(first, ss[0], ss[1])

    out = _gate_and_groupnorm(y_ref[...], xa_ref[:, :D_INNER], z_ref[...].astype(F32), drep_ref[...],
                              gnw_ref[...])
    yg_ref[...] = out.astype(yg_ref.dtype)

    @pl.when(c == nc - 1)
    def _():
        conv_ref[0] = tail[8 - (CONV_W - 1):, :]
        for hp in range(SSD_HEADS // 2):
            blk = st_ref[:, hp * 2 * SSD_HEADDIM:(hp + 1) * 2 * SSD_HEADDIM].T
            ssm_ref[0, 2 * hp:2 * hp + 2] = blk.reshape(2, SSD_HEADDIM, D_STATE)


def _ssd_prompt(xbc, z, dtr, conv_w, conv_b, dtb, alog, d_rep, gn_w, batch, seq):
    nc = seq // CHUNK
    row = lambda b, c: (b * nc + c, 0)
    fixed = lambda b, c: (0, 0)
    nbytes = 2 * CHUNK * (CONV_DIM * 4 + D_INNER * 4 + LANES * 4) + (CHUNK + 8) * CONV_DIM * 4 \
        + 4 * D_STATE * D_INNER * 4 + 16 * CHUNK * CONV_DIM * 4
    return pl.pallas_call(
        _ssd_prompt_kernel,
        grid=(batch, nc),
        in_specs=[pl.BlockSpec((CHUNK, CONV_DIM), row), pl.BlockSpec((CHUNK, D_INNER), row),
                  pl.BlockSpec((CHUNK, LANES), row),
                  pl.BlockSpec((CONV_W, CONV_DIM), fixed), pl.BlockSpec((1, CONV_DIM), fixed),
                  pl.BlockSpec((1, LANES), fixed), pl.BlockSpec((1, LANES), fixed),
                  pl.BlockSpec((1, D_INNER), fixed), pl.BlockSpec((1, D_INNER), fixed)],
        out_specs=[pl.BlockSpec((CHUNK, D_INNER), row),
                   pl.BlockSpec((1, SSD_HEADS, SSD_HEADDIM, D_STATE), lambda b, c: (b, 0, 0, 0)),
                   pl.BlockSpec((1, CONV_W - 1, CONV_DIM), lambda b, c: (b, 0, 0))],
        out_shape=[jax.ShapeDtypeStruct((batch * seq, D_INNER), BF16),
                   jax.ShapeDtypeStruct((batch, SSD_HEADS, SSD_HEADDIM, D_STATE), F32),
                   jax.ShapeDtypeStruct((batch, CONV_W - 1, CONV_DIM), F32)],
        scratch_shapes=[pltpu.VMEM((CHUNK + 8, CONV_DIM), F32), pltpu.VMEM((D_STATE, D_INNER), F32),
                        pltpu.VMEM((CHUNK, D_INNER), F32), pltpu.VMEM((CHUNK, CONV_DIM), F32)],
        compiler_params=_params(("arbitrary", "arbitrary"), nbytes),
        name="ssd_prompt",
    )(xbc, z, dtr, conv_w, conv_b, dtb, alog, d_rep, gn_w)


def _ssd_sample_pre_kernel(xbc_ref, conv_ref, dtr_ref, cw_ref, cb_ref, dtb_ref, alog_ref,
                           nconv_ref, xa_ref, xdt_ref, da_ref):
    xn = xbc_ref[...]
    acc = cb_ref[...] + cw_ref[3:4, :] * xn
    for k in range(CONV_W - 1):
        acc = acc + cw_ref[k:k + 1, :] * conv_ref[:, k * CONV_DIM:(k + 1) * CONV_DIM]
    xa = _silu(acc)
    xa_ref[...] = xa
    nconv_ref[:, 0:CONV_DIM] = conv_ref[:, CONV_DIM:2 * CONV_DIM]
    nconv_ref[:, CONV_DIM:2 * CONV_DIM] = conv_ref[:, 2 * CONV_DIM:3 * CONV_DIM]
    nconv_ref[:, 2 * CONV_DIM:3 * CONV_DIM] = xn

    dt = _softplus(dtr_ref[...] + dtb_ref[...])
    a_neg = -jnp.exp(alog_ref[...])
    hh = lax.broadcasted_iota(I32, (LANES, D_INNER), 0)
    cc = lax.broadcasted_iota(I32, (LANES, D_INNER), 1)
    expand = jnp.where(lax.shift_right_logical(cc, 6) == hh, 1.0, 0.0)
    dt_rep = jnp.dot(dt, expand, precision=HIGHEST, preferred_element_type=F32)
    dta_rep = jnp.dot(dt * a_neg, expand, precision=HIGHEST, preferred_element_type=F32)
    xdt_ref[...] = xa[:, :D_INNER] * dt_rep
    da_ref[...] = jnp.exp(dta_rep)


def _ssd_sample_pre(xbc, conv_flat, dtr, conv_w, conv_b, dtb, alog):
    n = xbc.shape[0]
    full = lambda shape: pl.BlockSpec(shape, lambda i: (0,) * len(shape))
    return pl.pallas_call(
        _ssd_sample_pre_kernel,
        grid=(1,),
        in_specs=[full((n, CONV_DIM)), full((n, 3 * CONV_DIM)), full((n, LANES)), full((CONV_W, CONV_DIM)),
                  full((1, CONV_DIM)), full((1, LANES)), full((1, LANES))],
        out_specs=[full((n, 3 * CONV_DIM)), full((n, CONV_DIM)), full((n, D_INNER)), full((n, D_INNER))],
        out_shape=[jax.ShapeDtypeStruct((n, 3 * CONV_DIM), F32), jax.ShapeDtypeStruct((n, CONV_DIM), F32),
                   jax.ShapeDtypeStruct((n, D_INNER), F32), jax.ShapeDtypeStruct((n, D_INNER), F32)],
        compiler_params=_params(("arbitrary",), 24 * n * CONV_DIM * 4),
        name="ssd_sample_pre",
    )(xbc, conv_flat, dtr, conv_w, conv_b, dtb, alog)


def _ssd_sample_state_kernel(st_ref, xdt_ref, da_ref, b_ref, c_ref, nst_ref, y_ref):
    nblk = D_INNER // LANES
    pad = jnp.zeros((LANES - nblk, LANES), F32)
    xdt_c = jnp.concatenate([xdt_ref[0], pad], axis=0).T
    da_c = jnp.concatenate([da_ref[0], pad], axis=0).T
    lane = lax.broadcasted_iota(I32, (LANES, LANES), 1)
    y_c = jnp.zeros((LANES, LANES), F32)
    for j in range(nblk):
        g = j // (nblk // SSD_GROUPS)
        h_old = st_ref[0, j * LANES:(j + 1) * LANES, :]
        h_new = da_c[:, j:j + 1] * h_old + xdt_c[:, j:j + 1] * b_ref[0, g:g + 1, :]
        nst_ref[0, j * LANES:(j + 1) * LANES, :] = h_new
        y_c = jnp.where(lane == j, jnp.sum(h_new * c_ref[0, g:g + 1, :], axis=1, keepdims=True), y_c)
    y_ref[0] = y_c.T[0:nblk, :]


def _ssd_sample_state(state, xdt, da, bm, cm):
    n = state.shape[0]
    nblk = D_INNER // LANES
    return pl.pallas_call(
        _ssd_sample_state_kernel,
        grid=(n,),
        in_specs=[pl.BlockSpec((1, D_INNER, D_STATE), lambda i: (i, 0, 0)),
                  pl.BlockSpec((1, nblk, LANES), lambda i: (i, 0, 0)),
                  pl.BlockSpec((1, nblk, LANES), lambda i: (i, 0, 0)),
                  pl.BlockSpec((1, SSD_GROUPS, D_STATE), lambda i: (i, 0, 0)),
                  pl.BlockSpec((1, SSD_GROUPS, D_STATE), lambda i: (i, 0, 0))],
        out_specs=[pl.BlockSpec((1, D_INNER, D_STATE), lambda i: (i, 0, 0)),
                   pl.BlockSpec((1, nblk, LANES), lambda i: (i, 0, 0))],
        out_shape=[jax.ShapeDtypeStruct((n, D_INNER, D_STATE), F32),
                   jax.ShapeDtypeStruct((n, nblk, LANES), F32)],
        compiler_params=_params(("arbitrary",), 6 * D_INNER * D_STATE * 4),
        name="ssd_sample_state",
    )(state, xdt, da, bm, cm)


def _ssd_sample_post_kernel(y_ref, xa_ref, z_ref, drep_ref, gnw_ref, o_ref):
    out = _gate_and_groupnorm(y_ref[...], xa_ref[:, :D_INNER], z_ref[...].astype(F32), drep_ref[...], gnw_ref[...])
    o_ref[...] = out.astype(o_ref.dtype)


def _ssd_sample_post(y, xa, z, d_rep, gn_w):
    n = y.shape[0]
    full = lambda shape: pl.BlockSpec(shape, lambda i: (0,) * len(shape))
    return pl.pallas_call(
        _ssd_sample_post_kernel,
        grid=(1,),
        in_specs=[full((n, D_INNER)), full((n, CONV_DIM)), full((n, D_INNER)), full((1, D_INNER)), full((1, D_INNER))],
        out_specs=full((n, D_INNER)),
        out_shape=jax.ShapeDtypeStruct((n, D_INNER), BF16),
        compiler_params=_params(("arbitrary",), 12 * n * CONV_DIM * 4),
        name="ssd_sample_post",
    )(y, xa, z, d_rep, gn_w)


def _layernorm(v, g, b):
    mu = jnp.mean(v, axis=-1, keepdims=True)
    d = v - mu
    var = jnp.mean(d * d, axis=-1, keepdims=True)
    return d * lax.rsqrt(var + EPS) * g + b


def _sgu_prompt_kernel(u_ref, v_ref, lng_ref, lnb_ref, ws_ref, bst_ref, o_ref):
    q = CHUNK
    vn = _layernorm(v_ref[...].astype(F32), lng_ref[...], lnb_ref[...]).astype(BF16)
    causal = lax.broadcasted_iota(I32, (q, q), 0) >= lax.broadcasted_iota(I32, (q, q), 1)
    for g in range(SGU_HEADS):
        cols = slice(g * SGU_HEAD_DIM, (g + 1) * SGU_HEAD_DIM)
        w = jnp.where(causal, ws_ref[g], 0.0).astype(BF16)
        s = jnp.dot(w, vn[:, cols], preferred_element_type=F32) + bst_ref[:, g:g + 1]
        o_ref[:, cols] = (u_ref[:, cols].astype(F32) * s).astype(o_ref.dtype)


def _sgu_prompt(uv, ln_g, ln_b, w_s, bs_t, batch, seq):
    nc = seq // CHUNK
    m = batch * seq
    fixed2 = lambda i: (0, 0)
    return pl.pallas_call(
        _sgu_prompt_kernel,
        grid=(m // CHUNK,),
        in_specs=[pl.BlockSpec((CHUNK, D_SGU), lambda i: (i, 0)), pl.BlockSpec((CHUNK, D_SGU), lambda i: (i, 1)),
                  pl.BlockSpec((1, D_SGU), fixed2), pl.BlockSpec((1, D_SGU), fixed2),
                  pl.BlockSpec((SGU_HEADS, CHUNK, CHUNK), lambda i: (0, 0, 0)),
                  pl.BlockSpec((CHUNK, LANES), fixed2)],
        out_specs=pl.BlockSpec((CHUNK, D_SGU), lambda i: (i, 0)),
        out_shape=jax.ShapeDtypeStruct((m, D_SGU), BF16),
        compiler_params=_params(("arbitrary",), 16 * CHUNK * D_SGU * 4),
        name="sgu_prompt",
    )(uv, uv, ln_g, ln_b, w_s, bs_t)


def _sgu_sample_kernel(u_ref, v_ref, lng_ref, lnb_ref, w0_ref, b0_ref, o_ref, vn_ref):
    vn = _layernorm(v_ref[...], lng_ref[...], lnb_ref[...])
    vn_ref[...] = vn
    o_ref[...] = (u_ref[...] * (w0_ref[...] * vn + b0_ref[...])).astype(o_ref.dtype)


def _sgu_sample(uv, ln_g, ln_b, w0_rep, b0_rep):
    n = uv.shape[0]
    fixed2 = lambda i: (0, 0)
    return pl.pallas_call(
        _sgu_sample_kernel,
        grid=(1,),
        in_specs=[pl.BlockSpec((n, D_SGU), lambda i: (0, 0)), pl.BlockSpec((n, D_SGU), lambda i: (0, 1)),
                  pl.BlockSpec((1, D_SGU), fixed2), pl.BlockSpec((1, D_SGU), fixed2),
                  pl.BlockSpec((1, D_SGU), fixed2), pl.BlockSpec((1, D_SGU), fixed2)],
        out_specs=[pl.BlockSpec((n, D_SGU), fixed2), pl.BlockSpec((n, D_SGU), fixed2)],
        out_shape=[jax.ShapeDtypeStruct((n, D_SGU), BF16), jax.ShapeDtypeStruct((n, D_SGU), F32)],
        compiler_params=_params(("arbitrary",), 12 * n * D_SGU * 4),
        name="sgu_sample",
    )(uv, uv, ln_g, ln_b, w0_rep, b0_rep)


def _experts_to_lanes(col, sub, lane):
    return jnp.sum(jnp.where(sub == lane, col, 0.0), axis=0, keepdims=True)


def _route_kernel(lg_ref, loc_ref, gate_ref, bmeta_ref, tmeta_ref, *, tile, n_valid):
    nb, _, tb = lg_ref.shape
    sub = lax.broadcasted_iota(I32, (N_EXPERTS, LANES), 0)
    lane = lax.broadcasted_iota(I32, (N_EXPERTS, LANES), 1)
    subf = lax.broadcasted_iota(I32, (N_EXPERTS, tb), 0).astype(F32)
    tok = lax.broadcasted_iota(I32, (N_EXPERTS, tb), 1)
    incl = jnp.where(lax.broadcasted_iota(I32, (tb, tb), 0) <= lax.broadcasted_iota(I32, (tb, tb), 1), 1.0, 0.0)
    neg = jnp.float32(-jnp.inf)
    none = jnp.float32(N_EXPERTS)

    def select(k):
        blk = lg_ref[k]
        l = blk[0:N_EXPERTS] + blk[N_EXPERTS:2 * N_EXPERTS]
        m1 = jnp.max(l, axis=0, keepdims=True)
        i1 = jnp.min(jnp.where(l == m1, subf, none), axis=0, keepdims=True)
        l2 = jnp.where(subf == i1, neg, l)
        m2 = jnp.max(l2, axis=0, keepdims=True)
        i2 = jnp.min(jnp.where(l2 == m2, subf, none), axis=0, keepdims=True)
        valid = (k * tb + tok) < n_valid
        sel = jnp.where(valid, jnp.where(subf == i1, 1.0, jnp.where(subf == i2, 1.0, 0.0)), 0.0)
        return m1, i1, m2, i2, valid, sel

    def run_rows(cnt):
        return jnp.ceil(cnt / SUBLANES) * SUBLANES

    def count_body(k, carry):
        return carry + run_rows(jnp.sum(select(k)[5], axis=1, keepdims=True))

    counts = lax.fori_loop(0, nb, count_body, jnp.zeros((N_EXPERTS, 1), F32))
    tiles = jnp.ceil(counts / tile)
    cum_incl = jnp.sum(jnp.where(lane <= sub, _experts_to_lanes(tiles, sub, lane), 0.0), axis=1, keepdims=True)
    offset = (cum_incl - tiles) * tile
    tile_expert = jnp.sum(jnp.where(lane.astype(F32) >= cum_incl, 1.0, 0.0), axis=0, keepdims=True)
    tile_expert = jnp.minimum(tile_expert, N_EXPERTS - 1.0)
    n_used = jnp.max(cum_incl, axis=0, keepdims=True)
    tmeta_ref[...] = jnp.where(sub == 0, tile_expert, jnp.where(sub == 1, n_used, 0.0)).astype(I32)

    def place_body(k, before):
        m1, i1, m2, i2, valid, sel = select(k)
        run = jnp.dot(sel, incl, preferred_element_type=F32)
        cnt = run_rows(run[:, tb - 1:tb])
        lstart = jnp.sum(jnp.where(lane < sub, _experts_to_lanes(cnt, sub, lane), 0.0), axis=1, keepdims=True)
        local = lstart + run - sel
        loc1 = jnp.sum(jnp.where(subf == i1, local, 0.0), axis=0, keepdims=True)
        loc2 = jnp.sum(jnp.where(subf == i2, local, 0.0), axis=0, keepdims=True)
        live = (k * tb + tok[0:1, :]) < n_valid
        loc_ref[k] = jnp.where(subf == 0.0, jnp.where(live, loc1, -1.0),
                               jnp.where(subf == 1.0, jnp.where(live, loc2, -1.0), 0.0))
        e = jnp.exp(m2 - m1)
        gate_ref[k] = jnp.where(subf == 0.0, 1.0 / (1.0 + e), jnp.where(subf == 1.0, e / (1.0 + e), 0.0))
        bmeta_ref[k] = jnp.where(lane == 0, offset + before,
                                 jnp.where(lane == 1, cnt, jnp.where(lane == 2, lstart, 0.0))).astype(I32)
        return before + cnt

    lax.fori_loop(0, nb, place_body, jnp.zeros((N_EXPERTS, 1), F32))


def _route(logits3, tile, n_valid):
    nb, _, tb = logits3.shape
    tok_blk = pl.BlockSpec((nb, N_EXPERTS, tb), lambda i: (0, 0, 0))
    return pl.pallas_call(
        functools.partial(_route_kernel, tile=tile, n_valid=n_valid),
        grid=(1,),
        in_specs=[pl.BlockSpec((nb, 2 * N_EXPERTS, tb), lambda i: (0, 0, 0))],
        out_specs=[tok_blk, tok_blk, pl.BlockSpec((nb, N_EXPERTS, LANES), lambda i: (0, 0, 0)),
                   pl.BlockSpec((N_EXPERTS, LANES), lambda i: (0, 0))],
        out_shape=[jax.ShapeDtypeStruct((nb, N_EXPERTS, tb), F32),
                   jax.ShapeDtypeStruct((nb, N_EXPERTS, tb), F32),
                   jax.ShapeDtypeStruct((nb, N_EXPERTS, LANES), I32),
                   jax.ShapeDtypeStruct((N_EXPERTS, LANES), I32)],
        compiler_params=_params(("arbitrary",), 16 * nb * N_EXPERTS * tb * 4),
        name="moe_route",
    )(logits3)


def _run_copies(bm_ref, blk, vmem_ref, hbm_ref, sem, *, tb, to_hbm, wait):
    for e in range(N_EXPERTS):
        base = (blk * N_EXPERTS + e) * 3
        start, cnt, lstart = bm_ref[base], bm_ref[base + 1], bm_ref[base + 2]
        off = 0
        size = tb
        while size >= SUBLANES:
            @pl.when((cnt & size) != 0)
            def _(size=size, off=off):
                v = vmem_ref.at[pl.ds(pl.multiple_of(lstart + off, SUBLANES), size)]
                h = hbm_ref.at[pl.ds(pl.multiple_of(start + off, SUBLANES), size)]
                cp = pltpu.make_async_copy(v, h, sem) if to_hbm else pltpu.make_async_copy(h, v, sem)
                if wait:
                    cp.wait()
                else:
                    cp.start()

            off = off + (cnt & size)
            size //= 2


def _compact_rows(tb):
    return 2 * tb + N_EXPERTS * SUBLANES


def _selection(loc_ref, rows, tb):
    r = lax.broadcasted_iota(I32, (rows, tb), 0).astype(F32)
    return r == loc_ref[0, 0:1, :], r == loc_ref[0, 1:2, :]


def _moe_scatter_kernel(bm_ref, a_ref, loc_ref, gate_ref, xs_in_ref, gs_in_ref, xs_ref, gs_ref,
                        buf_ref, gbuf_ref, sem, gsem, *, tb):
    del xs_in_ref, gs_in_ref
    blk = pl.program_id(0)
    first, second = _selection(loc_ref, _compact_rows(tb), tb)
    pick = jnp.where(first, 1.0, jnp.where(second, 1.0, 0.0)).astype(BF16)
    buf_ref[...] = jnp.dot(pick, a_ref[...], preferred_element_type=F32)
    gsel = jnp.where(first, gate_ref[0, 0:1, :], jnp.where(second, gate_ref[0, 1:2, :], 0.0))
    gbuf_ref[...] = jnp.broadcast_to(jnp.sum(gsel, axis=1, keepdims=True), gbuf_ref.shape)
    for wait in (False, True):
        _run_copies(bm_ref, blk, buf_ref, xs_ref, sem, tb=tb, to_hbm=True, wait=wait)
        _run_copies(bm_ref, blk, gbuf_ref, gs_ref, gsem, tb=tb, to_hbm=True, wait=wait)


def _moe_scatter(a, loc, gate, bmeta, n_slots, tb):
    m, d = a.shape
    xs0 = jnp.zeros((n_slots, d), F32)
    gs0 = jnp.zeros((n_slots, LANES), F32)
    any_spec = pl.BlockSpec(memory_space=pl.ANY)
    tok_spec = pl.BlockSpec((1, N_EXPERTS, tb), lambda i, bm: (i, 0, 0))
    grid_spec = pltpu.PrefetchScalarGridSpec(
        num_scalar_prefetch=1,
        grid=(m // tb,),
        in_specs=[pl.BlockSpec((tb, d), lambda i, bm: (i, 0)), tok_spec, tok_spec, any_spec, any_spec],
        out_specs=[any_spec, any_spec],
        scratch_shapes=[pltpu.VMEM((_compact_rows(tb), d), F32), pltpu.VMEM((_compact_rows(tb), LANES), F32),
                        pltpu.SemaphoreType.DMA(()), pltpu.SemaphoreType.DMA(())],
    )
    return pl.pallas_call(
        functools.partial(_moe_scatter_kernel, tb=tb),
        grid_spec=grid_spec,
        out_shape=[jax.ShapeDtypeStruct((n_slots, d), F32), jax.ShapeDtypeStruct((n_slots, LANES), F32)],
        input_output_aliases={4: 0, 5: 1},
        compiler_params=_params(("arbitrary",), 8 * tb * d * 4),
        name="moe_scatter",
    )(bmeta, a, loc, gate, xs0, gs0)


def _new_weights(te_ref):
    i = pl.program_id(1)
    return (i == 0) | (te_ref[i] != te_ref[jnp.maximum(i - 1, 0)])


def _moe_gu_kernel(te_ref, nu_ref, x_ref, wg_ref, wu_ref, o_ref, wgb_ref, wub_ref):
    i = pl.program_id(1)

    @pl.when(_new_weights(te_ref))
    def _():
        wgb_ref[...] = wg_ref[0].astype(BF16)
        wub_ref[...] = wu_ref[0].astype(BF16)

    @pl.when(i < nu_ref[0])
    def _():
        x = x_ref[...].astype(BF16)
        g = jnp.dot(x, wgb_ref[...], preferred_element_type=F32)
        u = jnp.dot(x, wub_ref[...], preferred_element_type=F32)
        o_ref[...] = (_silu(g) * u).astype(o_ref.dtype)

    @pl.when(i >= nu_ref[0])
    def _():
        o_ref[...] = jnp.zeros_like(o_ref)


def _moe_gu(xs, w_gu, te, nu, tm, tn):
    s, k = xs.shape
    f = D_FF_EXPERT
    nbytes = 2 * (2 * k * tn * 4 + k * tn * 2) + 2 * tm * k * 4 + 2 * tm * tn * 2 + 3 * tm * tn * 4
    grid_spec = pltpu.PrefetchScalarGridSpec(
        num_scalar_prefetch=2,
        grid=(f // tn, s // tm),
        in_specs=[pl.BlockSpec((tm, k), lambda j, i, te, nu: (i, 0)),
                  pl.BlockSpec((1, k, tn), lambda j, i, te, nu: (te[i], 0, j)),
                  pl.BlockSpec((1, k, tn), lambda j, i, te, nu: (te[i], 0, j + f // tn))],
        out_specs=pl.BlockSpec((tm, tn), lambda j, i, te, nu: (i, j)),
        scratch_shapes=[pltpu.VMEM((k, tn), BF16), pltpu.VMEM((k, tn), BF16)],
    )
    return pl.pallas_call(
        _moe_gu_kernel,
        grid_spec=grid_spec,
        out_shape=jax.ShapeDtypeStruct((s, f), BF16),
        compiler_params=_params(("arbitrary", "arbitrary"), nbytes),
        name="moe_gate_up",
    )(te, nu, xs, w_gu, w_gu)


def _moe_down_kernel(te_ref, nu_ref, x_ref, w_ref, gs_ref, o_ref, wb_ref):
    i = pl.program_id(1)

    @pl.when(_new_weights(te_ref))
    def _():
        wb_ref[...] = w_ref[0].astype(BF16)

    @pl.when(i < nu_ref[0])
    def _():
        o_ref[...] = gs_ref[:, 0:1] * jnp.dot(x_ref[...], wb_ref[...], preferred_element_type=F32)

    @pl.when(i >= nu_ref[0])
    def _():
        o_ref[...] = jnp.zeros_like(o_ref)


def _moe_down(act, w_down, gs, te, nu, tm, tn):
    s, k = act.shape
    d = w_down.shape[2]
    nbytes = 2 * k * tn * 4 + k * tn * 2 + 2 * tm * k * 2 + 3 * tm * tn * 4
    grid_spec = pltpu.PrefetchScalarGridSpec(
        num_scalar_prefetch=2,
        grid=(d // tn, s // tm),
        in_specs=[pl.BlockSpec((tm, k), lambda j, i, te, nu: (i, 0)),
                  pl.BlockSpec((1, k, tn), lambda j, i, te, nu: (te[i], 0, j)),
                  pl.BlockSpec((tm, LANES), lambda j, i, te, nu: (i, 0))],
        out_specs=pl.BlockSpec((tm, tn), lambda j, i, te, nu: (i, j)),
        scratch_shapes=[pltpu.VMEM((k, tn), BF16)],
    )
    return pl.pallas_call(
        _moe_down_kernel,
        grid_spec=grid_spec,
        out_shape=jax.ShapeDtypeStruct((s, d), F32),
        compiler_params=_params(("arbitrary", "arbitrary"), nbytes),
        name="moe_down",
    )(te, nu, act, w_down, gs)


def _moe_combine_kernel(bm_ref, h_ref, locc_ref, g_ref, ys_ref, op_ref, os_ref, ybuf_ref, sem, *, tb, prompt_blocks):
    blk = pl.program_id(0)

    @pl.when(blk == 0)
    def _():
        ybuf_ref[...] = jnp.zeros_like(ybuf_ref)

    _run_copies(bm_ref, blk, ybuf_ref, ys_ref, sem, tb=tb, to_hbm=False, wait=False)
    r = lax.broadcasted_iota(I32, (tb, _compact_rows(tb)), 1).astype(F32)
    pick = jnp.where(r == locc_ref[:, 0:1], 1.0, jnp.where(r == locc_ref[:, 1:2], 1.0, 0.0)).astype(BF16)
    _run_copies(bm_ref, blk, ybuf_ref, ys_ref, sem, tb=tb, to_hbm=False, wait=True)
    y = ybuf_ref[...]
    y_hi = y.astype(BF16)
    y_lo = (y - y_hi.astype(F32)).astype(BF16)
    moe = jnp.dot(pick, y_hi, preferred_element_type=F32) + jnp.dot(pick, y_lo, preferred_element_type=F32)
    out = _rms(h_ref[...] + moe, g_ref[...])

    @pl.when(blk < prompt_blocks)
    def _():
        op_ref[...] = out

    @pl.when(blk == prompt_blocks)
    def _():
        os_ref[...] = out


def _moe_combine(h, locc, bmeta, ys, g, tb, n_prompt):
    m, d = h.shape
    prompt_blocks = n_prompt // tb
    any_spec = pl.BlockSpec(memory_space=pl.ANY)
    grid_spec = pltpu.PrefetchScalarGridSpec(
        num_scalar_prefetch=1,
        grid=(m // tb,),
        in_specs=[pl.BlockSpec((tb, d), lambda i, bm: (i, 0)),
                  pl.BlockSpec((tb, N_EXPERTS), lambda i, bm: (i, 0)),
                  pl.BlockSpec((1, d), lambda i, bm: (0, 0)),
                  any_spec],
        out_specs=[pl.BlockSpec((tb, d), lambda i, bm: (jnp.minimum(i, prompt_blocks - 1), 0)),
                   pl.BlockSpec((tb, d), lambda i, bm: (0, 0))],
        scratch_shapes=[pltpu.VMEM((_compact_rows(tb), d), F32), pltpu.SemaphoreType.DMA(())],
    )
    return pl.pallas_call(
        functools.partial(_moe_combine_kernel, tb=tb, prompt_blocks=prompt_blocks),
        grid_spec=grid_spec,
        out_shape=[jax.ShapeDtypeStruct((n_prompt, d), F32), jax.ShapeDtypeStruct((tb, d), F32)],
        compiler_params=_params(("arbitrary",), 12 * tb * d * 4),
        name="moe_combine",
    )(bmeta, h, locc, g.reshape(1, d), ys)


MOE_TOKEN_BLOCK = 256
MOE_ROW_TILE = 512


def _moe(h, a, lg_parts, n_valid, n_prompt, w_gu, w_down, g_final):
    m, d = h.shape
    tb, tm = MOE_TOKEN_BLOCK, MOE_ROW_TILE
    nb = m // tb
    lg3 = lg_parts[:, :2 * N_EXPERTS].reshape(nb, tb, 2 * N_EXPERTS).transpose(0, 2, 1)
    n_tiles = pl.cdiv(2 * n_valid + nb * N_EXPERTS * (SUBLANES - 1), tm) + N_EXPERTS
    loc, gate, bmeta, tmeta = _route(lg3, tm, n_valid)
    te = tmeta[0, :n_tiles]
    nu = tmeta[1, 0:1]
    bm = bmeta[:, :, :3].reshape(-1)
    locc = loc.transpose(0, 2, 1).reshape(m, N_EXPERTS)
    xs, gs = _moe_scatter(a, loc, gate, bm, n_tiles * tm, tb)
    act = _moe_gu(xs, w_gu, te, nu, tm, 896)
    ys = _moe_down(act, w_down, gs, te, nu, tm, 512)
    return _moe_combine(h, locc, bm, ys, g_final, tb, n_prompt)


def _pad_lanes(v):
    return jnp.pad(v.reshape(1, -1), ((0, 0), (0, LANES - v.shape[-1])))


def kernel(x_prompt, x_sample, state_ssm, state_conv, norm_mix, norm_ffn, norm_final, ssd_w_in, ssd_conv_w,
           ssd_conv_b, ssd_dt_bias, ssd_a_log, ssd_d, ssd_gnorm, ssd_w_out, sgu_w_in, sgu_b_in, sgu_ln_g,
           sgu_ln_b, sgu_w_s, sgu_b_s, sgu_w_out, ffn_w_gu, ffn_w_down, moe_w_router, moe_w_gu, moe_w_down):
    batch, seq, d = x_prompt.shape
    n_s = x_sample.shape[0]
    mp = batch * seq
    xp = x_prompt.reshape(mp, d)
    xs = x_sample.reshape(n_s, d)

    w_in = ssd_w_in[0]
    w_dt = jnp.pad(w_in[:, D_INNER + CONV_DIM:], ((0, 0), (0, LANES - SSD_HEADS)))
    dtb = _pad_lanes(ssd_dt_bias[0])
    alog = _pad_lanes(ssd_a_log[0])
    d_rep = jnp.repeat(ssd_d[0], SSD_HEADDIM).reshape(1, D_INNER)
    gn_w = ssd_gnorm[0].reshape(1, D_INNER)
    conv_w = ssd_conv_w[0]
    conv_b = ssd_conv_b[0].reshape(1, CONV_DIM)
    ln_g = sgu_ln_g[0].reshape(1, D_SGU)
    ln_b = sgu_ln_b[0].reshape(1, D_SGU)
    bs_t = jnp.pad(sgu_b_s[0].T, ((0, 0), (0, LANES - SGU_HEADS)))
    w0_rep = jnp.repeat(sgu_w_s[0][:, 0, 0], SGU_HEAD_DIM).reshape(1, D_SGU)
    b0_rep = jnp.repeat(sgu_b_s[0][:, 0], SGU_HEAD_DIM).reshape(1, D_SGU)
    wr_hi = moe_w_router[0].astype(BF16)
    wr_lo = (moe_w_router[0] - wr_hi.astype(F32)).astype(BF16)
    w3 = jnp.concatenate([wr_hi, wr_lo, jnp.zeros((d, LANES - 2 * N_EXPERTS), BF16)], axis=1)

    def in_proj(a, tm):
        z = _mm_plain(a, w_in, 0, D_INNER, tm, 1024, BF16, "ssd_in_z")
        xbc = _mm_plain(a, w_in, D_INNER, CONV_DIM, tm, 1024, F32, "ssd_in_xbc")
        dtr = _mm_plain(a, w_dt, 0, LANES, tm, LANES, F32, "ssd_in_dt")
        return z, xbc, dtr

    def after_mixer0(yg, resid, tm):
        h1, a1 = _mm_resnorm(yg, ssd_w_out[0], resid, norm_ffn[0], tm, "ssd_out")
        act = _mm_swiglu(a1, ffn_w_gu[0], D_FF_DENSE, tm, 1408, "ffn_gate_up")
        return _mm_resnorm(act, ffn_w_down[0], h1, norm_mix[1], tm, "ffn_down")

    a0p = _rmsnorm(xp, norm_mix[0], 512, BF16)
    zp, xbcp, dtrp = in_proj(a0p, 1024)
    ygp, ssm_p, conv_p = _ssd_prompt(xbcp, zp, dtrp, conv_w, conv_b, dtb, alog, d_rep, gn_w, batch, seq)
    h2p, a2p = after_mixer0(ygp, xp, 512)

    a0s = _rmsnorm(xs, norm_mix[0], n_s, BF16)
    zs, xbcs, dtrs = in_proj(a0s, n_s)
    conv_s, xas, xdts, das = _ssd_sample_pre(xbcs, state_conv[0].reshape(n_s, 3 * CONV_DIM), dtrs, conv_w, conv_b,
                                             dtb, alog)
    nblk = D_INNER // LANES
    bms = xas[:, D_INNER:D_INNER + SSD_GROUPS * D_STATE].reshape(n_s, SSD_GROUPS, D_STATE)
    cms = xas[:, D_INNER + SSD_GROUPS * D_STATE:].reshape(n_s, SSD_GROUPS, D_STATE)
    ssm_s, ys3 = _ssd_sample_state(state_ssm[0].reshape(n_s, D_INNER, D_STATE), xdts.reshape(n_s, nblk, LANES),
                                   das.reshape(n_s, nblk, LANES), bms, cms)
    ygs = _ssd_sample_post(ys3.reshape(n_s, D_INNER), xas, zs, d_rep, gn_w)
    h2s, a2s = after_mixer0(ygs, xs, n_s)

    uvp = _mm_bias_gelu(a2p, sgu_w_in[0], sgu_b_in[0], 1024, 1024, BF16, "sgu_in")
    sp = _sgu_prompt(uvp, ln_g, ln_b, sgu_w_s[0], bs_t, batch, seq)
    uvs = _mm_bias_gelu(a2s, sgu_w_in[0], sgu_b_in[0], n_s, 1024, F32, "sgu_in")
    ss, v_s = _sgu_sample(uvs, ln_g, ln_b, w0_rep, b0_rep)
    tm_out = 512
    pad_rows = ((0, tm_out - n_s), (0, 0))
    h3, a3, lg = _mm_resnorm_router(sp, jnp.pad(ss, pad_rows), sgu_w_out[0], h2p, jnp.pad(h2s, pad_rows),
                                    norm_ffn[1], w3, tm_out, "sgu_out")

    y_prompt, y_tail = _moe(h3, a3, lg, mp + n_s, mp, moe_w_gu[0], moe_w_down[0], norm_final)
    y_sample = y_tail[:n_s]

    return (y_prompt.reshape(batch, seq, d),
            y_sample.reshape(n_s, 1, d),
            ssm_p[None],
            conv_p[None],
            ssm_s.reshape(1, n_s, SSD_HEADS, SSD_HEADDIM, D_STATE),
            conv_s.reshape(1, n_s, CONV_W - 1, CONV_DIM),
            v_s.reshape(1, n_s, 1, D_SGU))
```

```python
import functools

import jax
import jax.numpy as jnp
from jax import lax
from jax.experimental import pallas as pl
from jax.experimental.pallas import tpu as pltpu

F32 = jnp.float32
BF16 = jnp.bfloat16
I32 = jnp.int32
HIGHEST = lax.Precision.HIGHEST

D_MODEL = 1024
D_INNER = 2048
SSD_HEADS = 32
SSD_HEADDIM = 64
SSD_GROUPS = 4
D_STATE = 128
CONV_W = 4
CONV_DIM = D_INNER + 2 * SSD_GROUPS * D_STATE
CHUNK = 128
D_SGU = 2048
SGU_HEADS = 8
SGU_HEAD_DIM = D_SGU // SGU_HEADS
D_FF_DENSE = 2816
N_EXPERTS = 8
D_FF_EXPERT = 3584
EPS = 1e-5

MXU_COLS = 256
LANES = 128
SUBLANES = 8
VMEM_CAP = 60000 * 1024
VMEM_FLOOR = 32 * 1024 * 1024


def _vmem_limit(nbytes):
    return int(min(max(nbytes * 5 // 4 + (4 << 20), VMEM_FLOOR), VMEM_CAP))


def _params(sem, nbytes):
    return pltpu.CompilerParams(dimension_semantics=sem, vmem_limit_bytes=_vmem_limit(nbytes))


def _silu(x):
    return x / (1.0 + jnp.exp(-x))


def _rms(x, g):
    return x * lax.rsqrt(jnp.mean(x * x, axis=-1, keepdims=True) + EPS) * g


def _bdot(a, b):
    return jnp.dot(a.astype(BF16), b.astype(BF16), preferred_element_type=F32)


def _rmsnorm_kernel(x_ref, g_ref, o_ref):
    o_ref[...] = _rms(x_ref[...], g_ref[...]).astype(o_ref.dtype)


def _rmsnorm(x, g, tm, out_dtype):
    m, d = x.shape
    return pl.pallas_call(
        _rmsnorm_kernel,
        grid=(m // tm,),
        in_specs=[pl.BlockSpec((tm, d), lambda i: (i, 0)), pl.BlockSpec((1, d), lambda i: (0, 0))],
        out_specs=pl.BlockSpec((tm, d), lambda i: (i, 0)),
        out_shape=jax.ShapeDtypeStruct((m, d), out_dtype),
        compiler_params=_params(("arbitrary",), 4 * tm * d * 4),
        name="rmsnorm",
    )(x, g.reshape(1, d))


def _mm_plain_kernel(x_ref, w_ref, o_ref, wb_ref):
    @pl.when(pl.program_id(1) == 0)
    def _():
        wb_ref[...] = w_ref[...].astype(BF16)

    o_ref[...] = _bdot(x_ref[...], wb_ref[...]).astype(o_ref.dtype)


def _mm_bias_gelu_kernel(x_ref, w_ref, b_ref, o_ref, wb_ref):
    @pl.when(pl.program_id(1) == 0)
    def _():
        wb_ref[...] = w_ref[...].astype(BF16)

    y = _bdot(x_ref[...], wb_ref[...]) + b_ref[...]
    o_ref[...] = (0.5 * y * (1.0 + lax.erf(y * (2.0 ** -0.5)))).astype(o_ref.dtype)


def _mm_swiglu_kernel(x_ref, wg_ref, wu_ref, o_ref, wgb_ref, wub_ref):
    @pl.when(pl.program_id(1) == 0)
    def _():
        wgb_ref[...] = wg_ref[...].astype(BF16)
        wub_ref[...] = wu_ref[...].astype(BF16)

    x = x_ref[...].astype(BF16)
    g = jnp.dot(x, wgb_ref[...], preferred_element_type=F32)
    u = jnp.dot(x, wub_ref[...], preferred_element_type=F32)
    o_ref[...] = (_silu(g) * u).astype(o_ref.dtype)


def _wspec(k, tn, col_block0):
    return pl.BlockSpec((k, tn), lambda j, i: (0, j + col_block0), pipeline_mode=pl.Buffered(1))


def _mm_plain(x, w, col0, n, tm, tn, out_dtype, name):
    m, k = x.shape
    nbytes = k * tn * 6 + 2 * tm * k * x.dtype.itemsize + 2 * tm * tn * 4
    return pl.pallas_call(
        _mm_plain_kernel,
        grid=(n // tn, m // tm),
        in_specs=[pl.BlockSpec((tm, k), lambda j, i: (i, 0)), _wspec(k, tn, col0 // tn)],
        out_specs=pl.BlockSpec((tm, tn), lambda j, i: (i, j)),
        out_shape=jax.ShapeDtypeStruct((m, n), out_dtype),
        scratch_shapes=[pltpu.VMEM((k, tn), BF16)],
        compiler_params=_params(("arbitrary", "arbitrary"), nbytes),
        name=name,
    )(x, w)


def _mm_bias_gelu(x, w, b, tm, tn, out_dtype, name):
    m, k = x.shape
    n = w.shape[1]
    nbytes = k * tn * 6 + 2 * tm * k * x.dtype.itemsize + 2 * tm * tn * 4
    return pl.pallas_call(
        _mm_bias_gelu_kernel,
        grid=(n // tn, m // tm),
        in_specs=[pl.BlockSpec((tm, k), lambda j, i: (i, 0)), _wspec(k, tn, 0),
                  pl.BlockSpec((1, tn), lambda j, i: (0, j))],
        out_specs=pl.BlockSpec((tm, tn), lambda j, i: (i, j)),
        out_shape=jax.ShapeDtypeStruct((m, n), out_dtype),
        scratch_shapes=[pltpu.VMEM((k, tn), BF16)],
        compiler_params=_params(("arbitrary", "arbitrary"), nbytes),
        name=name,
    )(x, w, b.reshape(1, n))


def _mm_swiglu(x, w_gu, d_ff, tm, tn, name):
    m, k = x.shape
    nbytes = 2 * k * tn * 6 + 2 * tm * k * x.dtype.itemsize + 2 * tm * tn * 2 + 3 * tm * tn * 4
    return pl.pallas_call(
        _mm_swiglu_kernel,
        grid=(d_ff // tn, m // tm),
        in_specs=[pl.BlockSpec((tm, k), lambda j, i: (i, 0)), _wspec(k, tn, 0), _wspec(k, tn, d_ff // tn)],
        out_specs=pl.BlockSpec((tm, tn), lambda j, i: (i, j)),
        out_shape=jax.ShapeDtypeStruct((m, d_ff), BF16),
        scratch_shapes=[pltpu.VMEM((k, tn), BF16), pltpu.VMEM((k, tn), BF16)],
        compiler_params=_params(("arbitrary", "arbitrary"), nbytes),
        name=name,
    )(x, w_gu, w_gu)


def _mm_resnorm_kernel(x_ref, w_ref, r_ref, g_ref, h_ref, a_ref, wb_ref):
    @pl.when(pl.program_id(0) == 0)
    def _():
        wb_ref[...] = w_ref[...].astype(BF16)

    h = r_ref[...] + _bdot(x_ref[...], wb_ref[...])
    h_ref[...] = h
    a_ref[...] = _rms(h, g_ref[...]).astype(a_ref.dtype)


def _mm_resnorm(x, w, resid, g, tm, name):
    m, k = x.shape
    d = w.shape[1]
    nbytes = k * d * 6 + 2 * tm * k * 2 + 8 * tm * d * 4
    return pl.pallas_call(
        _mm_resnorm_kernel,
        grid=(m // tm,),
        in_specs=[pl.BlockSpec((tm, k), lambda i: (i, 0)),
                  pl.BlockSpec((k, d), lambda i: (0, 0), pipeline_mode=pl.Buffered(1)),
                  pl.BlockSpec((tm, d), lambda i: (i, 0)),
                  pl.BlockSpec((1, d), lambda i: (0, 0))],
        out_specs=[pl.BlockSpec((tm, d), lambda i: (i, 0)), pl.BlockSpec((tm, d), lambda i: (i, 0))],
        out_shape=[jax.ShapeDtypeStruct((m, d), F32), jax.ShapeDtypeStruct((m, d), BF16)],
        scratch_shapes=[pltpu.VMEM((k, d), BF16)],
        compiler_params=_params(("arbitrary",), nbytes),
        name=name,
    )(x, w, resid, g.reshape(1, d))


def _mm_resnorm_router_kernel(xp_ref, xt_ref, w_ref, rp_ref, rt_ref, g_ref, w3_ref, h_ref, a_ref, lg_ref, wb_ref,
                              *, prompt_tiles):
    i = pl.program_id(0)

    @pl.when(i == 0)
    def _():
        wb_ref[...] = w_ref[...].astype(BF16)

    def rows(x_ref, r_ref):
        h = r_ref[...] + _bdot(x_ref[...], wb_ref[...])
        h_ref[...] = h
        a = _rms(h, g_ref[...])
        a_hi = a.astype(BF16)
        a_lo = (a - a_hi.astype(F32)).astype(BF16)
        a_ref[...] = a_hi
        lg_ref[...] = (jnp.dot(a_hi, w3_ref[...], preferred_element_type=F32)
                       + jnp.dot(a_lo, w3_ref[...], preferred_element_type=F32))

    @pl.when(i < prompt_tiles)
    def _():
        rows(xp_ref, rp_ref)

    @pl.when(i == prompt_tiles)
    def _():
        rows(xt_ref, rt_ref)


def _mm_resnorm_router(xp, xt, w, rp, rt, g, w3, tm, name):
    mp, k = xp.shape
    d = w.shape[1]
    prompt_tiles = mp // tm
    last = prompt_tiles - 1
    rows = mp + tm
    nbytes = k * d * 6 + 4 * tm * k * 2 + 12 * tm * d * 4
    return pl.pallas_call(
        functools.partial(_mm_resnorm_router_kernel, prompt_tiles=prompt_tiles),
        grid=(prompt_tiles + 1,),
        in_specs=[pl.BlockSpec((tm, k), lambda i: (jnp.minimum(i, last), 0)),
                  pl.BlockSpec((tm, k), lambda i: (0, 0)),
                  pl.BlockSpec((k, d), lambda i: (0, 0), pipeline_mode=pl.Buffered(1)),
                  pl.BlockSpec((tm, d), lambda i: (jnp.minimum(i, last), 0)),
                  pl.BlockSpec((tm, d), lambda i: (0, 0)),
                  pl.BlockSpec((1, d), lambda i: (0, 0)),
                  pl.BlockSpec((d, LANES), lambda i: (0, 0))],
        out_specs=[pl.BlockSpec((tm, d), lambda i: (i, 0)), pl.BlockSpec((tm, d), lambda i: (i, 0)),
                   pl.BlockSpec((tm, LANES), lambda i: (i, 0))],
        out_shape=[jax.ShapeDtypeStruct((rows, d), F32), jax.ShapeDtypeStruct((rows, d), BF16),
                   jax.ShapeDtypeStruct((rows, LANES), F32)],
        scratch_shapes=[pltpu.VMEM((k, d), BF16)],
        compiler_params=_params(("arbitrary",), nbytes),
        name=name,
    )(xp, xt, w, rp, rt, g.reshape(1, d), w3)


def _gate_and_groupnorm(y, xs, z, d_rep, gn_w):
    yg = (y + d_rep * xs) * _silu(z)
    gw = D_INNER // SSD_GROUPS
    outs = []
    for g in range(SSD_GROUPS):
        blk = yg[:, g * gw:(g + 1) * gw]
        outs.append(blk * lax.rsqrt(jnp.mean(blk * blk, axis=-1, keepdims=True) + EPS))
    return jnp.concatenate(outs, axis=1) * gn_w


def _softplus(x):
    return jnp.maximum(x, 0.0) + jnp.log(1.0 + jnp.exp(-jnp.abs(x)))


def _ssd_prompt_kernel(xbc_ref, z_ref, dtr_ref, cw_ref, cb_ref, dtb_ref, alog_ref, drep_ref, gnw_ref,
                       yg_ref, ssm_ref, conv_ref, ext_ref, st_ref, y_ref, xa_ref):
    c = pl.program_id(1)
    nc = pl.num_programs(1)
    q = CHUNK

    @pl.when(c == 0)
    def _():
        ext_ref[0:8, :] = jnp.zeros((8, CONV_DIM), F32)
        st_ref[...] = jnp.zeros_like(st_ref)

    ext_ref[8:8 + q, :] = xbc_ref[...]
    acc = cb_ref[...] + cw_ref[3:4, :] * ext_ref[8:8 + q, :]
    for k in range(CONV_W - 1):
        acc = acc + cw_ref[k:k + 1, :] * ext_ref[5 + k:5 + k + q, :]
    xa_ref[...] = _silu(acc)
    tail = ext_ref[q:q + 8, :]
    ext_ref[0:8, :] = tail

    dt = _softplus(dtr_ref[...] + dtb_ref[...])
    a_neg = -jnp.exp(alog_ref[...])
    dta = dt * a_neg
    row_i = lax.broadcasted_iota(I32, (q, q), 0)
    col_i = lax.broadcasted_iota(I32, (q, q), 1)
    causal = row_i >= col_i
    tril = jnp.where(causal, 1.0, 0.0)
    cs = jnp.dot(tril, dta, precision=HIGHEST, preferred_element_type=F32)
    cs_t = cs.T
    dt_t = dt.T
    w_t = dt_t * jnp.exp(cs_t[:, q - 1:q] - cs_t)
    lane = lax.broadcasted_iota(I32, (q, 2 * SSD_HEADDIM), 1)
    first = lane < SSD_HEADDIM

    for g in range(SSD_GROUPS):
        b_g = xa_ref[:, D_INNER + g * D_STATE:D_INNER + (g + 1) * D_STATE]
        c_g = xa_ref[:, D_INNER + (SSD_GROUPS + g) * D_STATE:D_INNER + (SSD_GROUPS + g + 1) * D_STATE]
        b_gt = b_g.T
        cb = _bdot(c_g, b_gt)
        for pr in range(4):
            hp = g * 4 + pr
            cols = slice(hp * 2 * SSD_HEADDIM, (hp + 1) * 2 * SSD_HEADDIM)
            x_p = xa_ref[:, cols].astype(BF16)
            st_p = st_ref[:, cols]
            rhs = jnp.concatenate([x_p, st_p.astype(BF16)], axis=0)
            ys, ss, cds = [], [], []
            for s in range(2):
                h = 2 * hp + s
                colb = jnp.broadcast_to(cs[:, h:h + 1], (q, q))
                rowb = jnp.broadcast_to(cs_t[h:h + 1, :], (q, q))
                ecol = jnp.exp(colb)
                lmat = jnp.where(causal, jnp.exp(jnp.minimum(colb - rowb, 0.0)), 0.0)
                m_h = cb * lmat * dt_t[h:h + 1, :]
                lhs = jnp.concatenate([m_h.astype(BF16), (c_g * ecol).astype(BF16)], axis=1)
                ys.append(jnp.dot(lhs, rhs, preferred_element_type=F32))
                bw = (b_gt * w_t[h:h + 1, :]).astype(BF16)
                ss.append(jnp.dot(bw, x_p, preferred_element_type=F32))
                cds.append(ecol[q - 1:q, :])
            y_ref[:, cols] = jnp.where(first, ys[0], ys[1])
            cd = jnp.where(first[0:1, :], cds[0], cds[1])
            st_ref[:, cols] = st_p * cd + jnp.where(first, ss[0], ss[1])

    out = _gate_and_groupnorm(y_ref[...], xa_ref[:, :D_INNER], z_ref[...].astype(F32), drep_ref[...],
                              gnw_ref[...])
    yg_ref[...] = out.astype(yg_ref.dtype)

    @pl.when(c == nc - 1)
    def _():
        conv_ref[0] = tail[8 - (CONV_W - 1):, :]
        for hp in range(SSD_HEADS // 2):
            blk = st_ref[:, hp * 2 * SSD_HEADDIM:(hp + 1) * 2 * SSD_HEADDIM].T
            ssm_ref[0, 2 * hp:2 * hp + 2] = blk.reshape(2, SSD_HEADDIM, D_STATE)


def _ssd_prompt(xbc, z, dtr, conv_w, conv_b, dtb, alog, d_rep, gn_w, batch, seq):
    nc = seq // CHUNK
    row = lambda b, c: (b * nc + c, 0)
    fixed = lambda b, c: (0, 0)
    nbytes = 2 * CHUNK * (CONV_DIM * 4 + D_INNER * 4 + LANES * 4) + (CHUNK + 8) * CONV_DIM * 4 \
        + 4 * D_STATE * D_INNER * 4 + 16 * CHUNK * CONV_DIM * 4
    return pl.pallas_call(
        _ssd_prompt_kernel,
        grid=(batch, nc),
        in_specs=[pl.BlockSpec((CHUNK, CONV_DIM), row), pl.BlockSpec((CHUNK, D_INNER), row),
                  pl.BlockSpec((CHUNK, LANES), row),
                  pl.BlockSpec((CONV_W, CONV_DIM), fixed), pl.BlockSpec((1, CONV_DIM), fixed),
                  pl.BlockSpec((1, LANES), fixed), pl.BlockSpec((1, LANES), fixed),
                  pl.BlockSpec((1, D_INNER), fixed), pl.BlockSpec((1, D_INNER), fixed)],
        out_specs=[pl.BlockSpec((CHUNK, D_INNER), row),
                   pl.BlockSpec((1, SSD_HEADS, SSD_HEADDIM, D_STATE), lambda b, c: (b, 0, 0, 0)),
                   pl.BlockSpec((1, CONV_W - 1, CONV_DIM), lambda b, c: (b, 0, 0))],
        out_shape=[jax.ShapeDtypeStruct((batch * seq, D_INNER), BF16),
                   jax.ShapeDtypeStruct((batch, SSD_HEADS, SSD_HEADDIM, D_STATE), F32),
                   jax.ShapeDtypeStruct((batch, CONV_W - 1, CONV_DIM), F32)],
        scratch_shapes=[pltpu.VMEM((CHUNK + 8, CONV_DIM), F32), pltpu.VMEM((D_STATE, D_INNER), F32),
                        pltpu.VMEM((CHUNK, D_INNER), F32), pltpu.VMEM((CHUNK, CONV_DIM), F32)],
        compiler_params=_params(("arbitrary", "arbitrary"), nbytes),
        name="ssd_prompt",
    )(xbc, z, dtr, conv_w, conv_b, dtb, alog, d_rep, gn_w)


def _ssd_sample_pre_kernel(xbc_ref, conv_ref, dtr_ref, cw_ref, cb_ref, dtb_ref, alog_ref,
                           nconv_ref, xa_ref, xdt_ref, da_ref, bst_ref, cst_ref):
    xn = xbc_ref[...]
    acc = cb_ref[...] + cw_ref[3:4, :] * xn
    for k in range(CONV_W - 1):
        acc = acc + cw_ref[k:k + 1, :] * conv_ref[:, k * CONV_DIM:(k + 1) * CONV_DIM]
    xa = _silu(acc)
    xa_ref[...] = xa
    nconv_ref[:, 0:CONV_DIM] = conv_ref[:, CONV_DIM:2 * CONV_DIM]
    nconv_ref[:, CONV_DIM:2 * CONV_DIM] = conv_ref[:, 2 * CONV_DIM:3 * CONV_DIM]
    nconv_ref[:, 2 * CONV_DIM:3 * CONV_DIM] = xn

    dt = _softplus(dtr_ref[...] + dtb_ref[...])
    a_neg = -jnp.exp(alog_ref[...])
    hh = lax.broadcasted_iota(I32, (LANES, D_INNER), 0)
    cc = lax.broadcasted_iota(I32, (LANES, D_INNER), 1)
    expand = jnp.where(lax.shift_right_logical(cc, 6) == hh, 1.0, 0.0)
    dt_rep = jnp.dot(dt, expand, precision=HIGHEST, preferred_element_type=F32)
    xdt_ref[...] = xa[:, :D_INNER] * dt_rep
    da_ref[...] = jnp.exp(dt * a_neg)

    n = xn.shape[0]
    gs = SSD_GROUPS * D_STATE
    b1, b2, b3 = _split3(xa[:, D_INNER:D_INNER + gs])
    cm = xa[:, D_INNER + gs:]
    c1 = cm.astype(BF16)
    c2 = (cm - c1.astype(F32)).astype(BF16)
    zero = jnp.zeros((n, D_STATE), BF16)
    b_rows, c_rows = [], []
    for g in range(SSD_GROUPS):
        sl = slice(g * D_STATE, (g + 1) * D_STATE)
        b_rows += [b1[:, sl], b2[:, sl], b1[:, sl], b3[:, sl], b2[:, sl], b1[:, sl]] + [zero] * (TERM_ROWS - 6)
        c_rows += [c1[:, sl], c2[:, sl]] + [zero] * (TERM_ROWS - 2)
    bst_ref[...] = jnp.concatenate(b_rows, axis=1)
    cst_ref[...] = jnp.concatenate(c_rows, axis=1)


TERM_ROWS = 16


def _split3(v):
    v1 = v.astype(BF16)
    r1 = v - v1.astype(F32)
    v2 = r1.astype(BF16)
    v3 = (r1 - v2.astype(F32)).astype(BF16)
    return v1, v2, v3


def _ssd_sample_pre(xbc, conv_flat, dtr, conv_w, conv_b, dtb, alog):
    n = xbc.shape[0]
    full = lambda shape: pl.BlockSpec(shape, lambda i: (0,) * len(shape))
    stack = SSD_GROUPS * TERM_ROWS * D_STATE
    return pl.pallas_call(
        _ssd_sample_pre_kernel,
        grid=(1,),
        in_specs=[full((n, CONV_DIM)), full((n, 3 * CONV_DIM)), full((n, LANES)), full((CONV_W, CONV_DIM)),
                  full((1, CONV_DIM)), full((1, LANES)), full((1, LANES))],
        out_specs=[full((n, 3 * CONV_DIM)), full((n, CONV_DIM)), full((n, D_INNER)), full((n, LANES)),
                   full((n, stack)), full((n, stack))],
        out_shape=[jax.ShapeDtypeStruct((n, 3 * CONV_DIM), F32), jax.ShapeDtypeStruct((n, CONV_DIM), F32),
                   jax.ShapeDtypeStruct((n, D_INNER), F32), jax.ShapeDtypeStruct((n, LANES), F32),
                   jax.ShapeDtypeStruct((n, stack), BF16), jax.ShapeDtypeStruct((n, stack), BF16)],
        compiler_params=_params(("arbitrary",), 32 * n * CONV_DIM * 4),
        name="ssd_sample_pre",
    )(xbc, conv_flat, dtr, conv_w, conv_b, dtb, alog)


STATE_SAMPLES = 4


def _ssd_sample_state_kernel(da_ref, st_ref, xdt_ref, bst_ref, cst_ref, nst_ref, y_ref):
    first = pl.program_id(0) * STATE_SAMPLES
    rows_g = D_INNER // SSD_GROUPS
    sub = lax.broadcasted_iota(I32, (TERM_ROWS, D_INNER), 0)
    for s in range(STATE_SAMPLES):
        x1, x2, x3 = [t.astype(F32) for t in _split3(xdt_ref[s])]
        terms = jnp.where(sub == 0, x1, jnp.where(sub == 1, x1, jnp.where(sub == 2, x2, jnp.where(
            sub == 3, x1, jnp.where(sub == 4, x2, jnp.where(sub == 5, x3, 0.0)))))).astype(BF16)
        ys = []
        for g in range(SSD_GROUPS):
            tile = slice(g * TERM_ROWS, (g + 1) * TERM_ROWS)
            upd = lax.dot_general(terms[:, g * rows_g:(g + 1) * rows_g], bst_ref[s, tile, :],
                                  (((0,), (0,)), ((), ())), preferred_element_type=F32)
            halves = []
            for hh in range(rows_g // SSD_HEADDIM):
                head = g * (rows_g // SSD_HEADDIM) + hh
                rows = slice(head * SSD_HEADDIM, (head + 1) * SSD_HEADDIM)
                h_new = (da_ref[first + s, head] * st_ref[s, rows, :]
                         + upd[hh * SSD_HEADDIM:(hh + 1) * SSD_HEADDIM, :])
                nst_ref[s, rows, :] = h_new
                halves.append(h_new)
            h_g = jnp.concatenate(halves, axis=0)
            h_hi = h_g.astype(BF16)
            h_lo = (h_g - h_hi.astype(F32)).astype(BF16)
            nt = (((1,), (1,)), ((), ()))
            o_hi = lax.dot_general(cst_ref[s, tile, :], h_hi, nt, preferred_element_type=F32)
            o_lo = lax.dot_general(cst_ref[s, tile, :], h_lo, nt, preferred_element_type=F32)
            ys.append(o_hi[0:1, :] + o_hi[1:2, :] + o_lo[0:1, :])
        y_ref[s] = jnp.concatenate(ys, axis=1)


def _ssd_sample_state(state, xdt, da, bst, cst):
    n = state.shape[0]
    ns = STATE_SAMPLES
    stack_rows = SSD_GROUPS * TERM_ROWS
    return pl.pallas_call(
        _ssd_sample_state_kernel,
        grid=(n // ns,),
        in_specs=[pl.BlockSpec(memory_space=pltpu.SMEM),
                  pl.BlockSpec((ns, D_INNER, D_STATE), lambda i: (i, 0, 0)),
                  pl.BlockSpec((ns, 1, D_INNER), lambda i: (i, 0, 0)),
                  pl.BlockSpec((ns, stack_rows, D_STATE), lambda i: (i, 0, 0)),
                  pl.BlockSpec((ns, stack_rows, D_STATE), lambda i: (i, 0, 0))],
        out_specs=[pl.BlockSpec((ns, D_INNER, D_STATE), lambda i: (i, 0, 0)),
                   pl.BlockSpec((ns, 1, D_INNER), lambda i: (i, 0, 0))],
        out_shape=[jax.ShapeDtypeStruct((n, D_INNER, D_STATE), F32),
                   jax.ShapeDtypeStruct((n, 1, D_INNER), F32)],
        compiler_params=_params(("arbitrary",), 6 * ns * D_INNER * D_STATE * 4),
        name="ssd_sample_state",
    )(da, state, xdt, bst, cst)


def _ssd_sample_post_kernel(y_ref, xa_ref, z_ref, drep_ref, gnw_ref, o_ref):
    out = _gate_and_groupnorm(y_ref[...], xa_ref[:, :D_INNER], z_ref[...].astype(F32), drep_ref[...], gnw_ref[...])
    o_ref[...] = out.astype(o_ref.dtype)


def _ssd_sample_post(y, xa, z, d_rep, gn_w):
    n = y.shape[0]
    full = lambda shape: pl.BlockSpec(shape, lambda i: (0,) * len(shape))
    return pl.pallas_call(
        _ssd_sample_post_kernel,
        grid=(1,),
        in_specs=[full((n, D_INNER)), full((n, CONV_DIM)), full((n, D_INNER)), full((1, D_INNER)), full((1, D_INNER))],
        out_specs=full((n, D_INNER)),
        out_shape=jax.ShapeDtypeStruct((n, D_INNER), BF16),
        compiler_params=_params(("arbitrary",), 12 * n * CONV_DIM * 4),
        name="ssd_sample_post",
    )(y, xa, z, d_rep, gn_w)


def _layernorm(v, g, b):
    mu = jnp.mean(v, axis=-1, keepdims=True)
    d = v - mu
    var = jnp.mean(d * d, axis=-1, keepdims=True)
    return d * lax.rsqrt(var + EPS) * g + b


def _sgu_prompt_kernel(u_ref, v_ref, lng_ref, lnb_ref, ws_ref, bst_ref, o_ref):
    q = CHUNK
    vn = _layernorm(v_ref[...].astype(F32), lng_ref[...], lnb_ref[...]).astype(BF16)
    causal = lax.broadcasted_iota(I32, (q, q), 0) >= lax.broadcasted_iota(I32, (q, q), 1)
    for g in range(SGU_HEADS):
        cols = slice(g * SGU_HEAD_DIM, (g + 1) * SGU_HEAD_DIM)
        w = jnp.where(causal, ws_ref[g], 0.0).astype(BF16)
        s = jnp.dot(w, vn[:, cols], preferred_element_type=F32) + bst_ref[:, g:g + 1]
        o_ref[:, cols] = (u_ref[:, cols].astype(F32) * s).astype(o_ref.dtype)


def _sgu_prompt(uv, ln_g, ln_b, w_s, bs_t, batch, seq):
    nc = seq // CHUNK
    m = batch * seq
    fixed2 = lambda i: (0, 0)
    return pl.pallas_call(
        _sgu_prompt_kernel,
        grid=(m // CHUNK,),
        in_specs=[pl.BlockSpec((CHUNK, D_SGU), lambda i: (i, 0)), pl.BlockSpec((CHUNK, D_SGU), lambda i: (i, 1)),
                  pl.BlockSpec((1, D_SGU), fixed2), pl.BlockSpec((1, D_SGU), fixed2),
                  pl.BlockSpec((SGU_HEADS, CHUNK, CHUNK), lambda i: (0, 0, 0)),
                  pl.BlockSpec((CHUNK, LANES), fixed2)],
        out_specs=pl.BlockSpec((CHUNK, D_SGU), lambda i: (i, 0)),
        out_shape=jax.ShapeDtypeStruct((m, D_SGU), BF16),
        compiler_params=_params(("arbitrary",), 16 * CHUNK * D_SGU * 4),
        name="sgu_prompt",
    )(uv, uv, ln_g, ln_b, w_s, bs_t)


def _sgu_sample_kernel(u_ref, v_ref, lng_ref, lnb_ref, w0_ref, b0_ref, o_ref, vn_ref):
    vn = _layernorm(v_ref[...], lng_ref[...], lnb_ref[...])
    vn_ref[...] = vn
    o_ref[...] = (u_ref[...] * (w0_ref[...] * vn + b0_ref[...])).astype(o_ref.dtype)


def _sgu_sample(uv, ln_g, ln_b, w0_rep, b0_rep):
    n = uv.shape[0]
    fixed2 = lambda i: (0, 0)
    return pl.pallas_call(
        _sgu_sample_kernel,
        grid=(1,),
        in_specs=[pl.BlockSpec((n, D_SGU), lambda i: (0, 0)), pl.BlockSpec((n, D_SGU), lambda i: (0, 1)),
                  pl.BlockSpec((1, D_SGU), fixed2), pl.BlockSpec((1, D_SGU), fixed2),
                  pl.BlockSpec((1, D_SGU), fixed2), pl.BlockSpec((1, D_SGU), fixed2)],
        out_specs=[pl.BlockSpec((n, D_SGU), fixed2), pl.BlockSpec((n, D_SGU), fixed2)],
        out_shape=[jax.ShapeDtypeStruct((n, D_SGU), BF16), jax.ShapeDtypeStruct((n, D_SGU), F32)],
        compiler_params=_params(("arbitrary",), 12 * n * D_SGU * 4),
        name="sgu_sample",
    )(uv, uv, ln_g, ln_b, w0_rep, b0_rep)


def _experts_to_lanes(col, sub, lane):
    return jnp.sum(jnp.where(sub == lane, col, 0.0), axis=0, keepdims=True)


def _route_kernel(lg_ref, loc_ref, gate_ref, bmeta_ref, tmeta_ref, *, tile, n_valid):
    nb, _, tb = lg_ref.shape
    sub = lax.broadcasted_iota(I32, (N_EXPERTS, LANES), 0)
    lane = lax.broadcasted_iota(I32, (N_EXPERTS, LANES), 1)
    subf = lax.broadcasted_iota(I32, (N_EXPERTS, tb), 0).astype(F32)
    tok = lax.broadcasted_iota(I32, (N_EXPERTS, tb), 1)
    incl = jnp.where(lax.broadcasted_iota(I32, (tb, tb), 0) <= lax.broadcasted_iota(I32, (tb, tb), 1), 1.0, 0.0)
    neg = jnp.float32(-jnp.inf)
    none = jnp.float32(N_EXPERTS)

    def select(k):
        blk = lg_ref[k]
        l = blk[0:N_EXPERTS] + blk[N_EXPERTS:2 * N_EXPERTS]
        m1 = jnp.max(l, axis=0, keepdims=True)
        i1 = jnp.min(jnp.where(l == m1, subf, none), axis=0, keepdims=True)
        l2 = jnp.where(subf == i1, neg, l)
        m2 = jnp.max(l2, axis=0, keepdims=True)
        i2 = jnp.min(jnp.where(l2 == m2, subf, none), axis=0, keepdims=True)
        valid = (k * tb + tok) < n_valid
        sel = jnp.where(valid, jnp.where(subf == i1, 1.0, jnp.where(subf == i2, 1.0, 0.0)), 0.0)
        return m1, i1, m2, i2, valid, sel

    def run_rows(cnt):
        return jnp.ceil(cnt / SUBLANES) * SUBLANES

    def count_body(k, carry):
        return carry + run_rows(jnp.sum(select(k)[5], axis=1, keepdims=True))

    counts = lax.fori_loop(0, nb, count_body, jnp.zeros((N_EXPERTS, 1), F32))
    tiles = jnp.ceil(counts / tile)
    cum_incl = jnp.sum(jnp.where(lane <= sub, _experts_to_lanes(tiles, sub, lane), 0.0), axis=1, keepdims=True)
    offset = (cum_incl - tiles) * tile
    tile_expert = jnp.sum(jnp.where(lane.astype(F32) >= cum_incl, 1.0, 0.0), axis=0, keepdims=True)
    tile_expert = jnp.minimum(tile_expert, N_EXPERTS - 1.0)
    n_used = jnp.max(cum_incl, axis=0, keepdims=True)
    tmeta_ref[...] = jnp.where(sub == 0, tile_expert, jnp.where(sub == 1, n_used, 0.0)).astype(I32)

    def place_body(k, before):
        m1, i1, m2, i2, valid, sel = select(k)
        run = jnp.dot(sel, incl, preferred_element_type=F32)
        cnt = run_rows(run[:, tb - 1:tb])
        lstart = jnp.sum(jnp.where(lane < sub, _experts_to_lanes(cnt, sub, lane), 0.0), axis=1, keepdims=True)
        local = lstart + run - sel
        loc1 = jnp.sum(jnp.where(subf == i1, local, 0.0), axis=0, keepdims=True)
        loc2 = jnp.sum(jnp.where(subf == i2, local, 0.0), axis=0, keepdims=True)
        live = (k * tb + tok[0:1, :]) < n_valid
        loc_ref[k] = jnp.where(subf == 0.0, jnp.where(live, loc1, -1.0),
                               jnp.where(subf == 1.0, jnp.where(live, loc2, -1.0), 0.0))
        e = jnp.exp(m2 - m1)
        gate_ref[k] = jnp.where(subf == 0.0, 1.0 / (1.0 + e), jnp.where(subf == 1.0, e / (1.0 + e), 0.0))
        bmeta_ref[k] = jnp.where(lane == 0, offset + before,
                                 jnp.where(lane == 1, cnt, jnp.where(lane == 2, lstart, 0.0))).astype(I32)
        return before + cnt

    lax.fori_loop(0, nb, place_body, jnp.zeros((N_EXPERTS, 1), F32))


def _route(logits3, tile, n_valid):
    nb, _, tb = logits3.shape
    tok_blk = pl.BlockSpec((nb, N_EXPERTS, tb), lambda i: (0, 0, 0))
    return pl.pallas_call(
        functools.partial(_route_kernel, tile=tile, n_valid=n_valid),
        grid=(1,),
        in_specs=[pl.BlockSpec((nb, 2 * N_EXPERTS, tb), lambda i: (0, 0, 0))],
        out_specs=[tok_blk, tok_blk, pl.BlockSpec((nb, N_EXPERTS, LANES), lambda i: (0, 0, 0)),
                   pl.BlockSpec((N_EXPERTS, LANES), lambda i: (0, 0))],
        out_shape=[jax.ShapeDtypeStruct((nb, N_EXPERTS, tb), F32),
                   jax.ShapeDtypeStruct((nb, N_EXPERTS, tb), F32),
                   jax.ShapeDtypeStruct((nb, N_EXPERTS, LANES), I32),
                   jax.ShapeDtypeStruct((N_EXPERTS, LANES), I32)],
        compiler_params=_params(("arbitrary",), 16 * nb * N_EXPERTS * tb * 4),
        name="moe_route",
    )(logits3)


def _run_copies(bm_ref, blk, vmem_ref, hbm_ref, sem, *, tb, to_hbm, wait):
    for e in range(N_EXPERTS):
        base = (blk * N_EXPERTS + e) * 3
        start, cnt, lstart = bm_ref[base], bm_ref[base + 1], bm_ref[base + 2]
        off = 0
        size = tb
        while size >= SUBLANES:
            @pl.when((cnt & size) != 0)
            def _(size=size, off=off):
                v = vmem_ref.at[pl.ds(pl.multiple_of(lstart + off, SUBLANES), size)]
                h = hbm_ref.at[pl.ds(pl.multiple_of(start + off, SUBLANES), size)]
                cp = pltpu.make_async_copy(v, h, sem) if to_hbm else pltpu.make_async_copy(h, v, sem)
                if wait:
                    cp.wait()
                else:
                    cp.start()

            off = off + (cnt & size)
            size //= 2


def _compact_rows(tb):
    return 2 * tb + N_EXPERTS * SUBLANES


def _selection(loc_ref, rows, tb):
    r = lax.broadcasted_iota(I32, (rows, tb), 0).astype(F32)
    return r == loc_ref[0, 0:1, :], r == loc_ref[0, 1:2, :]


def _moe_scatter_kernel(bm_ref, a_ref, loc_ref, gate_ref, xs_in_ref, gs_in_ref, xs_ref, gs_ref,
                        buf_ref, gbuf_ref, sem, gsem, *, tb):
    del xs_in_ref, gs_in_ref
    blk = pl.program_id(0)
    last = pl.num_programs(0) - 1
    slot = blk % 2

    def copies(block, wait):
        s = block % 2
        _run_copies(bm_ref, block, buf_ref.at[s], xs_ref, sem.at[s], tb=tb, to_hbm=True, wait=wait)
        _run_copies(bm_ref, block, gbuf_ref.at[s], gs_ref, gsem.at[s], tb=tb, to_hbm=True, wait=wait)

    @pl.when(blk >= 2)
    def _():
        copies(blk - 2, True)

    first, second = _selection(loc_ref, _compact_rows(tb), tb)
    pick = jnp.where(first, 1.0, jnp.where(second, 1.0, 0.0)).astype(BF16)
    buf_ref[slot] = jnp.dot(pick, a_ref[...], preferred_element_type=F32)
    gsel = jnp.where(first, gate_ref[0, 0:1, :], jnp.where(second, gate_ref[0, 1:2, :], 0.0))
    gbuf_ref[slot] = jnp.broadcast_to(jnp.sum(gsel, axis=1, keepdims=True), gbuf_ref.shape[1:])
    copies(blk, False)

    @pl.when(blk == last)
    def _():
        @pl.when(blk >= 1)
        def _():
            copies(blk - 1, True)

        copies(blk, True)


def _moe_scatter(a, loc, gate, bmeta, n_slots, tb):
    m, d = a.shape
    xs0 = jnp.zeros((n_slots, d), F32)
    gs0 = jnp.zeros((n_slots, LANES), F32)
    any_spec = pl.BlockSpec(memory_space=pl.ANY)
    tok_spec = pl.BlockSpec((1, N_EXPERTS, tb), lambda i, bm: (i, 0, 0))
    grid_spec = pltpu.PrefetchScalarGridSpec(
        num_scalar_prefetch=1,
        grid=(m // tb,),
        in_specs=[pl.BlockSpec((tb, d), lambda i, bm: (i, 0)), tok_spec, tok_spec, any_spec, any_spec],
        out_specs=[any_spec, any_spec],
        scratch_shapes=[pltpu.VMEM((2, _compact_rows(tb), d), F32), pltpu.VMEM((2, _compact_rows(tb), LANES), F32),
                        pltpu.SemaphoreType.DMA((2,)), pltpu.SemaphoreType.DMA((2,))],
    )
    return pl.pallas_call(
        functools.partial(_moe_scatter_kernel, tb=tb),
        grid_spec=grid_spec,
        out_shape=[jax.ShapeDtypeStruct((n_slots, d), F32), jax.ShapeDtypeStruct((n_slots, LANES), F32)],
        input_output_aliases={4: 0, 5: 1},
        compiler_params=_params(("arbitrary",), 12 * tb * d * 4),
        name="moe_scatter",
    )(bmeta, a, loc, gate, xs0, gs0)


def _new_weights(te_ref):
    i = pl.program_id(1)
    return (i == 0) | (te_ref[i] != te_ref[jnp.maximum(i - 1, 0)])


def _moe_gu_kernel(te_ref, nu_ref, x_ref, wg_ref, wu_ref, o_ref, wgb_ref, wub_ref):
    i = pl.program_id(1)

    @pl.when(_new_weights(te_ref))
    def _():
        wgb_ref[...] = wg_ref[0].astype(BF16)
        wub_ref[...] = wu_ref[0].astype(BF16)

    @pl.when(i < nu_ref[0])
    def _():
        x = x_ref[...].astype(BF16)
        for c in range(o_ref.shape[1] // MXU_COLS):
            cols = slice(c * MXU_COLS, (c + 1) * MXU_COLS)
            g = jnp.dot(x, wgb_ref[:, cols], preferred_element_type=F32)
            u = jnp.dot(x, wub_ref[:, cols], preferred_element_type=F32)
            o_ref[:, cols] = (_silu(g) * u).astype(o_ref.dtype)

    @pl.when(i >= nu_ref[0])
    def _():
        o_ref[...] = jnp.zeros_like(o_ref)


def _used_tile(i, nu):
    return jnp.minimum(i, nu[0] - 1)


def _moe_gu(xs, w_gu, te, nu, tm, tn):
    s, k = xs.shape
    f = D_FF_EXPERT
    nbytes = 2 * (2 * k * tn * 4 + k * tn * 2) + 2 * tm * k * 4 + 2 * tm * tn * 2 + 4 * tm * MXU_COLS * 4
    grid_spec = pltpu.PrefetchScalarGridSpec(
        num_scalar_prefetch=2,
        grid=(f // tn, s // tm),
        in_specs=[pl.BlockSpec((tm, k), lambda j, i, te, nu: (_used_tile(i, nu), 0)),
                  pl.BlockSpec((1, k, tn), lambda j, i, te, nu: (te[i], 0, j)),
                  pl.BlockSpec((1, k, tn), lambda j, i, te, nu: (te[i], 0, j + f // tn))],
        out_specs=pl.BlockSpec((tm, tn), lambda j, i, te, nu: (i, j)),
        scratch_shapes=[pltpu.VMEM((k, tn), BF16), pltpu.VMEM((k, tn), BF16)],
    )
    return pl.pallas_call(
        _moe_gu_kernel,
        grid_spec=grid_spec,
        out_shape=jax.ShapeDtypeStruct((s, f), BF16),
        compiler_params=_params(("arbitrary", "arbitrary"), nbytes),
        name="moe_gate_up",
    )(te, nu, xs, w_gu, w_gu)


def _moe_down_kernel(te_ref, nu_ref, x_ref, w_ref, gs_ref, o_ref, wb_ref):
    i = pl.program_id(1)

    @pl.when(_new_weights(te_ref))
    def _():
        wb_ref[...] = w_ref[0].astype(BF16)

    @pl.when(i < nu_ref[0])
    def _():
        o_ref[...] = gs_ref[:, 0:1] * jnp.dot(x_ref[...], wb_ref[...], preferred_element_type=F32)

    @pl.when(i >= nu_ref[0])
    def _():
        o_ref[...] = jnp.zeros_like(o_ref)


def _moe_down(act, w_down, gs, te, nu, tm, tn):
    s, k = act.shape
    d = w_down.shape[2]
    nbytes = 2 * k * tn * 4 + k * tn * 2 + 2 * tm * k * 2 + 3 * tm * tn * 4
    grid_spec = pltpu.PrefetchScalarGridSpec(
        num_scalar_prefetch=2,
        grid=(d // tn, s // tm),
        in_specs=[pl.BlockSpec((tm, k), lambda j, i, te, nu: (_used_tile(i, nu), 0)),
                  pl.BlockSpec((1, k, tn), lambda j, i, te, nu: (te[i], 0, j)),
                  pl.BlockSpec((tm, LANES), lambda j, i, te, nu: (_used_tile(i, nu), 0))],
        out_specs=pl.BlockSpec((tm, tn), lambda j, i, te, nu: (i, j)),
        scratch_shapes=[pltpu.VMEM((k, tn), BF16)],
    )
    return pl.pallas_call(
        _moe_down_kernel,
        grid_spec=grid_spec,
        out_shape=jax.ShapeDtypeStruct((s, d), F32),
        compiler_params=_params(("arbitrary", "arbitrary"), nbytes),
        name="moe_down",
    )(te, nu, act, w_down, gs)


def _moe_combine_kernel(bm_ref, h_ref, locc_ref, g_ref, ys_ref, op_ref, os_ref, ybuf_ref, sem, *, tb, prompt_blocks):
    blk = pl.program_id(0)
    slot = blk % 2

    def copies(block, wait):
        s = block % 2
        _run_copies(bm_ref, block, ybuf_ref.at[s], ys_ref, sem.at[s], tb=tb, to_hbm=False, wait=wait)

    @pl.when(blk == 0)
    def _():
        ybuf_ref[...] = jnp.zeros_like(ybuf_ref)
        copies(blk, False)

    @pl.when(blk + 1 < pl.num_programs(0))
    def _():
        copies(blk + 1, False)

    r = lax.broadcasted_iota(I32, (tb, _compact_rows(tb)), 1).astype(F32)
    pick = jnp.where(r == locc_ref[:, 0:1], 1.0, jnp.where(r == locc_ref[:, 1:2], 1.0, 0.0)).astype(BF16)
    copies(blk, True)
    y = ybuf_ref[slot]
    y_hi = y.astype(BF16)
    y_lo = (y - y_hi.astype(F32)).astype(BF16)
    moe = jnp.dot(pick, y_hi, preferred_element_type=F32) + jnp.dot(pick, y_lo, preferred_element_type=F32)
    out = _rms(h_ref[...] + moe, g_ref[...])

    @pl.when(blk < prompt_blocks)
    def _():
        op_ref[...] = out

    @pl.when(blk == prompt_blocks)
    def _():
        os_ref[...] = out


def _moe_combine(h, locc, bmeta, ys, g, tb, n_prompt):
    m, d = h.shape
    prompt_blocks = n_prompt // tb
    any_spec = pl.BlockSpec(memory_space=pl.ANY)
    grid_spec = pltpu.PrefetchScalarGridSpec(
        num_scalar_prefetch=1,
        grid=(m // tb,),
        in_specs=[pl.BlockSpec((tb, d), lambda i, bm: (i, 0)),
                  pl.BlockSpec((tb, N_EXPERTS), lambda i, bm: (i, 0)),
                  pl.BlockSpec((1, d), lambda i, bm: (0, 0)),
                  any_spec],
        out_specs=[pl.BlockSpec((tb, d), lambda i, bm: (jnp.minimum(i, prompt_blocks - 1), 0)),
                   pl.BlockSpec((tb, d), lambda i, bm: (0, 0))],
        scratch_shapes=[pltpu.VMEM((2, _compact_rows(tb), d), F32), pltpu.SemaphoreType.DMA((2,))],
    )
    return pl.pallas_call(
        functools.partial(_moe_combine_kernel, tb=tb, prompt_blocks=prompt_blocks),
        grid_spec=grid_spec,
        out_shape=[jax.ShapeDtypeStruct((n_prompt, d), F32), jax.ShapeDtypeStruct((tb, d), F32)],
        compiler_params=_params(("arbitrary",), 16 * tb * d * 4),
        name="moe_combine",
    )(bmeta, h, locc, g.reshape(1, d), ys)


MOE_TOKEN_BLOCK = 256
MOE_ROW_TILE = 512


def _moe(h, a, lg_parts, n_valid, n_prompt, w_gu, w_down, g_final):
    m, d = h.shape
    tb, tm = MOE_TOKEN_BLOCK, MOE_ROW_TILE
    nb = m // tb
    lg3 = lg_parts[:, :2 * N_EXPERTS].reshape(nb, tb, 2 * N_EXPERTS).transpose(0, 2, 1)
    n_tiles = pl.cdiv(2 * n_valid + nb * N_EXPERTS * (SUBLANES - 1), tm) + N_EXPERTS
    loc, gate, bmeta, tmeta = _route(lg3, tm, n_valid)
    te = tmeta[0, :n_tiles]
    nu = tmeta[1, 0:1]
    bm = bmeta[:, :, :3].reshape(-1)
    locc = loc.transpose(0, 2, 1).reshape(m, N_EXPERTS)
    xs, gs = _moe_scatter(a, loc, gate, bm, n_tiles * tm, tb)
    act = _moe_gu(xs, w_gu, te, nu, tm, D_FF_EXPERT // 2)
    ys = _moe_down(act, w_down, gs, te, nu, tm, d)
    return _moe_combine(h, locc, bm, ys, g_final, tb, n_prompt)


def _pad_lanes(v):
    return jnp.pad(v.reshape(1, -1), ((0, 0), (0, LANES - v.shape[-1])))


def kernel(x_prompt, x_sample, state_ssm, state_conv, norm_mix, norm_ffn, norm_final, ssd_w_in, ssd_conv_w,
           ssd_conv_b, ssd_dt_bias, ssd_a_log, ssd_d, ssd_gnorm, ssd_w_out, sgu_w_in, sgu_b_in, sgu_ln_g,
           sgu_ln_b, sgu_w_s, sgu_b_s, sgu_w_out, ffn_w_gu, ffn_w_down, moe_w_router, moe_w_gu, moe_w_down):
    batch, seq, d = x_prompt.shape
    n_s = x_sample.shape[0]
    mp = batch * seq
    xp = x_prompt.reshape(mp, d)
    xs = x_sample.reshape(n_s, d)

    w_in = ssd_w_in[0]
    w_dt = jnp.pad(w_in[:, D_INNER + CONV_DIM:], ((0, 0), (0, LANES - SSD_HEADS)))
    dtb = _pad_lanes(ssd_dt_bias[0])
    alog = _pad_lanes(ssd_a_log[0])
    d_rep = jnp.repeat(ssd_d[0], SSD_HEADDIM).reshape(1, D_INNER)
    gn_w = ssd_gnorm[0].reshape(1, D_INNER)
    conv_w = ssd_conv_w[0]
    conv_b = ssd_conv_b[0].reshape(1, CONV_DIM)
    ln_g = sgu_ln_g[0].reshape(1, D_SGU)
    ln_b = sgu_ln_b[0].reshape(1, D_SGU)
    bs_t = jnp.pad(sgu_b_s[0].T, ((0, 0), (0, LANES - SGU_HEADS)))
    w0_rep = jnp.repeat(sgu_w_s[0][:, 0, 0], SGU_HEAD_DIM).reshape(1, D_SGU)
    b0_rep = jnp.repeat(sgu_b_s[0][:, 0], SGU_HEAD_DIM).reshape(1, D_SGU)
    wr_hi = moe_w_router[0].astype(BF16)
    wr_lo = (moe_w_router[0] - wr_hi.astype(F32)).astype(BF16)
    w3 = jnp.concatenate([wr_hi, wr_lo, jnp.zeros((d, LANES - 2 * N_EXPERTS), BF16)], axis=1)

    def in_proj(a, tm):
        z = _mm_plain(a, w_in, 0, D_INNER, tm, 1024, BF16, "ssd_in_z")
        xbc = _mm_plain(a, w_in, D_INNER, CONV_DIM, tm, 1024, F32, "ssd_in_xbc")
        dtr = _mm_plain(a, w_dt, 0, LANES, tm, LANES, F32, "ssd_in_dt")
        return z, xbc, dtr

    def after_mixer0(yg, resid, tm):
        h1, a1 = _mm_resnorm(yg, ssd_w_out[0], resid, norm_ffn[0], tm, "ssd_out")
        act = _mm_swiglu(a1, ffn_w_gu[0], D_FF_DENSE, tm, 1408, "ffn_gate_up")
        return _mm_resnorm(act, ffn_w_down[0], h1, norm_mix[1], tm, "ffn_down")

    a0p = _rmsnorm(xp, norm_mix[0], 512, BF16)
    zp, xbcp, dtrp = in_proj(a0p, 1024)
    ygp, ssm_p, conv_p = _ssd_prompt(xbcp, zp, dtrp, conv_w, conv_b, dtb, alog, d_rep, gn_w, batch, seq)
    h2p, a2p = after_mixer0(ygp, xp, 512)

    a0s = _rmsnorm(xs, norm_mix[0], n_s, BF16)
    zs, xbcs, dtrs = in_proj(a0s, n_s)
    conv_s, xas, xdts, das, bsts, csts = _ssd_sample_pre(xbcs, state_conv[0].reshape(n_s, 3 * CONV_DIM), dtrs,
                                                         conv_w, conv_b, dtb, alog)
    stack_rows = SSD_GROUPS * TERM_ROWS
    ssm_s, ys3 = _ssd_sample_state(state_ssm[0].reshape(n_s, D_INNER, D_STATE), xdts.reshape(n_s, 1, D_INNER),
                                   das[:, :SSD_HEADS], bsts.reshape(n_s, stack_rows, D_STATE),
                                   csts.reshape(n_s, stack_rows, D_STATE))
    ygs = _ssd_sample_post(ys3.reshape(n_s, D_INNER), xas, zs, d_rep, gn_w)
    h2s, a2s = after_mixer0(ygs, xs, n_s)

    uvp = _mm_bias_gelu(a2p, sgu_w_in[0], sgu_b_in[0], 1024, 1024, BF16, "sgu_in")
    sp = _sgu_prompt(uvp, ln_g, ln_b, sgu_w_s[0], bs_t, batch, seq)
    uvs = _mm_bias_gelu(a2s, sgu_w_in[0], sgu_b_in[0], n_s, 1024, F32, "sgu_in")
    ss, v_s = _sgu_sample(uvs, ln_g, ln_b, w0_rep, b0_rep)
    tm_out = 512
    pad_rows = ((0, tm_out - n_s), (0, 0))
    h3, a3, lg = _mm_resnorm_router(sp, jnp.pad(ss, pad_rows), sgu_w_out[0], h2p, jnp.pad(h2s, pad_rows),
                                    norm_ffn[1], w3, tm_out, "sgu_out")

    y_prompt, y_tail = _moe(h3, a3, lg, mp + n_s, mp, moe_w_gu[0], moe_w_down[0], norm_final)
    y_sample = y_tail[:n_s]

    return (y_prompt.reshape(batch, seq, d),
            y_sample.reshape(n_s, 1, d),
            ssm_p[None],
            conv_p[None],
            ssm_s.reshape(1, n_s, SSD_HEADS, SSD_HEADDIM, D_STATE),
            conv_s.reshape(1, n_s, CONV_W - 1, CONV_DIM),
            v_s.reshape(1, n_s, 1, D_SGU))
```

```python
import functools

import jax
import jax.numpy as jnp
from jax import lax
from jax.experimental import pallas as pl
from jax.experimental.pallas import tpu as pltpu

F32 = jnp.float32
BF16 = jnp.bfloat16
I32 = jnp.int32
HIGHEST = lax.Precision.HIGHEST

D_MODEL = 1024
D_INNER = 2048
SSD_HEADS = 32
SSD_HEADDIM = 64
SSD_GROUPS = 4
D_STATE = 128
CONV_W = 4
CONV_DIM = D_INNER + 2 * SSD_GROUPS * D_STATE
CHUNK = 128
D_SGU = 2048
SGU_HEADS = 8
SGU_HEAD_DIM = D_SGU // SGU_HEADS
D_FF_DENSE = 2816
N_EXPERTS = 8
D_FF_EXPERT = 3584
EPS = 1e-5

MXU_COLS = 256
ROW_SUB = 256
LANES = 128
SUBLANES = 8
VMEM_CAP = 60000 * 1024
VMEM_FLOOR = 32 * 1024 * 1024


def _vmem_limit(nbytes):
    return int(min(max(nbytes * 5 // 4 + (4 << 20), VMEM_FLOOR), VMEM_CAP))


def _params(sem, nbytes):
    return pltpu.CompilerParams(dimension_semantics=sem, vmem_limit_bytes=_vmem_limit(nbytes))


LOG2E = 1.4426950408889634


def _silu(x):
    return x / (1.0 + jnp.exp2(x * (-LOG2E)))


def _rms(x, g):
    return x * lax.rsqrt(jnp.mean(x * x, axis=-1, keepdims=True) + EPS) * g


def _bdot(a, b):
    return jnp.dot(a.astype(BF16), b.astype(BF16), preferred_element_type=F32)


def _rmsnorm_kernel(x_ref, g_ref, o_ref):
    o_ref[...] = _rms(x_ref[...], g_ref[...]).astype(o_ref.dtype)


def _rmsnorm(x, g, tm, out_dtype):
    m, d = x.shape
    return pl.pallas_call(
        _rmsnorm_kernel,
        grid=(m // tm,),
        in_specs=[pl.BlockSpec((tm, d), lambda i: (i, 0)), pl.BlockSpec((1, d), lambda i: (0, 0))],
        out_specs=pl.BlockSpec((tm, d), lambda i: (i, 0)),
        out_shape=jax.ShapeDtypeStruct((m, d), out_dtype),
        compiler_params=_params(("arbitrary",), 4 * tm * d * 4),
        name="rmsnorm",
    )(x, g.reshape(1, d))


def _mm_plain_kernel(x_ref, w_ref, o_ref, wb_ref):
    @pl.when(pl.program_id(1) == 0)
    def _():
        wb_ref[...] = w_ref[...].astype(BF16)

    o_ref[...] = _bdot(x_ref[...], wb_ref[...]).astype(o_ref.dtype)


def _mm_bias_gelu_kernel(x_ref, w_ref, b_ref, o_ref, wb_ref):
    @pl.when(pl.program_id(1) == 0)
    def _():
        wb_ref[...] = w_ref[...].astype(BF16)

    y = _bdot(x_ref[...], wb_ref[...]) + b_ref[...]
    o_ref[...] = (0.5 * y * (1.0 + lax.erf(y * (2.0 ** -0.5)))).astype(o_ref.dtype)


def _mm_swiglu_kernel(x_ref, wg_ref, wu_ref, o_ref, wgb_ref, wub_ref):
    @pl.when(pl.program_id(1) == 0)
    def _():
        wgb_ref[...] = wg_ref[...].astype(BF16)
        wub_ref[...] = wu_ref[...].astype(BF16)

    x = x_ref[...].astype(BF16)
    g = jnp.dot(x, wgb_ref[...], preferred_element_type=F32)
    u = jnp.dot(x, wub_ref[...], preferred_element_type=F32)
    o_ref[...] = (_silu(g) * u).astype(o_ref.dtype)


def _wspec(k, tn, col_block0):
    return pl.BlockSpec((k, tn), lambda j, i: (0, j + col_block0), pipeline_mode=pl.Buffered(1))


def _mm_plain(x, w, col0, n, tm, tn, out_dtype, name):
    m, k = x.shape
    nbytes = k * tn * 6 + 2 * tm * k * x.dtype.itemsize + 2 * tm * tn * 4
    return pl.pallas_call(
        _mm_plain_kernel,
        grid=(n // tn, m // tm),
        in_specs=[pl.BlockSpec((tm, k), lambda j, i: (i, 0)), _wspec(k, tn, col0 // tn)],
        out_specs=pl.BlockSpec((tm, tn), lambda j, i: (i, j)),
        out_shape=jax.ShapeDtypeStruct((m, n), out_dtype),
        scratch_shapes=[pltpu.VMEM((k, tn), BF16)],
        compiler_params=_params(("arbitrary", "arbitrary"), nbytes),
        name=name,
    )(x, w)


def _mm_conv_kernel(x_ref, w_ref, cw_ref, cb_ref, xa_ref, tail_ref, wb_ref, ext_ref, *, tiles_per_seq):
    i = pl.program_id(1)
    tm = x_ref.shape[0]

    @pl.when(i == 0)
    def _():
        wb_ref[...] = w_ref[...].astype(BF16)

    @pl.when(i % tiles_per_seq == 0)
    def _():
        ext_ref[0:SUBLANES, :] = jnp.zeros((SUBLANES, ext_ref.shape[1]), F32)

    for s in range(tm // ROW_SUB):
        rows = slice(s * ROW_SUB, (s + 1) * ROW_SUB)
        r0 = SUBLANES + s * ROW_SUB
        ext_ref[r0:r0 + ROW_SUB, :] = _bdot(x_ref[rows, :], wb_ref[...])
        acc = cb_ref[...] + cw_ref[CONV_W - 1:CONV_W, :] * ext_ref[r0:r0 + ROW_SUB, :]
        for k in range(CONV_W - 1):
            lo = r0 - (CONV_W - 1) + k
            acc = acc + cw_ref[k:k + 1, :] * ext_ref[lo:lo + ROW_SUB, :]
        xa_ref[rows, :] = _silu(acc)
    tail = ext_ref[tm:tm + SUBLANES, :]
    ext_ref[0:SUBLANES, :] = tail
    tail_ref[0] = tail


def _mm_conv(x, w, col0, conv_w, conv_b, tm, tn, seq, name):
    m, k = x.shape
    n = conv_w.shape[1]
    tiles_per_seq = seq // tm
    nbytes = k * tn * 6 + 2 * tm * k * 2 + 2 * tm * tn * 4 + 6 * (tm + SUBLANES) * tn * 4
    return pl.pallas_call(
        functools.partial(_mm_conv_kernel, tiles_per_seq=tiles_per_seq),
        grid=(n // tn, m // tm),
        in_specs=[pl.BlockSpec((tm, k), lambda j, i: (i, 0)), _wspec(k, tn, col0 // tn),
                  pl.BlockSpec((CONV_W, tn), lambda j, i: (0, j)), pl.BlockSpec((1, tn), lambda j, i: (0, j))],
        out_specs=[pl.BlockSpec((tm, tn), lambda j, i: (i, j)),
                   pl.BlockSpec((1, SUBLANES, tn), lambda j, i: (i // tiles_per_seq, 0, j))],
        out_shape=[jax.ShapeDtypeStruct((m, n), F32), jax.ShapeDtypeStruct((m // seq, SUBLANES, n), F32)],
        scratch_shapes=[pltpu.VMEM((k, tn), BF16), pltpu.VMEM((tm + SUBLANES, tn), F32)],
        compiler_params=_params(("arbitrary", "arbitrary"), nbytes),
        name=name,
    )(x, w, conv_w, conv_b)


def _mm_bias_gelu(x, w, b, tm, tn, out_dtype, name):
    m, k = x.shape
    n = w.shape[1]
    nbytes = k * tn * 6 + 2 * tm * k * x.dtype.itemsize + 2 * tm * tn * 4
    return pl.pallas_call(
        _mm_bias_gelu_kernel,
        grid=(n // tn, m // tm),
        in_specs=[pl.BlockSpec((tm, k), lambda j, i: (i, 0)), _wspec(k, tn, 0),
                  pl.BlockSpec((1, tn), lambda j, i: (0, j))],
        out_specs=pl.BlockSpec((tm, tn), lambda j, i: (i, j)),
        out_shape=jax.ShapeDtypeStruct((m, n), out_dtype),
        scratch_shapes=[pltpu.VMEM((k, tn), BF16)],
        compiler_params=_params(("arbitrary", "arbitrary"), nbytes),
        name=name,
    )(x, w, b.reshape(1, n))


def _mm_swiglu(x, w_gu, d_ff, tm, tn, name):
    m, k = x.shape
    nbytes = 2 * k * tn * 6 + 2 * tm * k * x.dtype.itemsize + 2 * tm * tn * 2 + 3 * tm * tn * 4
    return pl.pallas_call(
        _mm_swiglu_kernel,
        grid=(d_ff // tn, m // tm),
        in_specs=[pl.BlockSpec((tm, k), lambda j, i: (i, 0)), _wspec(k, tn, 0), _wspec(k, tn, d_ff // tn)],
        out_specs=pl.BlockSpec((tm, tn), lambda j, i: (i, j)),
        out_shape=jax.ShapeDtypeStruct((m, d_ff), BF16),
        scratch_shapes=[pltpu.VMEM((k, tn), BF16), pltpu.VMEM((k, tn), BF16)],
        compiler_params=_params(("arbitrary", "arbitrary"), nbytes),
        name=name,
    )(x, w_gu, w_gu)


def _mm_resnorm_kernel(x_ref, w_ref, r_ref, g_ref, h_ref, a_ref, wb_ref):
    @pl.when(pl.program_id(0) == 0)
    def _():
        wb_ref[...] = w_ref[...].astype(BF16)

    h = r_ref[...] + _bdot(x_ref[...], wb_ref[...])
    h_ref[...] = h
    a_ref[...] = _rms(h, g_ref[...]).astype(a_ref.dtype)


def _mm_resnorm(x, w, resid, g, tm, name):
    m, k = x.shape
    d = w.shape[1]
    nbytes = k * d * 6 + 2 * tm * k * 2 + 8 * tm * d * 4
    return pl.pallas_call(
        _mm_resnorm_kernel,
        grid=(m // tm,),
        in_specs=[pl.BlockSpec((tm, k), lambda i: (i, 0)),
                  pl.BlockSpec((k, d), lambda i: (0, 0), pipeline_mode=pl.Buffered(1)),
                  pl.BlockSpec((tm, d), lambda i: (i, 0)),
                  pl.BlockSpec((1, d), lambda i: (0, 0))],
        out_specs=[pl.BlockSpec((tm, d), lambda i: (i, 0)), pl.BlockSpec((tm, d), lambda i: (i, 0))],
        out_shape=[jax.ShapeDtypeStruct((m, d), F32), jax.ShapeDtypeStruct((m, d), BF16)],
        scratch_shapes=[pltpu.VMEM((k, d), BF16)],
        compiler_params=_params(("arbitrary",), nbytes),
        name=name,
    )(x, w, resid, g.reshape(1, d))


def _sgu_out_kernel(u_ref, v_ref, xt_ref, lng_ref, lnb_ref, ws_ref, bst_ref, w_ref, rp_ref, rt_ref, g_ref, w3_ref,
                    h_ref, a_ref, lg_ref, wb_ref, wsb_ref, y_ref, *, prompt_tiles):
    i = pl.program_id(0)
    q = CHUNK

    @pl.when(i == 0)
    def _():
        wb_ref[...] = w_ref[...].astype(BF16)
        causal = lax.broadcasted_iota(I32, (q, q), 0) >= lax.broadcasted_iota(I32, (q, q), 1)
        for g in range(SGU_HEADS):
            wsb_ref[g] = jnp.where(causal, ws_ref[g], 0.0).astype(BF16)

    def project(x_ref, r_ref, rows):
        h = r_ref[rows, :] + jnp.dot(x_ref[rows, :], wb_ref[...], preferred_element_type=F32)
        h_ref[rows, :] = h
        a = _rms(h, g_ref[...])
        a_hi = a.astype(BF16)
        a_lo = (a - a_hi.astype(F32)).astype(BF16)
        a_ref[rows, :] = a_hi
        lg_ref[rows, :] = (jnp.dot(a_hi, w3_ref[...], preferred_element_type=F32)
                           + jnp.dot(a_lo, w3_ref[...], preferred_element_type=F32))

    chunks = [slice(c * q, (c + 1) * q) for c in range(u_ref.shape[0] // q)]

    @pl.when(i < prompt_tiles)
    def _():
        for rows in chunks:
            vn = _layernorm(v_ref[rows, :].astype(F32), lng_ref[...], lnb_ref[...]).astype(BF16)
            for g in range(SGU_HEADS):
                cols = slice(g * SGU_HEAD_DIM, (g + 1) * SGU_HEAD_DIM)
                s = jnp.dot(wsb_ref[g], vn[:, cols], preferred_element_type=F32) + bst_ref[:, g:g + 1]
                y_ref[rows, cols] = (u_ref[rows, cols].astype(F32) * s).astype(BF16)
            project(y_ref, rp_ref, rows)

    @pl.when(i == prompt_tiles)
    def _():
        for rows in chunks:
            project(xt_ref, rt_ref, rows)


def _sgu_out(uv, xt, ln_g, ln_b, w_s, bs_t, w, rp, rt, g, w3, tm):
    mp = uv.shape[0]
    k, d = w.shape
    prompt_tiles = mp // tm
    last = prompt_tiles - 1
    rows = mp + tm
    prow = lambda i: (jnp.minimum(i, last), 0)
    fixed = lambda i: (0, 0)
    nbytes = k * d * 6 + 8 * tm * k * 2 + 12 * tm * d * 4 + 8 * CHUNK * k * 4
    return pl.pallas_call(
        functools.partial(_sgu_out_kernel, prompt_tiles=prompt_tiles),
        grid=(prompt_tiles + 1,),
        in_specs=[pl.BlockSpec((tm, k), prow),
                  pl.BlockSpec((tm, k), lambda i: (jnp.minimum(i, last), 1)),
                  pl.BlockSpec((tm, k), fixed),
                  pl.BlockSpec((1, k), fixed), pl.BlockSpec((1, k), fixed),
                  pl.BlockSpec((SGU_HEADS, CHUNK, CHUNK), lambda i: (0, 0, 0)),
                  pl.BlockSpec((CHUNK, LANES), fixed),
                  pl.BlockSpec((k, d), fixed, pipeline_mode=pl.Buffered(1)),
                  pl.BlockSpec((tm, d), prow),
                  pl.BlockSpec((tm, d), fixed),
                  pl.BlockSpec((1, d), fixed),
                  pl.BlockSpec((d, LANES), fixed)],
        out_specs=[pl.BlockSpec((tm, d), lambda i: (i, 0)), pl.BlockSpec((tm, d), lambda i: (i, 0)),
                   pl.BlockSpec((tm, LANES), lambda i: (i, 0))],
        out_shape=[jax.ShapeDtypeStruct((rows, d), F32), jax.ShapeDtypeStruct((rows, d), BF16),
                   jax.ShapeDtypeStruct((rows, LANES), F32)],
        scratch_shapes=[pltpu.VMEM((k, d), BF16), pltpu.VMEM((SGU_HEADS, CHUNK, CHUNK), BF16),
                        pltpu.VMEM((tm, k), BF16)],
        compiler_params=_params(("arbitrary",), nbytes),
        name="sgu_out",
    )(uv, uv, xt, ln_g, ln_b, w_s, bs_t, w, rp, rt, g.reshape(1, d), w3)


def _gate_and_groupnorm(y, xs, z, d_rep, gn_w):
    yg = (y + d_rep * xs) * _silu(z)
    gw = D_INNER // SSD_GROUPS
    outs = []
    for g in range(SSD_GROUPS):
        blk = yg[:, g * gw:(g + 1) * gw]
        outs.append(blk * lax.rsqrt(jnp.mean(blk * blk, axis=-1, keepdims=True) + EPS))
    return jnp.concatenate(outs, axis=1) * gn_w


def _softplus(x):
    return jnp.maximum(x, 0.0) + jnp.log(1.0 + jnp.exp(-jnp.abs(x)))


def _ssd_prompt_kernel(xa_ref, dtr_ref, dtb_ref, alog_ref, y_ref, ssm_ref, st_ref):
    c = pl.program_id(1)
    nc = pl.num_programs(1)
    q = CHUNK

    @pl.when(c == 0)
    def _():
        st_ref[...] = jnp.zeros_like(st_ref)

    dt = _softplus(dtr_ref[...] + dtb_ref[...])
    a_neg = -jnp.exp(alog_ref[...])
    causal = lax.broadcasted_iota(I32, (q, q), 0) >= lax.broadcasted_iota(I32, (q, q), 1)
    tril = jnp.where(causal, 1.0, 0.0)
    cs = jnp.dot(tril, dt * a_neg, precision=HIGHEST, preferred_element_type=F32) * LOG2E
    cs_t = cs.T
    dt_t = dt.T
    rowp = cs_t - jnp.log2(dt_t)
    w_t = dt_t * jnp.exp2(cs_t[:, q - 1:q] - cs_t)
    first = lax.broadcasted_iota(I32, (q, 2 * SSD_HEADDIM), 1) < SSD_HEADDIM

    def per_head(v):
        zero = jnp.zeros_like(v)
        return jnp.concatenate([jnp.where(first, v, zero), jnp.where(first, zero, v)], axis=0)

    for g in range(SSD_GROUPS):
        b_g = xa_ref[:, D_INNER + g * D_STATE:D_INNER + (g + 1) * D_STATE]
        c_g = xa_ref[:, D_INNER + (SSD_GROUPS + g) * D_STATE:D_INNER + (SSD_GROUPS + g + 1) * D_STATE]
        b_gt = b_g.T
        cb = _bdot(c_g, b_gt)
        for pr in range(4):
            hp = g * 4 + pr
            cols = slice(hp * 2 * SSD_HEADDIM, (hp + 1) * 2 * SSD_HEADDIM)
            x2 = per_head(xa_ref[:, cols].astype(BF16))
            st_p = st_ref[:, cols]
            st2 = per_head(st_p.astype(BF16))
            ms, cs_scaled, bws, cds = [], [], [], []
            for s in range(2):
                h = 2 * hp + s
                colb = jnp.broadcast_to(cs[:, h:h + 1], (q, q))
                rowb = jnp.broadcast_to(rowp[h:h + 1, :], (q, q))
                ecol = jnp.exp2(colb)
                ms.append((cb * jnp.where(causal, jnp.exp2(colb - rowb), 0.0)).astype(BF16))
                cs_scaled.append((c_g * ecol).astype(BF16))
                bws.append((b_gt * w_t[h:h + 1, :]).astype(BF16))
                cds.append(ecol[q - 1:q, :])
            y_ref[:, cols] = jnp.dot(jnp.concatenate(ms + cs_scaled, axis=1), jnp.concatenate([x2, st2], axis=0),
                                     preferred_element_type=F32)
            cd = jnp.where(first[0:1, :], cds[0], cds[1])
            st_ref[:, cols] = st_p * cd + jnp.dot(jnp.concatenate(bws, axis=1), x2, preferred_element_type=F32)

    @pl.when(c == nc - 1)
    def _():
        for hp in range(SSD_HEADS // 2):
            blk = st_ref[:, hp * 2 * SSD_HEADDIM:(hp + 1) * 2 * SSD_HEADDIM].T
            ssm_ref[0, 2 * hp:2 * hp + 2] = blk.reshape(2, SSD_HEADDIM, D_STATE)


def _ssd_prompt(xa, dtr, dtb, alog, batch, seq):
    nc = seq // CHUNK
    row = lambda b, c: (b * nc + c, 0)
    fixed = lambda b, c: (0, 0)
    nbytes = 2 * CHUNK * (CONV_DIM * 4 + D_INNER * 4 + LANES * 4) + 4 * D_STATE * D_INNER * 4 \
        + 8 * CHUNK * CONV_DIM * 4
    return pl.pallas_call(
        _ssd_prompt_kernel,
        grid=(batch, nc),
        in_specs=[pl.BlockSpec((CHUNK, CONV_DIM), row), pl.BlockSpec((CHUNK, LANES), row),
                  pl.BlockSpec((1, LANES), fixed), pl.BlockSpec((1, LANES), fixed)],
        out_specs=[pl.BlockSpec((CHUNK, D_INNER), row),
                   pl.BlockSpec((1, SSD_HEADS, SSD_HEADDIM, D_STATE), lambda b, c: (b, 0, 0, 0))],
        out_shape=[jax.ShapeDtypeStruct((batch * seq, D_INNER), F32),
                   jax.ShapeDtypeStruct((batch, SSD_HEADS, SSD_HEADDIM, D_STATE), F32)],
        scratch_shapes=[pltpu.VMEM((D_STATE, D_INNER), F32)],
        compiler_params=_params(("arbitrary", "arbitrary"), nbytes),
        name="ssd_prompt",
    )(xa, dtr, dtb, alog)


def _ssd_out_kernel(y_ref, xs_ref, z_ref, drep_ref, gnw_ref, w_ref, r_ref, g_ref, h_ref, a_ref, wb_ref):
    @pl.when(pl.program_id(0) == 0)
    def _():
        wb_ref[...] = w_ref[...].astype(BF16)

    sub = min(ROW_SUB // 2, y_ref.shape[0])
    for s in range(y_ref.shape[0] // sub):
        rows = slice(s * sub, (s + 1) * sub)
        yg = _gate_and_groupnorm(y_ref[rows, :], xs_ref[rows, :], z_ref[rows, :].astype(F32), drep_ref[...],
                                 gnw_ref[...])
        h = r_ref[rows, :] + _bdot(yg, wb_ref[...])
        h_ref[rows, :] = h
        a_ref[rows, :] = _rms(h, g_ref[...]).astype(a_ref.dtype)


def _ssd_out(y, xa, z, d_rep, gn_w, w, resid, g, tm):
    m, k = y.shape
    d = w.shape[1]
    row = lambda i: (i, 0)
    fixed = lambda i: (0, 0)
    nbytes = k * d * 6 + 2 * tm * k * 10 + 8 * tm * d * 4 + 6 * tm * k * 4
    return pl.pallas_call(
        _ssd_out_kernel,
        grid=(m // tm,),
        in_specs=[pl.BlockSpec((tm, k), row), pl.BlockSpec((tm, k), row), pl.BlockSpec((tm, k), row),
                  pl.BlockSpec((1, k), fixed), pl.BlockSpec((1, k), fixed),
                  pl.BlockSpec((k, d), fixed, pipeline_mode=pl.Buffered(1)),
                  pl.BlockSpec((tm, d), row), pl.BlockSpec((1, d), fixed)],
        out_specs=[pl.BlockSpec((tm, d), row), pl.BlockSpec((tm, d), row)],
        out_shape=[jax.ShapeDtypeStruct((m, d), F32), jax.ShapeDtypeStruct((m, d), BF16)],
        scratch_shapes=[pltpu.VMEM((k, d), BF16)],
        compiler_params=_params(("arbitrary",), nbytes),
        name="ssd_out",
    )(y, xa, z, d_rep, gn_w, w, resid, g.reshape(1, d))


def _ssd_sample_pre_kernel(xbc_ref, conv_ref, dtr_ref, cw_ref, cb_ref, dtb_ref, alog_ref,
                           nconv_ref, xa_ref, xdt_ref, da_ref, bst_ref, cst_ref):
    xn = xbc_ref[...]
    acc = cb_ref[...] + cw_ref[3:4, :] * xn
    for k in range(CONV_W - 1):
        acc = acc + cw_ref[k:k + 1, :] * conv_ref[:, k * CONV_DIM:(k + 1) * CONV_DIM]
    xa = _silu(acc)
    xa_ref[...] = xa
    nconv_ref[:, 0:CONV_DIM] = conv_ref[:, CONV_DIM:2 * CONV_DIM]
    nconv_ref[:, CONV_DIM:2 * CONV_DIM] = conv_ref[:, 2 * CONV_DIM:3 * CONV_DIM]
    nconv_ref[:, 2 * CONV_DIM:3 * CONV_DIM] = xn

    dt = _softplus(dtr_ref[...] + dtb_ref[...])
    a_neg = -jnp.exp(alog_ref[...])
    hh = lax.broadcasted_iota(I32, (LANES, D_INNER), 0)
    cc = lax.broadcasted_iota(I32, (LANES, D_INNER), 1)
    expand = jnp.where(lax.shift_right_logical(cc, 6) == hh, 1.0, 0.0)
    dt_rep = jnp.dot(dt, expand, precision=HIGHEST, preferred_element_type=F32)
    xdt_ref[...] = xa[:, :D_INNER] * dt_rep
    da_ref[...] = jnp.exp(dt * a_neg)

    n = xn.shape[0]
    gs = SSD_GROUPS * D_STATE
    b1, b2, b3 = _split3(xa[:, D_INNER:D_INNER + gs])
    cm = xa[:, D_INNER + gs:]
    c1 = cm.astype(BF16)
    c2 = (cm - c1.astype(F32)).astype(BF16)
    zero = jnp.zeros((n, D_STATE), BF16)
    b_rows, c_rows = [], []
    for g in range(SSD_GROUPS):
        sl = slice(g * D_STATE, (g + 1) * D_STATE)
        b_rows += [b1[:, sl], b2[:, sl], b1[:, sl], b3[:, sl], b2[:, sl], b1[:, sl]] + [zero] * (TERM_ROWS - 6)
        c_rows += [c1[:, sl], c2[:, sl]] + [zero] * (TERM_ROWS - 2)
    bst_ref[...] = jnp.concatenate(b_rows, axis=1)
    cst_ref[...] = jnp.concatenate(c_rows, axis=1)


TERM_ROWS = 16


def _split3(v):
    v1 = v.astype(BF16)
    r1 = v - v1.astype(F32)
    v2 = r1.astype(BF16)
    v3 = (r1 - v2.astype(F32)).astype(BF16)
    return v1, v2, v3


def _ssd_sample_pre(xbc, conv_flat, dtr, conv_w, conv_b, dtb, alog):
    n = xbc.shape[0]
    full = lambda shape: pl.BlockSpec(shape, lambda i: (0,) * len(shape))
    stack = SSD_GROUPS * TERM_ROWS * D_STATE
    return pl.pallas_call(
        _ssd_sample_pre_kernel,
        grid=(1,),
        in_specs=[full((n, CONV_DIM)), full((n, 3 * CONV_DIM)), full((n, LANES)), full((CONV_W, CONV_DIM)),
                  full((1, CONV_DIM)), full((1, LANES)), full((1, LANES))],
        out_specs=[full((n, 3 * CONV_DIM)), full((n, CONV_DIM)), full((n, D_INNER)), full((n, LANES)),
                   full((n, stack)), full((n, stack))],
        out_shape=[jax.ShapeDtypeStruct((n, 3 * CONV_DIM), F32), jax.ShapeDtypeStruct((n, CONV_DIM), F32),
                   jax.ShapeDtypeStruct((n, D_INNER), F32), jax.ShapeDtypeStruct((n, LANES), F32),
                   jax.ShapeDtypeStruct((n, stack), BF16), jax.ShapeDtypeStruct((n, stack), BF16)],
        compiler_params=_params(("arbitrary",), 32 * n * CONV_DIM * 4),
        name="ssd_sample_pre",
    )(xbc, conv_flat, dtr, conv_w, conv_b, dtb, alog)


STATE_SAMPLES = 4


def _ssd_sample_state_kernel(da_ref, st_ref, xdt_ref, bst_ref, cst_ref, nst_ref, y_ref):
    first = pl.program_id(0) * STATE_SAMPLES
    rows_g = D_INNER // SSD_GROUPS
    sub = lax.broadcasted_iota(I32, (TERM_ROWS, D_INNER), 0)
    for s in range(STATE_SAMPLES):
        x1, x2, x3 = [t.astype(F32) for t in _split3(xdt_ref[s])]
        terms = jnp.where(sub == 0, x1, jnp.where(sub == 1, x1, jnp.where(sub == 2, x2, jnp.where(
            sub == 3, x1, jnp.where(sub == 4, x2, jnp.where(sub == 5, x3, 0.0)))))).astype(BF16)
        ys = []
        for g in range(SSD_GROUPS):
            tile = slice(g * TERM_ROWS, (g + 1) * TERM_ROWS)
            upd = lax.dot_general(terms[:, g * rows_g:(g + 1) * rows_g], bst_ref[s, tile, :],
                                  (((0,), (0,)), ((), ())), preferred_element_type=F32)
            halves = []
            for hh in range(rows_g // SSD_HEADDIM):
                head = g * (rows_g // SSD_HEADDIM) + hh
                rows = slice(head * SSD_HEADDIM, (head + 1) * SSD_HEADDIM)
                h_new = (da_ref[first + s, head] * st_ref[s, rows, :]
                         + upd[hh * SSD_HEADDIM:(hh + 1) * SSD_HEADDIM, :])
                nst_ref[s, rows, :] = h_new
                halves.append(h_new)
            h_g = jnp.concatenate(halves, axis=0)
            h_hi = h_g.astype(BF16)
            h_lo = (h_g - h_hi.astype(F32)).astype(BF16)
            nt = (((1,), (1,)), ((), ()))
            o_hi = lax.dot_general(cst_ref[s, tile, :], h_hi, nt, preferred_element_type=F32)
            o_lo = lax.dot_general(cst_ref[s, tile, :], h_lo, nt, preferred_element_type=F32)
            ys.append(o_hi[0:1, :] + o_hi[1:2, :] + o_lo[0:1, :])
        y_ref[s] = jnp.concatenate(ys, axis=1)


def _ssd_sample_state(state, xdt, da, bst, cst):
    n = state.shape[0]
    ns = STATE_SAMPLES
    stack_rows = SSD_GROUPS * TERM_ROWS
    return pl.pallas_call(
        _ssd_sample_state_kernel,
        grid=(n // ns,),
        in_specs=[pl.BlockSpec(memory_space=pltpu.SMEM),
                  pl.BlockSpec((ns, D_INNER, D_STATE), lambda i: (i, 0, 0)),
                  pl.BlockSpec((ns, 1, D_INNER), lambda i: (i, 0, 0)),
                  pl.BlockSpec((ns, stack_rows, D_STATE), lambda i: (i, 0, 0)),
                  pl.BlockSpec((ns, stack_rows, D_STATE), lambda i: (i, 0, 0))],
        out_specs=[pl.BlockSpec((ns, D_INNER, D_STATE), lambda i: (i, 0, 0)),
                   pl.BlockSpec((ns, 1, D_INNER), lambda i: (i, 0, 0))],
        out_shape=[jax.ShapeDtypeStruct((n, D_INNER, D_STATE), F32),
                   jax.ShapeDtypeStruct((n, 1, D_INNER), F32)],
        compiler_params=_params(("arbitrary",), 6 * ns * D_INNER * D_STATE * 4),
        name="ssd_sample_state",
    )(da, state, xdt, bst, cst)


def _layernorm(v, g, b):
    mu = jnp.mean(v, axis=-1, keepdims=True)
    d = v - mu
    var = jnp.mean(d * d, axis=-1, keepdims=True)
    return d * lax.rsqrt(var + EPS) * g + b


def _sgu_sample_kernel(u_ref, v_ref, lng_ref, lnb_ref, w0_ref, b0_ref, o_ref, vn_ref):
    vn = _layernorm(v_ref[...], lng_ref[...], lnb_ref[...])
    vn_ref[...] = vn
    o_ref[...] = (u_ref[...] * (w0_ref[...] * vn + b0_ref[...])).astype(o_ref.dtype)


def _sgu_sample(uv, ln_g, ln_b, w0_rep, b0_rep):
    n = uv.shape[0]
    fixed2 = lambda i: (0, 0)
    return pl.pallas_call(
        _sgu_sample_kernel,
        grid=(1,),
        in_specs=[pl.BlockSpec((n, D_SGU), lambda i: (0, 0)), pl.BlockSpec((n, D_SGU), lambda i: (0, 1)),
                  pl.BlockSpec((1, D_SGU), fixed2), pl.BlockSpec((1, D_SGU), fixed2),
                  pl.BlockSpec((1, D_SGU), fixed2), pl.BlockSpec((1, D_SGU), fixed2)],
        out_specs=[pl.BlockSpec((n, D_SGU), fixed2), pl.BlockSpec((n, D_SGU), fixed2)],
        out_shape=[jax.ShapeDtypeStruct((n, D_SGU), BF16), jax.ShapeDtypeStruct((n, D_SGU), F32)],
        compiler_params=_params(("arbitrary",), 12 * n * D_SGU * 4),
        name="sgu_sample",
    )(uv, uv, ln_g, ln_b, w0_rep, b0_rep)


def _experts_to_lanes(col, sub, lane):
    return jnp.sum(jnp.where(sub == lane, col, 0.0), axis=0, keepdims=True)


def _route_kernel(lg_ref, loc_ref, gate_ref, bmeta_ref, tmeta_ref, *, tile, n_valid):
    nb, _, tb = lg_ref.shape
    sub = lax.broadcasted_iota(I32, (N_EXPERTS, LANES), 0)
    lane = lax.broadcasted_iota(I32, (N_EXPERTS, LANES), 1)
    subf = lax.broadcasted_iota(I32, (N_EXPERTS, tb), 0).astype(F32)
    tok = lax.broadcasted_iota(I32, (N_EXPERTS, tb), 1)
    incl = jnp.where(lax.broadcasted_iota(I32, (tb, tb), 0) <= lax.broadcasted_iota(I32, (tb, tb), 1), 1.0, 0.0)
    neg = jnp.float32(-jnp.inf)
    none = jnp.float32(N_EXPERTS)

    def select(k):
        blk = lg_ref[k]
        l = blk[0:N_EXPERTS] + blk[N_EXPERTS:2 * N_EXPERTS]
        m1 = jnp.max(l, axis=0, keepdims=True)
        i1 = jnp.min(jnp.where(l == m1, subf, none), axis=0, keepdims=True)
        l2 = jnp.where(subf == i1, neg, l)
        m2 = jnp.max(l2, axis=0, keepdims=True)
        i2 = jnp.min(jnp.where(l2 == m2, subf, none), axis=0, keepdims=True)
        valid = (k * tb + tok) < n_valid
        sel = jnp.where(valid, jnp.where(subf == i1, 1.0, jnp.where(subf == i2, 1.0, 0.0)), 0.0)
        return m1, i1, m2, i2, valid, sel

    def run_rows(cnt):
        return jnp.ceil(cnt / SUBLANES) * SUBLANES

    def count_body(k, carry):
        return carry + run_rows(jnp.sum(select(k)[5], axis=1, keepdims=True))

    counts = lax.fori_loop(0, nb, count_body, jnp.zeros((N_EXPERTS, 1), F32))
    tiles = jnp.ceil(counts / tile)
    cum_incl = jnp.sum(jnp.where(lane <= sub, _experts_to_lanes(tiles, sub, lane), 0.0), axis=1, keepdims=True)
    offset = (cum_incl - tiles) * tile
    tile_expert = jnp.sum(jnp.where(lane.astype(F32) >= cum_incl, 1.0, 0.0), axis=0, keepdims=True)
    tile_expert = jnp.minimum(tile_expert, N_EXPERTS - 1.0)
    n_used = jnp.max(cum_incl, axis=0, keepdims=True)
    tmeta_ref[...] = jnp.where(sub == 0, tile_expert, jnp.where(sub == 1, n_used, 0.0)).astype(I32)

    def place_body(k, before):
        m1, i1, m2, i2, valid, sel = select(k)
        run = jnp.dot(sel, incl, preferred_element_type=F32)
        cnt = run_rows(run[:, tb - 1:tb])
        lstart = jnp.sum(jnp.where(lane < sub, _experts_to_lanes(cnt, sub, lane), 0.0), axis=1, keepdims=True)
        local = lstart + run - sel
        loc1 = jnp.sum(jnp.where(subf == i1, local, 0.0), axis=0, keepdims=True)
        loc2 = jnp.sum(jnp.where(subf == i2, local, 0.0), axis=0, keepdims=True)
        live = (k * tb + tok[0:1, :]) < n_valid
        loc_ref[k] = jnp.where(subf == 0.0, jnp.where(live, loc1, -1.0),
                               jnp.where(subf == 1.0, jnp.where(live, loc2, -1.0), 0.0))
        e = jnp.exp(m2 - m1)
        gate_ref[k] = jnp.where(subf == 0.0, 1.0 / (1.0 + e), jnp.where(subf == 1.0, e / (1.0 + e), 0.0))
        bmeta_ref[k] = jnp.where(lane == 0, offset + before,
                                 jnp.where(lane == 1, cnt, jnp.where(lane == 2, lstart, 0.0))).astype(I32)
        return before + cnt

    lax.fori_loop(0, nb, place_body, jnp.zeros((N_EXPERTS, 1), F32))


def _route(logits3, tile, n_valid):
    nb, _, tb = logits3.shape
    tok_blk = pl.BlockSpec((nb, N_EXPERTS, tb), lambda i: (0, 0, 0))
    return pl.pallas_call(
        functools.partial(_route_kernel, tile=tile, n_valid=n_valid),
        grid=(1,),
        in_specs=[pl.BlockSpec((nb, 2 * N_EXPERTS, tb), lambda i: (0, 0, 0))],
        out_specs=[tok_blk, tok_blk, pl.BlockSpec((nb, N_EXPERTS, LANES), lambda i: (0, 0, 0)),
                   pl.BlockSpec((N_EXPERTS, LANES), lambda i: (0, 0))],
        out_shape=[jax.ShapeDtypeStruct((nb, N_EXPERTS, tb), F32),
                   jax.ShapeDtypeStruct((nb, N_EXPERTS, tb), F32),
                   jax.ShapeDtypeStruct((nb, N_EXPERTS, LANES), I32),
                   jax.ShapeDtypeStruct((N_EXPERTS, LANES), I32)],
        compiler_params=_params(("arbitrary",), 16 * nb * N_EXPERTS * tb * 4),
        name="moe_route",
    )(logits3)


def _run_copies(bm_ref, blk, vmem_ref, hbm_ref, sem, *, tb, to_hbm, wait):
    for e in range(N_EXPERTS):
        base = (blk * N_EXPERTS + e) * 3
        start, cnt, lstart = bm_ref[base], bm_ref[base + 1], bm_ref[base + 2]
        off = 0
        size = tb
        while size >= SUBLANES:
            @pl.when((cnt & size) != 0)
            def _(size=size, off=off):
                v = vmem_ref.at[pl.ds(pl.multiple_of(lstart + off, SUBLANES), size)]
                h = hbm_ref.at[pl.ds(pl.multiple_of(start + off, SUBLANES), size)]
                cp = pltpu.make_async_copy(v, h, sem) if to_hbm else pltpu.make_async_copy(h, v, sem)
                if wait:
                    cp.wait()
                else:
                    cp.start()

            off = off + (cnt & size)
            size //= 2


def _compact_rows(tb):
    return 2 * tb + N_EXPERTS * SUBLANES


def _selection(loc_ref, rows, tb):
    r = lax.broadcasted_iota(I32, (rows, tb), 0).astype(F32)
    return r == loc_ref[0, 0:1, :], r == loc_ref[0, 1:2, :]


def _moe_scatter_kernel(bm_ref, a_ref, loc_ref, gate_ref, xs_in_ref, gs_in_ref, xs_ref, gs_ref,
                        buf_ref, gbuf_ref, sem, gsem, *, tb):
    del xs_in_ref, gs_in_ref
    blk = pl.program_id(0)
    last = pl.num_programs(0) - 1
    slot = blk % 2

    def copies(block, wait):
        s = block % 2
        _run_copies(bm_ref, block, buf_ref.at[s], xs_ref, sem.at[s], tb=tb, to_hbm=True, wait=wait)
        _run_copies(bm_ref, block, gbuf_ref.at[s], gs_ref, gsem.at[s], tb=tb, to_hbm=True, wait=wait)

    @pl.when(blk >= 2)
    def _():
        copies(blk - 2, True)

    first, second = _selection(loc_ref, _compact_rows(tb), tb)
    pick = jnp.where(first, 1.0, jnp.where(second, 1.0, 0.0)).astype(BF16)
    buf_ref[slot] = jnp.dot(pick, a_ref[...], preferred_element_type=F32)
    gsel = jnp.where(first, gate_ref[0, 0:1, :], jnp.where(second, gate_ref[0, 1:2, :], 0.0))
    gbuf_ref[slot] = jnp.broadcast_to(jnp.sum(gsel, axis=1, keepdims=True), gbuf_ref.shape[1:])
    copies(blk, False)

    @pl.when(blk == last)
    def _():
        @pl.when(blk >= 1)
        def _():
            copies(blk - 1, True)

        copies(blk, True)


def _moe_scatter(a, loc, gate, bmeta, n_slots, tb):
    m, d = a.shape
    xs0 = jnp.zeros((n_slots, d), F32)
    gs0 = jnp.zeros((n_slots, LANES), F32)
    any_spec = pl.BlockSpec(memory_space=pl.ANY)
    tok_spec = pl.BlockSpec((1, N_EXPERTS, tb), lambda i, bm: (i, 0, 0))
    grid_spec = pltpu.PrefetchScalarGridSpec(
        num_scalar_prefetch=1,
        grid=(m // tb,),
        in_specs=[pl.BlockSpec((tb, d), lambda i, bm: (i, 0)), tok_spec, tok_spec, any_spec, any_spec],
        out_specs=[any_spec, any_spec],
        scratch_shapes=[pltpu.VMEM((2, _compact_rows(tb), d), F32), pltpu.VMEM((2, _compact_rows(tb), LANES), F32),
                        pltpu.SemaphoreType.DMA((2,)), pltpu.SemaphoreType.DMA((2,))],
    )
    return pl.pallas_call(
        functools.partial(_moe_scatter_kernel, tb=tb),
        grid_spec=grid_spec,
        out_shape=[jax.ShapeDtypeStruct((n_slots, d), F32), jax.ShapeDtypeStruct((n_slots, LANES), F32)],
        input_output_aliases={4: 0, 5: 1},
        compiler_params=_params(("arbitrary",), 12 * tb * d * 4),
        name="moe_scatter",
    )(bmeta, a, loc, gate, xs0, gs0)


def _new_weights(te_ref):
    i = pl.program_id(1)
    return (i == 0) | (te_ref[i] != te_ref[jnp.maximum(i - 1, 0)])


def _moe_gu_kernel(te_ref, nu_ref, x_ref, wg_ref, wu_ref, o_ref, wgb_ref, wub_ref):
    i = pl.program_id(1)

    @pl.when(_new_weights(te_ref))
    def _():
        wgb_ref[...] = wg_ref[0].astype(BF16)
        wub_ref[...] = wu_ref[0].astype(BF16)

    @pl.when(i < nu_ref[0])
    def _():
        x = x_ref[...].astype(BF16)
        for c in range(o_ref.shape[1] // MXU_COLS):
            cols = slice(c * MXU_COLS, (c + 1) * MXU_COLS)
            g = jnp.dot(x, wgb_ref[:, cols], preferred_element_type=F32)
            u = jnp.dot(x, wub_ref[:, cols], preferred_element_type=F32)
            o_ref[:, cols] = (_silu(g) * u).astype(o_ref.dtype)

    @pl.when(i >= nu_ref[0])
    def _():
        o_ref[...] = jnp.zeros_like(o_ref)


def _used_tile(i, nu):
    return jnp.minimum(i, nu[0] - 1)


def _moe_gu(xs, w_gu, te, nu, tm, tn):
    s, k = xs.shape
    f = D_FF_EXPERT
    nbytes = 2 * (2 * k * tn * 4 + k * tn * 2) + 2 * tm * k * 4 + 2 * tm * tn * 2 + 4 * tm * MXU_COLS * 4
    grid_spec = pltpu.PrefetchScalarGridSpec(
        num_scalar_prefetch=2,
        grid=(f // tn, s // tm),
        in_specs=[pl.BlockSpec((tm, k), lambda j, i, te, nu: (_used_tile(i, nu), 0)),
                  pl.BlockSpec((1, k, tn), lambda j, i, te, nu: (te[i], 0, j)),
                  pl.BlockSpec((1, k, tn), lambda j, i, te, nu: (te[i], 0, j + f // tn))],
        out_specs=pl.BlockSpec((tm, tn), lambda j, i, te, nu: (i, j)),
        scratch_shapes=[pltpu.VMEM((k, tn), BF16), pltpu.VMEM((k, tn), BF16)],
    )
    return pl.pallas_call(
        _moe_gu_kernel,
        grid_spec=grid_spec,
        out_shape=jax.ShapeDtypeStruct((s, f), BF16),
        compiler_params=_params(("arbitrary", "arbitrary"), nbytes),
        name="moe_gate_up",
    )(te, nu, xs, w_gu, w_gu)


def _moe_down_kernel(te_ref, nu_ref, x_ref, w_ref, gs_ref, o_ref, wb_ref):
    i = pl.program_id(1)

    @pl.when(_new_weights(te_ref))
    def _():
        wb_ref[...] = w_ref[0].astype(BF16)

    @pl.when(i < nu_ref[0])
    def _():
        o_ref[...] = gs_ref[:, 0:1] * jnp.dot(x_ref[...], wb_ref[...], preferred_element_type=F32)

    @pl.when(i >= nu_ref[0])
    def _():
        o_ref[...] = jnp.zeros_like(o_ref)


def _moe_down(act, w_down, gs, te, nu, tm, tn):
    s, k = act.shape
    d = w_down.shape[2]
    nbytes = 2 * k * tn * 4 + k * tn * 2 + 2 * tm * k * 2 + 3 * tm * tn * 4
    grid_spec = pltpu.PrefetchScalarGridSpec(
        num_scalar_prefetch=2,
        grid=(d // tn, s // tm),
        in_specs=[pl.BlockSpec((tm, k), lambda j, i, te, nu: (_used_tile(i, nu), 0)),
                  pl.BlockSpec((1, k, tn), lambda j, i, te, nu: (te[i], 0, j)),
                  pl.BlockSpec((tm, LANES), lambda j, i, te, nu: (_used_tile(i, nu), 0))],
        out_specs=pl.BlockSpec((tm, tn), lambda j, i, te, nu: (i, j)),
        scratch_shapes=[pltpu.VMEM((k, tn), BF16)],
    )
    return pl.pallas_call(
        _moe_down_kernel,
        grid_spec=grid_spec,
        out_shape=jax.ShapeDtypeStruct((s, d), F32),
        compiler_params=_params(("arbitrary", "arbitrary"), nbytes),
        name="moe_down",
    )(te, nu, act, w_down, gs)


def _moe_combine_kernel(bm_ref, h_ref, locc_ref, g_ref, ys_ref, op_ref, os_ref, ybuf_ref, sem, *, tb, prompt_blocks):
    blk = pl.program_id(0)
    slot = blk % 2

    def copies(block, wait):
        s = block % 2
        _run_copies(bm_ref, block, ybuf_ref.at[s], ys_ref, sem.at[s], tb=tb, to_hbm=False, wait=wait)

    @pl.when(blk == 0)
    def _():
        ybuf_ref[...] = jnp.zeros_like(ybuf_ref)
        copies(blk, False)

    @pl.when(blk + 1 < pl.num_programs(0))
    def _():
        copies(blk + 1, False)

    r = lax.broadcasted_iota(I32, (tb, _compact_rows(tb)), 1).astype(F32)
    pick = jnp.where(r == locc_ref[:, 0:1], 1.0, jnp.where(r == locc_ref[:, 1:2], 1.0, 0.0)).astype(BF16)
    copies(blk, True)
    y = ybuf_ref[slot]
    y_hi = y.astype(BF16)
    y_lo = (y - y_hi.astype(F32)).astype(BF16)
    moe = jnp.dot(pick, y_hi, preferred_element_type=F32) + jnp.dot(pick, y_lo, preferred_element_type=F32)
    out = _rms(h_ref[...] + moe, g_ref[...])

    @pl.when(blk < prompt_blocks)
    def _():
        op_ref[...] = out

    @pl.when(blk == prompt_blocks)
    def _():
        os_ref[...] = out


def _moe_combine(h, locc, bmeta, ys, g, tb, n_prompt):
    m, d = h.shape
    prompt_blocks = n_prompt // tb
    any_spec = pl.BlockSpec(memory_space=pl.ANY)
    grid_spec = pltpu.PrefetchScalarGridSpec(
        num_scalar_prefetch=1,
        grid=(m // tb,),
        in_specs=[pl.BlockSpec((tb, d), lambda i, bm: (i, 0)),
                  pl.BlockSpec((tb, N_EXPERTS), lambda i, bm: (i, 0)),
                  pl.BlockSpec((1, d), lambda i, bm: (0, 0)),
                  any_spec],
        out_specs=[pl.BlockSpec((tb, d), lambda i, bm: (jnp.minimum(i, prompt_blocks - 1), 0)),
                   pl.BlockSpec((tb, d), lambda i, bm: (0, 0))],
        scratch_shapes=[pltpu.VMEM((2, _compact_rows(tb), d), F32), pltpu.SemaphoreType.DMA((2,))],
    )
    return pl.pallas_call(
        functools.partial(_moe_combine_kernel, tb=tb, prompt_blocks=prompt_blocks),
        grid_spec=grid_spec,
        out_shape=[jax.ShapeDtypeStruct((n_prompt, d), F32), jax.ShapeDtypeStruct((tb, d), F32)],
        compiler_params=_params(("arbitrary",), 16 * tb * d * 4),
        name="moe_combine",
    )(bmeta, h, locc, g.reshape(1, d), ys)


MOE_TOKEN_BLOCK = 256
MOE_ROW_TILE = 512


def _moe(h, a, lg_parts, n_valid, n_prompt, w_gu, w_down, g_final):
    m, d = h.shape
    tb, tm = MOE_TOKEN_BLOCK, MOE_ROW_TILE
    nb = m // tb
    lg3 = lg_parts[:, :2 * N_EXPERTS].reshape(nb, tb, 2 * N_EXPERTS).transpose(0, 2, 1)
    n_tiles = pl.cdiv(2 * n_valid + nb * N_EXPERTS * (SUBLANES - 1), tm) + N_EXPERTS
    loc, gate, bmeta, tmeta = _route(lg3, tm, n_valid)
    te = tmeta[0, :n_tiles]
    nu = tmeta[1, 0:1]
    bm = bmeta[:, :, :3].reshape(-1)
    locc = loc.transpose(0, 2, 1).reshape(m, N_EXPERTS)
    xs, gs = _moe_scatter(a, loc, gate, bm, n_tiles * tm, tb)
    act = _moe_gu(xs, w_gu, te, nu, tm, D_FF_EXPERT // 2)
    ys = _moe_down(act, w_down, gs, te, nu, tm, d)
    return _moe_combine(h, locc, bm, ys, g_final, tb, n_prompt)


def _pad_lanes(v):
    return jnp.pad(v.reshape(1, -1), ((0, 0), (0, LANES - v.shape[-1])))


def kernel(x_prompt, x_sample, state_ssm, state_conv, norm_mix, norm_ffn, norm_final, ssd_w_in, ssd_conv_w,
           ssd_conv_b, ssd_dt_bias, ssd_a_log, ssd_d, ssd_gnorm, ssd_w_out, sgu_w_in, sgu_b_in, sgu_ln_g,
           sgu_ln_b, sgu_w_s, sgu_b_s, sgu_w_out, ffn_w_gu, ffn_w_down, moe_w_router, moe_w_gu, moe_w_down):
    batch, seq, d = x_prompt.shape
    n_s = x_sample.shape[0]
    mp = batch * seq
    xp = x_prompt.reshape(mp, d)
    xs = x_sample.reshape(n_s, d)

    w_in = ssd_w_in[0]
    w_dt = jnp.pad(w_in[:, D_INNER + CONV_DIM:], ((0, 0), (0, LANES - SSD_HEADS)))
    dtb = _pad_lanes(ssd_dt_bias[0])
    alog = _pad_lanes(ssd_a_log[0])
    d_rep = jnp.repeat(ssd_d[0], SSD_HEADDIM).reshape(1, D_INNER)
    gn_w = ssd_gnorm[0].reshape(1, D_INNER)
    conv_w = ssd_conv_w[0]
    conv_b = ssd_conv_b[0].reshape(1, CONV_DIM)
    ln_g = sgu_ln_g[0].reshape(1, D_SGU)
    ln_b = sgu_ln_b[0].reshape(1, D_SGU)
    bs_t = jnp.pad(sgu_b_s[0].T, ((0, 0), (0, LANES - SGU_HEADS)))
    w0_rep = jnp.repeat(sgu_w_s[0][:, 0, 0], SGU_HEAD_DIM).reshape(1, D_SGU)
    b0_rep = jnp.repeat(sgu_b_s[0][:, 0], SGU_HEAD_DIM).reshape(1, D_SGU)
    wr_hi = moe_w_router[0].astype(BF16)
    wr_lo = (moe_w_router[0] - wr_hi.astype(F32)).astype(BF16)
    w3 = jnp.concatenate([wr_hi, wr_lo, jnp.zeros((d, LANES - 2 * N_EXPERTS), BF16)], axis=1)

    def after_mixer0(y, xa, z, resid, tm):
        h1, a1 = _ssd_out(y, xa, z, d_rep, gn_w, ssd_w_out[0], resid, norm_ffn[0], min(tm, 256))
        act = _mm_swiglu(a1, ffn_w_gu[0], D_FF_DENSE, tm, 1408, "ffn_gate_up")
        return _mm_resnorm(act, ffn_w_down[0], h1, norm_mix[1], tm, "ffn_down")

    a0p = _rmsnorm(xp, norm_mix[0], 512, BF16)
    zp = _mm_plain(a0p, w_in, 0, D_INNER, 1024, 1024, BF16, "ssd_in_z")
    xap, tails = _mm_conv(a0p, w_in, D_INNER, conv_w, conv_b, 1024, 1024, seq, "ssd_in_conv")
    dtrp = _mm_plain(a0p, w_dt, 0, LANES, 1024, LANES, F32, "ssd_in_dt")
    yp, ssm_p = _ssd_prompt(xap, dtrp, dtb, alog, batch, seq)
    conv_p = tails[:, SUBLANES - (CONV_W - 1):, :]
    h2p, a2p = after_mixer0(yp, xap, zp, xp, 512)

    a0s = _rmsnorm(xs, norm_mix[0], n_s, BF16)
    zs = _mm_plain(a0s, w_in, 0, D_INNER, n_s, 1024, BF16, "ssd_in_z")
    xbcs = _mm_plain(a0s, w_in, D_INNER, CONV_DIM, n_s, 1024, F32, "ssd_in_xbc")
    dtrs = _mm_plain(a0s, w_dt, 0, LANES, n_s, LANES, F32, "ssd_in_dt")
    conv_s, xas, xdts, das, bsts, csts = _ssd_sample_pre(xbcs, state_conv[0].reshape(n_s, 3 * CONV_DIM), dtrs,
                                                         conv_w, conv_b, dtb, alog)
    stack_rows = SSD_GROUPS * TERM_ROWS
    ssm_s, ys3 = _ssd_sample_state(state_ssm[0].reshape(n_s, D_INNER, D_STATE), xdts.reshape(n_s, 1, D_INNER),
                                   das[:, :SSD_HEADS], bsts.reshape(n_s, stack_rows, D_STATE),
                                   csts.reshape(n_s, stack_rows, D_STATE))
    h2s, a2s = after_mixer0(ys3.reshape(n_s, D_INNER), xas, zs, xs, n_s)

    uvp = _mm_bias_gelu(a2p, sgu_w_in[0], sgu_b_in[0], 1024, 1024, BF16, "sgu_in")
    uvs = _mm_bias_gelu(a2s, sgu_w_in[0], sgu_b_in[0], n_s, 1024, F32, "sgu_in")
    ss, v_s = _sgu_sample(uvs, ln_g, ln_b, w0_rep, b0_rep)
    tm_out = 512
    pad_rows = ((0, tm_out - n_s), (0, 0))
    h3, a3, lg = _sgu_out(uvp, jnp.pad(ss, pad_rows), ln_g, ln_b, sgu_w_s[0], bs_t, sgu_w_out[0], h2p,
                          jnp.pad(h2s, pad_rows), norm_ffn[1], w3, tm_out)

    y_prompt, y_tail = _moe(h3, a3, lg, mp + n_s, mp, moe_w_gu[0], moe_w_down[0], norm_final)
    y_sample = y_tail[:n_s]

    return (y_prompt.reshape(batch, seq, d),
            y_sample.reshape(n_s, 1, d),
            ssm_p[None],
            conv_p[None],
            ssm_s.reshape(1, n_s, SSD_HEADS, SSD_HEADDIM, D_STATE),
            conv_s.reshape(1, n_s, CONV_W - 1, CONV_DIM),
            v_s.reshape(1, n_s, 1, D_SGU))
```

```python
import functools

import jax
import jax.numpy as jnp
from jax import lax
from jax.experimental import pallas as pl
from jax.experimental.pallas import tpu as pltpu

F32 = jnp.float32
BF16 = jnp.bfloat16
I32 = jnp.int32
HIGHEST = lax.Precision.HIGHEST

D_MODEL = 1024
D_INNER = 2048
SSD_HEADS = 32
SSD_HEADDIM = 64
SSD_GROUPS = 4
D_STATE = 128
CONV_W = 4
CONV_DIM = D_INNER + 2 * SSD_GROUPS * D_STATE
CHUNK = 128
D_SGU = 2048
SGU_HEADS = 8
SGU_HEAD_DIM = D_SGU // SGU_HEADS
D_FF_DENSE = 2816
N_EXPERTS = 8
D_FF_EXPERT = 3584
EPS = 1e-5

MXU_COLS = 256
ROW_SUB = 256
LANES = 128
SUBLANES = 8
VMEM_CAP = 60000 * 1024
VMEM_FLOOR = 32 * 1024 * 1024


def _vmem_limit(nbytes):
    return int(min(max(nbytes * 5 // 4 + (4 << 20), VMEM_FLOOR), VMEM_CAP))


def _params(sem, nbytes):
    return pltpu.CompilerParams(dimension_semantics=sem, vmem_limit_bytes=_vmem_limit(nbytes))


LOG2E = 1.4426950408889634


def _silu(x):
    return x / (1.0 + jnp.exp2(x * (-LOG2E)))


def _rms(x, g):
    return x * lax.rsqrt(jnp.mean(x * x, axis=-1, keepdims=True) + EPS) * g


def _bdot(a, b):
    return jnp.dot(a.astype(BF16), b.astype(BF16), preferred_element_type=F32)


def _mm_plain_kernel(x_ref, w_ref, o_ref, wb_ref):
    @pl.when(pl.program_id(1) == 0)
    def _():
        wb_ref[...] = w_ref[...].astype(BF16)

    o_ref[...] = _bdot(x_ref[...], wb_ref[...]).astype(o_ref.dtype)


def _mm_bias_gelu_kernel(x_ref, w_ref, b_ref, o_ref, wb_ref):
    @pl.when(pl.program_id(1) == 0)
    def _():
        wb_ref[...] = w_ref[...].astype(BF16)

    y = _bdot(x_ref[...], wb_ref[...]) + b_ref[...]
    o_ref[...] = (0.5 * y * (1.0 + lax.erf(y * (2.0 ** -0.5)))).astype(o_ref.dtype)


def _mm_swiglu_kernel(x_ref, wg_ref, wu_ref, o_ref, wgb_ref, wub_ref):
    @pl.when(pl.program_id(1) == 0)
    def _():
        wgb_ref[...] = wg_ref[...].astype(BF16)
        wub_ref[...] = wu_ref[...].astype(BF16)

    x = x_ref[...].astype(BF16)
    tn = o_ref.shape[1]
    for c0 in range(0, tn, MXU_COLS):
        cols = slice(c0, min(c0 + MXU_COLS, tn))
        g = jnp.dot(x, wgb_ref[:, cols], preferred_element_type=F32)
        u = jnp.dot(x, wub_ref[:, cols], preferred_element_type=F32)
        o_ref[:, cols] = (_silu(g) * u).astype(o_ref.dtype)


def _wspec(k, tn, col_block0):
    return pl.BlockSpec((k, tn), lambda j, i: (0, j + col_block0), pipeline_mode=pl.Buffered(1))


def _mm_plain(x, w, col0, n, tm, tn, out_dtype, name):
    m, k = x.shape
    nbytes = k * tn * 6 + 2 * tm * k * x.dtype.itemsize + 2 * tm * tn * 4
    return pl.pallas_call(
        _mm_plain_kernel,
        grid=(n // tn, m // tm),
        in_specs=[pl.BlockSpec((tm, k), lambda j, i: (i, 0)), _wspec(k, tn, col0 // tn)],
        out_specs=pl.BlockSpec((tm, tn), lambda j, i: (i, j)),
        out_shape=jax.ShapeDtypeStruct((m, n), out_dtype),
        scratch_shapes=[pltpu.VMEM((k, tn), BF16)],
        compiler_params=_params(("arbitrary", "arbitrary"), nbytes),
        name=name,
    )(x, w)


def _norm_in_proj_kernel(x_ref, g_ref, wz_ref, wdt_ref, a_ref, z_ref, dtr_ref, wzb_ref, wdtb_ref):
    @pl.when(pl.program_id(0) == 0)
    def _():
        wzb_ref[...] = wz_ref[...].astype(BF16)
        wdtb_ref[...] = wdt_ref[...].astype(BF16)

    a = _rms(x_ref[...], g_ref[...]).astype(BF16)
    a_ref[...] = a
    z_ref[...] = jnp.dot(a, wzb_ref[...], preferred_element_type=F32).astype(z_ref.dtype)
    dtr_ref[...] = jnp.dot(a, wdtb_ref[...], preferred_element_type=F32)


def _norm_in_proj(x, g, w_in, w_dt, tm):
    m, k = x.shape
    n = D_INNER
    row = lambda i: (i, 0)
    fixed = lambda i: (0, 0)
    nbytes = k * n * 6 + k * LANES * 6 + 2 * tm * k * 4 + 2 * tm * k * 2 + 2 * tm * n * 2 + 2 * tm * n * 4
    return pl.pallas_call(
        _norm_in_proj_kernel,
        grid=(m // tm,),
        in_specs=[pl.BlockSpec((tm, k), row), pl.BlockSpec((1, k), fixed),
                  pl.BlockSpec((k, n), fixed, pipeline_mode=pl.Buffered(1)),
                  pl.BlockSpec((k, LANES), fixed)],
        out_specs=[pl.BlockSpec((tm, k), row), pl.BlockSpec((tm, n), row), pl.BlockSpec((tm, LANES), row)],
        out_shape=[jax.ShapeDtypeStruct((m, k), BF16), jax.ShapeDtypeStruct((m, n), BF16),
                   jax.ShapeDtypeStruct((m, LANES), F32)],
        scratch_shapes=[pltpu.VMEM((k, n), BF16), pltpu.VMEM((k, LANES), BF16)],
        compiler_params=_params(("arbitrary",), nbytes),
        name="norm_in_proj",
    )(x, g.reshape(1, k), w_in, w_dt)


def _mm_bias_gelu(x, w, b, tm, tn, out_dtype, name):
    m, k = x.shape
    n = w.shape[1]
    nbytes = k * tn * 6 + 2 * tm * k * x.dtype.itemsize + 2 * tm * tn * 4
    return pl.pallas_call(
        _mm_bias_gelu_kernel,
        grid=(n // tn, m // tm),
        in_specs=[pl.BlockSpec((tm, k), lambda j, i: (i, 0)), _wspec(k, tn, 0),
                  pl.BlockSpec((1, tn), lambda j, i: (0, j))],
        out_specs=pl.BlockSpec((tm, tn), lambda j, i: (i, j)),
        out_shape=jax.ShapeDtypeStruct((m, n), out_dtype),
        scratch_shapes=[pltpu.VMEM((k, tn), BF16)],
        compiler_params=_params(("arbitrary", "arbitrary"), nbytes),
        name=name,
    )(x, w, b.reshape(1, n))


def _mm_swiglu(x, w_gu, d_ff, tm, tn, name):
    m, k = x.shape
    nbytes = 2 * k * tn * 6 + 2 * tm * k * x.dtype.itemsize + 2 * tm * tn * 2 + 3 * tm * tn * 4
    return pl.pallas_call(
        _mm_swiglu_kernel,
        grid=(d_ff // tn, m // tm),
        in_specs=[pl.BlockSpec((tm, k), lambda j, i: (i, 0)), _wspec(k, tn, 0), _wspec(k, tn, d_ff // tn)],
        out_specs=pl.BlockSpec((tm, tn), lambda j, i: (i, j)),
        out_shape=jax.ShapeDtypeStruct((m, d_ff), BF16),
        scratch_shapes=[pltpu.VMEM((k, tn), BF16), pltpu.VMEM((k, tn), BF16)],
        compiler_params=_params(("arbitrary", "arbitrary"), nbytes),
        name=name,
    )(x, w_gu, w_gu)


def _mm_resnorm_kernel(x_ref, w_ref, r_ref, g_ref, h_ref, a_ref, wb_ref):
    @pl.when(pl.program_id(0) == 0)
    def _():
        wb_ref[...] = w_ref[...].astype(BF16)

    h = r_ref[...] + _bdot(x_ref[...], wb_ref[...])
    h_ref[...] = h
    a_ref[...] = _rms(h, g_ref[...]).astype(a_ref.dtype)


def _mm_resnorm(x, w, resid, g, tm, name):
    m, k = x.shape
    d = w.shape[1]
    nbytes = k * d * 6 + 2 * tm * k * 2 + 8 * tm * d * 4
    return pl.pallas_call(
        _mm_resnorm_kernel,
        grid=(m // tm,),
        in_specs=[pl.BlockSpec((tm, k), lambda i: (i, 0)),
                  pl.BlockSpec((k, d), lambda i: (0, 0), pipeline_mode=pl.Buffered(1)),
                  pl.BlockSpec((tm, d), lambda i: (i, 0)),
                  pl.BlockSpec((1, d), lambda i: (0, 0))],
        out_specs=[pl.BlockSpec((tm, d), lambda i: (i, 0)), pl.BlockSpec((tm, d), lambda i: (i, 0))],
        out_shape=[jax.ShapeDtypeStruct((m, d), F32), jax.ShapeDtypeStruct((m, d), BF16)],
        scratch_shapes=[pltpu.VMEM((k, d), BF16)],
        compiler_params=_params(("arbitrary",), nbytes),
        name=name,
    )(x, w, resid, g.reshape(1, d))


def _sgu_out_kernel(u_ref, v_ref, xt_ref, lng_ref, lnb_ref, ws_ref, bst_ref, w_ref, rp_ref, rt_ref, g_ref, w3_ref,
                    h_ref, a_ref, lg_ref, wb_ref, wsb_ref, y_ref, *, prompt_tiles):
    i = pl.program_id(0)
    q = CHUNK

    @pl.when(i == 0)
    def _():
        wb_ref[...] = w_ref[...].astype(BF16)
        causal = lax.broadcasted_iota(I32, (q, q), 0) >= lax.broadcasted_iota(I32, (q, q), 1)
        for g in range(SGU_HEADS):
            wsb_ref[g] = jnp.where(causal, ws_ref[g], 0.0).astype(BF16)

    def project(x_ref, r_ref, rows):
        h = r_ref[rows, :] + jnp.dot(x_ref[rows, :], wb_ref[...], preferred_element_type=F32)
        h_ref[rows, :] = h
        a = _rms(h, g_ref[...])
        a_hi = a.astype(BF16)
        a_lo = (a - a_hi.astype(F32)).astype(BF16)
        a_ref[rows, :] = a_hi
        lg_ref[rows, :] = (jnp.dot(a_hi, w3_ref[...], preferred_element_type=F32)
                           + jnp.dot(a_lo, w3_ref[...], preferred_element_type=F32))

    chunks = [slice(c * q, (c + 1) * q) for c in range(u_ref.shape[0] // q)]

    @pl.when(i < prompt_tiles)
    def _():
        for rows in chunks:
            vn = _layernorm(v_ref[rows, :].astype(F32), lng_ref[...], lnb_ref[...]).astype(BF16)
            for g in range(SGU_HEADS):
                cols = slice(g * SGU_HEAD_DIM, (g + 1) * SGU_HEAD_DIM)
                s = jnp.dot(wsb_ref[g], vn[:, cols], preferred_element_type=F32) + bst_ref[:, g:g + 1]
                y_ref[rows, cols] = (u_ref[rows, cols].astype(F32) * s).astype(BF16)
            project(y_ref, rp_ref, rows)

    @pl.when(i == prompt_tiles)
    def _():
        for rows in chunks:
            project(xt_ref, rt_ref, rows)


def _sgu_out(uv, xt, ln_g, ln_b, w_s, bs_t, w, rp, rt, g, w3, tm):
    mp = uv.shape[0]
    k, d = w.shape
    prompt_tiles = mp // tm
    last = prompt_tiles - 1
    rows = mp + tm
    prow = lambda i: (jnp.minimum(i, last), 0)
    fixed = lambda i: (0, 0)
    nbytes = k * d * 6 + 8 * tm * k * 2 + 12 * tm * d * 4 + 8 * CHUNK * k * 4
    return pl.pallas_call(
        functools.partial(_sgu_out_kernel, prompt_tiles=prompt_tiles),
        grid=(prompt_tiles + 1,),
        in_specs=[pl.BlockSpec((tm, k), prow),
                  pl.BlockSpec((tm, k), lambda i: (jnp.minimum(i, last), 1)),
                  pl.BlockSpec((tm, k), fixed),
                  pl.BlockSpec((1, k), fixed), pl.BlockSpec((1, k), fixed),
                  pl.BlockSpec((SGU_HEADS, CHUNK, CHUNK), lambda i: (0, 0, 0)),
                  pl.BlockSpec((CHUNK, LANES), fixed),
                  pl.BlockSpec((k, d), fixed, pipeline_mode=pl.Buffered(1)),
                  pl.BlockSpec((tm, d), prow),
                  pl.BlockSpec((tm, d), fixed),
                  pl.BlockSpec((1, d), fixed),
                  pl.BlockSpec((d, LANES), fixed)],
        out_specs=[pl.BlockSpec((tm, d), lambda i: (i, 0)), pl.BlockSpec((tm, d), lambda i: (i, 0)),
                   pl.BlockSpec((tm, LANES), lambda i: (i, 0))],
        out_shape=[jax.ShapeDtypeStruct((rows, d), F32), jax.ShapeDtypeStruct((rows, d), BF16),
                   jax.ShapeDtypeStruct((rows, LANES), F32)],
        scratch_shapes=[pltpu.VMEM((k, d), BF16), pltpu.VMEM((SGU_HEADS, CHUNK, CHUNK), BF16),
                        pltpu.VMEM((tm, k), BF16)],
        compiler_params=_params(("arbitrary",), nbytes),
        name="sgu_out",
    )(uv, uv, xt, ln_g, ln_b, w_s, bs_t, w, rp, rt, g.reshape(1, d), w3)


def _gate_and_groupnorm(y, xs, z, d_rep, gn_w):
    yg = (y + d_rep * xs) * _silu(z)
    gw = D_INNER // SSD_GROUPS
    outs = []
    for g in range(SSD_GROUPS):
        blk = yg[:, g * gw:(g + 1) * gw]
        outs.append(blk * lax.rsqrt(jnp.mean(blk * blk, axis=-1, keepdims=True) + EPS))
    return jnp.concatenate(outs, axis=1) * gn_w


def _softplus(x):
    return jnp.maximum(x, 0.0) + jnp.log(1.0 + jnp.exp(-jnp.abs(x)))


SSD_SEQS = 2


def _ssd_prompt_kernel(xbc_ref, z_ref, dtr_ref, cw_ref, cb_ref, dtb_ref, alog_ref, drep_ref, gnw_ref,
                       yg_ref, ssm_ref, conv_ref, ext_ref, st_ref, y_ref, xa_ref):
    for n in range(SSD_SEQS):
        _ssd_chunk(xbc_ref.at[n], z_ref.at[n], dtr_ref.at[n], cw_ref, cb_ref, dtb_ref, alog_ref, drep_ref, gnw_ref,
                   yg_ref.at[n], ssm_ref.at[n], conv_ref.at[n], ext_ref.at[n], st_ref.at[n], y_ref.at[n],
                   xa_ref.at[n])


def _ssd_chunk(xbc_ref, z_ref, dtr_ref, cw_ref, cb_ref, dtb_ref, alog_ref, drep_ref, gnw_ref,
               yg_ref, ssm_ref, conv_ref, ext_ref, st_ref, y_ref, xa_ref):
    c = pl.program_id(1)
    nc = pl.num_programs(1)
    q = CHUNK

    @pl.when(c == 0)
    def _():
        ext_ref[0:SUBLANES, :] = jnp.zeros((SUBLANES, CONV_DIM), F32)
        st_ref[...] = jnp.zeros_like(st_ref)

    ext_ref[SUBLANES:SUBLANES + q, :] = xbc_ref[...]
    acc = cb_ref[...] + cw_ref[CONV_W - 1:CONV_W, :] * ext_ref[SUBLANES:SUBLANES + q, :]
    for k in range(CONV_W - 1):
        lo = SUBLANES - (CONV_W - 1) + k
        acc = acc + cw_ref[k:k + 1, :] * ext_ref[lo:lo + q, :]
    xa_ref[...] = _silu(acc)
    tail = ext_ref[q:q + SUBLANES, :]
    ext_ref[0:SUBLANES, :] = tail

    dt = _softplus(dtr_ref[...] + dtb_ref[...])
    a_neg = -jnp.exp(alog_ref[...])
    causal = lax.broadcasted_iota(I32, (q, q), 0) >= lax.broadcasted_iota(I32, (q, q), 1)
    tril = jnp.where(causal, 1.0, 0.0)
    cs = jnp.dot(tril, dt * a_neg, precision=HIGHEST, preferred_element_type=F32) * LOG2E
    cs_t = cs.T
    dt_t = dt.T
    rowp = cs_t - jnp.log2(dt_t)
    w_t = dt_t * jnp.exp2(cs_t[:, q - 1:q] - cs_t)
    first = lax.broadcasted_iota(I32, (q, 2 * SSD_HEADDIM), 1) < SSD_HEADDIM

    def per_head(v):
        zero = jnp.zeros_like(v)
        return jnp.concatenate([jnp.where(first, v, zero), jnp.where(first, zero, v)], axis=0)

    for g in range(SSD_GROUPS):
        b_g = xa_ref[:, D_INNER + g * D_STATE:D_INNER + (g + 1) * D_STATE]
        c_g = xa_ref[:, D_INNER + (SSD_GROUPS + g) * D_STATE:D_INNER + (SSD_GROUPS + g + 1) * D_STATE]
        b_gt = b_g.T
        cb = _bdot(c_g, b_gt)
        for pr in range(4):
            hp = g * 4 + pr
            cols = slice(hp * 2 * SSD_HEADDIM, (hp + 1) * 2 * SSD_HEADDIM)
            x2 = per_head(xa_ref[:, cols].astype(BF16))
            st_p = st_ref[:, cols]
            st2 = per_head(st_p.astype(BF16))
            ms, cs_scaled, bws, cds = [], [], [], []
            for s in range(2):
                h = 2 * hp + s
                colb = jnp.broadcast_to(cs[:, h:h + 1], (q, q))
                rowb = jnp.broadcast_to(rowp[h:h + 1, :], (q, q))
                ecol = jnp.exp2(colb)
                ms.append((cb * jnp.where(causal, jnp.exp2(colb - rowb), 0.0)).astype(BF16))
                cs_scaled.append((c_g * ecol).astype(BF16))
                bws.append((b_gt * w_t[h:h + 1, :]).astype(BF16))
                cds.append(ecol[q - 1:q, :])
            y_ref[:, cols] = jnp.dot(jnp.concatenate(ms + cs_scaled, axis=1), jnp.concatenate([x2, st2], axis=0),
                                     preferred_element_type=F32)
            cd = jnp.where(first[0:1, :], cds[0], cds[1])
            st_ref[:, cols] = st_p * cd + jnp.dot(jnp.concatenate(bws, axis=1), x2, preferred_element_type=F32)

    out = _gate_and_groupnorm(y_ref[...], xa_ref[:, :D_INNER], z_ref[...].astype(F32), drep_ref[...],
                              gnw_ref[...])
    yg_ref[...] = out.astype(yg_ref.dtype)

    @pl.when(c == nc - 1)
    def _():
        conv_ref[...] = tail[SUBLANES - (CONV_W - 1):, :]
        for hp in range(SSD_HEADS // 2):
            blk = st_ref[:, hp * 2 * SSD_HEADDIM:(hp + 1) * 2 * SSD_HEADDIM].T
            ssm_ref[2 * hp:2 * hp + 2] = blk.reshape(2, SSD_HEADDIM, D_STATE)


def _ssd_prompt(xbc, z, dtr, conv_w, conv_b, dtb, alog, d_rep, gn_w):
    batch, seq, _ = xbc.shape
    nb = SSD_SEQS
    blk = lambda b, c: (b, c, 0)
    fixed = lambda b, c: (0, 0)
    nbytes = nb * (2 * CHUNK * (CONV_DIM * 4 + D_INNER * 4 + LANES * 4) + (CHUNK + SUBLANES) * CONV_DIM * 4
                   + 4 * D_STATE * D_INNER * 4 + 12 * CHUNK * CONV_DIM * 4)
    return pl.pallas_call(
        _ssd_prompt_kernel,
        grid=(batch // nb, seq // CHUNK),
        in_specs=[pl.BlockSpec((nb, CHUNK, CONV_DIM), blk), pl.BlockSpec((nb, CHUNK, D_INNER), blk),
                  pl.BlockSpec((nb, CHUNK, LANES), blk),
                  pl.BlockSpec((CONV_W, CONV_DIM), fixed), pl.BlockSpec((1, CONV_DIM), fixed),
                  pl.BlockSpec((1, LANES), fixed), pl.BlockSpec((1, LANES), fixed),
                  pl.BlockSpec((1, D_INNER), fixed), pl.BlockSpec((1, D_INNER), fixed)],
        out_specs=[pl.BlockSpec((nb, CHUNK, D_INNER), blk),
                   pl.BlockSpec((nb, SSD_HEADS, SSD_HEADDIM, D_STATE), lambda b, c: (b, 0, 0, 0)),
                   pl.BlockSpec((nb, CONV_W - 1, CONV_DIM), lambda b, c: (b, 0, 0))],
        out_shape=[jax.ShapeDtypeStruct((batch, seq, D_INNER), BF16),
                   jax.ShapeDtypeStruct((batch, SSD_HEADS, SSD_HEADDIM, D_STATE), F32),
                   jax.ShapeDtypeStruct((batch, CONV_W - 1, CONV_DIM), F32)],
        scratch_shapes=[pltpu.VMEM((nb, CHUNK + SUBLANES, CONV_DIM), F32), pltpu.VMEM((nb, D_STATE, D_INNER), F32),
                        pltpu.VMEM((nb, CHUNK, D_INNER), F32), pltpu.VMEM((nb, CHUNK, CONV_DIM), F32)],
        compiler_params=_params(("arbitrary", "arbitrary"), nbytes),
        name="ssd_prompt",
    )(xbc, z, dtr, conv_w, conv_b, dtb, alog, d_rep, gn_w)


def _ssd_out_kernel(y_ref, xs_ref, z_ref, drep_ref, gnw_ref, w_ref, r_ref, g_ref, h_ref, a_ref, wb_ref):
    @pl.when(pl.program_id(0) == 0)
    def _():
        wb_ref[...] = w_ref[...].astype(BF16)

    sub = min(ROW_SUB // 2, y_ref.shape[0])
    for s in range(y_ref.shape[0] // sub):
        rows = slice(s * sub, (s + 1) * sub)
        yg = _gate_and_groupnorm(y_ref[rows, :], xs_ref[rows, :], z_ref[rows, :].astype(F32), drep_ref[...],
                                 gnw_ref[...])
        h = r_ref[rows, :] + _bdot(yg, wb_ref[...])
        h_ref[rows, :] = h
        a_ref[rows, :] = _rms(h, g_ref[...]).astype(a_ref.dtype)


def _ssd_out(y, xa, z, d_rep, gn_w, w, resid, g, tm):
    m, k = y.shape
    d = w.shape[1]
    row = lambda i: (i, 0)
    fixed = lambda i: (0, 0)
    nbytes = k * d * 6 + 2 * tm * k * 10 + 8 * tm * d * 4 + 6 * tm * k * 4
    return pl.pallas_call(
        _ssd_out_kernel,
        grid=(m // tm,),
        in_specs=[pl.BlockSpec((tm, k), row), pl.BlockSpec((tm, k), row), pl.BlockSpec((tm, k), row),
                  pl.BlockSpec((1, k), fixed), pl.BlockSpec((1, k), fixed),
                  pl.BlockSpec((k, d), fixed, pipeline_mode=pl.Buffered(1)),
                  pl.BlockSpec((tm, d), row), pl.BlockSpec((1, d), fixed)],
        out_specs=[pl.BlockSpec((tm, d), row), pl.BlockSpec((tm, d), row)],
        out_shape=[jax.ShapeDtypeStruct((m, d), F32), jax.ShapeDtypeStruct((m, d), BF16)],
        scratch_shapes=[pltpu.VMEM((k, d), BF16)],
        compiler_params=_params(("arbitrary",), nbytes),
        name="ssd_out",
    )(y, xa, z, d_rep, gn_w, w, resid, g.reshape(1, d))


def _ssd_sample_pre_kernel(xbc_ref, conv_ref, dtr_ref, cw_ref, cb_ref, dtb_ref, alog_ref,
                           nconv_ref, xa_ref, xdt_ref, da_ref, bst_ref, cst_ref):
    xn = xbc_ref[...]
    acc = cb_ref[...] + cw_ref[3:4, :] * xn
    for k in range(CONV_W - 1):
        acc = acc + cw_ref[k:k + 1, :] * conv_ref[:, k * CONV_DIM:(k + 1) * CONV_DIM]
    xa = _silu(acc)
    xa_ref[...] = xa
    nconv_ref[:, 0:CONV_DIM] = conv_ref[:, CONV_DIM:2 * CONV_DIM]
    nconv_ref[:, CONV_DIM:2 * CONV_DIM] = conv_ref[:, 2 * CONV_DIM:3 * CONV_DIM]
    nconv_ref[:, 2 * CONV_DIM:3 * CONV_DIM] = xn

    dt = _softplus(dtr_ref[...] + dtb_ref[...])
    a_neg = -jnp.exp(alog_ref[...])
    hh = lax.broadcasted_iota(I32, (LANES, D_INNER), 0)
    cc = lax.broadcasted_iota(I32, (LANES, D_INNER), 1)
    expand = jnp.where(lax.shift_right_logical(cc, 6) == hh, 1.0, 0.0)
    dt_rep = jnp.dot(dt, expand, precision=HIGHEST, preferred_element_type=F32)
    xdt_ref[...] = xa[:, :D_INNER] * dt_rep
    da_ref[...] = jnp.exp(dt * a_neg)

    n = xn.shape[0]
    gs = SSD_GROUPS * D_STATE
    b1, b2, b3 = _split3(xa[:, D_INNER:D_INNER + gs])
    cm = xa[:, D_INNER + gs:]
    c1 = cm.astype(BF16)
    c2 = (cm - c1.astype(F32)).astype(BF16)
    zero = jnp.zeros((n, D_STATE), BF16)
    b_rows, c_rows = [], []
    for g in range(SSD_GROUPS):
        sl = slice(g * D_STATE, (g + 1) * D_STATE)
        b_rows += [b1[:, sl], b2[:, sl], b1[:, sl], b3[:, sl], b2[:, sl], b1[:, sl]] + [zero] * (TERM_ROWS - 6)
        c_rows += [c1[:, sl], c2[:, sl]] + [zero] * (TERM_ROWS - 2)
    bst_ref[...] = jnp.concatenate(b_rows, axis=1)
    cst_ref[...] = jnp.concatenate(c_rows, axis=1)


TERM_ROWS = 16


def _split3(v):
    v1 = v.astype(BF16)
    r1 = v - v1.astype(F32)
    v2 = r1.astype(BF16)
    v3 = (r1 - v2.astype(F32)).astype(BF16)
    return v1, v2, v3


def _ssd_sample_pre(xbc, conv_flat, dtr, conv_w, conv_b, dtb, alog):
    n = xbc.shape[0]
    full = lambda shape: pl.BlockSpec(shape, lambda i: (0,) * len(shape))
    stack = SSD_GROUPS * TERM_ROWS * D_STATE
    return pl.pallas_call(
        _ssd_sample_pre_kernel,
        grid=(1,),
        in_specs=[full((n, CONV_DIM)), full((n, 3 * CONV_DIM)), full((n, LANES)), full((CONV_W, CONV_DIM)),
                  full((1, CONV_DIM)), full((1, LANES)), full((1, LANES))],
        out_specs=[full((n, 3 * CONV_DIM)), full((n, CONV_DIM)), full((n, D_INNER)), full((n, LANES)),
                   full((n, stack)), full((n, stack))],
        out_shape=[jax.ShapeDtypeStruct((n, 3 * CONV_DIM), F32), jax.ShapeDtypeStruct((n, CONV_DIM), F32),
                   jax.ShapeDtypeStruct((n, D_INNER), F32), jax.ShapeDtypeStruct((n, LANES), F32),
                   jax.ShapeDtypeStruct((n, stack), BF16), jax.ShapeDtypeStruct((n, stack), BF16)],
        compiler_params=_params(("arbitrary",), 32 * n * CONV_DIM * 4),
        name="ssd_sample_pre",
    )(xbc, conv_flat, dtr, conv_w, conv_b, dtb, alog)


STATE_SAMPLES = 4


def _ssd_sample_state_kernel(da_ref, st_ref, xdt_ref, bst_ref, cst_ref, nst_ref, y_ref):
    first = pl.program_id(0) * STATE_SAMPLES
    rows_g = D_INNER // SSD_GROUPS
    sub = lax.broadcasted_iota(I32, (TERM_ROWS, D_INNER), 0)
    for s in range(STATE_SAMPLES):
        x1, x2, x3 = [t.astype(F32) for t in _split3(xdt_ref[s])]
        terms = jnp.where(sub == 0, x1, jnp.where(sub == 1, x1, jnp.where(sub == 2, x2, jnp.where(
            sub == 3, x1, jnp.where(sub == 4, x2, jnp.where(sub == 5, x3, 0.0)))))).astype(BF16)
        ys = []
        for g in range(SSD_GROUPS):
            tile = slice(g * TERM_ROWS, (g + 1) * TERM_ROWS)
            upd = lax.dot_general(terms[:, g * rows_g:(g + 1) * rows_g], bst_ref[s, tile, :],
                                  (((0,), (0,)), ((), ())), preferred_element_type=F32)
            halves = []
            for hh in range(rows_g // SSD_HEADDIM):
                head = g * (rows_g // SSD_HEADDIM) + hh
                rows = slice(head * SSD_HEADDIM, (head + 1) * SSD_HEADDIM)
                h_new = (da_ref[first + s, head] * st_ref[s, rows, :]
                         + upd[hh * SSD_HEADDIM:(hh + 1) * SSD_HEADDIM, :])
                nst_ref[s, rows, :] = h_new
                halves.append(h_new)
            h_g = jnp.concatenate(halves, axis=0)
            h_hi = h_g.astype(BF16)
            h_lo = (h_g - h_hi.astype(F32)).astype(BF16)
            nt = (((1,), (1,)), ((), ()))
            o_hi = lax.dot_general(cst_ref[s, tile, :], h_hi, nt, preferred_element_type=F32)
            o_lo = lax.dot_general(cst_ref[s, tile, :], h_lo, nt, preferred_element_type=F32)
            ys.append(o_hi[0:1, :] + o_hi[1:2, :] + o_lo[0:1, :])
        y_ref[s] = jnp.concatenate(ys, axis=1)


def _ssd_sample_state(state, xdt, da, bst, cst):
    n = state.shape[0]
    ns = STATE_SAMPLES
    stack_rows = SSD_GROUPS * TERM_ROWS
    return pl.pallas_call(
        _ssd_sample_state_kernel,
        grid=(n // ns,),
        in_specs=[pl.BlockSpec(memory_space=pltpu.SMEM),
                  pl.BlockSpec((ns, D_INNER, D_STATE), lambda i: (i, 0, 0)),
                  pl.BlockSpec((ns, 1, D_INNER), lambda i: (i, 0, 0)),
                  pl.BlockSpec((ns, stack_rows, D_STATE), lambda i: (i, 0, 0)),
                  pl.BlockSpec((ns, stack_rows, D_STATE), lambda i: (i, 0, 0))],
        out_specs=[pl.BlockSpec((ns, D_INNER, D_STATE), lambda i: (i, 0, 0)),
                   pl.BlockSpec((ns, 1, D_INNER), lambda i: (i, 0, 0))],
        out_shape=[jax.ShapeDtypeStruct((n, D_INNER, D_STATE), F32),
                   jax.ShapeDtypeStruct((n, 1, D_INNER), F32)],
        compiler_params=_params(("arbitrary",), 6 * ns * D_INNER * D_STATE * 4),
        name="ssd_sample_state",
    )(da, state, xdt, bst, cst)


def _layernorm(v, g, b):
    mu = jnp.mean(v, axis=-1, keepdims=True)
    d = v - mu
    var = jnp.mean(d * d, axis=-1, keepdims=True)
    return d * lax.rsqrt(var + EPS) * g + b


def _sgu_sample_kernel(u_ref, v_ref, lng_ref, lnb_ref, w0_ref, b0_ref, o_ref, vn_ref):
    vn = _layernorm(v_ref[...], lng_ref[...], lnb_ref[...])
    vn_ref[...] = vn
    o_ref[...] = (u_ref[...] * (w0_ref[...] * vn + b0_ref[...])).astype(o_ref.dtype)


def _sgu_sample(uv, ln_g, ln_b, w0_rep, b0_rep):
    n = uv.shape[0]
    fixed2 = lambda i: (0, 0)
    return pl.pallas_call(
        _sgu_sample_kernel,
        grid=(1,),
        in_specs=[pl.BlockSpec((n, D_SGU), lambda i: (0, 0)), pl.BlockSpec((n, D_SGU), lambda i: (0, 1)),
                  pl.BlockSpec((1, D_SGU), fixed2), pl.BlockSpec((1, D_SGU), fixed2),
                  pl.BlockSpec((1, D_SGU), fixed2), pl.BlockSpec((1, D_SGU), fixed2)],
        out_specs=[pl.BlockSpec((n, D_SGU), fixed2), pl.BlockSpec((n, D_SGU), fixed2)],
        out_shape=[jax.ShapeDtypeStruct((n, D_SGU), BF16), jax.ShapeDtypeStruct((n, D_SGU), F32)],
        compiler_params=_params(("arbitrary",), 12 * n * D_SGU * 4),
        name="sgu_sample",
    )(uv, uv, ln_g, ln_b, w0_rep, b0_rep)


def _experts_to_lanes(col, sub, lane):
    return jnp.sum(jnp.where(sub == lane, col, 0.0), axis=0, keepdims=True)


def _route_kernel(lg_ref, loc_ref, gate_ref, bmeta_ref, tmeta_ref, *, tile, n_valid):
    nb, _, tb = lg_ref.shape
    sub = lax.broadcasted_iota(I32, (N_EXPERTS, LANES), 0)
    lane = lax.broadcasted_iota(I32, (N_EXPERTS, LANES), 1)
    subf = lax.broadcasted_iota(I32, (N_EXPERTS, tb), 0).astype(F32)
    tok = lax.broadcasted_iota(I32, (N_EXPERTS, tb), 1)
    incl = jnp.where(lax.broadcasted_iota(I32, (tb, tb), 0) <= lax.broadcasted_iota(I32, (tb, tb), 1), 1.0, 0.0)
    neg = jnp.float32(-jnp.inf)
    none = jnp.float32(N_EXPERTS)

    def select(k):
        blk = lg_ref[k]
        l = blk[0:N_EXPERTS] + blk[N_EXPERTS:2 * N_EXPERTS]
        m1 = jnp.max(l, axis=0, keepdims=True)
        i1 = jnp.min(jnp.where(l == m1, subf, none), axis=0, keepdims=True)
        l2 = jnp.where(subf == i1, neg, l)
        m2 = jnp.max(l2, axis=0, keepdims=True)
        i2 = jnp.min(jnp.where(l2 == m2, subf, none), axis=0, keepdims=True)
        valid = (k * tb + tok) < n_valid
        sel = jnp.where(valid, jnp.where(subf == i1, 1.0, jnp.where(subf == i2, 1.0, 0.0)), 0.0)
        return m1, i1, m2, i2, valid, sel

    def run_rows(cnt):
        return jnp.ceil(cnt / SUBLANES) * SUBLANES

    def count_body(k, carry):
        return carry + run_rows(jnp.sum(select(k)[5], axis=1, keepdims=True))

    counts = lax.fori_loop(0, nb, count_body, jnp.zeros((N_EXPERTS, 1), F32))
    tiles = jnp.ceil(counts / tile)
    cum_incl = jnp.sum(jnp.where(lane <= sub, _experts_to_lanes(tiles, sub, lane), 0.0), axis=1, keepdims=True)
    offset = (cum_incl - tiles) * tile
    tile_expert = jnp.sum(jnp.where(lane.astype(F32) >= cum_incl, 1.0, 0.0), axis=0, keepdims=True)
    tile_expert = jnp.minimum(tile_expert, N_EXPERTS - 1.0)
    n_used = jnp.max(cum_incl, axis=0, keepdims=True)
    tmeta_ref[...] = jnp.where(sub == 0, tile_expert, jnp.where(sub == 1, n_used, 0.0)).astype(I32)

    def place_body(k, before):
        m1, i1, m2, i2, valid, sel = select(k)
        run = jnp.dot(sel, incl, preferred_element_type=F32)
        cnt = run_rows(run[:, tb - 1:tb])
        lstart = jnp.sum(jnp.where(lane < sub, _experts_to_lanes(cnt, sub, lane), 0.0), axis=1, keepdims=True)
        local = lstart + run - sel
        loc1 = jnp.sum(jnp.where(subf == i1, local, 0.0), axis=0, keepdims=True)
        loc2 = jnp.sum(jnp.where(subf == i2, local, 0.0), axis=0, keepdims=True)
        live = (k * tb + tok[0:1, :]) < n_valid
        loc_ref[k] = jnp.where(subf == 0.0, jnp.where(live, loc1, -1.0),
                               jnp.where(subf == 1.0, jnp.where(live, loc2, -1.0), 0.0))
        e = jnp.exp(m2 - m1)
        gate_ref[k] = jnp.where(subf == 0.0, 1.0 / (1.0 + e), jnp.where(subf == 1.0, e / (1.0 + e), 0.0))
        bmeta_ref[k] = jnp.where(lane == 0, offset + before,
                                 jnp.where(lane == 1, cnt, jnp.where(lane == 2, lstart, 0.0))).astype(I32)
        return before + cnt

    lax.fori_loop(0, nb, place_body, jnp.zeros((N_EXPERTS, 1), F32))


def _route(logits3, tile, n_valid):
    nb, _, tb = logits3.shape
    tok_blk = pl.BlockSpec((nb, N_EXPERTS, tb), lambda i: (0, 0, 0))
    return pl.pallas_call(
        functools.partial(_route_kernel, tile=tile, n_valid=n_valid),
        grid=(1,),
        in_specs=[pl.BlockSpec((nb, 2 * N_EXPERTS, tb), lambda i: (0, 0, 0))],
        out_specs=[tok_blk, tok_blk, pl.BlockSpec((nb, N_EXPERTS, LANES), lambda i: (0, 0, 0)),
                   pl.BlockSpec((N_EXPERTS, LANES), lambda i: (0, 0))],
        out_shape=[jax.ShapeDtypeStruct((nb, N_EXPERTS, tb), F32),
                   jax.ShapeDtypeStruct((nb, N_EXPERTS, tb), F32),
                   jax.ShapeDtypeStruct((nb, N_EXPERTS, LANES), I32),
                   jax.ShapeDtypeStruct((N_EXPERTS, LANES), I32)],
        compiler_params=_params(("arbitrary",), 16 * nb * N_EXPERTS * tb * 4),
        name="moe_route",
    )(logits3)


def _run_copies(bm_ref, blk, vmem_ref, hbm_ref, sem, *, tb, to_hbm, wait):
    for e in range(N_EXPERTS):
        base = (blk * N_EXPERTS + e) * 3
        start, cnt, lstart = bm_ref[base], bm_ref[base + 1], bm_ref[base + 2]
        off = 0
        size = tb
        while size >= SUBLANES:
            @pl.when((cnt & size) != 0)
            def _(size=size, off=off):
                v = vmem_ref.at[pl.ds(pl.multiple_of(lstart + off, SUBLANES), size)]
                h = hbm_ref.at[pl.ds(pl.multiple_of(start + off, SUBLANES), size)]
                cp = pltpu.make_async_copy(v, h, sem) if to_hbm else pltpu.make_async_copy(h, v, sem)
                if wait:
                    cp.wait()
                else:
                    cp.start()

            off = off + (cnt & size)
            size //= 2


def _compact_rows(tb):
    return 2 * tb + N_EXPERTS * SUBLANES


def _selection(loc_ref, rows, tb):
    r = lax.broadcasted_iota(I32, (rows, tb), 0).astype(F32)
    return r == loc_ref[0, 0:1, :], r == loc_ref[0, 1:2, :]


def _moe_scatter_kernel(bm_ref, a_ref, loc_ref, gate_ref, xs_in_ref, gs_in_ref, xs_ref, gs_ref,
                        buf_ref, gbuf_ref, sem, gsem, *, tb):
    del xs_in_ref, gs_in_ref
    blk = pl.program_id(0)
    last = pl.num_programs(0) - 1
    slot = blk % 2

    def copies(block, wait):
        s = block % 2
        _run_copies(bm_ref, block, buf_ref.at[s], xs_ref, sem.at[s], tb=tb, to_hbm=True, wait=wait)
        _run_copies(bm_ref, block, gbuf_ref.at[s], gs_ref, gsem.at[s], tb=tb, to_hbm=True, wait=wait)

    @pl.when(blk >= 2)
    def _():
        copies(blk - 2, True)

    first, second = _selection(loc_ref, _compact_rows(tb), tb)
    pick = jnp.where(first, 1.0, jnp.where(second, 1.0, 0.0)).astype(BF16)
    buf_ref[slot] = jnp.dot(pick, a_ref[...], preferred_element_type=F32)
    gsel = jnp.where(first, gate_ref[0, 0:1, :], jnp.where(second, gate_ref[0, 1:2, :], 0.0))
    gbuf_ref[slot] = jnp.broadcast_to(jnp.sum(gsel, axis=1, keepdims=True), gbuf_ref.shape[1:])
    copies(blk, False)

    @pl.when(blk == last)
    def _():
        @pl.when(blk >= 1)
        def _():
            copies(blk - 1, True)

        copies(blk, True)


def _moe_scatter(a, loc, gate, bmeta, n_slots, tb):
    m, d = a.shape
    xs0 = jnp.zeros((n_slots, d), F32)
    gs0 = jnp.zeros((n_slots, LANES), F32)
    any_spec = pl.BlockSpec(memory_space=pl.ANY)
    tok_spec = pl.BlockSpec((1, N_EXPERTS, tb), lambda i, bm: (i, 0, 0))
    grid_spec = pltpu.PrefetchScalarGridSpec(
        num_scalar_prefetch=1,
        grid=(m // tb,),
        in_specs=[pl.BlockSpec((tb, d), lambda i, bm: (i, 0)), tok_spec, tok_spec, any_spec, any_spec],
        out_specs=[any_spec, any_spec],
        scratch_shapes=[pltpu.VMEM((2, _compact_rows(tb), d), F32), pltpu.VMEM((2, _compact_rows(tb), LANES), F32),
                        pltpu.SemaphoreType.DMA((2,)), pltpu.SemaphoreType.DMA((2,))],
    )
    return pl.pallas_call(
        functools.partial(_moe_scatter_kernel, tb=tb),
        grid_spec=grid_spec,
        out_shape=[jax.ShapeDtypeStruct((n_slots, d), F32), jax.ShapeDtypeStruct((n_slots, LANES), F32)],
        input_output_aliases={4: 0, 5: 1},
        compiler_params=_params(("arbitrary",), 12 * tb * d * 4),
        name="moe_scatter",
    )(bmeta, a, loc, gate, xs0, gs0)


def _new_weights(te_ref):
    i = pl.program_id(1)
    return (i == 0) | (te_ref[i] != te_ref[jnp.maximum(i - 1, 0)])


def _moe_gu_kernel(te_ref, nu_ref, x_ref, wg_ref, wu_ref, o_ref, wgb_ref, wub_ref):
    i = pl.program_id(1)

    @pl.when(_new_weights(te_ref))
    def _():
        wgb_ref[...] = wg_ref[0].astype(BF16)
        wub_ref[...] = wu_ref[0].astype(BF16)

    @pl.when(i < nu_ref[0])
    def _():
        x = x_ref[...].astype(BF16)
        for c in range(o_ref.shape[1] // MXU_COLS):
            cols = slice(c * MXU_COLS, (c + 1) * MXU_COLS)
            g = jnp.dot(x, wgb_ref[:, cols], preferred_element_type=F32)
            u = jnp.dot(x, wub_ref[:, cols], preferred_element_type=F32)
            o_ref[:, cols] = (_silu(g) * u).astype(o_ref.dtype)

    @pl.when(i >= nu_ref[0])
    def _():
        o_ref[...] = jnp.zeros_like(o_ref)


def _used_tile(i, nu):
    return jnp.minimum(i, nu[0] - 1)


def _moe_gu(xs, w_gu, te, nu, tm, tn):
    s, k = xs.shape
    f = D_FF_EXPERT
    nbytes = 2 * (2 * k * tn * 4 + k * tn * 2) + 2 * tm * k * 4 + 2 * tm * tn * 2 + 4 * tm * MXU_COLS * 4
    grid_spec = pltpu.PrefetchScalarGridSpec(
        num_scalar_prefetch=2,
        grid=(f // tn, s // tm),
        in_specs=[pl.BlockSpec((tm, k), lambda j, i, te, nu: (_used_tile(i, nu), 0)),
                  pl.BlockSpec((1, k, tn), lambda j, i, te, nu: (te[i], 0, j)),
                  pl.BlockSpec((1, k, tn), lambda j, i, te, nu: (te[i], 0, j + f // tn))],
        out_specs=pl.BlockSpec((tm, tn), lambda j, i, te, nu: (i, j)),
        scratch_shapes=[pltpu.VMEM((k, tn), BF16), pltpu.VMEM((k, tn), BF16)],
    )
    return pl.pallas_call(
        _moe_gu_kernel,
        grid_spec=grid_spec,
        out_shape=jax.ShapeDtypeStruct((s, f), BF16),
        compiler_params=_params(("arbitrary", "arbitrary"), nbytes),
        name="moe_gate_up",
    )(te, nu, xs, w_gu, w_gu)


def _moe_down_kernel(te_ref, nu_ref, x_ref, w_ref, gs_ref, o_ref, wb_ref):
    i = pl.program_id(1)

    @pl.when(_new_weights(te_ref))
    def _():
        wb_ref[...] = w_ref[0].astype(BF16)

    @pl.when(i < nu_ref[0])
    def _():
        o_ref[...] = gs_ref[:, 0:1] * jnp.dot(x_ref[...], wb_ref[...], preferred_element_type=F32)

    @pl.when(i >= nu_ref[0])
    def _():
        o_ref[...] = jnp.zeros_like(o_ref)


def _moe_down(act, w_down, gs, te, nu, tm, tn):
    s, k = act.shape
    d = w_down.shape[2]
    nbytes = 2 * k * tn * 4 + k * tn * 2 + 2 * tm * k * 2 + 3 * tm * tn * 4
    grid_spec = pltpu.PrefetchScalarGridSpec(
        num_scalar_prefetch=2,
        grid=(d // tn, s // tm),
        in_specs=[pl.BlockSpec((tm, k), lambda j, i, te, nu: (_used_tile(i, nu), 0)),
                  pl.BlockSpec((1, k, tn), lambda j, i, te, nu: (te[i], 0, j)),
                  pl.BlockSpec((tm, LANES), lambda j, i, te, nu: (_used_tile(i, nu), 0))],
        out_specs=pl.BlockSpec((tm, tn), lambda j, i, te, nu: (i, j)),
        scratch_shapes=[pltpu.VMEM((k, tn), BF16)],
    )
    return pl.pallas_call(
        _moe_down_kernel,
        grid_spec=grid_spec,
        out_shape=jax.ShapeDtypeStruct((s, d), F32),
        compiler_params=_params(("arbitrary", "arbitrary"), nbytes),
        name="moe_down",
    )(te, nu, act, w_down, gs)


def _moe_combine_kernel(bm_ref, h_ref, locc_ref, g_ref, ys_ref, op_ref, os_ref, ybuf_ref, sem, *, tb, prompt_blocks):
    blk = pl.program_id(0)
    slot = blk % 2

    def copies(block, wait):
        s = block % 2
        _run_copies(bm_ref, block, ybuf_ref.at[s], ys_ref, sem.at[s], tb=tb, to_hbm=False, wait=wait)

    @pl.when(blk == 0)
    def _():
        ybuf_ref[...] = jnp.zeros_like(ybuf_ref)
        copies(blk, False)

    @pl.when(blk + 1 < pl.num_programs(0))
    def _():
        copies(blk + 1, False)

    r = lax.broadcasted_iota(I32, (tb, _compact_rows(tb)), 1).astype(F32)
    pick = jnp.where(r == locc_ref[:, 0:1], 1.0, jnp.where(r == locc_ref[:, 1:2], 1.0, 0.0)).astype(BF16)
    copies(blk, True)
    y = ybuf_ref[slot]
    y_hi = y.astype(BF16)
    y_lo = (y - y_hi.astype(F32)).astype(BF16)
    moe = jnp.dot(pick, y_hi, preferred_element_type=F32) + jnp.dot(pick, y_lo, preferred_element_type=F32)
    out = _rms(h_ref[...] + moe, g_ref[...])

    @pl.when(blk < prompt_blocks)
    def _():
        op_ref[...] = out

    @pl.when(blk == prompt_blocks)
    def _():
        os_ref[...] = out


def _moe_combine(h, locc, bmeta, ys, g, tb, n_prompt):
    m, d = h.shape
    prompt_blocks = n_prompt // tb
    any_spec = pl.BlockSpec(memory_space=pl.ANY)
    grid_spec = pltpu.PrefetchScalarGridSpec(
        num_scalar_prefetch=1,
        grid=(m // tb,),
        in_specs=[pl.BlockSpec((tb, d), lambda i, bm: (i, 0)),
                  pl.BlockSpec((tb, N_EXPERTS), lambda i, bm: (i, 0)),
                  pl.BlockSpec((1, d), lambda i, bm: (0, 0)),
                  any_spec],
        out_specs=[pl.BlockSpec((tb, d), lambda i, bm: (jnp.minimum(i, prompt_blocks - 1), 0)),
                   pl.BlockSpec((tb, d), lambda i, bm: (0, 0))],
        scratch_shapes=[pltpu.VMEM((2, _compact_rows(tb), d), F32), pltpu.SemaphoreType.DMA((2,))],
    )
    return pl.pallas_call(
        functools.partial(_moe_combine_kernel, tb=tb, prompt_blocks=prompt_blocks),
        grid_spec=grid_spec,
        out_shape=[jax.ShapeDtypeStruct((n_prompt, d), F32), jax.ShapeDtypeStruct((tb, d), F32)],
        compiler_params=_params(("arbitrary",), 16 * tb * d * 4),
        name="moe_combine",
    )(bmeta, h, locc, g.reshape(1, d), ys)


MOE_TOKEN_BLOCK = 256
MOE_ROW_TILE = 512


def _moe(h, a, lg_parts, n_valid, n_prompt, w_gu, w_down, g_final):
    m, d = h.shape
    tb, tm = MOE_TOKEN_BLOCK, MOE_ROW_TILE
    nb = m // tb
    lg3 = lg_parts[:, :2 * N_EXPERTS].reshape(nb, tb, 2 * N_EXPERTS).transpose(0, 2, 1)
    n_tiles = pl.cdiv(2 * n_valid + nb * N_EXPERTS * (SUBLANES - 1), tm) + N_EXPERTS
    loc, gate, bmeta, tmeta = _route(lg3, tm, n_valid)
    te = tmeta[0, :n_tiles]
    nu = tmeta[1, 0:1]
    bm = bmeta[:, :, :3].reshape(-1)
    locc = loc.transpose(0, 2, 1).reshape(m, N_EXPERTS)
    xs, gs = _moe_scatter(a, loc, gate, bm, n_tiles * tm, tb)
    act = _moe_gu(xs, w_gu, te, nu, tm, D_FF_EXPERT // 2)
    ys = _moe_down(act, w_down, gs, te, nu, tm, d)
    return _moe_combine(h, locc, bm, ys, g_final, tb, n_prompt)


def _pad_lanes(v):
    return jnp.pad(v.reshape(1, -1), ((0, 0), (0, LANES - v.shape[-1])))


def kernel(x_prompt, x_sample, state_ssm, state_conv, norm_mix, norm_ffn, norm_final, ssd_w_in, ssd_conv_w,
           ssd_conv_b, ssd_dt_bias, ssd_a_log, ssd_d, ssd_gnorm, ssd_w_out, sgu_w_in, sgu_b_in, sgu_ln_g,
           sgu_ln_b, sgu_w_s, sgu_b_s, sgu_w_out, ffn_w_gu, ffn_w_down, moe_w_router, moe_w_gu, moe_w_down):
    batch, seq, d = x_prompt.shape
    n_s = x_sample.shape[0]
    mp = batch * seq
    xp = x_prompt.reshape(mp, d)
    xs = x_sample.reshape(n_s, d)

    w_in = ssd_w_in[0]
    w_dt = jnp.pad(w_in[:, D_INNER + CONV_DIM:], ((0, 0), (0, LANES - SSD_HEADS)))
    dtb = _pad_lanes(ssd_dt_bias[0])
    alog = _pad_lanes(ssd_a_log[0])
    d_rep = jnp.repeat(ssd_d[0], SSD_HEADDIM).reshape(1, D_INNER)
    gn_w = ssd_gnorm[0].reshape(1, D_INNER)
    conv_w = ssd_conv_w[0]
    conv_b = ssd_conv_b[0].reshape(1, CONV_DIM)
    ln_g = sgu_ln_g[0].reshape(1, D_SGU)
    ln_b = sgu_ln_b[0].reshape(1, D_SGU)
    bs_t = jnp.pad(sgu_b_s[0].T, ((0, 0), (0, LANES - SGU_HEADS)))
    w0_rep = jnp.repeat(sgu_w_s[0][:, 0, 0], SGU_HEAD_DIM).reshape(1, D_SGU)
    b0_rep = jnp.repeat(sgu_b_s[0][:, 0], SGU_HEAD_DIM).reshape(1, D_SGU)
    wr_hi = moe_w_router[0].astype(BF16)
    wr_lo = (moe_w_router[0] - wr_hi.astype(F32)).astype(BF16)
    w3 = jnp.concatenate([wr_hi, wr_lo, jnp.zeros((d, LANES - 2 * N_EXPERTS), BF16)], axis=1)

    def dense_ffn(h1, a1, tm):
        act = _mm_swiglu(a1, ffn_w_gu[0], D_FF_DENSE, tm, 1408, "ffn_gate_up")
        return _mm_resnorm(act, ffn_w_down[0], h1, norm_mix[1], tm, "ffn_down")

    a0p, zp, dtrp = _norm_in_proj(xp, norm_mix[0], w_in, w_dt, 1024)
    xbcp = _mm_plain(a0p, w_in, D_INNER, CONV_DIM, 1024, 1024, F32, "ssd_in_xbc")
    per_seq = lambda t: t.reshape(batch, seq, t.shape[-1])
    ygp, ssm_p, conv_p = _ssd_prompt(per_seq(xbcp), per_seq(zp), per_seq(dtrp), conv_w, conv_b, dtb, alog, d_rep,
                                     gn_w)
    h1p, a1p = _mm_resnorm(ygp.reshape(mp, D_INNER), ssd_w_out[0], xp, norm_ffn[0], 512, "ssd_out")
    h2p, a2p = dense_ffn(h1p, a1p, 512)

    a0s, zs, dtrs = _norm_in_proj(xs, norm_mix[0], w_in, w_dt, n_s)
    xbcs = _mm_plain(a0s, w_in, D_INNER, CONV_DIM, n_s, 1024, F32, "ssd_in_xbc")
    conv_s, xas, xdts, das, bsts, csts = _ssd_sample_pre(xbcs, state_conv[0].reshape(n_s, 3 * CONV_DIM), dtrs,
                                                         conv_w, conv_b, dtb, alog)
    stack_rows = SSD_GROUPS * TERM_ROWS
    ssm_s, ys3 = _ssd_sample_state(state_ssm[0].reshape(n_s, D_INNER, D_STATE), xdts.reshape(n_s, 1, D_INNER),
                                   das[:, :SSD_HEADS], bsts.reshape(n_s, stack_rows, D_STATE),
                                   csts.reshape(n_s, stack_rows, D_STATE))
    h1s, a1s = _ssd_out(ys3.reshape(n_s, D_INNER), xas, zs, d_rep, gn_w, ssd_w_out[0], xs, norm_ffn[0], n_s)
    h2s, a2s = dense_ffn(h1s, a1s, n_s)

    uvp = _mm_bias_gelu(a2p, sgu_w_in[0], sgu_b_in[0], 1024, 1024, BF16, "sgu_in")
    uvs = _mm_bias_gelu(a2s, sgu_w_in[0], sgu_b_in[0], n_s, 1024, F32, "sgu_in")
    ss, v_s = _sgu_sample(uvs, ln_g, ln_b, w0_rep, b0_rep)
    tm_out = 512
    pad_rows = ((0, tm_out - n_s), (0, 0))
    h3, a3, lg = _sgu_out(uvp, jnp.pad(ss, pad_rows), ln_g, ln_b, sgu_w_s[0], bs_t, sgu_w_out[0], h2p,
                          jnp.pad(h2s, pad_rows), norm_ffn[1], w3, tm_out)

    y_prompt, y_tail = _moe(h3, a3, lg, mp + n_s, mp, moe_w_gu[0], moe_w_down[0], norm_final)
    y_sample = y_tail[:n_s]

    return (y_prompt.reshape(batch, seq, d),
            y_sample.reshape(n_s, 1, d),
            ssm_p[None],
            conv_p[None],
            ssm_s.reshape(1, n_s, SSD_HEADS, SSD_HEADDIM, D_STATE),
            conv_s.reshape(1, n_s, CONV_W - 1, CONV_DIM),
            v_s.reshape(1, n_s, 1, D_SGU))
```

```python
import functools

import jax
import jax.numpy as jnp
from jax import lax
from jax.experimental import pallas as pl
from jax.experimental.pallas import tpu as pltpu

F32 = jnp.float32
BF16 = jnp.bfloat16
I32 = jnp.int32
HIGHEST = lax.Precision.HIGHEST

D_MODEL = 1024
D_INNER = 2048
SSD_HEADS = 32
SSD_HEADDIM = 64
SSD_GROUPS = 4
D_STATE = 128
CONV_W = 4
CONV_DIM = D_INNER + 2 * SSD_GROUPS * D_STATE
CHUNK = 128
D_SGU = 2048
SGU_HEADS = 8
SGU_HEAD_DIM = D_SGU // SGU_HEADS
D_FF_DENSE = 2816
N_EXPERTS = 8
D_FF_EXPERT = 3584
EPS = 1e-5

MXU_COLS = 256
ROW_SUB = 256
LANES = 128
SUBLANES = 8
VMEM_CAP = 60000 * 1024
VMEM_FLOOR = 32 * 1024 * 1024


def _vmem_limit(nbytes):
    return int(min(max(nbytes * 5 // 4 + (4 << 20), VMEM_FLOOR), VMEM_CAP))


def _params(sem, nbytes):
    return pltpu.CompilerParams(dimension_semantics=sem, vmem_limit_bytes=_vmem_limit(nbytes))


LOG2E = 1.4426950408889634


def _silu(x):
    return x / (1.0 + jnp.exp2(x * (-LOG2E)))


def _rms(x, g):
    return x * lax.rsqrt(jnp.mean(x * x, axis=-1, keepdims=True) + EPS) * g


def _bdot(a, b):
    return jnp.dot(a.astype(BF16), b.astype(BF16), preferred_element_type=F32)


def _mm_plain_kernel(x_ref, w_ref, o_ref, wb_ref):
    @pl.when(pl.program_id(1) == 0)
    def _():
        wb_ref[...] = w_ref[...].astype(BF16)

    o_ref[...] = _bdot(x_ref[...], wb_ref[...]).astype(o_ref.dtype)


def _mm_bias_gelu_kernel(x_ref, w_ref, b_ref, o_ref, wb_ref):
    @pl.when(pl.program_id(1) == 0)
    def _():
        wb_ref[...] = w_ref[...].astype(BF16)

    x = x_ref[...].astype(BF16)
    for c0 in range(0, o_ref.shape[1], 2 * MXU_COLS):
        cols = slice(c0, c0 + 2 * MXU_COLS)
        y = jnp.dot(x, wb_ref[:, cols], preferred_element_type=F32) + b_ref[:, cols]
        o_ref[:, cols] = (0.5 * y * (1.0 + lax.erf(y * (2.0 ** -0.5)))).astype(o_ref.dtype)


def _mm_swiglu_kernel(x_ref, wg_ref, wu_ref, o_ref, wgb_ref, wub_ref):
    @pl.when(pl.program_id(1) == 0)
    def _():
        wgb_ref[...] = wg_ref[...].astype(BF16)
        wub_ref[...] = wu_ref[...].astype(BF16)

    x = x_ref[...].astype(BF16)
    tn = o_ref.shape[1]
    for c0 in range(0, tn, MXU_COLS):
        cols = slice(c0, min(c0 + MXU_COLS, tn))
        g = jnp.dot(x, wgb_ref[:, cols], preferred_element_type=F32)
        u = jnp.dot(x, wub_ref[:, cols], preferred_element_type=F32)
        o_ref[:, cols] = (_silu(g) * u).astype(o_ref.dtype)


def _wspec(k, tn, col_block0):
    return pl.BlockSpec((k, tn), lambda j, i: (0, j + col_block0), pipeline_mode=pl.Buffered(1))


def _mm_plain(x, w, col0, n, tm, tn, out_dtype, name):
    m, k = x.shape
    nbytes = k * tn * 6 + 2 * tm * k * x.dtype.itemsize + 2 * tm * tn * 4
    return pl.pallas_call(
        _mm_plain_kernel,
        grid=(n // tn, m // tm),
        in_specs=[pl.BlockSpec((tm, k), lambda j, i: (i, 0)), _wspec(k, tn, col0 // tn)],
        out_specs=pl.BlockSpec((tm, tn), lambda j, i: (i, j)),
        out_shape=jax.ShapeDtypeStruct((m, n), out_dtype),
        scratch_shapes=[pltpu.VMEM((k, tn), BF16)],
        compiler_params=_params(("arbitrary", "arbitrary"), nbytes),
        name=name,
    )(x, w)


def _norm_in_proj_kernel(x_ref, g_ref, wz_ref, wdt_ref, a_ref, z_ref, dtr_ref, wzb_ref, wdtb_ref):
    @pl.when(pl.program_id(0) == 0)
    def _():
        wzb_ref[...] = wz_ref[...].astype(BF16)
        wdtb_ref[...] = wdt_ref[...].astype(BF16)

    a = _rms(x_ref[...], g_ref[...]).astype(BF16)
    a_ref[...] = a
    z_ref[...] = jnp.dot(a, wzb_ref[...], preferred_element_type=F32).astype(z_ref.dtype)
    dtr_ref[...] = jnp.dot(a, wdtb_ref[...], preferred_element_type=F32)


def _norm_in_proj(x, g, w_in, w_dt, tm):
    m, k = x.shape
    n = D_INNER
    row = lambda i: (i, 0)
    fixed = lambda i: (0, 0)
    nbytes = k * n * 6 + k * LANES * 6 + 2 * tm * k * 4 + 2 * tm * k * 2 + 2 * tm * n * 2 + 2 * tm * n * 4
    return pl.pallas_call(
        _norm_in_proj_kernel,
        grid=(m // tm,),
        in_specs=[pl.BlockSpec((tm, k), row), pl.BlockSpec((1, k), fixed),
                  pl.BlockSpec((k, n), fixed, pipeline_mode=pl.Buffered(1)),
                  pl.BlockSpec((k, LANES), fixed)],
        out_specs=[pl.BlockSpec((tm, k), row), pl.BlockSpec((tm, n), row), pl.BlockSpec((tm, LANES), row)],
        out_shape=[jax.ShapeDtypeStruct((m, k), BF16), jax.ShapeDtypeStruct((m, n), BF16),
                   jax.ShapeDtypeStruct((m, LANES), F32)],
        scratch_shapes=[pltpu.VMEM((k, n), BF16), pltpu.VMEM((k, LANES), BF16)],
        compiler_params=_params(("arbitrary",), nbytes),
        name="norm_in_proj",
    )(x, g.reshape(1, k), w_in, w_dt)


def _mm_bias_gelu(x, w, b, tm, tn, out_dtype, name):
    m, k = x.shape
    n = w.shape[1]
    nbytes = k * tn * 6 + 2 * tm * k * x.dtype.itemsize + 2 * tm * tn * 4
    return pl.pallas_call(
        _mm_bias_gelu_kernel,
        grid=(n // tn, m // tm),
        in_specs=[pl.BlockSpec((tm, k), lambda j, i: (i, 0)), _wspec(k, tn, 0),
                  pl.BlockSpec((1, tn), lambda j, i: (0, j))],
        out_specs=pl.BlockSpec((tm, tn), lambda j, i: (i, j)),
        out_shape=jax.ShapeDtypeStruct((m, n), out_dtype),
        scratch_shapes=[pltpu.VMEM((k, tn), BF16)],
        compiler_params=_params(("arbitrary", "arbitrary"), nbytes),
        name=name,
    )(x, w, b.reshape(1, n))


def _mm_swiglu(x, w_gu, d_ff, tm, tn, name):
    m, k = x.shape
    nbytes = 2 * k * tn * 6 + 2 * tm * k * x.dtype.itemsize + 2 * tm * tn * 2 + 3 * tm * tn * 4
    return pl.pallas_call(
        _mm_swiglu_kernel,
        grid=(d_ff // tn, m // tm),
        in_specs=[pl.BlockSpec((tm, k), lambda j, i: (i, 0)), _wspec(k, tn, 0), _wspec(k, tn, d_ff // tn)],
        out_specs=pl.BlockSpec((tm, tn), lambda j, i: (i, j)),
        out_shape=jax.ShapeDtypeStruct((m, d_ff), BF16),
        scratch_shapes=[pltpu.VMEM((k, tn), BF16), pltpu.VMEM((k, tn), BF16)],
        compiler_params=_params(("arbitrary", "arbitrary"), nbytes),
        name=name,
    )(x, w_gu, w_gu)


def _mm_resnorm_kernel(x_ref, w_ref, r_ref, g_ref, h_ref, a_ref, wb_ref):
    @pl.when(pl.program_id(0) == 0)
    def _():
        wb_ref[...] = w_ref[...].astype(BF16)

    h = r_ref[...] + _bdot(x_ref[...], wb_ref[...])
    h_ref[...] = h
    a_ref[...] = _rms(h, g_ref[...]).astype(a_ref.dtype)


def _mm_resnorm(x, w, resid, g, tm, name):
    m, k = x.shape
    d = w.shape[1]
    nbytes = k * d * 6 + 2 * tm * k * 2 + 8 * tm * d * 4
    return pl.pallas_call(
        _mm_resnorm_kernel,
        grid=(m // tm,),
        in_specs=[pl.BlockSpec((tm, k), lambda i: (i, 0)),
                  pl.BlockSpec((k, d), lambda i: (0, 0), pipeline_mode=pl.Buffered(1)),
                  pl.BlockSpec((tm, d), lambda i: (i, 0)),
                  pl.BlockSpec((1, d), lambda i: (0, 0))],
        out_specs=[pl.BlockSpec((tm, d), lambda i: (i, 0)), pl.BlockSpec((tm, d), lambda i: (i, 0))],
        out_shape=[jax.ShapeDtypeStruct((m, d), F32), jax.ShapeDtypeStruct((m, d), BF16)],
        scratch_shapes=[pltpu.VMEM((k, d), BF16)],
        compiler_params=_params(("arbitrary",), nbytes),
        name=name,
    )(x, w, resid, g.reshape(1, d))


def _sgu_out_kernel(u_ref, v_ref, xt_ref, lng_ref, lnb_ref, ws_ref, bst_ref, w_ref, rp_ref, rt_ref, g_ref, w3_ref,
                    h_ref, a_ref, lg_ref, wb_ref, wsb_ref, y_ref, *, prompt_tiles):
    i = pl.program_id(0)
    q = CHUNK

    @pl.when(i == 0)
    def _():
        wb_ref[...] = w_ref[...].astype(BF16)
        causal = lax.broadcasted_iota(I32, (q, q), 0) >= lax.broadcasted_iota(I32, (q, q), 1)
        for g in range(SGU_HEADS):
            wsb_ref[g] = jnp.where(causal, ws_ref[g], 0.0).astype(BF16)

    def project(x_ref, r_ref, rows):
        h = r_ref[rows, :] + jnp.dot(x_ref[rows, :], wb_ref[...], preferred_element_type=F32)
        h_ref[rows, :] = h
        a = _rms(h, g_ref[...])
        a_hi = a.astype(BF16)
        a_lo = (a - a_hi.astype(F32)).astype(BF16)
        a_ref[rows, :] = a_hi
        lg_ref[rows, :] = (jnp.dot(a_hi, w3_ref[...], preferred_element_type=F32)
                           + jnp.dot(a_lo, w3_ref[...], preferred_element_type=F32))

    chunks = [slice(c * q, (c + 1) * q) for c in range(u_ref.shape[0] // q)]

    @pl.when(i < prompt_tiles)
    def _():
        for rows in chunks:
            vn = _layernorm(v_ref[rows, :].astype(F32), lng_ref[...], lnb_ref[...]).astype(BF16)
            for g in range(SGU_HEADS):
                cols = slice(g * SGU_HEAD_DIM, (g + 1) * SGU_HEAD_DIM)
                s = jnp.dot(wsb_ref[g], vn[:, cols], preferred_element_type=F32) + bst_ref[:, g:g + 1]
                y_ref[rows, cols] = (u_ref[rows, cols].astype(F32) * s).astype(BF16)
            project(y_ref, rp_ref, rows)

    @pl.when(i == prompt_tiles)
    def _():
        for rows in chunks:
            project(xt_ref, rt_ref, rows)


def _sgu_out(uv, xt, ln_g, ln_b, w_s, bs_t, w, rp, rt, g, w3, tm):
    mp = uv.shape[0]
    k, d = w.shape
    prompt_tiles = mp // tm
    last = prompt_tiles - 1
    rows = mp + tm
    prow = lambda i: (jnp.minimum(i, last), 0)
    fixed = lambda i: (0, 0)
    nbytes = k * d * 6 + 8 * tm * k * 2 + 12 * tm * d * 4 + 8 * CHUNK * k * 4
    return pl.pallas_call(
        functools.partial(_sgu_out_kernel, prompt_tiles=prompt_tiles),
        grid=(prompt_tiles + 1,),
        in_specs=[pl.BlockSpec((tm, k), prow),
                  pl.BlockSpec((tm, k), lambda i: (jnp.minimum(i, last), 1)),
                  pl.BlockSpec((tm, k), fixed),
                  pl.BlockSpec((1, k), fixed), pl.BlockSpec((1, k), fixed),
                  pl.BlockSpec((SGU_HEADS, CHUNK, CHUNK), lambda i: (0, 0, 0)),
                  pl.BlockSpec((CHUNK, LANES), fixed),
                  pl.BlockSpec((k, d), fixed, pipeline_mode=pl.Buffered(1)),
                  pl.BlockSpec((tm, d), prow),
                  pl.BlockSpec((tm, d), fixed),
                  pl.BlockSpec((1, d), fixed),
                  pl.BlockSpec((d, LANES), fixed)],
        out_specs=[pl.BlockSpec((tm, d), lambda i: (i, 0)), pl.BlockSpec((tm, d), lambda i: (i, 0)),
                   pl.BlockSpec((tm, LANES), lambda i: (i, 0))],
        out_shape=[jax.ShapeDtypeStruct((rows, d), F32), jax.ShapeDtypeStruct((rows, d), BF16),
                   jax.ShapeDtypeStruct((rows, LANES), F32)],
        scratch_shapes=[pltpu.VMEM((k, d), BF16), pltpu.VMEM((SGU_HEADS, CHUNK, CHUNK), BF16),
                        pltpu.VMEM((tm, k), BF16)],
        compiler_params=_params(("arbitrary",), nbytes),
        name="sgu_out",
    )(uv, uv, xt, ln_g, ln_b, w_s, bs_t, w, rp, rt, g.reshape(1, d), w3)


def _gate_and_groupnorm(y, xs, z, d_rep, gn_w):
    yg = (y + d_rep * xs) * _silu(z)
    gw = D_INNER // SSD_GROUPS
    outs = []
    for g in range(SSD_GROUPS):
        blk = yg[:, g * gw:(g + 1) * gw]
        outs.append(blk * lax.rsqrt(jnp.mean(blk * blk, axis=-1, keepdims=True) + EPS))
    return jnp.concatenate(outs, axis=1) * gn_w


def _softplus(x):
    return jnp.maximum(x, 0.0) + jnp.log(1.0 + jnp.exp(-jnp.abs(x)))


SSD_SEQS = 2


def _ssd_prompt_kernel(xbc_ref, z_ref, dtr_ref, cw_ref, cb_ref, dtb_ref, alog_ref, drep_ref, gnw_ref,
                       yg_ref, ssm_ref, conv_ref, ext_ref, st_ref, y_ref, xa_ref):
    for n in range(SSD_SEQS):
        _ssd_chunk(xbc_ref.at[n], z_ref.at[n], dtr_ref.at[n], cw_ref, cb_ref, dtb_ref, alog_ref, drep_ref, gnw_ref,
                   yg_ref.at[n], ssm_ref.at[n], conv_ref.at[n], ext_ref.at[n], st_ref.at[n], y_ref.at[n],
                   xa_ref.at[n])


def _ssd_chunk(xbc_ref, z_ref, dtr_ref, cw_ref, cb_ref, dtb_ref, alog_ref, drep_ref, gnw_ref,
               yg_ref, ssm_ref, conv_ref, ext_ref, st_ref, y_ref, xa_ref):
    c = pl.program_id(1)
    nc = pl.num_programs(1)
    q = CHUNK

    @pl.when(c == 0)
    def _():
        ext_ref[0:SUBLANES, :] = jnp.zeros((SUBLANES, CONV_DIM), F32)
        st_ref[...] = jnp.zeros_like(st_ref)

    ext_ref[SUBLANES:SUBLANES + q, :] = xbc_ref[...]
    acc = cb_ref[...] + cw_ref[CONV_W - 1:CONV_W, :] * ext_ref[SUBLANES:SUBLANES + q, :]
    for k in range(CONV_W - 1):
        lo = SUBLANES - (CONV_W - 1) + k
        acc = acc + cw_ref[k:k + 1, :] * ext_ref[lo:lo + q, :]
    xa_ref[...] = _silu(acc)
    tail = ext_ref[q:q + SUBLANES, :]
    ext_ref[0:SUBLANES, :] = tail

    dt = _softplus(dtr_ref[...] + dtb_ref[...])
    a_neg = -jnp.exp(alog_ref[...])
    causal = lax.broadcasted_iota(I32, (q, q), 0) >= lax.broadcasted_iota(I32, (q, q), 1)
    tril = jnp.where(causal, 1.0, 0.0)
    cs = jnp.dot(tril, dt * a_neg, precision=HIGHEST, preferred_element_type=F32) * LOG2E
    cs_t = cs.T
    dt_t = dt.T
    rowp = cs_t - jnp.log2(dt_t)
    w_t = dt_t * jnp.exp2(cs_t[:, q - 1:q] - cs_t)
    first = lax.broadcasted_iota(I32, (q, 2 * SSD_HEADDIM), 1) < SSD_HEADDIM

    def per_head(v):
        zero = jnp.zeros_like(v)
        return jnp.concatenate([jnp.where(first, v, zero), jnp.where(first, zero, v)], axis=0)

    for g in range(SSD_GROUPS):
        b_g = xa_ref[:, D_INNER + g * D_STATE:D_INNER + (g + 1) * D_STATE]
        c_g = xa_ref[:, D_INNER + (SSD_GROUPS + g) * D_STATE:D_INNER + (SSD_GROUPS + g + 1) * D_STATE]
        b_gt = b_g.T
        cb = _bdot(c_g, b_gt)
        for pr in range(4):
            hp = g * 4 + pr
            cols = slice(hp * 2 * SSD_HEADDIM, (hp + 1) * 2 * SSD_HEADDIM)
            x2 = per_head(xa_ref[:, cols].astype(BF16))
            st_p = st_ref[:, cols]
            st2 = per_head(st_p.astype(BF16))
            ms, cs_scaled, bws, cds = [], [], [], []
            for s in range(2):
                h = 2 * hp + s
                colb = jnp.broadcast_to(cs[:, h:h + 1], (q, q))
                rowb = jnp.broadcast_to(rowp[h:h + 1, :], (q, q))
                ecol = jnp.exp2(colb)
                ms.append((cb * jnp.where(causal, jnp.exp2(colb - rowb), 0.0)).astype(BF16))
                cs_scaled.append((c_g * ecol).astype(BF16))
                bws.append((b_gt * w_t[h:h + 1, :]).astype(BF16))
                cds.append(ecol[q - 1:q, :])
            y_ref[:, cols] = jnp.dot(jnp.concatenate(ms + cs_scaled, axis=1), jnp.concatenate([x2, st2], axis=0),
                                     preferred_element_type=F32)
            cd = jnp.where(first[0:1, :], cds[0], cds[1])
            st_ref[:, cols] = st_p * cd + jnp.dot(jnp.concatenate(bws, axis=1), x2, preferred_element_type=F32)

    out = _gate_and_groupnorm(y_ref[...], xa_ref[:, :D_INNER], z_ref[...].astype(F32), drep_ref[...],
                              gnw_ref[...])
    yg_ref[...] = out.astype(yg_ref.dtype)

    @pl.when(c == nc - 1)
    def _():
        conv_ref[...] = tail[SUBLANES - (CONV_W - 1):, :]
        for hp in range(SSD_HEADS // 2):
            blk = st_ref[:, hp * 2 * SSD_HEADDIM:(hp + 1) * 2 * SSD_HEADDIM].T
            ssm_ref[2 * hp:2 * hp + 2] = blk.reshape(2, SSD_HEADDIM, D_STATE)


def _ssd_prompt(xbc, z, dtr, conv_w, conv_b, dtb, alog, d_rep, gn_w):
    batch, seq, _ = xbc.shape
    nb = SSD_SEQS
    blk = lambda b, c: (b, c, 0)
    fixed = lambda b, c: (0, 0)
    nbytes = nb * (2 * CHUNK * (CONV_DIM * 4 + D_INNER * 4 + LANES * 4) + (CHUNK + SUBLANES) * CONV_DIM * 4
                   + 4 * D_STATE * D_INNER * 4 + 12 * CHUNK * CONV_DIM * 4)
    return pl.pallas_call(
        _ssd_prompt_kernel,
        grid=(batch // nb, seq // CHUNK),
        in_specs=[pl.BlockSpec((nb, CHUNK, CONV_DIM), blk), pl.BlockSpec((nb, CHUNK, D_INNER), blk),
                  pl.BlockSpec((nb, CHUNK, LANES), blk),
                  pl.BlockSpec((CONV_W, CONV_DIM), fixed), pl.BlockSpec((1, CONV_DIM), fixed),
                  pl.BlockSpec((1, LANES), fixed), pl.BlockSpec((1, LANES), fixed),
                  pl.BlockSpec((1, D_INNER), fixed), pl.BlockSpec((1, D_INNER), fixed)],
        out_specs=[pl.BlockSpec((nb, CHUNK, D_INNER), blk),
                   pl.BlockSpec((nb, SSD_HEADS, SSD_HEADDIM, D_STATE), lambda b, c: (b, 0, 0, 0)),
                   pl.BlockSpec((nb, CONV_W - 1, CONV_DIM), lambda b, c: (b, 0, 0))],
        out_shape=[jax.ShapeDtypeStruct((batch, seq, D_INNER), BF16),
                   jax.ShapeDtypeStruct((batch, SSD_HEADS, SSD_HEADDIM, D_STATE), F32),
                   jax.ShapeDtypeStruct((batch, CONV_W - 1, CONV_DIM), F32)],
        scratch_shapes=[pltpu.VMEM((nb, CHUNK + SUBLANES, CONV_DIM), F32), pltpu.VMEM((nb, D_STATE, D_INNER), F32),
                        pltpu.VMEM((nb, CHUNK, D_INNER), F32), pltpu.VMEM((nb, CHUNK, CONV_DIM), F32)],
        compiler_params=_params(("arbitrary", "arbitrary"), nbytes),
        name="ssd_prompt",
    )(xbc, z, dtr, conv_w, conv_b, dtb, alog, d_rep, gn_w)


def _ssd_out_kernel(y_ref, xs_ref, z_ref, drep_ref, gnw_ref, w_ref, r_ref, g_ref, h_ref, a_ref, wb_ref):
    @pl.when(pl.program_id(0) == 0)
    def _():
        wb_ref[...] = w_ref[...].astype(BF16)

    sub = min(ROW_SUB // 2, y_ref.shape[0])
    for s in range(y_ref.shape[0] // sub):
        rows = slice(s * sub, (s + 1) * sub)
        yg = _gate_and_groupnorm(y_ref[rows, :], xs_ref[rows, :], z_ref[rows, :].astype(F32), drep_ref[...],
                                 gnw_ref[...])
        h = r_ref[rows, :] + _bdot(yg, wb_ref[...])
        h_ref[rows, :] = h
        a_ref[rows, :] = _rms(h, g_ref[...]).astype(a_ref.dtype)


def _ssd_out(y, xa, z, d_rep, gn_w, w, resid, g, tm):
    m, k = y.shape
    d = w.shape[1]
    row = lambda i: (i, 0)
    fixed = lambda i: (0, 0)
    nbytes = k * d * 6 + 2 * tm * k * 10 + 8 * tm * d * 4 + 6 * tm * k * 4
    return pl.pallas_call(
        _ssd_out_kernel,
        grid=(m // tm,),
        in_specs=[pl.BlockSpec((tm, k), row), pl.BlockSpec((tm, k), row), pl.BlockSpec((tm, k), row),
                  pl.BlockSpec((1, k), fixed), pl.BlockSpec((1, k), fixed),
                  pl.BlockSpec((k, d), fixed, pipeline_mode=pl.Buffered(1)),
                  pl.BlockSpec((tm, d), row), pl.BlockSpec((1, d), fixed)],
        out_specs=[pl.BlockSpec((tm, d), row), pl.BlockSpec((tm, d), row)],
        out_shape=[jax.ShapeDtypeStruct((m, d), F32), jax.ShapeDtypeStruct((m, d), BF16)],
        scratch_shapes=[pltpu.VMEM((k, d), BF16)],
        compiler_params=_params(("arbitrary",), nbytes),
        name="ssd_out",
    )(y, xa, z, d_rep, gn_w, w, resid, g.reshape(1, d))


def _ssd_sample_pre_kernel(xbc_ref, conv_ref, dtr_ref, cw_ref, cb_ref, dtb_ref, alog_ref,
                           nconv_ref, xa_ref, xdt_ref, da_ref, bst_ref, cst_ref):
    xn = xbc_ref[...]
    acc = cb_ref[...] + cw_ref[CONV_W - 1:CONV_W, :] * xn
    for k in range(CONV_W - 1):
        acc = acc + cw_ref[k:k + 1, :] * conv_ref[:, k * CONV_DIM:(k + 1) * CONV_DIM]
    xa = _silu(acc)
    xa_ref[...] = xa
    nconv_ref[:, 0:CONV_DIM] = conv_ref[:, CONV_DIM:2 * CONV_DIM]
    nconv_ref[:, CONV_DIM:2 * CONV_DIM] = conv_ref[:, 2 * CONV_DIM:3 * CONV_DIM]
    nconv_ref[:, 2 * CONV_DIM:3 * CONV_DIM] = xn

    dt = _softplus(dtr_ref[...] + dtb_ref[...])
    a_neg = -jnp.exp(alog_ref[...])
    hh = lax.broadcasted_iota(I32, (LANES, D_INNER), 0)
    cc = lax.broadcasted_iota(I32, (LANES, D_INNER), 1)
    expand = jnp.where(lax.shift_right_logical(cc, 6) == hh, 1.0, 0.0)
    dt_rep = jnp.dot(dt, expand, precision=HIGHEST, preferred_element_type=F32)
    xdt_ref[...] = xa[:, :D_INNER] * dt_rep
    da_ref[...] = jnp.exp(dt * a_neg)

    n = xn.shape[0]
    gs = SSD_GROUPS * D_STATE
    b1, b2, b3 = _split3(xa[:, D_INNER:D_INNER + gs])
    cm = xa[:, D_INNER + gs:]
    c1 = cm.astype(BF16)
    c2 = (cm - c1.astype(F32)).astype(BF16)
    zero = jnp.zeros((n, D_STATE), BF16)
    b_rows, c_rows = [], []
    for g in range(SSD_GROUPS):
        sl = slice(g * D_STATE, (g + 1) * D_STATE)
        b_rows += [b1[:, sl], b2[:, sl], b1[:, sl], b3[:, sl], b2[:, sl], b1[:, sl]] + [zero] * (TERM_ROWS - 6)
        c_rows += [c1[:, sl], c2[:, sl]] + [zero] * (TERM_ROWS - 2)
    bst_ref[...] = jnp.concatenate(b_rows, axis=1)
    cst_ref[...] = jnp.concatenate(c_rows, axis=1)


TERM_ROWS = 16


def _split3(v):
    v1 = v.astype(BF16)
    r1 = v - v1.astype(F32)
    v2 = r1.astype(BF16)
    v3 = (r1 - v2.astype(F32)).astype(BF16)
    return v1, v2, v3


def _ssd_sample_pre(xbc, conv_flat, dtr, conv_w, conv_b, dtb, alog):
    n = xbc.shape[0]
    full = lambda shape: pl.BlockSpec(shape, lambda i: (0,) * len(shape))
    stack = SSD_GROUPS * TERM_ROWS * D_STATE
    return pl.pallas_call(
        _ssd_sample_pre_kernel,
        grid=(1,),
        in_specs=[full((n, CONV_DIM)), full((n, 3 * CONV_DIM)), full((n, LANES)), full((CONV_W, CONV_DIM)),
                  full((1, CONV_DIM)), full((1, LANES)), full((1, LANES))],
        out_specs=[full((n, 3 * CONV_DIM)), full((n, CONV_DIM)), full((n, D_INNER)), full((n, LANES)),
                   full((n, stack)), full((n, stack))],
        out_shape=[jax.ShapeDtypeStruct((n, 3 * CONV_DIM), F32), jax.ShapeDtypeStruct((n, CONV_DIM), F32),
                   jax.ShapeDtypeStruct((n, D_INNER), F32), jax.ShapeDtypeStruct((n, LANES), F32),
                   jax.ShapeDtypeStruct((n, stack), BF16), jax.ShapeDtypeStruct((n, stack), BF16)],
        compiler_params=_params(("arbitrary",), 32 * n * CONV_DIM * 4),
        name="ssd_sample_pre",
    )(xbc, conv_flat, dtr, conv_w, conv_b, dtb, alog)


STATE_SAMPLES = 4


def _ssd_sample_state_kernel(da_ref, st_ref, xdt_ref, bst_ref, cst_ref, nst_ref, y_ref):
    first = pl.program_id(0) * STATE_SAMPLES
    rows_g = D_INNER // SSD_GROUPS
    sub = lax.broadcasted_iota(I32, (TERM_ROWS, D_INNER), 0)
    for s in range(STATE_SAMPLES):
        x1, x2, x3 = [t.astype(F32) for t in _split3(xdt_ref[s])]
        terms = jnp.where(sub == 0, x1, jnp.where(sub == 1, x1, jnp.where(sub == 2, x2, jnp.where(
            sub == 3, x1, jnp.where(sub == 4, x2, jnp.where(sub == 5, x3, 0.0)))))).astype(BF16)
        ys = []
        for g in range(SSD_GROUPS):
            tile = slice(g * TERM_ROWS, (g + 1) * TERM_ROWS)
            upd = lax.dot_general(terms[:, g * rows_g:(g + 1) * rows_g], bst_ref[s, tile, :],
                                  (((0,), (0,)), ((), ())), preferred_element_type=F32)
            halves = []
            for hh in range(rows_g // SSD_HEADDIM):
                head = g * (rows_g // SSD_HEADDIM) + hh
                rows = slice(head * SSD_HEADDIM, (head + 1) * SSD_HEADDIM)
                h_new = (da_ref[first + s, head] * st_ref[s, rows, :]
                         + upd[hh * SSD_HEADDIM:(hh + 1) * SSD_HEADDIM, :])
                nst_ref[s, rows, :] = h_new
                halves.append(h_new)
            h_g = jnp.concatenate(halves, axis=0)
            h_hi = h_g.astype(BF16)
            h_lo = (h_g - h_hi.astype(F32)).astype(BF16)
            nt = (((1,), (1,)), ((), ()))
            o_hi = lax.dot_general(cst_ref[s, tile, :], h_hi, nt, preferred_element_type=F32)
            o_lo = lax.dot_general(cst_ref[s, tile, :], h_lo, nt, preferred_element_type=F32)
            ys.append(o_hi[0:1, :] + o_hi[1:2, :] + o_lo[0:1, :])
        y_ref[s] = jnp.concatenate(ys, axis=1)


def _ssd_sample_state(state, xdt, da, bst, cst):
    n = state.shape[0]
    ns = STATE_SAMPLES
    stack_rows = SSD_GROUPS * TERM_ROWS
    return pl.pallas_call(
        _ssd_sample_state_kernel,
        grid=(n // ns,),
        in_specs=[pl.BlockSpec(memory_space=pltpu.SMEM),
                  pl.BlockSpec((ns, D_INNER, D_STATE), lambda i: (i, 0, 0)),
                  pl.BlockSpec((ns, 1, D_INNER), lambda i: (i, 0, 0)),
                  pl.BlockSpec((ns, stack_rows, D_STATE), lambda i: (i, 0, 0)),
                  pl.BlockSpec((ns, stack_rows, D_STATE), lambda i: (i, 0, 0))],
        out_specs=[pl.BlockSpec((ns, D_INNER, D_STATE), lambda i: (i, 0, 0)),
                   pl.BlockSpec((ns, 1, D_INNER), lambda i: (i, 0, 0))],
        out_shape=[jax.ShapeDtypeStruct((n, D_INNER, D_STATE), F32),
                   jax.ShapeDtypeStruct((n, 1, D_INNER), F32)],
        compiler_params=_params(("arbitrary",), 6 * ns * D_INNER * D_STATE * 4),
        name="ssd_sample_state",
    )(da, state, xdt, bst, cst)


def _layernorm(v, g, b):
    mu = jnp.mean(v, axis=-1, keepdims=True)
    d = v - mu
    var = jnp.mean(d * d, axis=-1, keepdims=True)
    return d * lax.rsqrt(var + EPS) * g + b


def _sgu_sample_kernel(u_ref, v_ref, lng_ref, lnb_ref, w0_ref, b0_ref, o_ref, vn_ref):
    vn = _layernorm(v_ref[...], lng_ref[...], lnb_ref[...])
    vn_ref[...] = vn
    o_ref[...] = (u_ref[...] * (w0_ref[...] * vn + b0_ref[...])).astype(o_ref.dtype)


def _sgu_sample(uv, ln_g, ln_b, w0_rep, b0_rep):
    n = uv.shape[0]
    fixed2 = lambda i: (0, 0)
    return pl.pallas_call(
        _sgu_sample_kernel,
        grid=(1,),
        in_specs=[pl.BlockSpec((n, D_SGU), lambda i: (0, 0)), pl.BlockSpec((n, D_SGU), lambda i: (0, 1)),
                  pl.BlockSpec((1, D_SGU), fixed2), pl.BlockSpec((1, D_SGU), fixed2),
                  pl.BlockSpec((1, D_SGU), fixed2), pl.BlockSpec((1, D_SGU), fixed2)],
        out_specs=[pl.BlockSpec((n, D_SGU), fixed2), pl.BlockSpec((n, D_SGU), fixed2)],
        out_shape=[jax.ShapeDtypeStruct((n, D_SGU), BF16), jax.ShapeDtypeStruct((n, D_SGU), F32)],
        compiler_params=_params(("arbitrary",), 12 * n * D_SGU * 4),
        name="sgu_sample",
    )(uv, uv, ln_g, ln_b, w0_rep, b0_rep)


def _experts_to_lanes(col, sub, lane):
    return jnp.sum(jnp.where(sub == lane, col, 0.0), axis=0, keepdims=True)


def _route_kernel(lg_ref, loc_ref, gate_ref, bmeta_ref, tmeta_ref, *, tile, n_valid):
    nb, _, tb = lg_ref.shape
    sub = lax.broadcasted_iota(I32, (N_EXPERTS, LANES), 0)
    lane = lax.broadcasted_iota(I32, (N_EXPERTS, LANES), 1)
    subf = lax.broadcasted_iota(I32, (N_EXPERTS, tb), 0).astype(F32)
    tok = lax.broadcasted_iota(I32, (N_EXPERTS, tb), 1)
    incl = jnp.where(lax.broadcasted_iota(I32, (tb, tb), 0) <= lax.broadcasted_iota(I32, (tb, tb), 1), 1.0, 0.0)
    neg = jnp.float32(-jnp.inf)
    none = jnp.float32(N_EXPERTS)

    def select(k):
        blk = lg_ref[k]
        l = blk[0:N_EXPERTS] + blk[N_EXPERTS:2 * N_EXPERTS]
        m1 = jnp.max(l, axis=0, keepdims=True)
        i1 = jnp.min(jnp.where(l == m1, subf, none), axis=0, keepdims=True)
        l2 = jnp.where(subf == i1, neg, l)
        m2 = jnp.max(l2, axis=0, keepdims=True)
        i2 = jnp.min(jnp.where(l2 == m2, subf, none), axis=0, keepdims=True)
        valid = (k * tb + tok) < n_valid
        sel = jnp.where(valid, jnp.where(subf == i1, 1.0, jnp.where(subf == i2, 1.0, 0.0)), 0.0)
        return m1, i1, m2, i2, valid, sel

    def run_rows(cnt):
        return jnp.ceil(cnt / SUBLANES) * SUBLANES

    def count_body(k, carry):
        return carry + run_rows(jnp.sum(select(k)[5], axis=1, keepdims=True))

    counts = lax.fori_loop(0, nb, count_body, jnp.zeros((N_EXPERTS, 1), F32))
    tiles = jnp.ceil(counts / tile)
    cum_incl = jnp.sum(jnp.where(lane <= sub, _experts_to_lanes(tiles, sub, lane), 0.0), axis=1, keepdims=True)
    offset = (cum_incl - tiles) * tile
    tile_expert = jnp.sum(jnp.where(lane.astype(F32) >= cum_incl, 1.0, 0.0), axis=0, keepdims=True)
    tile_expert = jnp.minimum(tile_expert, N_EXPERTS - 1.0)
    n_used = jnp.max(cum_incl, axis=0, keepdims=True)
    region_end = _experts_to_lanes(offset + counts, sub, lane)
    region_pad = _experts_to_lanes(tiles * tile - counts, sub, lane)
    tmeta_ref[...] = jnp.where(sub == 0, tile_expert, jnp.where(sub == 1, n_used, jnp.where(
        sub == 2, region_end, jnp.where(sub == 3, region_pad, 0.0)))).astype(I32)

    def place_body(k, before):
        m1, i1, m2, i2, valid, sel = select(k)
        run = jnp.dot(sel, incl, preferred_element_type=F32)
        cnt = run_rows(run[:, tb - 1:tb])
        lstart = jnp.sum(jnp.where(lane < sub, _experts_to_lanes(cnt, sub, lane), 0.0), axis=1, keepdims=True)
        local = lstart + run - sel
        loc1 = jnp.sum(jnp.where(subf == i1, local, 0.0), axis=0, keepdims=True)
        loc2 = jnp.sum(jnp.where(subf == i2, local, 0.0), axis=0, keepdims=True)
        live = (k * tb + tok[0:1, :]) < n_valid
        loc_ref[k] = jnp.where(subf == 0.0, jnp.where(live, loc1, -1.0),
                               jnp.where(subf == 1.0, jnp.where(live, loc2, -1.0), 0.0))
        e = jnp.exp(m2 - m1)
        gate_ref[k] = jnp.where(subf == 0.0, 1.0 / (1.0 + e), jnp.where(subf == 1.0, e / (1.0 + e), 0.0))
        bmeta_ref[k] = jnp.where(lane == 0, offset + before,
                                 jnp.where(lane == 1, cnt, jnp.where(lane == 2, lstart, 0.0))).astype(I32)
        return before + cnt

    lax.fori_loop(0, nb, place_body, jnp.zeros((N_EXPERTS, 1), F32))


def _route(logits3, tile, n_valid):
    nb, _, tb = logits3.shape
    tok_blk = pl.BlockSpec((nb, N_EXPERTS, tb), lambda i: (0, 0, 0))
    return pl.pallas_call(
        functools.partial(_route_kernel, tile=tile, n_valid=n_valid),
        grid=(1,),
        in_specs=[pl.BlockSpec((nb, 2 * N_EXPERTS, tb), lambda i: (0, 0, 0))],
        out_specs=[tok_blk, tok_blk, pl.BlockSpec((nb, N_EXPERTS, LANES), lambda i: (0, 0, 0)),
                   pl.BlockSpec((N_EXPERTS, LANES), lambda i: (0, 0))],
        out_shape=[jax.ShapeDtypeStruct((nb, N_EXPERTS, tb), F32),
                   jax.ShapeDtypeStruct((nb, N_EXPERTS, tb), F32),
                   jax.ShapeDtypeStruct((nb, N_EXPERTS, LANES), I32),
                   jax.ShapeDtypeStruct((N_EXPERTS, LANES), I32)],
        compiler_params=_params(("arbitrary",), 16 * nb * N_EXPERTS * tb * 4),
        name="moe_route",
    )(logits3)


def _run_copies(bm_ref, blk, vmem_ref, hbm_ref, sem, *, tb, to_hbm, wait):
    for e in range(N_EXPERTS):
        base = (blk * N_EXPERTS + e) * 3
        start, cnt, lstart = bm_ref[base], bm_ref[base + 1], bm_ref[base + 2]
        off = 0
        size = tb
        while size >= SUBLANES:
            @pl.when((cnt & size) != 0)
            def _(size=size, off=off):
                v = vmem_ref.at[pl.ds(pl.multiple_of(lstart + off, SUBLANES), size)]
                h = hbm_ref.at[pl.ds(pl.multiple_of(start + off, SUBLANES), size)]
                cp = pltpu.make_async_copy(v, h, sem) if to_hbm else pltpu.make_async_copy(h, v, sem)
                if wait:
                    cp.wait()
                else:
                    cp.start()

            off = off + (cnt & size)
            size //= 2


def _compact_rows(tb):
    return 2 * tb + N_EXPERTS * SUBLANES


def _selection(loc_ref, rows, tb):
    r = lax.broadcasted_iota(I32, (rows, tb), 0).astype(F32)
    return r == loc_ref[0, 0:1, :], r == loc_ref[0, 1:2, :]


def _zero_fill(zm_ref, zbuf_ref, zgbuf_ref, xs_ref, gs_ref, zsem, *, tm, n_tiles, wait):
    def fill(row0, size):
        for src, dst in ((zbuf_ref, xs_ref), (zgbuf_ref, gs_ref)):
            cp = pltpu.make_async_copy(src.at[pl.ds(0, size)], dst.at[pl.ds(pl.multiple_of(row0, SUBLANES), size)],
                                       zsem)
            if wait:
                cp.wait()
            else:
                cp.start()

    for e in range(N_EXPERTS):
        end, pad = zm_ref[1 + 2 * e], zm_ref[2 + 2 * e]
        off = 0
        size = tm // 2
        while size >= SUBLANES:
            @pl.when((pad & size) != 0)
            def _(size=size, off=off):
                fill(end + off, size)

            off = off + (pad & size)
            size //= 2
    for t in range(n_tiles):
        @pl.when(t >= zm_ref[0])
        def _(t=t):
            fill(t * tm, tm)


def _moe_scatter_kernel(bm_ref, zm_ref, a_ref, loc_ref, gate_ref, xs_ref, gs_ref,
                        buf_ref, gbuf_ref, zbuf_ref, zgbuf_ref, sem, gsem, zsem, *, tb, tm, n_tiles):
    blk = pl.program_id(0)
    last = pl.num_programs(0) - 1
    slot = blk % 2

    @pl.when(blk == 0)
    def _():
        zbuf_ref[...] = jnp.zeros_like(zbuf_ref)
        zgbuf_ref[...] = jnp.zeros_like(zgbuf_ref)
        _zero_fill(zm_ref, zbuf_ref, zgbuf_ref, xs_ref, gs_ref, zsem, tm=tm, n_tiles=n_tiles, wait=False)

    def copies(block, wait):
        s = block % 2
        _run_copies(bm_ref, block, buf_ref.at[s], xs_ref, sem.at[s], tb=tb, to_hbm=True, wait=wait)
        _run_copies(bm_ref, block, gbuf_ref.at[s], gs_ref, gsem.at[s], tb=tb, to_hbm=True, wait=wait)

    @pl.when(blk >= 2)
    def _():
        copies(blk - 2, True)

    first, second = _selection(loc_ref, _compact_rows(tb), tb)
    pick = jnp.where(first, 1.0, jnp.where(second, 1.0, 0.0)).astype(BF16)
    buf_ref[slot] = jnp.dot(pick, a_ref[...], preferred_element_type=F32)
    gsel = jnp.where(first, gate_ref[0, 0:1, :], jnp.where(second, gate_ref[0, 1:2, :], 0.0))
    gbuf_ref[slot] = jnp.broadcast_to(jnp.sum(gsel, axis=1, keepdims=True), gbuf_ref.shape[1:])
    copies(blk, False)

    @pl.when(blk == last)
    def _():
        @pl.when(blk >= 1)
        def _():
            copies(blk - 1, True)

        copies(blk, True)
        _zero_fill(zm_ref, zbuf_ref, zgbuf_ref, xs_ref, gs_ref, zsem, tm=tm, n_tiles=n_tiles, wait=True)


def _moe_scatter(a, loc, gate, bmeta, zmeta, n_slots, tb, tm):
    m, d = a.shape
    any_spec = pl.BlockSpec(memory_space=pl.ANY)
    tok_spec = pl.BlockSpec((1, N_EXPERTS, tb), lambda i, bm, zm: (i, 0, 0))
    grid_spec = pltpu.PrefetchScalarGridSpec(
        num_scalar_prefetch=2,
        grid=(m // tb,),
        in_specs=[pl.BlockSpec((tb, d), lambda i, bm, zm: (i, 0)), tok_spec, tok_spec],
        out_specs=[any_spec, any_spec],
        scratch_shapes=[pltpu.VMEM((2, _compact_rows(tb), d), F32), pltpu.VMEM((2, _compact_rows(tb), LANES), F32),
                        pltpu.VMEM((tm, d), F32), pltpu.VMEM((tm, LANES), F32),
                        pltpu.SemaphoreType.DMA((2,)), pltpu.SemaphoreType.DMA((2,)), pltpu.SemaphoreType.DMA(())],
    )
    return pl.pallas_call(
        functools.partial(_moe_scatter_kernel, tb=tb, tm=tm, n_tiles=n_slots // tm),
        grid_spec=grid_spec,
        out_shape=[jax.ShapeDtypeStruct((n_slots, d), F32), jax.ShapeDtypeStruct((n_slots, LANES), F32)],
        compiler_params=_params(("arbitrary",), 16 * tb * d * 4),
        name="moe_scatter",
    )(bmeta, zmeta, a, loc, gate)


def _new_weights(te_ref):
    i = pl.program_id(1)
    return (i == 0) | (te_ref[i] != te_ref[jnp.maximum(i - 1, 0)])


def _moe_gu_kernel(te_ref, nu_ref, x_ref, wg_ref, wu_ref, o_ref, wgb_ref, wub_ref):
    i = pl.program_id(1)

    @pl.when(_new_weights(te_ref))
    def _():
        wgb_ref[...] = wg_ref[0].astype(BF16)
        wub_ref[...] = wu_ref[0].astype(BF16)

    @pl.when(i < nu_ref[0])
    def _():
        x = x_ref[...].astype(BF16)
        for c in range(o_ref.shape[1] // MXU_COLS):
            cols = slice(c * MXU_COLS, (c + 1) * MXU_COLS)
            g = jnp.dot(x, wgb_ref[:, cols], preferred_element_type=F32)
            u = jnp.dot(x, wub_ref[:, cols], preferred_element_type=F32)
            o_ref[:, cols] = (_silu(g) * u).astype(o_ref.dtype)

    @pl.when(i >= nu_ref[0])
    def _():
        o_ref[...] = jnp.zeros_like(o_ref)


def _used_tile(i, nu):
    return jnp.minimum(i, nu[0] - 1)


def _moe_gu(xs, w_gu, te, nu, tm, tn):
    s, k = xs.shape
    f = D_FF_EXPERT
    nbytes = 2 * (2 * k * tn * 4 + k * tn * 2) + 2 * tm * k * 4 + 2 * tm * tn * 2 + 4 * tm * MXU_COLS * 4
    grid_spec = pltpu.PrefetchScalarGridSpec(
        num_scalar_prefetch=2,
        grid=(f // tn, s // tm),
        in_specs=[pl.BlockSpec((tm, k), lambda j, i, te, nu: (_used_tile(i, nu), 0)),
                  pl.BlockSpec((1, k, tn), lambda j, i, te, nu: (te[i], 0, j)),
                  pl.BlockSpec((1, k, tn), lambda j, i, te, nu: (te[i], 0, j + f // tn))],
        out_specs=pl.BlockSpec((tm, tn), lambda j, i, te, nu: (i, j)),
        scratch_shapes=[pltpu.VMEM((k, tn), BF16), pltpu.VMEM((k, tn), BF16)],
    )
    return pl.pallas_call(
        _moe_gu_kernel,
        grid_spec=grid_spec,
        out_shape=jax.ShapeDtypeStruct((s, f), BF16),
        compiler_params=_params(("arbitrary", "arbitrary"), nbytes),
        name="moe_gate_up",
    )(te, nu, xs, w_gu, w_gu)


def _moe_down_kernel(te_ref, nu_ref, x_ref, w_ref, gs_ref, o_ref, wb_ref):
    i = pl.program_id(1)

    @pl.when(_new_weights(te_ref))
    def _():
        wb_ref[...] = w_ref[0].astype(BF16)

    @pl.when(i < nu_ref[0])
    def _():
        o_ref[...] = gs_ref[:, 0:1] * jnp.dot(x_ref[...], wb_ref[...], preferred_element_type=F32)

    @pl.when(i >= nu_ref[0])
    def _():
        o_ref[...] = jnp.zeros_like(o_ref)


def _moe_down(act, w_down, gs, te, nu, tm, tn):
    s, k = act.shape
    d = w_down.shape[2]
    nbytes = 2 * k * tn * 4 + k * tn * 2 + 2 * tm * k * 2 + 3 * tm * tn * 4
    grid_spec = pltpu.PrefetchScalarGridSpec(
        num_scalar_prefetch=2,
        grid=(d // tn, s // tm),
        in_specs=[pl.BlockSpec((tm, k), lambda j, i, te, nu: (_used_tile(i, nu), 0)),
                  pl.BlockSpec((1, k, tn), lambda j, i, te, nu: (te[i], 0, j)),
                  pl.BlockSpec((tm, LANES), lambda j, i, te, nu: (_used_tile(i, nu), 0))],
        out_specs=pl.BlockSpec((tm, tn), lambda j, i, te, nu: (i, j)),
        scratch_shapes=[pltpu.VMEM((k, tn), BF16)],
    )
    return pl.pallas_call(
        _moe_down_kernel,
        grid_spec=grid_spec,
        out_shape=jax.ShapeDtypeStruct((s, d), F32),
        compiler_params=_params(("arbitrary", "arbitrary"), nbytes),
        name="moe_down",
    )(te, nu, act, w_down, gs)


def _moe_combine_kernel(bm_ref, h_ref, locc_ref, g_ref, ys_ref, op_ref, os_ref, ybuf_ref, sem, *, tb, prompt_blocks):
    blk = pl.program_id(0)
    slot = blk % 2

    def copies(block, wait):
        s = block % 2
        _run_copies(bm_ref, block, ybuf_ref.at[s], ys_ref, sem.at[s], tb=tb, to_hbm=False, wait=wait)

    @pl.when(blk == 0)
    def _():
        ybuf_ref[...] = jnp.zeros_like(ybuf_ref)
        copies(blk, False)

    @pl.when(blk + 1 < pl.num_programs(0))
    def _():
        copies(blk + 1, False)

    r = lax.broadcasted_iota(I32, (tb, _compact_rows(tb)), 1).astype(F32)
    pick = jnp.where(r == locc_ref[:, 0:1], 1.0, jnp.where(r == locc_ref[:, 1:2], 1.0, 0.0)).astype(BF16)
    copies(blk, True)
    y = ybuf_ref[slot]
    y_hi = y.astype(BF16)
    y_lo = (y - y_hi.astype(F32)).astype(BF16)
    moe = jnp.dot(pick, y_hi, preferred_element_type=F32) + jnp.dot(pick, y_lo, preferred_element_type=F32)
    out = _rms(h_ref[...] + moe, g_ref[...])

    @pl.when(blk < prompt_blocks)
    def _():
        op_ref[...] = out

    @pl.when(blk == prompt_blocks)
    def _():
        os_ref[...] = out


def _moe_combine(h, locc, bmeta, ys, g, tb, n_prompt):
    m, d = h.shape
    prompt_blocks = n_prompt // tb
    any_spec = pl.BlockSpec(memory_space=pl.ANY)
    grid_spec = pltpu.PrefetchScalarGridSpec(
        num_scalar_prefetch=1,
        grid=(m // tb,),
        in_specs=[pl.BlockSpec((tb, d), lambda i, bm: (i, 0)),
                  pl.BlockSpec((tb, N_EXPERTS), lambda i, bm: (i, 0)),
                  pl.BlockSpec((1, d), lambda i, bm: (0, 0)),
                  any_spec],
        out_specs=[pl.BlockSpec((tb, d), lambda i, bm: (jnp.minimum(i, prompt_blocks - 1), 0)),
                   pl.BlockSpec((tb, d), lambda i, bm: (0, 0))],
        scratch_shapes=[pltpu.VMEM((2, _compact_rows(tb), d), F32), pltpu.SemaphoreType.DMA((2,))],
    )
    return pl.pallas_call(
        functools.partial(_moe_combine_kernel, tb=tb, prompt_blocks=prompt_blocks),
        grid_spec=grid_spec,
        out_shape=[jax.ShapeDtypeStruct((n_prompt, d), F32), jax.ShapeDtypeStruct((tb, d), F32)],
        compiler_params=_params(("arbitrary",), 16 * tb * d * 4),
        name="moe_combine",
    )(bmeta, h, locc, g.reshape(1, d), ys)


MOE_TOKEN_BLOCK = 256
MOE_ROW_TILE = 512


def _moe(h, a, lg_parts, n_valid, n_prompt, w_gu, w_down, g_final):
    m, d = h.shape
    tb, tm = MOE_TOKEN_BLOCK, MOE_ROW_TILE
    nb = m // tb
    lg3 = lg_parts[:, :2 * N_EXPERTS].reshape(nb, tb, 2 * N_EXPERTS).transpose(0, 2, 1)
    n_tiles = pl.cdiv(2 * n_valid + nb * N_EXPERTS * (SUBLANES - 1), tm) + N_EXPERTS
    loc, gate, bmeta, tmeta = _route(lg3, tm, n_valid)
    te = tmeta[0, :n_tiles]
    nu = tmeta[1, 0:1]
    bm = bmeta[:, :, :3].reshape(-1)
    zm = jnp.concatenate([nu, tmeta[2:4, :N_EXPERTS].T.reshape(-1)])
    locc = loc.transpose(0, 2, 1).reshape(m, N_EXPERTS)
    xs, gs = _moe_scatter(a, loc, gate, bm, zm, n_tiles * tm, tb, tm)
    act = _moe_gu(xs, w_gu, te, nu, tm, D_FF_EXPERT // 2)
    ys = _moe_down(act, w_down, gs, te, nu, tm, d)
    return _moe_combine(h, locc, bm, ys, g_final, tb, n_prompt)


def _pad_lanes(v):
    return jnp.pad(v.reshape(1, -1), ((0, 0), (0, LANES - v.shape[-1])))


def kernel(x_prompt, x_sample, state_ssm, state_conv, norm_mix, norm_ffn, norm_final, ssd_w_in, ssd_conv_w,
           ssd_conv_b, ssd_dt_bias, ssd_a_log, ssd_d, ssd_gnorm, ssd_w_out, sgu_w_in, sgu_b_in, sgu_ln_g,
           sgu_ln_b, sgu_w_s, sgu_b_s, sgu_w_out, ffn_w_gu, ffn_w_down, moe_w_router, moe_w_gu, moe_w_down):
    batch, seq, d = x_prompt.shape
    n_s = x_sample.shape[0]
    mp = batch * seq
    xp = x_prompt.reshape(mp, d)
    xs = x_sample.reshape(n_s, d)

    w_in = ssd_w_in[0]
    w_dt = jnp.pad(w_in[:, D_INNER + CONV_DIM:], ((0, 0), (0, LANES - SSD_HEADS)))
    dtb = _pad_lanes(ssd_dt_bias[0])
    alog = _pad_lanes(ssd_a_log[0])
    d_rep = jnp.repeat(ssd_d[0], SSD_HEADDIM).reshape(1, D_INNER)
    gn_w = ssd_gnorm[0].reshape(1, D_INNER)
    conv_w = ssd_conv_w[0]
    conv_b = ssd_conv_b[0].reshape(1, CONV_DIM)
    ln_g = sgu_ln_g[0].reshape(1, D_SGU)
    ln_b = sgu_ln_b[0].reshape(1, D_SGU)
    bs_t = jnp.pad(sgu_b_s[0].T, ((0, 0), (0, LANES - SGU_HEADS)))
    w0_rep = jnp.repeat(sgu_w_s[0][:, 0, 0], SGU_HEAD_DIM).reshape(1, D_SGU)
    b0_rep = jnp.repeat(sgu_b_s[0][:, 0], SGU_HEAD_DIM).reshape(1, D_SGU)
    wr_hi = moe_w_router[0].astype(BF16)
    wr_lo = (moe_w_router[0] - wr_hi.astype(F32)).astype(BF16)
    w3 = jnp.concatenate([wr_hi, wr_lo, jnp.zeros((d, LANES - 2 * N_EXPERTS), BF16)], axis=1)

    def dense_ffn(h1, a1, tm):
        act = _mm_swiglu(a1, ffn_w_gu[0], D_FF_DENSE, min(2 * tm, a1.shape[0]), 1408, "ffn_gate_up")
        return _mm_resnorm(act, ffn_w_down[0], h1, norm_mix[1], tm, "ffn_down")

    a0p, zp, dtrp = _norm_in_proj(xp, norm_mix[0], w_in, w_dt, 1024)
    xbcp = _mm_plain(a0p, w_in, D_INNER, CONV_DIM, 1024, 1024, F32, "ssd_in_xbc")
    per_seq = lambda t: t.reshape(batch, seq, t.shape[-1])
    ygp, ssm_p, conv_p = _ssd_prompt(per_seq(xbcp), per_seq(zp), per_seq(dtrp), conv_w, conv_b, dtb, alog, d_rep,
                                     gn_w)
    h1p, a1p = _mm_resnorm(ygp.reshape(mp, D_INNER), ssd_w_out[0], xp, norm_ffn[0], 512, "ssd_out")
    h2p, a2p = dense_ffn(h1p, a1p, 512)

    a0s, zs, dtrs = _norm_in_proj(xs, norm_mix[0], w_in, w_dt, n_s)
    xbcs = _mm_plain(a0s, w_in, D_INNER, CONV_DIM, n_s, 1024, F32, "ssd_in_xbc")
    conv_s, xas, xdts, das, bsts, csts = _ssd_sample_pre(xbcs, state_conv[0].reshape(n_s, 3 * CONV_DIM), dtrs,
                                                         conv_w, conv_b, dtb, alog)
    stack_rows = SSD_GROUPS * TERM_ROWS
    ssm_s, ys3 = _ssd_sample_state(state_ssm[0].reshape(n_s, D_INNER, D_STATE), xdts.reshape(n_s, 1, D_INNER),
                                   das[:, :SSD_HEADS], bsts.reshape(n_s, stack_rows, D_STATE),
                                   csts.reshape(n_s, stack_rows, D_STATE))
    h1s, a1s = _ssd_out(ys3.reshape(n_s, D_INNER), xas, zs, d_rep, gn_w, ssd_w_out[0], xs, norm_ffn[0], n_s)
    h2s, a2s = dense_ffn(h1s, a1s, n_s)

    uvp = _mm_bias_gelu(a2p, sgu_w_in[0], sgu_b_in[0], 1024, 2048, BF16, "sgu_in")
    uvs = _mm_bias_gelu(a2s, sgu_w_in[0], sgu_b_in[0], n_s, 1024, F32, "sgu_in")
    ss, v_s = _sgu_sample(uvs, ln_g, ln_b, w0_rep, b0_rep)
    tm_out = 512
    pad_rows = ((0, tm_out - n_s), (0, 0))
    h3, a3, lg = _sgu_out(uvp, jnp.pad(ss, pad_rows), ln_g, ln_b, sgu_w_s[0], bs_t, sgu_w_out[0], h2p,
                          jnp.pad(h2s, pad_rows), norm_ffn[1], w3, tm_out)

    y_prompt, y_tail = _moe(h3, a3, lg, mp + n_s, mp, moe_w_gu[0], moe_w_down[0], norm_final)
    y_sample = y_tail[:n_s]

    return (y_prompt.reshape(batch, seq, d),
            y_sample.reshape(n_s, 1, d),
            ssm_p[None],
            conv_p[None],
            ssm_s.reshape(1, n_s, SSD_HEADS, SSD_HEADDIM, D_STATE),
            conv_s.reshape(1, n_s, CONV_W - 1, CONV_DIM),
            v_s.reshape(1, n_s, 1, D_SGU))
```

```python
import functools

import jax
import jax.numpy as jnp
from jax import lax
from jax.experimental import pallas as pl
from jax.experimental.pallas import tpu as pltpu

F32 = jnp.float32
BF16 = jnp.bfloat16
I32 = jnp.int32
HIGHEST = lax.Precision.HIGHEST

D_MODEL = 1024
D_INNER = 2048
SSD_HEADS = 32
SSD_HEADDIM = 64
SSD_GROUPS = 4
D_STATE = 128
CONV_W = 4
CONV_DIM = D_INNER + 2 * SSD_GROUPS * D_STATE
CHUNK = 128
D_SGU = 2048
SGU_HEADS = 8
SGU_HEAD_DIM = D_SGU // SGU_HEADS
D_FF_DENSE = 2816
N_EXPERTS = 8
D_FF_EXPERT = 3584
EPS = 1e-5

MXU_COLS = 256
ROW_SUB = 256
LANES = 128
SUBLANES = 8
VMEM_CAP = 60000 * 1024
VMEM_FLOOR = 32 * 1024 * 1024


def _vmem_limit(nbytes):
    return int(min(max(nbytes * 5 // 4 + (4 << 20), VMEM_FLOOR), VMEM_CAP))


def _params(sem, nbytes):
    return pltpu.CompilerParams(dimension_semantics=sem, vmem_limit_bytes=_vmem_limit(nbytes))


LOG2E = 1.4426950408889634


def _silu(x):
    return x / (1.0 + jnp.exp2(x * (-LOG2E)))


def _rms(x, g):
    return x * lax.rsqrt(jnp.mean(x * x, axis=-1, keepdims=True) + EPS) * g


def _bdot(a, b):
    return jnp.dot(a.astype(BF16), b.astype(BF16), preferred_element_type=F32)


def _cast_transposed(wt_ref, wb_ref):
    step = 2 * MXU_COLS
    for r0 in range(0, wt_ref.shape[0], step):
        r1 = min(r0 + step, wt_ref.shape[0])
        wb_ref[:, r0:r1] = wt_ref[r0:r1, :].T.astype(BF16)


def _mm_plain_kernel(x_ref, wt_ref, o_ref, wb_ref):
    @pl.when(pl.program_id(1) == 0)
    def _():
        _cast_transposed(wt_ref, wb_ref)

    o_ref[...] = _bdot(x_ref[...], wb_ref[...]).astype(o_ref.dtype)


def _mm_bias_gelu_kernel(x_ref, w_ref, b_ref, o_ref, wb_ref):
    @pl.when(pl.program_id(1) == 0)
    def _():
        wb_ref[...] = w_ref[...].astype(BF16)

    x = x_ref[...].astype(BF16)
    for c0 in range(0, o_ref.shape[1], 2 * MXU_COLS):
        cols = slice(c0, c0 + 2 * MXU_COLS)
        y = jnp.dot(x, wb_ref[:, cols], preferred_element_type=F32) + b_ref[:, cols]
        o_ref[:, cols] = (0.5 * y * (1.0 + lax.erf(y * (2.0 ** -0.5)))).astype(o_ref.dtype)


def _mm_swiglu_kernel(x_ref, wg_ref, wu_ref, o_ref, wgb_ref, wub_ref):
    @pl.when(pl.program_id(1) == 0)
    def _():
        wgb_ref[...] = wg_ref[...].astype(BF16)
        wub_ref[...] = wu_ref[...].astype(BF16)

    x = x_ref[...].astype(BF16)
    tn = o_ref.shape[1]
    for c0 in range(0, tn, MXU_COLS):
        cols = slice(c0, min(c0 + MXU_COLS, tn))
        g = jnp.dot(x, wgb_ref[:, cols], preferred_element_type=F32)
        u = jnp.dot(x, wub_ref[:, cols], preferred_element_type=F32)
        o_ref[:, cols] = (_silu(g) * u).astype(o_ref.dtype)


def _wspec(k, tn, col_block0):
    return pl.BlockSpec((k, tn), lambda j, i: (0, j + col_block0), pipeline_mode=pl.Buffered(1))


def _mm_plain(x, wt, col0, n, tm, tn, out_dtype, name):
    m, k = x.shape
    nbytes = k * tn * 6 + 2 * tm * k * x.dtype.itemsize + 2 * tm * tn * 4 + 2 * MXU_COLS * k * 8
    row_block0 = col0 // tn
    return pl.pallas_call(
        _mm_plain_kernel,
        grid=(n // tn, m // tm),
        in_specs=[pl.BlockSpec((tm, k), lambda j, i: (i, 0)),
                  pl.BlockSpec((tn, k), lambda j, i: (j + row_block0, 0), pipeline_mode=pl.Buffered(1))],
        out_specs=pl.BlockSpec((tm, tn), lambda j, i: (i, j)),
        out_shape=jax.ShapeDtypeStruct((m, n), out_dtype),
        scratch_shapes=[pltpu.VMEM((k, tn), BF16)],
        compiler_params=_params(("arbitrary", "arbitrary"), nbytes),
        name=name,
    )(x, wt)


def _norm_in_proj_kernel(x_ref, g_ref, wz_ref, wdt_ref, a_ref, z_ref, dtr_ref, wzb_ref, wdtb_ref):
    @pl.when(pl.program_id(0) == 0)
    def _():
        _cast_transposed(wz_ref, wzb_ref)
        _cast_transposed(wdt_ref, wdtb_ref)

    a = _rms(x_ref[...], g_ref[...]).astype(BF16)
    a_ref[...] = a
    z_ref[...] = jnp.dot(a, wzb_ref[...], preferred_element_type=F32).astype(z_ref.dtype)
    dtr_ref[...] = jnp.dot(a, wdtb_ref[...], preferred_element_type=F32)


def _norm_in_proj(x, g, w_in_t, w_dt_t, tm):
    m, k = x.shape
    n = D_INNER
    row = lambda i: (i, 0)
    fixed = lambda i: (0, 0)
    nbytes = k * n * 6 + k * LANES * 6 + 2 * tm * k * 4 + 2 * tm * k * 2 + 2 * tm * n * 2 + 2 * tm * n * 4
    return pl.pallas_call(
        _norm_in_proj_kernel,
        grid=(m // tm,),
        in_specs=[pl.BlockSpec((tm, k), row), pl.BlockSpec((1, k), fixed),
                  pl.BlockSpec((n, k), fixed, pipeline_mode=pl.Buffered(1)),
                  pl.BlockSpec((LANES, k), fixed)],
        out_specs=[pl.BlockSpec((tm, k), row), pl.BlockSpec((tm, n), row), pl.BlockSpec((tm, LANES), row)],
        out_shape=[jax.ShapeDtypeStruct((m, k), BF16), jax.ShapeDtypeStruct((m, n), BF16),
                   jax.ShapeDtypeStruct((m, LANES), F32)],
        scratch_shapes=[pltpu.VMEM((k, n), BF16), pltpu.VMEM((k, LANES), BF16)],
        compiler_params=_params(("arbitrary",), nbytes),
        name="norm_in_proj",
    )(x, g.reshape(1, k), w_in_t, w_dt_t)


def _mm_bias_gelu(x, w, b, tm, tn, out_dtype, name):
    m, k = x.shape
    n = w.shape[1]
    nbytes = k * tn * 6 + 2 * tm * k * x.dtype.itemsize + 2 * tm * tn * 4
    return pl.pallas_call(
        _mm_bias_gelu_kernel,
        grid=(n // tn, m // tm),
        in_specs=[pl.BlockSpec((tm, k), lambda j, i: (i, 0)), _wspec(k, tn, 0),
                  pl.BlockSpec((1, tn), lambda j, i: (0, j))],
        out_specs=pl.BlockSpec((tm, tn), lambda j, i: (i, j)),
        out_shape=jax.ShapeDtypeStruct((m, n), out_dtype),
        scratch_shapes=[pltpu.VMEM((k, tn), BF16)],
        compiler_params=_params(("arbitrary", "arbitrary"), nbytes),
        name=name,
    )(x, w, b.reshape(1, n))


def _mm_swiglu(x, w_gu, d_ff, tm, tn, name):
    m, k = x.shape
    nbytes = 2 * k * tn * 6 + 2 * tm * k * x.dtype.itemsize + 2 * tm * tn * 2 + 3 * tm * tn * 4
    return pl.pallas_call(
        _mm_swiglu_kernel,
        grid=(d_ff // tn, m // tm),
        in_specs=[pl.BlockSpec((tm, k), lambda j, i: (i, 0)), _wspec(k, tn, 0), _wspec(k, tn, d_ff // tn)],
        out_specs=pl.BlockSpec((tm, tn), lambda j, i: (i, j)),
        out_shape=jax.ShapeDtypeStruct((m, d_ff), BF16),
        scratch_shapes=[pltpu.VMEM((k, tn), BF16), pltpu.VMEM((k, tn), BF16)],
        compiler_params=_params(("arbitrary", "arbitrary"), nbytes),
        name=name,
    )(x, w_gu, w_gu)


def _mm_resnorm_kernel(x_ref, w_ref, r_ref, g_ref, h_ref, a_ref, wb_ref):
    @pl.when(pl.program_id(0) == 0)
    def _():
        wb_ref[...] = w_ref[...].astype(BF16)

    h = r_ref[...] + _bdot(x_ref[...], wb_ref[...])
    h_ref[...] = h
    a_ref[...] = _rms(h, g_ref[...]).astype(a_ref.dtype)


def _mm_resnorm(x, w, resid, g, tm, name):
    m, k = x.shape
    d = w.shape[1]
    nbytes = k * d * 6 + 2 * tm * k * 2 + 8 * tm * d * 4
    return pl.pallas_call(
        _mm_resnorm_kernel,
        grid=(m // tm,),
        in_specs=[pl.BlockSpec((tm, k), lambda i: (i, 0)),
                  pl.BlockSpec((k, d), lambda i: (0, 0), pipeline_mode=pl.Buffered(1)),
                  pl.BlockSpec((tm, d), lambda i: (i, 0)),
                  pl.BlockSpec((1, d), lambda i: (0, 0))],
        out_specs=[pl.BlockSpec((tm, d), lambda i: (i, 0)), pl.BlockSpec((tm, d), lambda i: (i, 0))],
        out_shape=[jax.ShapeDtypeStruct((m, d), F32), jax.ShapeDtypeStruct((m, d), BF16)],
        scratch_shapes=[pltpu.VMEM((k, d), BF16)],
        compiler_params=_params(("arbitrary",), nbytes),
        name=name,
    )(x, w, resid, g.reshape(1, d))


def _sgu_out_kernel(u_ref, v_ref, xt_ref, lng_ref, lnb_ref, ws_ref, bst_ref, w_ref, rp_ref, rt_ref, g_ref, w3_ref,
                    h_ref, a_ref, lg_ref, wb_ref, wsb_ref, y_ref, *, prompt_tiles):
    i = pl.program_id(0)
    q = CHUNK

    @pl.when(i == 0)
    def _():
        wb_ref[...] = w_ref[...].astype(BF16)
        causal = lax.broadcasted_iota(I32, (q, q), 0) >= lax.broadcasted_iota(I32, (q, q), 1)
        for g in range(SGU_HEADS):
            wsb_ref[g] = jnp.where(causal, ws_ref[g], 0.0).astype(BF16)

    def project(x_ref, r_ref, rows):
        h = r_ref[rows, :] + jnp.dot(x_ref[rows, :], wb_ref[...], preferred_element_type=F32)
        h_ref[rows, :] = h
        a = _rms(h, g_ref[...])
        a_hi = a.astype(BF16)
        a_lo = (a - a_hi.astype(F32)).astype(BF16)
        a_ref[rows, :] = a_hi
        lg_ref[rows, :] = (jnp.dot(a_hi, w3_ref[...], preferred_element_type=F32)
                           + jnp.dot(a_lo, w3_ref[...], preferred_element_type=F32))

    chunks = [slice(c * q, (c + 1) * q) for c in range(u_ref.shape[0] // q)]

    @pl.when(i < prompt_tiles)
    def _():
        for rows in chunks:
            vn = _layernorm(v_ref[rows, :].astype(F32), lng_ref[...], lnb_ref[...]).astype(BF16)
            for g in range(SGU_HEADS):
                cols = slice(g * SGU_HEAD_DIM, (g + 1) * SGU_HEAD_DIM)
                s = jnp.dot(wsb_ref[g], vn[:, cols], preferred_element_type=F32) + bst_ref[:, g:g + 1]
                y_ref[rows, cols] = (u_ref[rows, cols].astype(F32) * s).astype(BF16)
            project(y_ref, rp_ref, rows)

    @pl.when(i == prompt_tiles)
    def _():
        for rows in chunks:
            project(xt_ref, rt_ref, rows)


def _sgu_out(uv, xt, ln_g, ln_b, w_s, bs_t, w, rp, rt, g, w3, tm):
    mp = uv.shape[0]
    k, d = w.shape
    prompt_tiles = mp // tm
    last = prompt_tiles - 1
    rows = mp + tm
    prow = lambda i: (jnp.minimum(i, last), 0)
    fixed = lambda i: (0, 0)
    nbytes = k * d * 6 + 8 * tm * k * 2 + 12 * tm * d * 4 + 8 * CHUNK * k * 4
    return pl.pallas_call(
        functools.partial(_sgu_out_kernel, prompt_tiles=prompt_tiles),
        grid=(prompt_tiles + 1,),
        in_specs=[pl.BlockSpec((tm, k), prow),
                  pl.BlockSpec((tm, k), lambda i: (jnp.minimum(i, last), 1)),
                  pl.BlockSpec((tm, k), fixed),
                  pl.BlockSpec((1, k), fixed), pl.BlockSpec((1, k), fixed),
                  pl.BlockSpec((SGU_HEADS, CHUNK, CHUNK), lambda i: (0, 0, 0)),
                  pl.BlockSpec((CHUNK, LANES), fixed),
                  pl.BlockSpec((k, d), fixed, pipeline_mode=pl.Buffered(1)),
                  pl.BlockSpec((tm, d), prow),
                  pl.BlockSpec((tm, d), fixed),
                  pl.BlockSpec((1, d), fixed),
                  pl.BlockSpec((d, LANES), fixed)],
        out_specs=[pl.BlockSpec((tm, d), lambda i: (i, 0)), pl.BlockSpec((tm, d), lambda i: (i, 0)),
                   pl.BlockSpec((tm, LANES), lambda i: (i, 0))],
        out_shape=[jax.ShapeDtypeStruct((rows, d), F32), jax.ShapeDtypeStruct((rows, d), BF16),
                   jax.ShapeDtypeStruct((rows, LANES), F32)],
        scratch_shapes=[pltpu.VMEM((k, d), BF16), pltpu.VMEM((SGU_HEADS, CHUNK, CHUNK), BF16),
                        pltpu.VMEM((tm, k), BF16)],
        compiler_params=_params(("arbitrary",), nbytes),
        name="sgu_out",
    )(uv, uv, xt, ln_g, ln_b, w_s, bs_t, w, rp, rt, g.reshape(1, d), w3)


def _gate_and_groupnorm(y, xs, z, d_rep, gn_w):
    yg = (y + d_rep * xs) * _silu(z)
    gw = D_INNER // SSD_GROUPS
    outs = []
    for g in range(SSD_GROUPS):
        blk = yg[:, g * gw:(g + 1) * gw]
        outs.append(blk * lax.rsqrt(jnp.mean(blk * blk, axis=-1, keepdims=True) + EPS))
    return jnp.concatenate(outs, axis=1) * gn_w


def _softplus(x):
    return jnp.maximum(x, 0.0) + jnp.log(1.0 + jnp.exp(-jnp.abs(x)))


SSD_SEQS = 2


def _ssd_prompt_kernel(xbc_ref, z_ref, dtr_ref, cw_ref, cb_ref, dtb_ref, alog_ref, drep_ref, gnw_ref,
                       yg_ref, ssm_ref, conv_ref, ext_ref, st_ref, y_ref, xa_ref):
    for n in range(SSD_SEQS):
        _ssd_chunk(xbc_ref.at[n], z_ref.at[n], dtr_ref.at[n], cw_ref, cb_ref, dtb_ref, alog_ref, drep_ref, gnw_ref,
                   yg_ref.at[n], ssm_ref.at[n], conv_ref.at[n], ext_ref.at[n], st_ref.at[n], y_ref.at[n],
                   xa_ref.at[n])


def _ssd_chunk(xbc_ref, z_ref, dtr_ref, cw_ref, cb_ref, dtb_ref, alog_ref, drep_ref, gnw_ref,
               yg_ref, ssm_ref, conv_ref, ext_ref, st_ref, y_ref, xa_ref):
    c = pl.program_id(1)
    nc = pl.num_programs(1)
    q = CHUNK

    @pl.when(c == 0)
    def _():
        ext_ref[0:SUBLANES, :] = jnp.zeros((SUBLANES, CONV_DIM), F32)
        st_ref[...] = jnp.zeros_like(st_ref)

    ext_ref[SUBLANES:SUBLANES + q, :] = xbc_ref[...]
    acc = cb_ref[...] + cw_ref[CONV_W - 1:CONV_W, :] * ext_ref[SUBLANES:SUBLANES + q, :]
    for k in range(CONV_W - 1):
        lo = SUBLANES - (CONV_W - 1) + k
        acc = acc + cw_ref[k:k + 1, :] * ext_ref[lo:lo + q, :]
    xa_ref[...] = _silu(acc)
    tail = ext_ref[q:q + SUBLANES, :]
    ext_ref[0:SUBLANES, :] = tail

    dt = _softplus(dtr_ref[...] + dtb_ref[...])
    a_neg = -jnp.exp(alog_ref[...])
    causal = lax.broadcasted_iota(I32, (q, q), 0) >= lax.broadcasted_iota(I32, (q, q), 1)
    tril = jnp.where(causal, 1.0, 0.0)
    cs = jnp.dot(tril, dt * a_neg, precision=HIGHEST, preferred_element_type=F32) * LOG2E
    cs_t = cs.T
    dt_t = dt.T
    rowp = cs_t - jnp.log2(dt_t)
    w_t = dt_t * jnp.exp2(cs_t[:, q - 1:q] - cs_t)
    first = lax.broadcasted_iota(I32, (q, 2 * SSD_HEADDIM), 1) < SSD_HEADDIM

    def per_head(v):
        zero = jnp.zeros_like(v)
        return jnp.concatenate([jnp.where(first, v, zero), jnp.where(first, zero, v)], axis=0)

    for g in range(SSD_GROUPS):
        b_g = xa_ref[:, D_INNER + g * D_STATE:D_INNER + (g + 1) * D_STATE]
        c_g = xa_ref[:, D_INNER + (SSD_GROUPS + g) * D_STATE:D_INNER + (SSD_GROUPS + g + 1) * D_STATE]
        b_gt = b_g.T
        cb = _bdot(c_g, b_gt)
        for pr in range(4):
            hp = g * 4 + pr
            cols = slice(hp * 2 * SSD_HEADDIM, (hp + 1) * 2 * SSD_HEADDIM)
            x2 = per_head(xa_ref[:, cols].astype(BF16))
            st_p = st_ref[:, cols]
            st2 = per_head(st_p.astype(BF16))
            ms, cs_scaled, bws, cds = [], [], [], []
            for s in range(2):
                h = 2 * hp + s
                colb = jnp.broadcast_to(cs[:, h:h + 1], (q, q))
                rowb = jnp.broadcast_to(rowp[h:h + 1, :], (q, q))
                ecol = jnp.exp2(colb)
                ms.append((cb * jnp.where(causal, jnp.exp2(colb - rowb), 0.0)).astype(BF16))
                cs_scaled.append((c_g * ecol).astype(BF16))
                bws.append((b_gt * w_t[h:h + 1, :]).astype(BF16))
                cds.append(ecol[q - 1:q, :])
            y_ref[:, cols] = jnp.dot(jnp.concatenate(ms + cs_scaled, axis=1), jnp.concatenate([x2, st2], axis=0),
                                     preferred_element_type=F32)
            cd = jnp.where(first[0:1, :], cds[0], cds[1])
            st_ref[:, cols] = st_p * cd + jnp.dot(jnp.concatenate(bws, axis=1), x2, preferred_element_type=F32)

    out = _gate_and_groupnorm(y_ref[...], xa_ref[:, :D_INNER], z_ref[...].astype(F32), drep_ref[...],
                              gnw_ref[...])
    yg_ref[...] = out.astype(yg_ref.dtype)

    @pl.when(c == nc - 1)
    def _():
        conv_ref[...] = tail[SUBLANES - (CONV_W - 1):, :]
        for hp in range(SSD_HEADS // 2):
            blk = st_ref[:, hp * 2 * SSD_HEADDIM:(hp + 1) * 2 * SSD_HEADDIM].T
            ssm_ref[2 * hp:2 * hp + 2] = blk.reshape(2, SSD_HEADDIM, D_STATE)


def _ssd_prompt(xbc, z, dtr, conv_w, conv_b, dtb, alog, d_rep, gn_w):
    batch, seq, _ = xbc.shape
    nb = SSD_SEQS
    blk = lambda b, c: (b, c, 0)
    fixed = lambda b, c: (0, 0)
    nbytes = nb * (2 * CHUNK * (CONV_DIM * 4 + D_INNER * 4 + LANES * 4) + (CHUNK + SUBLANES) * CONV_DIM * 4
                   + 4 * D_STATE * D_INNER * 4 + 12 * CHUNK * CONV_DIM * 4)
    return pl.pallas_call(
        _ssd_prompt_kernel,
        grid=(batch // nb, seq // CHUNK),
        in_specs=[pl.BlockSpec((nb, CHUNK, CONV_DIM), blk), pl.BlockSpec((nb, CHUNK, D_INNER), blk),
                  pl.BlockSpec((nb, CHUNK, LANES), blk),
                  pl.BlockSpec((CONV_W, CONV_DIM), fixed), pl.BlockSpec((1, CONV_DIM), fixed),
                  pl.BlockSpec((1, LANES), fixed), pl.BlockSpec((1, LANES), fixed),
                  pl.BlockSpec((1, D_INNER), fixed), pl.BlockSpec((1, D_INNER), fixed)],
        out_specs=[pl.BlockSpec((nb, CHUNK, D_INNER), blk),
                   pl.BlockSpec((nb, SSD_HEADS, SSD_HEADDIM, D_STATE), lambda b, c: (b, 0, 0, 0)),
                   pl.BlockSpec((nb, CONV_W - 1, CONV_DIM), lambda b, c: (b, 0, 0))],
        out_shape=[jax.ShapeDtypeStruct((batch, seq, D_INNER), BF16),
                   jax.ShapeDtypeStruct((batch, SSD_HEADS, SSD_HEADDIM, D_STATE), F32),
                   jax.ShapeDtypeStruct((batch, CONV_W - 1, CONV_DIM), F32)],
        scratch_shapes=[pltpu.VMEM((nb, CHUNK + SUBLANES, CONV_DIM), F32), pltpu.VMEM((nb, D_STATE, D_INNER), F32),
                        pltpu.VMEM((nb, CHUNK, D_INNER), F32), pltpu.VMEM((nb, CHUNK, CONV_DIM), F32)],
        compiler_params=_params(("arbitrary", "arbitrary"), nbytes),
        name="ssd_prompt",
    )(xbc, z, dtr, conv_w, conv_b, dtb, alog, d_rep, gn_w)


def _ssd_out_kernel(y_ref, xs_ref, z_ref, drep_ref, gnw_ref, w_ref, r_ref, g_ref, h_ref, a_ref, wb_ref):
    @pl.when(pl.program_id(0) == 0)
    def _():
        wb_ref[...] = w_ref[...].astype(BF16)

    sub = min(ROW_SUB // 2, y_ref.shape[0])
    for s in range(y_ref.shape[0] // sub):
        rows = slice(s * sub, (s + 1) * sub)
        yg = _gate_and_groupnorm(y_ref[rows, :], xs_ref[rows, :], z_ref[rows, :].astype(F32), drep_ref[...],
                                 gnw_ref[...])
        h = r_ref[rows, :] + _bdot(yg, wb_ref[...])
        h_ref[rows, :] = h
        a_ref[rows, :] = _rms(h, g_ref[...]).astype(a_ref.dtype)


def _ssd_out(y, xa, z, d_rep, gn_w, w, resid, g, tm):
    m, k = y.shape
    d = w.shape[1]
    row = lambda i: (i, 0)
    fixed = lambda i: (0, 0)
    nbytes = k * d * 6 + 2 * tm * k * 10 + 8 * tm * d * 4 + 6 * tm * k * 4
    return pl.pallas_call(
        _ssd_out_kernel,
        grid=(m // tm,),
        in_specs=[pl.BlockSpec((tm, k), row), pl.BlockSpec((tm, k), row), pl.BlockSpec((tm, k), row),
                  pl.BlockSpec((1, k), fixed), pl.BlockSpec((1, k), fixed),
                  pl.BlockSpec((k, d), fixed, pipeline_mode=pl.Buffered(1)),
                  pl.BlockSpec((tm, d), row), pl.BlockSpec((1, d), fixed)],
        out_specs=[pl.BlockSpec((tm, d), row), pl.BlockSpec((tm, d), row)],
        out_shape=[jax.ShapeDtypeStruct((m, d), F32), jax.ShapeDtypeStruct((m, d), BF16)],
        scratch_shapes=[pltpu.VMEM((k, d), BF16)],
        compiler_params=_params(("arbitrary",), nbytes),
        name="ssd_out",
    )(y, xa, z, d_rep, gn_w, w, resid, g.reshape(1, d))


def _ssd_sample_pre_kernel(xbc_ref, conv_ref, dtr_ref, cw_ref, cb_ref, dtb_ref, alog_ref,
                           nconv_ref, xa_ref, xdt_ref, da_ref, bst_ref, cst_ref):
    xn = xbc_ref[...]
    acc = cb_ref[...] + cw_ref[CONV_W - 1:CONV_W, :] * xn
    for k in range(CONV_W - 1):
        acc = acc + cw_ref[k:k + 1, :] * conv_ref[:, k * CONV_DIM:(k + 1) * CONV_DIM]
    xa = _silu(acc)
    xa_ref[...] = xa
    nconv_ref[:, 0:CONV_DIM] = conv_ref[:, CONV_DIM:2 * CONV_DIM]
    nconv_ref[:, CONV_DIM:2 * CONV_DIM] = conv_ref[:, 2 * CONV_DIM:3 * CONV_DIM]
    nconv_ref[:, 2 * CONV_DIM:3 * CONV_DIM] = xn

    dt = _softplus(dtr_ref[...] + dtb_ref[...])
    a_neg = -jnp.exp(alog_ref[...])
    hh = lax.broadcasted_iota(I32, (LANES, D_INNER), 0)
    cc = lax.broadcasted_iota(I32, (LANES, D_INNER), 1)
    expand = jnp.where(lax.shift_right_logical(cc, 6) == hh, 1.0, 0.0)
    dt_rep = jnp.dot(dt, expand, precision=HIGHEST, preferred_element_type=F32)
    xdt_ref[...] = xa[:, :D_INNER] * dt_rep
    da_ref[...] = jnp.exp(dt * a_neg)

    n = xn.shape[0]
    gs = SSD_GROUPS * D_STATE
    b1, b2, b3 = _split3(xa[:, D_INNER:D_INNER + gs])
    cm = xa[:, D_INNER + gs:]
    c1 = cm.astype(BF16)
    c2 = (cm - c1.astype(F32)).astype(BF16)
    zero = jnp.zeros((n, D_STATE), BF16)
    b_rows, c_rows = [], []
    for g in range(SSD_GROUPS):
        sl = slice(g * D_STATE, (g + 1) * D_STATE)
        b_rows += [b1[:, sl], b2[:, sl], b1[:, sl], b3[:, sl], b2[:, sl], b1[:, sl]] + [zero] * (TERM_ROWS - 6)
        c_rows += [c1[:, sl], c2[:, sl]] + [zero] * (TERM_ROWS - 2)
    bst_ref[...] = jnp.concatenate(b_rows, axis=1)
    cst_ref[...] = jnp.concatenate(c_rows, axis=1)


TERM_ROWS = 16


def _split3(v):
    v1 = v.astype(BF16)
    r1 = v - v1.astype(F32)
    v2 = r1.astype(BF16)
    v3 = (r1 - v2.astype(F32)).astype(BF16)
    return v1, v2, v3


def _ssd_sample_pre(xbc, conv_flat, dtr, conv_w, conv_b, dtb, alog):
    n = xbc.shape[0]
    full = lambda shape: pl.BlockSpec(shape, lambda i: (0,) * len(shape))
    stack = SSD_GROUPS * TERM_ROWS * D_STATE
    return pl.pallas_call(
        _ssd_sample_pre_kernel,
        grid=(1,),
        in_specs=[full((n, CONV_DIM)), full((n, 3 * CONV_DIM)), full((n, LANES)), full((CONV_W, CONV_DIM)),
                  full((1, CONV_DIM)), full((1, LANES)), full((1, LANES))],
        out_specs=[full((n, 3 * CONV_DIM)), full((n, CONV_DIM)), full((n, D_INNER)), full((n, LANES)),
                   full((n, stack)), full((n, stack))],
        out_shape=[jax.ShapeDtypeStruct((n, 3 * CONV_DIM), F32), jax.ShapeDtypeStruct((n, CONV_DIM), F32),
                   jax.ShapeDtypeStruct((n, D_INNER), F32), jax.ShapeDtypeStruct((n, LANES), F32),
                   jax.ShapeDtypeStruct((n, stack), BF16), jax.ShapeDtypeStruct((n, stack), BF16)],
        compiler_params=_params(("arbitrary",), 32 * n * CONV_DIM * 4),
        name="ssd_sample_pre",
    )(xbc, conv_flat, dtr, conv_w, conv_b, dtb, alog)


STATE_SAMPLES = 8


def _ssd_sample_state_kernel(da_ref, st_ref, xdt_ref, bst_ref, cst_ref, nst_ref, y_ref):
    first = pl.program_id(0) * STATE_SAMPLES
    rows_g = D_INNER // SSD_GROUPS
    sub = lax.broadcasted_iota(I32, (TERM_ROWS, D_INNER), 0)
    for s in range(STATE_SAMPLES):
        x1, x2, x3 = [t.astype(F32) for t in _split3(xdt_ref[s])]
        terms = jnp.where(sub == 0, x1, jnp.where(sub == 1, x1, jnp.where(sub == 2, x2, jnp.where(
            sub == 3, x1, jnp.where(sub == 4, x2, jnp.where(sub == 5, x3, 0.0)))))).astype(BF16)
        ys = []
        for g in range(SSD_GROUPS):
            tile = slice(g * TERM_ROWS, (g + 1) * TERM_ROWS)
            upd = lax.dot_general(terms[:, g * rows_g:(g + 1) * rows_g], bst_ref[s, tile, :],
                                  (((0,), (0,)), ((), ())), preferred_element_type=F32)
            halves = []
            for hh in range(rows_g // SSD_HEADDIM):
                head = g * (rows_g // SSD_HEADDIM) + hh
                rows = slice(head * SSD_HEADDIM, (head + 1) * SSD_HEADDIM)
                h_new = (da_ref[first + s, head] * st_ref[s, rows, :]
                         + upd[hh * SSD_HEADDIM:(hh + 1) * SSD_HEADDIM, :])
                nst_ref[s, rows, :] = h_new
                halves.append(h_new)
            h_g = jnp.concatenate(halves, axis=0)
            h_hi = h_g.astype(BF16)
            h_lo = (h_g - h_hi.astype(F32)).astype(BF16)
            nt = (((1,), (1,)), ((), ()))
            o_hi = lax.dot_general(cst_ref[s, tile, :], h_hi, nt, preferred_element_type=F32)
            o_lo = lax.dot_general(cst_ref[s, tile, :], h_lo, nt, preferred_element_type=F32)
            ys.append(o_hi[0:1, :] + o_hi[1:2, :] + o_lo[0:1, :])
        y_ref[s] = jnp.concatenate(ys, axis=1)


def _ssd_sample_state(state, xdt, da, bst, cst):
    n = state.shape[0]
    ns = STATE_SAMPLES
    stack_rows = SSD_GROUPS * TERM_ROWS
    return pl.pallas_call(
        _ssd_sample_state_kernel,
        grid=(n // ns,),
        in_specs=[pl.BlockSpec(memory_space=pltpu.SMEM),
                  pl.BlockSpec((ns, D_INNER, D_STATE), lambda i: (i, 0, 0)),
                  pl.BlockSpec((ns, 1, D_INNER), lambda i: (i, 0, 0)),
                  pl.BlockSpec((ns, stack_rows, D_STATE), lambda i: (i, 0, 0)),
                  pl.BlockSpec((ns, stack_rows, D_STATE), lambda i: (i, 0, 0))],
        out_specs=[pl.BlockSpec((ns, D_INNER, D_STATE), lambda i: (i, 0, 0)),
                   pl.BlockSpec((ns, 1, D_INNER), lambda i: (i, 0, 0))],
        out_shape=[jax.ShapeDtypeStruct((n, D_INNER, D_STATE), F32),
                   jax.ShapeDtypeStruct((n, 1, D_INNER), F32)],
        compiler_params=_params(("arbitrary",), 6 * ns * D_INNER * D_STATE * 4),
        name="ssd_sample_state",
    )(da, state, xdt, bst, cst)


def _layernorm(v, g, b):
    mu = jnp.mean(v, axis=-1, keepdims=True)
    d = v - mu
    var = jnp.mean(d * d, axis=-1, keepdims=True)
    return d * lax.rsqrt(var + EPS) * g + b


def _sgu_sample_kernel(u_ref, v_ref, lng_ref, lnb_ref, w0_ref, b0_ref, o_ref, vn_ref):
    vn = _layernorm(v_ref[...], lng_ref[...], lnb_ref[...])
    vn_ref[...] = vn
    o_ref[...] = (u_ref[...] * (w0_ref[...] * vn + b0_ref[...])).astype(o_ref.dtype)


def _sgu_sample(uv, ln_g, ln_b, w0_rep, b0_rep):
    n = uv.shape[0]
    fixed2 = lambda i: (0, 0)
    return pl.pallas_call(
        _sgu_sample_kernel,
        grid=(1,),
        in_specs=[pl.BlockSpec((n, D_SGU), lambda i: (0, 0)), pl.BlockSpec((n, D_SGU), lambda i: (0, 1)),
                  pl.BlockSpec((1, D_SGU), fixed2), pl.BlockSpec((1, D_SGU), fixed2),
                  pl.BlockSpec((1, D_SGU), fixed2), pl.BlockSpec((1, D_SGU), fixed2)],
        out_specs=[pl.BlockSpec((n, D_SGU), fixed2), pl.BlockSpec((n, D_SGU), fixed2)],
        out_shape=[jax.ShapeDtypeStruct((n, D_SGU), BF16), jax.ShapeDtypeStruct((n, D_SGU), F32)],
        compiler_params=_params(("arbitrary",), 12 * n * D_SGU * 4),
        name="sgu_sample",
    )(uv, uv, ln_g, ln_b, w0_rep, b0_rep)


def _experts_to_lanes(col, sub, lane):
    return jnp.sum(jnp.where(sub == lane, col, 0.0), axis=0, keepdims=True)


def _route_kernel(lg_ref, loc_ref, gate_ref, bmeta_ref, tmeta_ref, *, tile, n_valid):
    nb, _, tb = lg_ref.shape
    sub = lax.broadcasted_iota(I32, (N_EXPERTS, LANES), 0)
    lane = lax.broadcasted_iota(I32, (N_EXPERTS, LANES), 1)
    subf = lax.broadcasted_iota(I32, (N_EXPERTS, tb), 0).astype(F32)
    tok = lax.broadcasted_iota(I32, (N_EXPERTS, tb), 1)
    incl = jnp.where(lax.broadcasted_iota(I32, (tb, tb), 0) <= lax.broadcasted_iota(I32, (tb, tb), 1), 1.0, 0.0)
    neg = jnp.float32(-jnp.inf)
    none = jnp.float32(N_EXPERTS)

    def select(k):
        blk = lg_ref[k]
        l = blk[0:N_EXPERTS] + blk[N_EXPERTS:2 * N_EXPERTS]
        m1 = jnp.max(l, axis=0, keepdims=True)
        i1 = jnp.min(jnp.where(l == m1, subf, none), axis=0, keepdims=True)
        l2 = jnp.where(subf == i1, neg, l)
        m2 = jnp.max(l2, axis=0, keepdims=True)
        i2 = jnp.min(jnp.where(l2 == m2, subf, none), axis=0, keepdims=True)
        valid = (k * tb + tok) < n_valid
        sel = jnp.where(valid, jnp.where(subf == i1, 1.0, jnp.where(subf == i2, 1.0, 0.0)), 0.0)
        return m1, i1, m2, i2, valid, sel

    def run_rows(cnt):
        return jnp.ceil(cnt / SUBLANES) * SUBLANES

    def count_body(k, carry):
        return carry + run_rows(jnp.sum(select(k)[5], axis=1, keepdims=True))

    counts = lax.fori_loop(0, nb, count_body, jnp.zeros((N_EXPERTS, 1), F32))
    tiles = jnp.ceil(counts / tile)
    cum_incl = jnp.sum(jnp.where(lane <= sub, _experts_to_lanes(tiles, sub, lane), 0.0), axis=1, keepdims=True)
    offset = (cum_incl - tiles) * tile
    tile_expert = jnp.sum(jnp.where(lane.astype(F32) >= cum_incl, 1.0, 0.0), axis=0, keepdims=True)
    tile_expert = jnp.minimum(tile_expert, N_EXPERTS - 1.0)
    n_used = jnp.max(cum_incl, axis=0, keepdims=True)
    region_end = _experts_to_lanes(offset + counts, sub, lane)
    region_pad = _experts_to_lanes(tiles * tile - counts, sub, lane)
    tmeta_ref[...] = jnp.where(sub == 0, tile_expert, jnp.where(sub == 1, n_used, jnp.where(
        sub == 2, region_end, jnp.where(sub == 3, region_pad, 0.0)))).astype(I32)

    def place_body(k, before):
        m1, i1, m2, i2, valid, sel = select(k)
        run = jnp.dot(sel, incl, preferred_element_type=F32)
        cnt = run_rows(run[:, tb - 1:tb])
        lstart = jnp.sum(jnp.where(lane < sub, _experts_to_lanes(cnt, sub, lane), 0.0), axis=1, keepdims=True)
        local = lstart + run - sel
        loc1 = jnp.sum(jnp.where(subf == i1, local, 0.0), axis=0, keepdims=True)
        loc2 = jnp.sum(jnp.where(subf == i2, local, 0.0), axis=0, keepdims=True)
        live = (k * tb + tok[0:1, :]) < n_valid
        loc_ref[k] = jnp.where(subf == 0.0, jnp.where(live, loc1, -1.0),
                               jnp.where(subf == 1.0, jnp.where(live, loc2, -1.0), 0.0))
        e = jnp.exp(m2 - m1)
        gate_ref[k] = jnp.where(subf == 0.0, 1.0 / (1.0 + e), jnp.where(subf == 1.0, e / (1.0 + e), 0.0))
        bmeta_ref[k] = jnp.where(lane == 0, offset + before,
                                 jnp.where(lane == 1, cnt, jnp.where(lane == 2, lstart, 0.0))).astype(I32)
        return before + cnt

    lax.fori_loop(0, nb, place_body, jnp.zeros((N_EXPERTS, 1), F32))


def _route(logits3, tile, n_valid):
    nb, _, tb = logits3.shape
    tok_blk = pl.BlockSpec((nb, N_EXPERTS, tb), lambda i: (0, 0, 0))
    return pl.pallas_call(
        functools.partial(_route_kernel, tile=tile, n_valid=n_valid),
        grid=(1,),
        in_specs=[pl.BlockSpec((nb, 2 * N_EXPERTS, tb), lambda i: (0, 0, 0))],
        out_specs=[tok_blk, tok_blk, pl.BlockSpec((nb, N_EXPERTS, LANES), lambda i: (0, 0, 0)),
                   pl.BlockSpec((N_EXPERTS, LANES), lambda i: (0, 0))],
        out_shape=[jax.ShapeDtypeStruct((nb, N_EXPERTS, tb), F32),
                   jax.ShapeDtypeStruct((nb, N_EXPERTS, tb), F32),
                   jax.ShapeDtypeStruct((nb, N_EXPERTS, LANES), I32),
                   jax.ShapeDtypeStruct((N_EXPERTS, LANES), I32)],
        compiler_params=_params(("arbitrary",), 16 * nb * N_EXPERTS * tb * 4),
        name="moe_route",
    )(logits3)


def _run_copies(bm_ref, blk, vmem_ref, hbm_ref, sem, *, tb, to_hbm, wait):
    for e in range(N_EXPERTS):
        base = (blk * N_EXPERTS + e) * 3
        start, cnt, lstart = bm_ref[base], bm_ref[base + 1], bm_ref[base + 2]
        off = 0
        size = tb
        while size >= SUBLANES:
            @pl.when((cnt & size) != 0)
            def _(size=size, off=off):
                v = vmem_ref.at[pl.ds(pl.multiple_of(lstart + off, SUBLANES), size)]
                h = hbm_ref.at[pl.ds(pl.multiple_of(start + off, SUBLANES), size)]
                cp = pltpu.make_async_copy(v, h, sem) if to_hbm else pltpu.make_async_copy(h, v, sem)
                if wait:
                    cp.wait()
                else:
                    cp.start()

            off = off + (cnt & size)
            size //= 2


def _compact_rows(tb):
    return 2 * tb + N_EXPERTS * SUBLANES


def _selection(loc_ref, rows, tb):
    r = lax.broadcasted_iota(I32, (rows, tb), 0).astype(F32)
    return r == loc_ref[0, 0:1, :], r == loc_ref[0, 1:2, :]


def _zero_fill(zm_ref, zbuf_ref, zgbuf_ref, xs_ref, gs_ref, zsem, *, tm, n_tiles, wait):
    def fill(row0, size):
        for src, dst in ((zbuf_ref, xs_ref), (zgbuf_ref, gs_ref)):
            cp = pltpu.make_async_copy(src.at[pl.ds(0, size)], dst.at[pl.ds(pl.multiple_of(row0, SUBLANES), size)],
                                       zsem)
            if wait:
                cp.wait()
            else:
                cp.start()

    for e in range(N_EXPERTS):
        end, pad = zm_ref[1 + 2 * e], zm_ref[2 + 2 * e]
        off = 0
        size = tm // 2
        while size >= SUBLANES:
            @pl.when((pad & size) != 0)
            def _(size=size, off=off):
                fill(end + off, size)

            off = off + (pad & size)
            size //= 2
    for t in range(n_tiles):
        @pl.when(t >= zm_ref[0])
        def _(t=t):
            fill(t * tm, tm)


def _moe_scatter_kernel(bm_ref, zm_ref, a_ref, loc_ref, gate_ref, xs_ref, gs_ref,
                        buf_ref, gbuf_ref, zbuf_ref, zgbuf_ref, sem, gsem, zsem, *, tb, tm, n_tiles):
    blk = pl.program_id(0)
    last = pl.num_programs(0) - 1
    slot = blk % 2

    @pl.when(blk == 0)
    def _():
        zbuf_ref[...] = jnp.zeros_like(zbuf_ref)
        zgbuf_ref[...] = jnp.zeros_like(zgbuf_ref)
        _zero_fill(zm_ref, zbuf_ref, zgbuf_ref, xs_ref, gs_ref, zsem, tm=tm, n_tiles=n_tiles, wait=False)

    def copies(block, wait):
        s = block % 2
        _run_copies(bm_ref, block, buf_ref.at[s], xs_ref, sem.at[s], tb=tb, to_hbm=True, wait=wait)
        _run_copies(bm_ref, block, gbuf_ref.at[s], gs_ref, gsem.at[s], tb=tb, to_hbm=True, wait=wait)

    @pl.when(blk >= 2)
    def _():
        copies(blk - 2, True)

    first, second = _selection(loc_ref, _compact_rows(tb), tb)
    pick = jnp.where(first, 1.0, jnp.where(second, 1.0, 0.0)).astype(BF16)
    buf_ref[slot] = jnp.dot(pick, a_ref[...], preferred_element_type=F32)
    gsel = jnp.where(first, gate_ref[0, 0:1, :], jnp.where(second, gate_ref[0, 1:2, :], 0.0))
    gbuf_ref[slot] = jnp.broadcast_to(jnp.sum(gsel, axis=1, keepdims=True), gbuf_ref.shape[1:])
    copies(blk, False)

    @pl.when(blk == last)
    def _():
        @pl.when(blk >= 1)
        def _():
            copies(blk - 1, True)

        copies(blk, True)
        _zero_fill(zm_ref, zbuf_ref, zgbuf_ref, xs_ref, gs_ref, zsem, tm=tm, n_tiles=n_tiles, wait=True)


def _moe_scatter(a, loc, gate, bmeta, zmeta, n_slots, tb, tm):
    m, d = a.shape
    any_spec = pl.BlockSpec(memory_space=pl.ANY)
    tok_spec = pl.BlockSpec((1, N_EXPERTS, tb), lambda i, bm, zm: (i, 0, 0))
    grid_spec = pltpu.PrefetchScalarGridSpec(
        num_scalar_prefetch=2,
        grid=(m // tb,),
        in_specs=[pl.BlockSpec((tb, d), lambda i, bm, zm: (i, 0)), tok_spec, tok_spec],
        out_specs=[any_spec, any_spec],
        scratch_shapes=[pltpu.VMEM((2, _compact_rows(tb), d), F32), pltpu.VMEM((2, _compact_rows(tb), LANES), F32),
                        pltpu.VMEM((tm, d), F32), pltpu.VMEM((tm, LANES), F32),
                        pltpu.SemaphoreType.DMA((2,)), pltpu.SemaphoreType.DMA((2,)), pltpu.SemaphoreType.DMA(())],
    )
    return pl.pallas_call(
        functools.partial(_moe_scatter_kernel, tb=tb, tm=tm, n_tiles=n_slots // tm),
        grid_spec=grid_spec,
        out_shape=[jax.ShapeDtypeStruct((n_slots, d), F32), jax.ShapeDtypeStruct((n_slots, LANES), F32)],
        compiler_params=_params(("arbitrary",), 16 * tb * d * 4),
        name="moe_scatter",
    )(bmeta, zmeta, a, loc, gate)


def _new_weights(te_ref):
    i = pl.program_id(1)
    return (i == 0) | (te_ref[i] != te_ref[jnp.maximum(i - 1, 0)])


def _moe_gu_kernel(te_ref, nu_ref, x_ref, wg_ref, wu_ref, o_ref, wgb_ref, wub_ref):
    i = pl.program_id(1)

    @pl.when(_new_weights(te_ref))
    def _():
        wgb_ref[...] = wg_ref[0].astype(BF16)
        wub_ref[...] = wu_ref[0].astype(BF16)

    @pl.when(i < nu_ref[0])
    def _():
        x = x_ref[...].astype(BF16)
        for c in range(o_ref.shape[1] // MXU_COLS):
            cols = slice(c * MXU_COLS, (c + 1) * MXU_COLS)
            g = jnp.dot(x, wgb_ref[:, cols], preferred_element_type=F32)
            u = jnp.dot(x, wub_ref[:, cols], preferred_element_type=F32)
            o_ref[:, cols] = (_silu(g) * u).astype(o_ref.dtype)

    @pl.when(i >= nu_ref[0])
    def _():
        o_ref[...] = jnp.zeros_like(o_ref)


def _used_tile(i, nu):
    return jnp.minimum(i, nu[0] - 1)


def _moe_gu(xs, w_gu, te, nu, tm, tn):
    s, k = xs.shape
    f = D_FF_EXPERT
    nbytes = 2 * (2 * k * tn * 4 + k * tn * 2) + 2 * tm * k * 4 + 2 * tm * tn * 2 + 4 * tm * MXU_COLS * 4
    grid_spec = pltpu.PrefetchScalarGridSpec(
        num_scalar_prefetch=2,
        grid=(f // tn, s // tm),
        in_specs=[pl.BlockSpec((tm, k), lambda j, i, te, nu: (_used_tile(i, nu), 0)),
                  pl.BlockSpec((1, k, tn), lambda j, i, te, nu: (te[i], 0, j)),
                  pl.BlockSpec((1, k, tn), lambda j, i, te, nu: (te[i], 0, j + f // tn))],
        out_specs=pl.BlockSpec((tm, tn), lambda j, i, te, nu: (i, j)),
        scratch_shapes=[pltpu.VMEM((k, tn), BF16), pltpu.VMEM((k, tn), BF16)],
    )
    return pl.pallas_call(
        _moe_gu_kernel,
        grid_spec=grid_spec,
        out_shape=jax.ShapeDtypeStruct((s, f), BF16),
        compiler_params=_params(("arbitrary", "arbitrary"), nbytes),
        name="moe_gate_up",
    )(te, nu, xs, w_gu, w_gu)


def _moe_down_kernel(te_ref, nu_ref, x_ref, w_ref, gs_ref, o_ref, wb_ref):
    i = pl.program_id(1)

    @pl.when(_new_weights(te_ref))
    def _():
        wb_ref[...] = w_ref[0].astype(BF16)

    @pl.when(i < nu_ref[0])
    def _():
        o_ref[...] = gs_ref[:, 0:1] * jnp.dot(x_ref[...], wb_ref[...], preferred_element_type=F32)

    @pl.when(i >= nu_ref[0])
    def _():
        o_ref[...] = jnp.zeros_like(o_ref)


def _moe_down(act, w_down, gs, te, nu, tm, tn):
    s, k = act.shape
    d = w_down.shape[2]
    nbytes = 2 * k * tn * 4 + k * tn * 2 + 2 * tm * k * 2 + 3 * tm * tn * 4
    grid_spec = pltpu.PrefetchScalarGridSpec(
        num_scalar_prefetch=2,
        grid=(d // tn, s // tm),
        in_specs=[pl.BlockSpec((tm, k), lambda j, i, te, nu: (_used_tile(i, nu), 0)),
                  pl.BlockSpec((1, k, tn), lambda j, i, te, nu: (te[i], 0, j)),
                  pl.BlockSpec((tm, LANES), lambda j, i, te, nu: (_used_tile(i, nu), 0))],
        out_specs=pl.BlockSpec((tm, tn), lambda j, i, te, nu: (i, j)),
        scratch_shapes=[pltpu.VMEM((k, tn), BF16)],
    )
    return pl.pallas_call(
        _moe_down_kernel,
        grid_spec=grid_spec,
        out_shape=jax.ShapeDtypeStruct((s, d), F32),
        compiler_params=_params(("arbitrary", "arbitrary"), nbytes),
        name="moe_down",
    )(te, nu, act, w_down, gs)


def _moe_combine_kernel(bm_ref, h_ref, locc_ref, g_ref, ys_ref, op_ref, os_ref, ybuf_ref, sem, *, tb, prompt_blocks):
    blk = pl.program_id(0)
    slot = blk % 2

    def copies(block, wait):
        s = block % 2
        _run_copies(bm_ref, block, ybuf_ref.at[s], ys_ref, sem.at[s], tb=tb, to_hbm=False, wait=wait)

    @pl.when(blk == 0)
    def _():
        ybuf_ref[...] = jnp.zeros_like(ybuf_ref)
        copies(blk, False)

    @pl.when(blk + 1 < pl.num_programs(0))
    def _():
        copies(blk + 1, False)

    r = lax.broadcasted_iota(I32, (tb, _compact_rows(tb)), 1).astype(F32)
    pick = jnp.where(r == locc_ref[:, 0:1], 1.0, jnp.where(r == locc_ref[:, 1:2], 1.0, 0.0)).astype(BF16)
    copies(blk, True)
    moe = jnp.dot(pick, ybuf_ref[slot].astype(BF16), preferred_element_type=F32)
    out = _rms(h_ref[...] + moe, g_ref[...])

    @pl.when(blk < prompt_blocks)
    def _():
        op_ref[...] = out

    @pl.when(blk == prompt_blocks)
    def _():
        os_ref[...] = out


def _moe_combine(h, locc, bmeta, ys, g, tb, n_prompt):
    m, d = h.shape
    prompt_blocks = n_prompt // tb
    any_spec = pl.BlockSpec(memory_space=pl.ANY)
    grid_spec = pltpu.PrefetchScalarGridSpec(
        num_scalar_prefetch=1,
        grid=(m // tb,),
        in_specs=[pl.BlockSpec((tb, d), lambda i, bm: (i, 0)),
                  pl.BlockSpec((tb, N_EXPERTS), lambda i, bm: (i, 0)),
                  pl.BlockSpec((1, d), lambda i, bm: (0, 0)),
                  any_spec],
        out_specs=[pl.BlockSpec((tb, d), lambda i, bm: (jnp.minimum(i, prompt_blocks - 1), 0)),
                   pl.BlockSpec((tb, d), lambda i, bm: (0, 0))],
        scratch_shapes=[pltpu.VMEM((2, _compact_rows(tb), d), F32), pltpu.SemaphoreType.DMA((2,))],
    )
    return pl.pallas_call(
        functools.partial(_moe_combine_kernel, tb=tb, prompt_blocks=prompt_blocks),
        grid_spec=grid_spec,
        out_shape=[jax.ShapeDtypeStruct((n_prompt, d), F32), jax.ShapeDtypeStruct((tb, d), F32)],
        compiler_params=_params(("arbitrary",), 16 * tb * d * 4),
        name="moe_combine",
    )(bmeta, h, locc, g.reshape(1, d), ys)


MOE_TOKEN_BLOCK = 256
MOE_ROW_TILE = 512


def _moe(h, a, lg_parts, n_valid, n_prompt, w_gu, w_down, g_final):
    m, d = h.shape
    tb, tm = MOE_TOKEN_BLOCK, MOE_ROW_TILE
    nb = m // tb
    lg3 = lg_parts[:, :2 * N_EXPERTS].reshape(nb, tb, 2 * N_EXPERTS).transpose(0, 2, 1)
    n_tiles = pl.cdiv(2 * n_valid + nb * N_EXPERTS * (SUBLANES - 1), tm) + N_EXPERTS
    loc, gate, bmeta, tmeta = _route(lg3, tm, n_valid)
    te = tmeta[0, :n_tiles]
    nu = tmeta[1, 0:1]
    bm = bmeta[:, :, :3].reshape(-1)
    zm = jnp.concatenate([nu, tmeta[2:4, :N_EXPERTS].T.reshape(-1)])
    locc = loc.transpose(0, 2, 1).reshape(m, N_EXPERTS)
    xs, gs = _moe_scatter(a, loc, gate, bm, zm, n_tiles * tm, tb, tm)
    act = _moe_gu(xs, w_gu, te, nu, tm, D_FF_EXPERT // 2)
    ys = _moe_down(act, w_down, gs, te, nu, tm, d)
    return _moe_combine(h, locc, bm, ys, g_final, tb, n_prompt)


def _pad_lanes(v):
    return jnp.pad(v.reshape(1, -1), ((0, 0), (0, LANES - v.shape[-1])))


def kernel(x_prompt, x_sample, state_ssm, state_conv, norm_mix, norm_ffn, norm_final, ssd_w_in, ssd_conv_w,
           ssd_conv_b, ssd_dt_bias, ssd_a_log, ssd_d, ssd_gnorm, ssd_w_out, sgu_w_in, sgu_b_in, sgu_ln_g,
           sgu_ln_b, sgu_w_s, sgu_b_s, sgu_w_out, ffn_w_gu, ffn_w_down, moe_w_router, moe_w_gu, moe_w_down):
    batch, seq, d = x_prompt.shape
    n_s = x_sample.shape[0]
    mp = batch * seq
    xp = x_prompt.reshape(mp, d)
    xs = x_sample.reshape(n_s, d)

    w_in = ssd_w_in[0].T
    w_dt = jnp.pad(w_in[D_INNER + CONV_DIM:], ((0, LANES - SSD_HEADS), (0, 0)))
    dtb = _pad_lanes(ssd_dt_bias[0])
    alog = _pad_lanes(ssd_a_log[0])
    d_rep = jnp.repeat(ssd_d[0], SSD_HEADDIM).reshape(1, D_INNER)
    gn_w = ssd_gnorm[0].reshape(1, D_INNER)
    conv_w = ssd_conv_w[0]
    conv_b = ssd_conv_b[0].reshape(1, CONV_DIM)
    ln_g = sgu_ln_g[0].reshape(1, D_SGU)
    ln_b = sgu_ln_b[0].reshape(1, D_SGU)
    bs_t = jnp.pad(sgu_b_s[0].T, ((0, 0), (0, LANES - SGU_HEADS)))
    w0_rep = jnp.repeat(sgu_w_s[0][:, 0, 0], SGU_HEAD_DIM).reshape(1, D_SGU)
    b0_rep = jnp.repeat(sgu_b_s[0][:, 0], SGU_HEAD_DIM).reshape(1, D_SGU)
    wr_hi = moe_w_router[0].astype(BF16)
    wr_lo = (moe_w_router[0] - wr_hi.astype(F32)).astype(BF16)
    w3 = jnp.concatenate([wr_hi, wr_lo, jnp.zeros((d, LANES - 2 * N_EXPERTS), BF16)], axis=1)

    def dense_ffn(h1, a1, tm):
        act = _mm_swiglu(a1, ffn_w_gu[0], D_FF_DENSE, min(2 * tm, a1.shape[0]), 1408, "ffn_gate_up")
        return _mm_resnorm(act, ffn_w_down[0], h1, norm_mix[1], tm, "ffn_down")

    a0p, zp, dtrp = _norm_in_proj(xp, norm_mix[0], w_in, w_dt, 1024)
    xbcp = _mm_plain(a0p, w_in, D_INNER, CONV_DIM, 1024, 1024, F32, "ssd_in_xbc")
    per_seq = lambda t: t.reshape(batch, seq, t.shape[-1])
    ygp, ssm_p, conv_p = _ssd_prompt(per_seq(xbcp), per_seq(zp), per_seq(dtrp), conv_w, conv_b, dtb, alog, d_rep,
                                     gn_w)
    h1p, a1p = _mm_resnorm(ygp.reshape(mp, D_INNER), ssd_w_out[0], xp, norm_ffn[0], 512, "ssd_out")
    h2p, a2p = dense_ffn(h1p, a1p, 512)

    a0s, zs, dtrs = _norm_in_proj(xs, norm_mix[0], w_in, w_dt, n_s)
    xbcs = _mm_plain(a0s, w_in, D_INNER, CONV_DIM, n_s, 1024, F32, "ssd_in_xbc")
    conv_s, xas, xdts, das, bsts, csts = _ssd_sample_pre(xbcs, state_conv[0].reshape(n_s, 3 * CONV_DIM), dtrs,
                                                         conv_w, conv_b, dtb, alog)
    stack_rows = SSD_GROUPS * TERM_ROWS
    ssm_s, ys3 = _ssd_sample_state(state_ssm[0].reshape(n_s, D_INNER, D_STATE), xdts.reshape(n_s, 1, D_INNER),
                                   das[:, :SSD_HEADS], bsts.reshape(n_s, stack_rows, D_STATE),
                                   csts.reshape(n_s, stack_rows, D_STATE))
    h1s, a1s = _ssd_out(ys3.reshape(n_s, D_INNER), xas, zs, d_rep, gn_w, ssd_w_out[0], xs, norm_ffn[0], n_s)
    h2s, a2s = dense_ffn(h1s, a1s, n_s)

    uvp = _mm_bias_gelu(a2p, sgu_w_in[0], sgu_b_in[0], 1024, 2048, BF16, "sgu_in")
    uvs = _mm_bias_gelu(a2s, sgu_w_in[0], sgu_b_in[0], n_s, 1024, F32, "sgu_in")
    ss, v_s = _sgu_sample(uvs, ln_g, ln_b, w0_rep, b0_rep)
    tm_out = 512
    pad_rows = ((0, tm_out - n_s), (0, 0))
    h3, a3, lg = _sgu_out(uvp, jnp.pad(ss, pad_rows), ln_g, ln_b, sgu_w_s[0], bs_t, sgu_w_out[0], h2p,
                          jnp.pad(h2s, pad_rows), norm_ffn[1], w3, tm_out)

    y_prompt, y_tail = _moe(h3, a3, lg, mp + n_s, mp, moe_w_gu[0], moe_w_down[0], norm_final)
    y_sample = y_tail[:n_s]

    return (y_prompt.reshape(batch, seq, d),
            y_sample.reshape(n_s, 1, d),
            ssm_p[None],
            conv_p[None],
            ssm_s.reshape(1, n_s, SSD_HEADS, SSD_HEADDIM, D_STATE),
            conv_s.reshape(1, n_s, CONV_W - 1, CONV_DIM),
            v_s.reshape(1, n_s, 1, D_SGU))
```

```python
import functools

import jax
import jax.numpy as jnp
from jax import lax
from jax.experimental import pallas as pl
from jax.experimental.pallas import tpu as pltpu

F32 = jnp.float32
BF16 = jnp.bfloat16
I32 = jnp.int32
HIGHEST = lax.Precision.HIGHEST

D_MODEL = 1024
D_INNER = 2048
SSD_HEADS = 32
SSD_HEADDIM = 64
SSD_GROUPS = 4
D_STATE = 128
CONV_W = 4
CONV_DIM = D_INNER + 2 * SSD_GROUPS * D_STATE
CHUNK = 128
D_SGU = 2048
SGU_HEADS = 8
SGU_HEAD_DIM = D_SGU // SGU_HEADS
D_FF_DENSE = 2816
N_EXPERTS = 8
D_FF_EXPERT = 3584
EPS = 1e-5

MXU_COLS = 256
ROW_SUB = 256
LANES = 128
SUBLANES = 8
VMEM_CAP = 60000 * 1024
VMEM_FLOOR = 32 * 1024 * 1024


def _vmem_limit(nbytes):
    return int(min(max(nbytes * 5 // 4 + (4 << 20), VMEM_FLOOR), VMEM_CAP))


def _params(sem, nbytes):
    return pltpu.CompilerParams(dimension_semantics=sem, vmem_limit_bytes=_vmem_limit(nbytes))


LOG2E = 1.4426950408889634


def _silu(x):
    return x / (1.0 + jnp.exp2(x * (-LOG2E)))


def _rms(x, g):
    return x * lax.rsqrt(jnp.mean(x * x, axis=-1, keepdims=True) + EPS) * g


def _bdot(a, b):
    return jnp.dot(a.astype(BF16), b.astype(BF16), preferred_element_type=F32)


def _cast_transposed(wt_ref, wb_ref):
    step = 2 * MXU_COLS
    for r0 in range(0, wt_ref.shape[0], step):
        r1 = min(r0 + step, wt_ref.shape[0])
        wb_ref[:, r0:r1] = wt_ref[r0:r1, :].T.astype(BF16)


def _mm_plain_kernel(x_ref, wt_ref, o_ref, wb_ref):
    @pl.when(pl.program_id(1) == 0)
    def _():
        _cast_transposed(wt_ref, wb_ref)

    o_ref[...] = _bdot(x_ref[...], wb_ref[...]).astype(o_ref.dtype)


def _mm_bias_gelu_kernel(x_ref, w_ref, b_ref, o_ref, wb_ref):
    @pl.when(pl.program_id(1) == 0)
    def _():
        wb_ref[...] = w_ref[...].astype(BF16)

    x = x_ref[...].astype(BF16)
    for c0 in range(0, o_ref.shape[1], 2 * MXU_COLS):
        cols = slice(c0, c0 + 2 * MXU_COLS)
        y = jnp.dot(x, wb_ref[:, cols], preferred_element_type=F32) + b_ref[:, cols]
        o_ref[:, cols] = (0.5 * y * (1.0 + lax.erf(y * (2.0 ** -0.5)))).astype(o_ref.dtype)


def _mm_swiglu_kernel(x_ref, wg_ref, wu_ref, o_ref, wgb_ref, wub_ref):
    @pl.when(pl.program_id(1) == 0)
    def _():
        wgb_ref[...] = wg_ref[...].astype(BF16)
        wub_ref[...] = wu_ref[...].astype(BF16)

    x = x_ref[...].astype(BF16)
    tn = o_ref.shape[1]
    for c0 in range(0, tn, MXU_COLS):
        cols = slice(c0, min(c0 + MXU_COLS, tn))
        g = jnp.dot(x, wgb_ref[:, cols], preferred_element_type=F32)
        u = jnp.dot(x, wub_ref[:, cols], preferred_element_type=F32)
        o_ref[:, cols] = (_silu(g) * u).astype(o_ref.dtype)


def _wspec(k, tn, col_block0):
    return pl.BlockSpec((k, tn), lambda j, i: (0, j + col_block0), pipeline_mode=pl.Buffered(1))


def _mm_plain(x, wt, col0, n, tm, tn, out_dtype, name):
    m, k = x.shape
    nbytes = k * tn * 6 + 2 * tm * k * x.dtype.itemsize + 2 * tm * tn * 4 + 2 * MXU_COLS * k * 8
    row_block0 = col0 // tn
    return pl.pallas_call(
        _mm_plain_kernel,
        grid=(n // tn, m // tm),
        in_specs=[pl.BlockSpec((tm, k), lambda j, i: (i, 0)),
                  pl.BlockSpec((tn, k), lambda j, i: (j + row_block0, 0), pipeline_mode=pl.Buffered(1))],
        out_specs=pl.BlockSpec((tm, tn), lambda j, i: (i, j)),
        out_shape=jax.ShapeDtypeStruct((m, n), out_dtype),
        scratch_shapes=[pltpu.VMEM((k, tn), BF16)],
        compiler_params=_params(("arbitrary", "arbitrary"), nbytes),
        name=name,
    )(x, wt)


def _norm_in_proj_kernel(x_ref, g_ref, wz_ref, wdt_ref, a_ref, z_ref, dtr_ref, wzb_ref, wdtb_ref):
    @pl.when(pl.program_id(0) == 0)
    def _():
        _cast_transposed(wz_ref, wzb_ref)
        _cast_transposed(wdt_ref, wdtb_ref)

    a = _rms(x_ref[...], g_ref[...]).astype(BF16)
    a_ref[...] = a
    z_ref[...] = jnp.dot(a, wzb_ref[...], preferred_element_type=F32).astype(z_ref.dtype)
    dtr_ref[...] = jnp.dot(a, wdtb_ref[...], preferred_element_type=F32)


def _norm_in_proj(x, g, w_in_t, w_dt_t, tm):
    m, k = x.shape
    n = D_INNER
    row = lambda i: (i, 0)
    fixed = lambda i: (0, 0)
    nbytes = k * n * 6 + k * LANES * 6 + 2 * tm * k * 4 + 2 * tm * k * 2 + 2 * tm * n * 2 + 2 * tm * n * 4
    return pl.pallas_call(
        _norm_in_proj_kernel,
        grid=(m // tm,),
        in_specs=[pl.BlockSpec((tm, k), row), pl.BlockSpec((1, k), fixed),
                  pl.BlockSpec((n, k), fixed, pipeline_mode=pl.Buffered(1)),
                  pl.BlockSpec((LANES, k), fixed)],
        out_specs=[pl.BlockSpec((tm, k), row), pl.BlockSpec((tm, n), row), pl.BlockSpec((tm, LANES), row)],
        out_shape=[jax.ShapeDtypeStruct((m, k), BF16), jax.ShapeDtypeStruct((m, n), BF16),
                   jax.ShapeDtypeStruct((m, LANES), F32)],
        scratch_shapes=[pltpu.VMEM((k, n), BF16), pltpu.VMEM((k, LANES), BF16)],
        compiler_params=_params(("arbitrary",), nbytes),
        name="norm_in_proj",
    )(x, g.reshape(1, k), w_in_t, w_dt_t)


def _mm_bias_gelu(x, w, b, tm, tn, out_dtype, name):
    m, k = x.shape
    n = w.shape[1]
    nbytes = k * tn * 6 + 2 * tm * k * x.dtype.itemsize + 2 * tm * tn * 4
    return pl.pallas_call(
        _mm_bias_gelu_kernel,
        grid=(n // tn, m // tm),
        in_specs=[pl.BlockSpec((tm, k), lambda j, i: (i, 0)), _wspec(k, tn, 0),
                  pl.BlockSpec((1, tn), lambda j, i: (0, j))],
        out_specs=pl.BlockSpec((tm, tn), lambda j, i: (i, j)),
        out_shape=jax.ShapeDtypeStruct((m, n), out_dtype),
        scratch_shapes=[pltpu.VMEM((k, tn), BF16)],
        compiler_params=_params(("arbitrary", "arbitrary"), nbytes),
        name=name,
    )(x, w, b.reshape(1, n))


def _mm_swiglu(x, w_gu, d_ff, tm, tn, name):
    m, k = x.shape
    nbytes = 2 * k * tn * 6 + 2 * tm * k * x.dtype.itemsize + 2 * tm * tn * 2 + 3 * tm * tn * 4
    return pl.pallas_call(
        _mm_swiglu_kernel,
        grid=(d_ff // tn, m // tm),
        in_specs=[pl.BlockSpec((tm, k), lambda j, i: (i, 0)), _wspec(k, tn, 0), _wspec(k, tn, d_ff // tn)],
        out_specs=pl.BlockSpec((tm, tn), lambda j, i: (i, j)),
        out_shape=jax.ShapeDtypeStruct((m, d_ff), BF16),
        scratch_shapes=[pltpu.VMEM((k, tn), BF16), pltpu.VMEM((k, tn), BF16)],
        compiler_params=_params(("arbitrary", "arbitrary"), nbytes),
        name=name,
    )(x, w_gu, w_gu)


def _mm_resnorm_kernel(x_ref, w_ref, r_ref, g_ref, h_ref, a_ref, wb_ref):
    @pl.when(pl.program_id(0) == 0)
    def _():
        wb_ref[...] = w_ref[...].astype(BF16)

    h = r_ref[...] + _bdot(x_ref[...], wb_ref[...])
    h_ref[...] = h
    a_ref[...] = _rms(h, g_ref[...]).astype(a_ref.dtype)


def _mm_resnorm(x, w, resid, g, tm, name):
    m, k = x.shape
    d = w.shape[1]
    nbytes = k * d * 6 + 2 * tm * k * 2 + 8 * tm * d * 4
    return pl.pallas_call(
        _mm_resnorm_kernel,
        grid=(m // tm,),
        in_specs=[pl.BlockSpec((tm, k), lambda i: (i, 0)),
                  pl.BlockSpec((k, d), lambda i: (0, 0), pipeline_mode=pl.Buffered(1)),
                  pl.BlockSpec((tm, d), lambda i: (i, 0)),
                  pl.BlockSpec((1, d), lambda i: (0, 0))],
        out_specs=[pl.BlockSpec((tm, d), lambda i: (i, 0)), pl.BlockSpec((tm, d), lambda i: (i, 0))],
        out_shape=[jax.ShapeDtypeStruct((m, d), F32), jax.ShapeDtypeStruct((m, d), BF16)],
        scratch_shapes=[pltpu.VMEM((k, d), BF16)],
        compiler_params=_params(("arbitrary",), nbytes),
        name=name,
    )(x, w, resid, g.reshape(1, d))


def _sgu_out_kernel(u_ref, v_ref, xt_ref, lng_ref, lnb_ref, ws_ref, bst_ref, w_ref, rp_ref, rt_ref, g_ref, w3_ref,
                    h_ref, a_ref, lg_ref, wb_ref, wsb_ref, *, prompt_tiles):
    i = pl.program_id(0)
    q = CHUNK

    @pl.when(i == 0)
    def _():
        wb_ref[...] = w_ref[...].astype(BF16)
        causal = lax.broadcasted_iota(I32, (q, q), 0) >= lax.broadcasted_iota(I32, (q, q), 1)
        for g in range(SGU_HEADS):
            wsb_ref[g] = jnp.where(causal, ws_ref[g], 0.0).astype(BF16)

    def project(x, r_ref, rows):
        h = r_ref[rows, :] + jnp.dot(x, wb_ref[...], preferred_element_type=F32)
        h_ref[rows, :] = h
        a = _rms(h, g_ref[...])
        a_hi = a.astype(BF16)
        a_lo = (a - a_hi.astype(F32)).astype(BF16)
        a_ref[rows, :] = a_hi
        lg_ref[rows, :] = (jnp.dot(a_hi, w3_ref[...], preferred_element_type=F32)
                           + jnp.dot(a_lo, w3_ref[...], preferred_element_type=F32))

    chunks = [slice(c * q, (c + 1) * q) for c in range(u_ref.shape[0] // q)]

    @pl.when(i < prompt_tiles)
    def _():
        for rows in chunks:
            vn = _layernorm(v_ref[rows, :].astype(F32), lng_ref[...], lnb_ref[...]).astype(BF16)
            gated = []
            for g in range(SGU_HEADS):
                cols = slice(g * SGU_HEAD_DIM, (g + 1) * SGU_HEAD_DIM)
                s = jnp.dot(wsb_ref[g], vn[:, cols], preferred_element_type=F32) + bst_ref[:, g:g + 1]
                gated.append((u_ref[rows, cols].astype(F32) * s).astype(BF16))
            project(jnp.concatenate(gated, axis=1), rp_ref, rows)

    @pl.when(i == prompt_tiles)
    def _():
        for rows in chunks:
            project(xt_ref[rows, :], rt_ref, rows)


def _sgu_out(uv, xt, ln_g, ln_b, w_s, bs_t, w, rp, rt, g, w3, tm):
    mp = uv.shape[0]
    k, d = w.shape
    prompt_tiles = mp // tm
    last = prompt_tiles - 1
    rows = mp + tm
    prow = lambda i: (jnp.minimum(i, last), 0)
    fixed = lambda i: (0, 0)
    nbytes = k * d * 6 + 8 * tm * k * 2 + 12 * tm * d * 4 + 8 * CHUNK * k * 4
    return pl.pallas_call(
        functools.partial(_sgu_out_kernel, prompt_tiles=prompt_tiles),
        grid=(prompt_tiles + 1,),
        in_specs=[pl.BlockSpec((tm, k), prow),
                  pl.BlockSpec((tm, k), lambda i: (jnp.minimum(i, last), 1)),
                  pl.BlockSpec((tm, k), fixed),
                  pl.BlockSpec((1, k), fixed), pl.BlockSpec((1, k), fixed),
                  pl.BlockSpec((SGU_HEADS, CHUNK, CHUNK), lambda i: (0, 0, 0)),
                  pl.BlockSpec((CHUNK, LANES), fixed),
                  pl.BlockSpec((k, d), fixed, pipeline_mode=pl.Buffered(1)),
                  pl.BlockSpec((tm, d), prow),
                  pl.BlockSpec((tm, d), fixed),
                  pl.BlockSpec((1, d), fixed),
                  pl.BlockSpec((d, LANES), fixed)],
        out_specs=[pl.BlockSpec((tm, d), lambda i: (i, 0)), pl.BlockSpec((tm, d), lambda i: (i, 0)),
                   pl.BlockSpec((tm, LANES), lambda i: (i, 0))],
        out_shape=[jax.ShapeDtypeStruct((rows, d), F32), jax.ShapeDtypeStruct((rows, d), BF16),
                   jax.ShapeDtypeStruct((rows, LANES), F32)],
        scratch_shapes=[pltpu.VMEM((k, d), BF16), pltpu.VMEM((SGU_HEADS, CHUNK, CHUNK), BF16)],
        compiler_params=_params(("arbitrary",), nbytes),
        name="sgu_out",
    )(uv, uv, xt, ln_g, ln_b, w_s, bs_t, w, rp, rt, g.reshape(1, d), w3)


def _gate_and_groupnorm(y, xs, z, d_rep, gn_w):
    yg = (y + d_rep * xs) * _silu(z)
    gw = D_INNER // SSD_GROUPS
    outs = []
    for g in range(SSD_GROUPS):
        blk = yg[:, g * gw:(g + 1) * gw]
        outs.append(blk * lax.rsqrt(jnp.mean(blk * blk, axis=-1, keepdims=True) + EPS))
    return jnp.concatenate(outs, axis=1) * gn_w


def _softplus(x):
    return jnp.maximum(x, 0.0) + jnp.log(1.0 + jnp.exp(-jnp.abs(x)))


SSD_SEQS = 2


def _ssd_prompt_kernel(xbc_ref, z_ref, dtr_ref, cw_ref, cb_ref, dtb_ref, alog_ref, drep_ref, gnw_ref,
                       yg_ref, ssm_ref, conv_ref, ext_ref, st_ref, y_ref, xa_ref):
    for n in range(SSD_SEQS):
        _ssd_chunk(xbc_ref.at[n], z_ref.at[n], dtr_ref.at[n], cw_ref, cb_ref, dtb_ref, alog_ref, drep_ref, gnw_ref,
                   yg_ref.at[n], ssm_ref.at[n], conv_ref.at[n], ext_ref.at[n], st_ref.at[n], y_ref.at[n],
                   xa_ref.at[n])


def _ssd_chunk(xbc_ref, z_ref, dtr_ref, cw_ref, cb_ref, dtb_ref, alog_ref, drep_ref, gnw_ref,
               yg_ref, ssm_ref, conv_ref, ext_ref, st_ref, y_ref, xa_ref):
    c = pl.program_id(1)
    nc = pl.num_programs(1)
    q = CHUNK

    @pl.when(c == 0)
    def _():
        ext_ref[0:SUBLANES, :] = jnp.zeros((SUBLANES, CONV_DIM), F32)
        st_ref[...] = jnp.zeros_like(st_ref)

    ext_ref[SUBLANES:SUBLANES + q, :] = xbc_ref[...]
    acc = cb_ref[...] + cw_ref[CONV_W - 1:CONV_W, :] * ext_ref[SUBLANES:SUBLANES + q, :]
    for k in range(CONV_W - 1):
        lo = SUBLANES - (CONV_W - 1) + k
        acc = acc + cw_ref[k:k + 1, :] * ext_ref[lo:lo + q, :]
    xa_ref[...] = _silu(acc)
    tail = ext_ref[q:q + SUBLANES, :]
    ext_ref[0:SUBLANES, :] = tail

    dt = _softplus(dtr_ref[...] + dtb_ref[...])
    a_neg = -jnp.exp(alog_ref[...])
    causal = lax.broadcasted_iota(I32, (q, q), 0) >= lax.broadcasted_iota(I32, (q, q), 1)
    tril = jnp.where(causal, 1.0, 0.0)
    cs = jnp.dot(tril, dt * a_neg, precision=HIGHEST, preferred_element_type=F32) * LOG2E
    cs_t = cs.T
    dt_t = dt.T
    rowp = cs_t - jnp.log2(dt_t)
    w_t = dt_t * jnp.exp2(cs_t[:, q - 1:q] - cs_t)
    first = lax.broadcasted_iota(I32, (q, 2 * SSD_HEADDIM), 1) < SSD_HEADDIM

    def per_head(v):
        zero = jnp.zeros_like(v)
        return jnp.concatenate([jnp.where(first, v, zero), jnp.where(first, zero, v)], axis=0)

    for g in range(SSD_GROUPS):
        b_g = xa_ref[:, D_INNER + g * D_STATE:D_INNER + (g + 1) * D_STATE]
        c_g = xa_ref[:, D_INNER + (SSD_GROUPS + g) * D_STATE:D_INNER + (SSD_GROUPS + g + 1) * D_STATE]
        b_gt = b_g.T
        cb = _bdot(c_g, b_gt)
        for pr in range(4):
            hp = g * 4 + pr
            cols = slice(hp * 2 * SSD_HEADDIM, (hp + 1) * 2 * SSD_HEADDIM)
            x2 = per_head(xa_ref[:, cols].astype(BF16))
            st_p = st_ref[:, cols]
            st2 = per_head(st_p.astype(BF16))
            ms, cs_scaled, bws, cds = [], [], [], []
            for s in range(2):
                h = 2 * hp + s
                colb = jnp.broadcast_to(cs[:, h:h + 1], (q, q))
                rowb = jnp.broadcast_to(rowp[h:h + 1, :], (q, q))
                ecol = jnp.exp2(colb)
                ms.append((cb * jnp.where(causal, jnp.exp2(colb - rowb), 0.0)).astype(BF16))
                cs_scaled.append((c_g * ecol).astype(BF16))
                bws.append((b_gt * w_t[h:h + 1, :]).astype(BF16))
                cds.append(ecol[q - 1:q, :])
            y_ref[:, cols] = jnp.dot(jnp.concatenate(ms + cs_scaled, axis=1), jnp.concatenate([x2, st2], axis=0),
                                     preferred_element_type=F32)
            cd = jnp.where(first[0:1, :], cds[0], cds[1])
            st_ref[:, cols] = st_p * cd + jnp.dot(jnp.concatenate(bws, axis=1), x2, preferred_element_type=F32)

    out = _gate_and_groupnorm(y_ref[...], xa_ref[:, :D_INNER], z_ref[...].astype(F32), drep_ref[...],
                              gnw_ref[...])
    yg_ref[...] = out.astype(yg_ref.dtype)

    @pl.when(c == nc - 1)
    def _():
        conv_ref[...] = tail[SUBLANES - (CONV_W - 1):, :]
        for hp in range(SSD_HEADS // 2):
            blk = st_ref[:, hp * 2 * SSD_HEADDIM:(hp + 1) * 2 * SSD_HEADDIM].T
            ssm_ref[2 * hp:2 * hp + 2] = blk.reshape(2, SSD_HEADDIM, D_STATE)


def _ssd_prompt(xbc, z, dtr, conv_w, conv_b, dtb, alog, d_rep, gn_w):
    batch, seq, _ = xbc.shape
    nb = SSD_SEQS
    blk = lambda b, c: (b, c, 0)
    fixed = lambda b, c: (0, 0)
    nbytes = nb * (2 * CHUNK * (CONV_DIM * 4 + D_INNER * 4 + LANES * 4) + (CHUNK + SUBLANES) * CONV_DIM * 4
                   + 4 * D_STATE * D_INNER * 4 + 12 * CHUNK * CONV_DIM * 4)
    return pl.pallas_call(
        _ssd_prompt_kernel,
        grid=(batch // nb, seq // CHUNK),
        in_specs=[pl.BlockSpec((nb, CHUNK, CONV_DIM), blk), pl.BlockSpec((nb, CHUNK, D_INNER), blk),
                  pl.BlockSpec((nb, CHUNK, LANES), blk),
                  pl.BlockSpec((CONV_W, CONV_DIM), fixed), pl.BlockSpec((1, CONV_DIM), fixed),
                  pl.BlockSpec((1, LANES), fixed), pl.BlockSpec((1, LANES), fixed),
                  pl.BlockSpec((1, D_INNER), fixed), pl.BlockSpec((1, D_INNER), fixed)],
        out_specs=[pl.BlockSpec((nb, CHUNK, D_INNER), blk),
                   pl.BlockSpec((nb, SSD_HEADS, SSD_HEADDIM, D_STATE), lambda b, c: (b, 0, 0, 0)),
                   pl.BlockSpec((nb, CONV_W - 1, CONV_DIM), lambda b, c: (b, 0, 0))],
        out_shape=[jax.ShapeDtypeStruct((batch, seq, D_INNER), BF16),
                   jax.ShapeDtypeStruct((batch, SSD_HEADS, SSD_HEADDIM, D_STATE), F32),
                   jax.ShapeDtypeStruct((batch, CONV_W - 1, CONV_DIM), F32)],
        scratch_shapes=[pltpu.VMEM((nb, CHUNK + SUBLANES, CONV_DIM), F32), pltpu.VMEM((nb, D_STATE, D_INNER), F32),
                        pltpu.VMEM((nb, CHUNK, D_INNER), F32), pltpu.VMEM((nb, CHUNK, CONV_DIM), F32)],
        compiler_params=_params(("arbitrary", "arbitrary"), nbytes),
        name="ssd_prompt",
    )(xbc, z, dtr, conv_w, conv_b, dtb, alog, d_rep, gn_w)


def _ssd_out_kernel(y_ref, xs_ref, z_ref, drep_ref, gnw_ref, w_ref, r_ref, g_ref, h_ref, a_ref, wb_ref):
    @pl.when(pl.program_id(0) == 0)
    def _():
        wb_ref[...] = w_ref[...].astype(BF16)

    sub = min(ROW_SUB // 2, y_ref.shape[0])
    for s in range(y_ref.shape[0] // sub):
        rows = slice(s * sub, (s + 1) * sub)
        yg = _gate_and_groupnorm(y_ref[rows, :], xs_ref[rows, :], z_ref[rows, :].astype(F32), drep_ref[...],
                                 gnw_ref[...])
        h = r_ref[rows, :] + _bdot(yg, wb_ref[...])
        h_ref[rows, :] = h
        a_ref[rows, :] = _rms(h, g_ref[...]).astype(a_ref.dtype)


def _ssd_out(y, xa, z, d_rep, gn_w, w, resid, g, tm):
    m, k = y.shape
    d = w.shape[1]
    row = lambda i: (i, 0)
    fixed = lambda i: (0, 0)
    nbytes = k * d * 6 + 2 * tm * k * 10 + 8 * tm * d * 4 + 6 * tm * k * 4
    return pl.pallas_call(
        _ssd_out_kernel,
        grid=(m // tm,),
        in_specs=[pl.BlockSpec((tm, k), row), pl.BlockSpec((tm, k), row), pl.BlockSpec((tm, k), row),
                  pl.BlockSpec((1, k), fixed), pl.BlockSpec((1, k), fixed),
                  pl.BlockSpec((k, d), fixed, pipeline_mode=pl.Buffered(1)),
                  pl.BlockSpec((tm, d), row), pl.BlockSpec((1, d), fixed)],
        out_specs=[pl.BlockSpec((tm, d), row), pl.BlockSpec((tm, d), row)],
        out_shape=[jax.ShapeDtypeStruct((m, d), F32), jax.ShapeDtypeStruct((m, d), BF16)],
        scratch_shapes=[pltpu.VMEM((k, d), BF16)],
        compiler_params=_params(("arbitrary",), nbytes),
        name="ssd_out",
    )(y, xa, z, d_rep, gn_w, w, resid, g.reshape(1, d))


def _ssd_sample_pre_kernel(xbc_ref, conv_ref, dtr_ref, cw_ref, cb_ref, dtb_ref, alog_ref,
                           nconv_ref, xa_ref, xdt_ref, da_ref, bst_ref, cst_ref):
    xn = xbc_ref[...]
    acc = cb_ref[...] + cw_ref[CONV_W - 1:CONV_W, :] * xn
    for k in range(CONV_W - 1):
        acc = acc + cw_ref[k:k + 1, :] * conv_ref[:, k * CONV_DIM:(k + 1) * CONV_DIM]
    xa = _silu(acc)
    xa_ref[...] = xa
    nconv_ref[:, 0:CONV_DIM] = conv_ref[:, CONV_DIM:2 * CONV_DIM]
    nconv_ref[:, CONV_DIM:2 * CONV_DIM] = conv_ref[:, 2 * CONV_DIM:3 * CONV_DIM]
    nconv_ref[:, 2 * CONV_DIM:3 * CONV_DIM] = xn

    dt = _softplus(dtr_ref[...] + dtb_ref[...])
    a_neg = -jnp.exp(alog_ref[...])
    hh = lax.broadcasted_iota(I32, (LANES, D_INNER), 0)
    cc = lax.broadcasted_iota(I32, (LANES, D_INNER), 1)
    expand = jnp.where(lax.shift_right_logical(cc, 6) == hh, 1.0, 0.0)
    dt_rep = jnp.dot(dt, expand, precision=HIGHEST, preferred_element_type=F32)
    xdt_ref[...] = xa[:, :D_INNER] * dt_rep
    da_ref[...] = jnp.exp(dt * a_neg)

    n = xn.shape[0]
    gs = SSD_GROUPS * D_STATE
    b1, b2, b3 = _split3(xa[:, D_INNER:D_INNER + gs])
    cm = xa[:, D_INNER + gs:]
    c1 = cm.astype(BF16)
    c2 = (cm - c1.astype(F32)).astype(BF16)
    zero = jnp.zeros((n, D_STATE), BF16)
    b_rows, c_rows = [], []
    for g in range(SSD_GROUPS):
        sl = slice(g * D_STATE, (g + 1) * D_STATE)
        b_rows += [b1[:, sl], b2[:, sl], b1[:, sl], b3[:, sl], b2[:, sl], b1[:, sl]] + [zero] * (TERM_ROWS - 6)
        c_rows += [c1[:, sl], c2[:, sl]] + [zero] * (TERM_ROWS - 2)
    bst_ref[...] = jnp.concatenate(b_rows, axis=1)
    cst_ref[...] = jnp.concatenate(c_rows, axis=1)


TERM_ROWS = 16


def _split3(v):
    v1 = v.astype(BF16)
    r1 = v - v1.astype(F32)
    v2 = r1.astype(BF16)
    v3 = (r1 - v2.astype(F32)).astype(BF16)
    return v1, v2, v3


def _ssd_sample_pre(xbc, conv_flat, dtr, conv_w, conv_b, dtb, alog):
    n = xbc.shape[0]
    full = lambda shape: pl.BlockSpec(shape, lambda i: (0,) * len(shape))
    stack = SSD_GROUPS * TERM_ROWS * D_STATE
    return pl.pallas_call(
        _ssd_sample_pre_kernel,
        grid=(1,),
        in_specs=[full((n, CONV_DIM)), full((n, 3 * CONV_DIM)), full((n, LANES)), full((CONV_W, CONV_DIM)),
                  full((1, CONV_DIM)), full((1, LANES)), full((1, LANES))],
        out_specs=[full((n, 3 * CONV_DIM)), full((n, CONV_DIM)), full((n, D_INNER)), full((n, LANES)),
                   full((n, stack)), full((n, stack))],
        out_shape=[jax.ShapeDtypeStruct((n, 3 * CONV_DIM), F32), jax.ShapeDtypeStruct((n, CONV_DIM), F32),
                   jax.ShapeDtypeStruct((n, D_INNER), F32), jax.ShapeDtypeStruct((n, LANES), F32),
                   jax.ShapeDtypeStruct((n, stack), BF16), jax.ShapeDtypeStruct((n, stack), BF16)],
        compiler_params=_params(("arbitrary",), 32 * n * CONV_DIM * 4),
        name="ssd_sample_pre",
    )(xbc, conv_flat, dtr, conv_w, conv_b, dtb, alog)


STATE_SAMPLES = 8


def _ssd_sample_state_kernel(da_ref, st_ref, xdt_ref, bst_ref, cst_ref, nst_ref, y_ref):
    first = pl.program_id(0) * STATE_SAMPLES
    rows_g = D_INNER // SSD_GROUPS
    sub = lax.broadcasted_iota(I32, (TERM_ROWS, D_INNER), 0)
    for s in range(STATE_SAMPLES):
        x1, x2, x3 = [t.astype(F32) for t in _split3(xdt_ref[s])]
        terms = jnp.where(sub == 0, x1, jnp.where(sub == 1, x1, jnp.where(sub == 2, x2, jnp.where(
            sub == 3, x1, jnp.where(sub == 4, x2, jnp.where(sub == 5, x3, 0.0)))))).astype(BF16)
        ys = []
        for g in range(SSD_GROUPS):
            tile = slice(g * TERM_ROWS, (g + 1) * TERM_ROWS)
            upd = lax.dot_general(terms[:, g * rows_g:(g + 1) * rows_g], bst_ref[s, tile, :],
                                  (((0,), (0,)), ((), ())), preferred_element_type=F32)
            halves = []
            for hh in range(rows_g // SSD_HEADDIM):
                head = g * (rows_g // SSD_HEADDIM) + hh
                rows = slice(head * SSD_HEADDIM, (head + 1) * SSD_HEADDIM)
                h_new = (da_ref[first + s, head] * st_ref[s, rows, :]
                         + upd[hh * SSD_HEADDIM:(hh + 1) * SSD_HEADDIM, :])
                nst_ref[s, rows, :] = h_new
                halves.append(h_new)
            h_g = jnp.concatenate(halves, axis=0)
            h_hi = h_g.astype(BF16)
            h_lo = (h_g - h_hi.astype(F32)).astype(BF16)
            nt = (((1,), (1,)), ((), ()))
            o_hi = lax.dot_general(cst_ref[s, tile, :], h_hi, nt, preferred_element_type=F32)
            o_lo = lax.dot_general(cst_ref[s, tile, :], h_lo, nt, preferred_element_type=F32)
            ys.append(o_hi[0:1, :] + o_hi[1:2, :] + o_lo[0:1, :])
        y_ref[s] = jnp.concatenate(ys, axis=1)


def _ssd_sample_state(state, xdt, da, bst, cst):
    n = state.shape[0]
    ns = STATE_SAMPLES
    stack_rows = SSD_GROUPS * TERM_ROWS
    return pl.pallas_call(
        _ssd_sample_state_kernel,
        grid=(n // ns,),
        in_specs=[pl.BlockSpec(memory_space=pltpu.SMEM),
                  pl.BlockSpec((ns, D_INNER, D_STATE), lambda i: (i, 0, 0)),
                  pl.BlockSpec((ns, 1, D_INNER), lambda i: (i, 0, 0)),
                  pl.BlockSpec((ns, stack_rows, D_STATE), lambda i: (i, 0, 0)),
                  pl.BlockSpec((ns, stack_rows, D_STATE), lambda i: (i, 0, 0))],
        out_specs=[pl.BlockSpec((ns, D_INNER, D_STATE), lambda i: (i, 0, 0)),
                   pl.BlockSpec((ns, 1, D_INNER), lambda i: (i, 0, 0))],
        out_shape=[jax.ShapeDtypeStruct((n, D_INNER, D_STATE), F32),
                   jax.ShapeDtypeStruct((n, 1, D_INNER), F32)],
        compiler_params=_params(("arbitrary",), 6 * ns * D_INNER * D_STATE * 4),
        name="ssd_sample_state",
    )(da, state, xdt, bst, cst)


def _layernorm(v, g, b):
    mu = jnp.mean(v, axis=-1, keepdims=True)
    d = v - mu
    var = jnp.mean(d * d, axis=-1, keepdims=True)
    return d * lax.rsqrt(var + EPS) * g + b


def _sgu_sample_kernel(u_ref, v_ref, lng_ref, lnb_ref, w0_ref, b0_ref, o_ref, vn_ref):
    vn = _layernorm(v_ref[...], lng_ref[...], lnb_ref[...])
    vn_ref[...] = vn
    o_ref[...] = (u_ref[...] * (w0_ref[...] * vn + b0_ref[...])).astype(o_ref.dtype)


def _sgu_sample(uv, ln_g, ln_b, w0_rep, b0_rep):
    n = uv.shape[0]
    fixed2 = lambda i: (0, 0)
    return pl.pallas_call(
        _sgu_sample_kernel,
        grid=(1,),
        in_specs=[pl.BlockSpec((n, D_SGU), lambda i: (0, 0)), pl.BlockSpec((n, D_SGU), lambda i: (0, 1)),
                  pl.BlockSpec((1, D_SGU), fixed2), pl.BlockSpec((1, D_SGU), fixed2),
                  pl.BlockSpec((1, D_SGU), fixed2), pl.BlockSpec((1, D_SGU), fixed2)],
        out_specs=[pl.BlockSpec((n, D_SGU), fixed2), pl.BlockSpec((n, D_SGU), fixed2)],
        out_shape=[jax.ShapeDtypeStruct((n, D_SGU), BF16), jax.ShapeDtypeStruct((n, D_SGU), F32)],
        compiler_params=_params(("arbitrary",), 12 * n * D_SGU * 4),
        name="sgu_sample",
    )(uv, uv, ln_g, ln_b, w0_rep, b0_rep)


def _experts_to_lanes(col, sub, lane):
    return jnp.sum(jnp.where(sub == lane, col, 0.0), axis=0, keepdims=True)


def _route_kernel(lg_ref, loc_ref, gate_ref, bmeta_ref, tmeta_ref, *, tile, n_valid):
    nb, _, tb = lg_ref.shape
    sub = lax.broadcasted_iota(I32, (N_EXPERTS, LANES), 0)
    lane = lax.broadcasted_iota(I32, (N_EXPERTS, LANES), 1)
    subf = lax.broadcasted_iota(I32, (N_EXPERTS, tb), 0).astype(F32)
    tok = lax.broadcasted_iota(I32, (N_EXPERTS, tb), 1)
    incl = jnp.where(lax.broadcasted_iota(I32, (tb, tb), 0) <= lax.broadcasted_iota(I32, (tb, tb), 1), 1.0, 0.0)
    neg = jnp.float32(-jnp.inf)
    none = jnp.float32(N_EXPERTS)

    def select(k):
        blk = lg_ref[k]
        l = blk[0:N_EXPERTS] + blk[N_EXPERTS:2 * N_EXPERTS]
        m1 = jnp.max(l, axis=0, keepdims=True)
        i1 = jnp.min(jnp.where(l == m1, subf, none), axis=0, keepdims=True)
        l2 = jnp.where(subf == i1, neg, l)
        m2 = jnp.max(l2, axis=0, keepdims=True)
        i2 = jnp.min(jnp.where(l2 == m2, subf, none), axis=0, keepdims=True)
        valid = (k * tb + tok) < n_valid
        sel = jnp.where(valid, jnp.where(subf == i1, 1.0, jnp.where(subf == i2, 1.0, 0.0)), 0.0)
        return m1, i1, m2, i2, valid, sel

    def run_rows(cnt):
        return jnp.ceil(cnt / SUBLANES) * SUBLANES

    def count_body(k, carry):
        return carry + run_rows(jnp.sum(select(k)[5], axis=1, keepdims=True))

    counts = lax.fori_loop(0, nb, count_body, jnp.zeros((N_EXPERTS, 1), F32))
    tiles = jnp.ceil(counts / tile)
    cum_incl = jnp.sum(jnp.where(lane <= sub, _experts_to_lanes(tiles, sub, lane), 0.0), axis=1, keepdims=True)
    offset = (cum_incl - tiles) * tile
    tile_expert = jnp.sum(jnp.where(lane.astype(F32) >= cum_incl, 1.0, 0.0), axis=0, keepdims=True)
    tile_expert = jnp.minimum(tile_expert, N_EXPERTS - 1.0)
    n_used = jnp.max(cum_incl, axis=0, keepdims=True)
    region_end = _experts_to_lanes(offset + counts, sub, lane)
    region_pad = _experts_to_lanes(tiles * tile - counts, sub, lane)
    tmeta_ref[...] = jnp.where(sub == 0, tile_expert, jnp.where(sub == 1, n_used, jnp.where(
        sub == 2, region_end, jnp.where(sub == 3, region_pad, 0.0)))).astype(I32)

    def place_body(k, before):
        m1, i1, m2, i2, valid, sel = select(k)
        run = jnp.dot(sel, incl, preferred_element_type=F32)
        cnt = run_rows(run[:, tb - 1:tb])
        lstart = jnp.sum(jnp.where(lane < sub, _experts_to_lanes(cnt, sub, lane), 0.0), axis=1, keepdims=True)
        local = lstart + run - sel
        loc1 = jnp.sum(jnp.where(subf == i1, local, 0.0), axis=0, keepdims=True)
        loc2 = jnp.sum(jnp.where(subf == i2, local, 0.0), axis=0, keepdims=True)
        live = (k * tb + tok[0:1, :]) < n_valid
        loc_ref[k] = jnp.where(subf == 0.0, jnp.where(live, loc1, -1.0),
                               jnp.where(subf == 1.0, jnp.where(live, loc2, -1.0), 0.0))
        e = jnp.exp(m2 - m1)
        gate_ref[k] = jnp.where(subf == 0.0, 1.0 / (1.0 + e), jnp.where(subf == 1.0, e / (1.0 + e), 0.0))
        bmeta_ref[k] = jnp.where(lane == 0, offset + before,
                                 jnp.where(lane == 1, cnt, jnp.where(lane == 2, lstart, 0.0))).astype(I32)
        return before + cnt

    lax.fori_loop(0, nb, place_body, jnp.zeros((N_EXPERTS, 1), F32))


def _route(logits3, tile, n_valid):
    nb, _, tb = logits3.shape
    tok_blk = pl.BlockSpec((nb, N_EXPERTS, tb), lambda i: (0, 0, 0))
    return pl.pallas_call(
        functools.partial(_route_kernel, tile=tile, n_valid=n_valid),
        grid=(1,),
        in_specs=[pl.BlockSpec((nb, 2 * N_EXPERTS, tb), lambda i: (0, 0, 0))],
        out_specs=[tok_blk, tok_blk, pl.BlockSpec((nb, N_EXPERTS, LANES), lambda i: (0, 0, 0)),
                   pl.BlockSpec((N_EXPERTS, LANES), lambda i: (0, 0))],
        out_shape=[jax.ShapeDtypeStruct((nb, N_EXPERTS, tb), F32),
                   jax.ShapeDtypeStruct((nb, N_EXPERTS, tb), F32),
                   jax.ShapeDtypeStruct((nb, N_EXPERTS, LANES), I32),
                   jax.ShapeDtypeStruct((N_EXPERTS, LANES), I32)],
        compiler_params=_params(("arbitrary",), 16 * nb * N_EXPERTS * tb * 4),
        name="moe_route",
    )(logits3)


def _run_copies(bm_ref, blk, vmem_ref, hbm_ref, sem, *, tb, to_hbm, wait):
    for e in range(N_EXPERTS):
        base = (blk * N_EXPERTS + e) * 3
        start, cnt, lstart = bm_ref[base], bm_ref[base + 1], bm_ref[base + 2]
        off = 0
        size = tb
        while size >= SUBLANES:
            @pl.when((cnt & size) != 0)
            def _(size=size, off=off):
                v = vmem_ref.at[pl.ds(pl.multiple_of(lstart + off, SUBLANES), size)]
                h = hbm_ref.at[pl.ds(pl.multiple_of(start + off, SUBLANES), size)]
                cp = pltpu.make_async_copy(v, h, sem) if to_hbm else pltpu.make_async_copy(h, v, sem)
                if wait:
                    cp.wait()
                else:
                    cp.start()

            off = off + (cnt & size)
            size //= 2


def _compact_rows(tb):
    return 2 * tb + N_EXPERTS * SUBLANES


def _selection(loc_ref, rows, tb):
    r = lax.broadcasted_iota(I32, (rows, tb), 0).astype(F32)
    return r == loc_ref[0, 0:1, :], r == loc_ref[0, 1:2, :]


def _zero_fill(zm_ref, zbuf_ref, xs_ref, zsem, *, tm, n_tiles, wait):
    def fill(row0, size):
        cp = pltpu.make_async_copy(zbuf_ref.at[pl.ds(0, size)],
                                   xs_ref.at[pl.ds(pl.multiple_of(row0, SUBLANES), size)], zsem)
        if wait:
            cp.wait()
        else:
            cp.start()

    for e in range(N_EXPERTS):
        end, pad = zm_ref[1 + 2 * e], zm_ref[2 + 2 * e]
        off = 0
        size = tm // 2
        while size >= SUBLANES:
            @pl.when((pad & size) != 0)
            def _(size=size, off=off):
                fill(end + off, size)

            off = off + (pad & size)
            size //= 2
    for t in range(n_tiles):
        @pl.when(t >= zm_ref[0])
        def _(t=t):
            fill(t * tm, tm)


def _moe_scatter_kernel(bm_ref, zm_ref, a_ref, loc_ref, gate_ref, xs_ref,
                        buf_ref, zbuf_ref, sem, zsem, *, tb, tm, n_tiles):
    blk = pl.program_id(0)
    last = pl.num_programs(0) - 1
    slot = blk % 2
    d = a_ref.shape[1]

    @pl.when(blk == 0)
    def _():
        zbuf_ref[...] = jnp.zeros_like(zbuf_ref)
        _zero_fill(zm_ref, zbuf_ref, xs_ref, zsem, tm=tm, n_tiles=n_tiles, wait=False)

    def copies(block, wait):
        s = block % 2
        _run_copies(bm_ref, block, buf_ref.at[s], xs_ref, sem.at[s], tb=tb, to_hbm=True, wait=wait)

    @pl.when(blk >= 2)
    def _():
        copies(blk - 2, True)

    first, second = _selection(loc_ref, _compact_rows(tb), tb)
    pick = jnp.where(first, 1.0, jnp.where(second, 1.0, 0.0)).astype(BF16)
    buf_ref[slot, :, 0:d] = jnp.dot(pick, a_ref[...], preferred_element_type=F32)
    gsel = jnp.where(first, gate_ref[0, 0:1, :], jnp.where(second, gate_ref[0, 1:2, :], 0.0))
    buf_ref[slot, :, d:d + LANES] = jnp.broadcast_to(jnp.sum(gsel, axis=1, keepdims=True),
                                                    (buf_ref.shape[1], LANES))
    copies(blk, False)

    @pl.when(blk == last)
    def _():
        @pl.when(blk >= 1)
        def _():
            copies(blk - 1, True)

        copies(blk, True)
        _zero_fill(zm_ref, zbuf_ref, xs_ref, zsem, tm=tm, n_tiles=n_tiles, wait=True)


def _moe_scatter(a, loc, gate, bmeta, zmeta, n_slots, tb, tm):
    m, d = a.shape
    width = d + LANES
    tok_spec = pl.BlockSpec((1, N_EXPERTS, tb), lambda i, bm, zm: (i, 0, 0))
    grid_spec = pltpu.PrefetchScalarGridSpec(
        num_scalar_prefetch=2,
        grid=(m // tb,),
        in_specs=[pl.BlockSpec((tb, d), lambda i, bm, zm: (i, 0)), tok_spec, tok_spec],
        out_specs=pl.BlockSpec(memory_space=pl.ANY),
        scratch_shapes=[pltpu.VMEM((2, _compact_rows(tb), width), F32), pltpu.VMEM((tm, width), F32),
                        pltpu.SemaphoreType.DMA((2,)), pltpu.SemaphoreType.DMA(())],
    )
    return pl.pallas_call(
        functools.partial(_moe_scatter_kernel, tb=tb, tm=tm, n_tiles=n_slots // tm),
        grid_spec=grid_spec,
        out_shape=jax.ShapeDtypeStruct((n_slots, width), F32),
        compiler_params=_params(("arbitrary",), 16 * tb * d * 4),
        name="moe_scatter",
    )(bmeta, zmeta, a, loc, gate)


def _new_weights(te_ref):
    i = pl.program_id(1)
    return (i == 0) | (te_ref[i] != te_ref[jnp.maximum(i - 1, 0)])


def _moe_gu_kernel(te_ref, nu_ref, x_ref, wg_ref, wu_ref, o_ref, wgb_ref, wub_ref):
    i = pl.program_id(1)

    @pl.when(_new_weights(te_ref))
    def _():
        wgb_ref[...] = wg_ref[0].astype(BF16)
        wub_ref[...] = wu_ref[0].astype(BF16)

    @pl.when(i < nu_ref[0])
    def _():
        x = x_ref[...].astype(BF16)
        for c in range(o_ref.shape[1] // MXU_COLS):
            cols = slice(c * MXU_COLS, (c + 1) * MXU_COLS)
            g = jnp.dot(x, wgb_ref[:, cols], preferred_element_type=F32)
            u = jnp.dot(x, wub_ref[:, cols], preferred_element_type=F32)
            o_ref[:, cols] = (_silu(g) * u).astype(o_ref.dtype)

    @pl.when(i >= nu_ref[0])
    def _():
        o_ref[...] = jnp.zeros_like(o_ref)


def _used_tile(i, nu):
    return jnp.minimum(i, nu[0] - 1)


def _moe_gu(xs, w_gu, te, nu, tm, tn):
    s = xs.shape[0]
    k = w_gu.shape[1]
    f = D_FF_EXPERT
    nbytes = 2 * (2 * k * tn * 4 + k * tn * 2) + 2 * tm * k * 4 + 2 * tm * tn * 2 + 4 * tm * MXU_COLS * 4
    grid_spec = pltpu.PrefetchScalarGridSpec(
        num_scalar_prefetch=2,
        grid=(f // tn, s // tm),
        in_specs=[pl.BlockSpec((tm, k), lambda j, i, te, nu: (_used_tile(i, nu), 0)),
                  pl.BlockSpec((1, k, tn), lambda j, i, te, nu: (te[i], 0, j)),
                  pl.BlockSpec((1, k, tn), lambda j, i, te, nu: (te[i], 0, j + f // tn))],
        out_specs=pl.BlockSpec((tm, tn), lambda j, i, te, nu: (i, j)),
        scratch_shapes=[pltpu.VMEM((k, tn), BF16), pltpu.VMEM((k, tn), BF16)],
    )
    return pl.pallas_call(
        _moe_gu_kernel,
        grid_spec=grid_spec,
        out_shape=jax.ShapeDtypeStruct((s, f), BF16),
        compiler_params=_params(("arbitrary", "arbitrary"), nbytes),
        name="moe_gate_up",
    )(te, nu, xs, w_gu, w_gu)


def _moe_down_kernel(te_ref, nu_ref, x_ref, w_ref, gs_ref, o_ref, wb_ref):
    i = pl.program_id(1)

    @pl.when(_new_weights(te_ref))
    def _():
        wb_ref[...] = w_ref[0].astype(BF16)

    @pl.when(i < nu_ref[0])
    def _():
        o_ref[...] = gs_ref[:, 0:1] * jnp.dot(x_ref[...], wb_ref[...], preferred_element_type=F32)

    @pl.when(i >= nu_ref[0])
    def _():
        o_ref[...] = jnp.zeros_like(o_ref)


def _moe_down(act, w_down, xs, te, nu, tm, tn):
    s, k = act.shape
    d = w_down.shape[2]
    gate_block = d // LANES
    nbytes = 2 * k * tn * 4 + k * tn * 2 + 2 * tm * k * 2 + 3 * tm * tn * 4
    grid_spec = pltpu.PrefetchScalarGridSpec(
        num_scalar_prefetch=2,
        grid=(d // tn, s // tm),
        in_specs=[pl.BlockSpec((tm, k), lambda j, i, te, nu: (_used_tile(i, nu), 0)),
                  pl.BlockSpec((1, k, tn), lambda j, i, te, nu: (te[i], 0, j)),
                  pl.BlockSpec((tm, LANES), lambda j, i, te, nu: (_used_tile(i, nu), gate_block))],
        out_specs=pl.BlockSpec((tm, tn), lambda j, i, te, nu: (i, j)),
        scratch_shapes=[pltpu.VMEM((k, tn), BF16)],
    )
    return pl.pallas_call(
        _moe_down_kernel,
        grid_spec=grid_spec,
        out_shape=jax.ShapeDtypeStruct((s, d), F32),
        compiler_params=_params(("arbitrary", "arbitrary"), nbytes),
        name="moe_down",
    )(te, nu, act, w_down, xs)


def _moe_combine_kernel(bm_ref, h_ref, locc_ref, g_ref, ys_ref, op_ref, os_ref, ybuf_ref, sem, *, tb, prompt_blocks):
    blk = pl.program_id(0)
    slot = blk % 2

    def copies(block, wait):
        s = block % 2
        _run_copies(bm_ref, block, ybuf_ref.at[s], ys_ref, sem.at[s], tb=tb, to_hbm=False, wait=wait)

    @pl.when(blk == 0)
    def _():
        ybuf_ref[...] = jnp.zeros_like(ybuf_ref)
        copies(blk, False)

    @pl.when(blk + 1 < pl.num_programs(0))
    def _():
        copies(blk + 1, False)

    r = lax.broadcasted_iota(I32, (tb, _compact_rows(tb)), 1).astype(F32)
    pick = jnp.where(r == locc_ref[:, 0:1], 1.0, jnp.where(r == locc_ref[:, 1:2], 1.0, 0.0)).astype(BF16)
    copies(blk, True)
    moe = jnp.dot(pick, ybuf_ref[slot].astype(BF16), preferred_element_type=F32)
    out = _rms(h_ref[...] + moe, g_ref[...])

    @pl.when(blk < prompt_blocks)
    def _():
        op_ref[...] = out

    @pl.when(blk == prompt_blocks)
    def _():
        os_ref[...] = out


def _moe_combine(h, locc, bmeta, ys, g, tb, n_prompt):
    m, d = h.shape
    prompt_blocks = n_prompt // tb
    any_spec = pl.BlockSpec(memory_space=pl.ANY)
    grid_spec = pltpu.PrefetchScalarGridSpec(
        num_scalar_prefetch=1,
        grid=(m // tb,),
        in_specs=[pl.BlockSpec((tb, d), lambda i, bm: (i, 0)),
                  pl.BlockSpec((tb, N_EXPERTS), lambda i, bm: (i, 0)),
                  pl.BlockSpec((1, d), lambda i, bm: (0, 0)),
                  any_spec],
        out_specs=[pl.BlockSpec((tb, d), lambda i, bm: (jnp.minimum(i, prompt_blocks - 1), 0)),
                   pl.BlockSpec((tb, d), lambda i, bm: (0, 0))],
        scratch_shapes=[pltpu.VMEM((2, _compact_rows(tb), d), F32), pltpu.SemaphoreType.DMA((2,))],
    )
    return pl.pallas_call(
        functools.partial(_moe_combine_kernel, tb=tb, prompt_blocks=prompt_blocks),
        grid_spec=grid_spec,
        out_shape=[jax.ShapeDtypeStruct((n_prompt, d), F32), jax.ShapeDtypeStruct((tb, d), F32)],
        compiler_params=_params(("arbitrary",), 16 * tb * d * 4),
        name="moe_combine",
    )(bmeta, h, locc, g.reshape(1, d), ys)


MOE_TOKEN_BLOCK = 256
MOE_ROW_TILE = 512


def _moe(h, a, lg_parts, n_valid, n_prompt, w_gu, w_down, g_final):
    m, d = h.shape
    tb, tm = MOE_TOKEN_BLOCK, MOE_ROW_TILE
    nb = m // tb
    lg3 = lg_parts[:, :2 * N_EXPERTS].reshape(nb, tb, 2 * N_EXPERTS).transpose(0, 2, 1)
    n_tiles = pl.cdiv(2 * n_valid + nb * N_EXPERTS * (SUBLANES - 1), tm) + N_EXPERTS
    loc, gate, bmeta, tmeta = _route(lg3, tm, n_valid)
    te = tmeta[0, :n_tiles]
    nu = tmeta[1, 0:1]
    bm = bmeta[:, :, :3].reshape(-1)
    zm = jnp.concatenate([nu, tmeta[2:4, :N_EXPERTS].T.reshape(-1)])
    locc = loc.transpose(0, 2, 1).reshape(m, N_EXPERTS)
    xs = _moe_scatter(a, loc, gate, bm, zm, n_tiles * tm, tb, tm)
    act = _moe_gu(xs, w_gu, te, nu, tm, D_FF_EXPERT // 2)
    ys = _moe_down(act, w_down, xs, te, nu, tm, d)
    return _moe_combine(h, locc, bm, ys, g_final, tb, n_prompt)


def _pad_lanes(v):
    return jnp.pad(v.reshape(1, -1), ((0, 0), (0, LANES - v.shape[-1])))


def kernel(x_prompt, x_sample, state_ssm, state_conv, norm_mix, norm_ffn, norm_final, ssd_w_in, ssd_conv_w,
           ssd_conv_b, ssd_dt_bias, ssd_a_log, ssd_d, ssd_gnorm, ssd_w_out, sgu_w_in, sgu_b_in, sgu_ln_g,
           sgu_ln_b, sgu_w_s, sgu_b_s, sgu_w_out, ffn_w_gu, ffn_w_down, moe_w_router, moe_w_gu, moe_w_down):
    batch, seq, d = x_prompt.shape
    n_s = x_sample.shape[0]
    mp = batch * seq
    xp = x_prompt.reshape(mp, d)
    xs = x_sample.reshape(n_s, d)

    w_in = ssd_w_in[0].T
    w_dt = jnp.pad(w_in[D_INNER + CONV_DIM:], ((0, LANES - SSD_HEADS), (0, 0)))
    dtb = _pad_lanes(ssd_dt_bias[0])
    alog = _pad_lanes(ssd_a_log[0])
    d_rep = jnp.repeat(ssd_d[0], SSD_HEADDIM).reshape(1, D_INNER)
    gn_w = ssd_gnorm[0].reshape(1, D_INNER)
    conv_w = ssd_conv_w[0]
    conv_b = ssd_conv_b[0].reshape(1, CONV_DIM)
    ln_g = sgu_ln_g[0].reshape(1, D_SGU)
    ln_b = sgu_ln_b[0].reshape(1, D_SGU)
    bs_t = jnp.pad(sgu_b_s[0].T, ((0, 0), (0, LANES - SGU_HEADS)))
    w0_rep = jnp.repeat(sgu_w_s[0][:, 0, 0], SGU_HEAD_DIM).reshape(1, D_SGU)
    b0_rep = jnp.repeat(sgu_b_s[0][:, 0], SGU_HEAD_DIM).reshape(1, D_SGU)
    wr_hi = moe_w_router[0].astype(BF16)
    wr_lo = (moe_w_router[0] - wr_hi.astype(F32)).astype(BF16)
    w3 = jnp.concatenate([wr_hi, wr_lo, jnp.zeros((d, LANES - 2 * N_EXPERTS), BF16)], axis=1)

    def dense_ffn(h1, a1, tm):
        act = _mm_swiglu(a1, ffn_w_gu[0], D_FF_DENSE, min(2 * tm, a1.shape[0]), 1408, "ffn_gate_up")
        return _mm_resnorm(act, ffn_w_down[0], h1, norm_mix[1], tm, "ffn_down")

    a0p, zp, dtrp = _norm_in_proj(xp, norm_mix[0], w_in, w_dt, 1024)
    xbcp = _mm_plain(a0p, w_in, D_INNER, CONV_DIM, 1024, 1024, F32, "ssd_in_xbc")
    per_seq = lambda t: t.reshape(batch, seq, t.shape[-1])
    ygp, ssm_p, conv_p = _ssd_prompt(per_seq(xbcp), per_seq(zp), per_seq(dtrp), conv_w, conv_b, dtb, alog, d_rep,
                                     gn_w)
    h1p, a1p = _mm_resnorm(ygp.reshape(mp, D_INNER), ssd_w_out[0], xp, norm_ffn[0], 512, "ssd_out")
    h2p, a2p = dense_ffn(h1p, a1p, 512)

    a0s, zs, dtrs = _norm_in_proj(xs, norm_mix[0], w_in, w_dt, n_s)
    xbcs = _mm_plain(a0s, w_in, D_INNER, CONV_DIM, n_s, 1024, F32, "ssd_in_xbc")
    conv_s, xas, xdts, das, bsts, csts = _ssd_sample_pre(xbcs, state_conv[0].reshape(n_s, 3 * CONV_DIM), dtrs,
                                                         conv_w, conv_b, dtb, alog)
    stack_rows = SSD_GROUPS * TERM_ROWS
    ssm_s, ys3 = _ssd_sample_state(state_ssm[0].reshape(n_s, D_INNER, D_STATE), xdts.reshape(n_s, 1, D_INNER),
                                   das[:, :SSD_HEADS], bsts.reshape(n_s, stack_rows, D_STATE),
                                   csts.reshape(n_s, stack_rows, D_STATE))
    h1s, a1s = _ssd_out(ys3.reshape(n_s, D_INNER), xas, zs, d_rep, gn_w, ssd_w_out[0], xs, norm_ffn[0], n_s)
    h2s, a2s = dense_ffn(h1s, a1s, n_s)

    uvp = _mm_bias_gelu(a2p, sgu_w_in[0], sgu_b_in[0], 1024, 2048, BF16, "sgu_in")
    uvs = _mm_bias_gelu(a2s, sgu_w_in[0], sgu_b_in[0], n_s, 1024, F32, "sgu_in")
    ss, v_s = _sgu_sample(uvs, ln_g, ln_b, w0_rep, b0_rep)
    tm_out = 512
    pad_rows = ((0, tm_out - n_s), (0, 0))
    h3, a3, lg = _sgu_out(uvp, jnp.pad(ss, pad_rows), ln_g, ln_b, sgu_w_s[0], bs_t, sgu_w_out[0], h2p,
                          jnp.pad(h2s, pad_rows), norm_ffn[1], w3, tm_out)

    y_prompt, y_tail = _moe(h3, a3, lg, mp + n_s, mp, moe_w_gu[0], moe_w_down[0], norm_final)
    y_sample = y_tail[:n_s]

    return (y_prompt.reshape(batch, seq, d),
            y_sample.reshape(n_s, 1, d),
            ssm_p[None],
            conv_p[None],
            ssm_s.reshape(1, n_s, SSD_HEADS, SSD_HEADDIM, D_STATE),
            conv_s.reshape(1, n_s, CONV_W - 1, CONV_DIM),
            v_s.reshape(1, n_s, 1, D_SGU))
```

```python
import functools

import jax
import jax.numpy as jnp
from jax import lax
from jax.experimental import pallas as pl
from jax.experimental.pallas import tpu as pltpu

F32 = jnp.float32
BF16 = jnp.bfloat16
I32 = jnp.int32
HIGHEST = lax.Precision.HIGHEST

D_MODEL = 1024
D_INNER = 2048
SSD_HEADS = 32
SSD_HEADDIM = 64
SSD_GROUPS = 4
D_STATE = 128
CONV_W = 4
CONV_DIM = D_INNER + 2 * SSD_GROUPS * D_STATE
CHUNK = 128
D_SGU = 2048
SGU_HEADS = 8
SGU_HEAD_DIM = D_SGU // SGU_HEADS
D_FF_DENSE = 2816
N_EXPERTS = 8
D_FF_EXPERT = 3584
EPS = 1e-5

MXU_COLS = 256
ROW_SUB = 256
LANES = 128
SUBLANES = 8
VMEM_CAP = 60000 * 1024
VMEM_FLOOR = 32 * 1024 * 1024


def _vmem_limit(nbytes):
    return int(min(max(nbytes * 5 // 4 + (4 << 20), VMEM_FLOOR), VMEM_CAP))


def _params(sem, nbytes):
    return pltpu.CompilerParams(dimension_semantics=sem, vmem_limit_bytes=_vmem_limit(nbytes))


LOG2E = 1.4426950408889634


def _silu(x):
    return x / (1.0 + jnp.exp2(x * (-LOG2E)))


def _rms(x, g):
    return x * lax.rsqrt(jnp.mean(x * x, axis=-1, keepdims=True) + EPS) * g


def _bdot(a, b):
    return jnp.dot(a.astype(BF16), b.astype(BF16), preferred_element_type=F32)


def _cast_transposed(wt_ref, wb_ref):
    step = 2 * MXU_COLS
    for r0 in range(0, wt_ref.shape[0], step):
        r1 = min(r0 + step, wt_ref.shape[0])
        wb_ref[:, r0:r1] = wt_ref[r0:r1, :].T.astype(BF16)


def _mm_plain_kernel(x_ref, wt_ref, o_ref, wb_ref):
    @pl.when(pl.program_id(1) == 0)
    def _():
        _cast_transposed(wt_ref, wb_ref)

    o_ref[...] = _bdot(x_ref[...], wb_ref[...]).astype(o_ref.dtype)


def _mm_bias_gelu_kernel(x_ref, w_ref, b_ref, o_ref, wb_ref):
    @pl.when(pl.program_id(1) == 0)
    def _():
        wb_ref[...] = w_ref[...].astype(BF16)

    x = x_ref[...].astype(BF16)
    for c0 in range(0, o_ref.shape[1], 2 * MXU_COLS):
        cols = slice(c0, c0 + 2 * MXU_COLS)
        y = jnp.dot(x, wb_ref[:, cols], preferred_element_type=F32) + b_ref[:, cols]
        o_ref[:, cols] = (0.5 * y * (1.0 + lax.erf(y * (2.0 ** -0.5)))).astype(o_ref.dtype)


def _mm_swiglu_kernel(x_ref, wg_ref, wu_ref, o_ref, wgb_ref, wub_ref):
    @pl.when(pl.program_id(1) == 0)
    def _():
        wgb_ref[...] = wg_ref[...].astype(BF16)
        wub_ref[...] = wu_ref[...].astype(BF16)

    x = x_ref[...].astype(BF16)
    tn = o_ref.shape[1]
    for c0 in range(0, tn, MXU_COLS):
        cols = slice(c0, min(c0 + MXU_COLS, tn))
        g = jnp.dot(x, wgb_ref[:, cols], preferred_element_type=F32)
        u = jnp.dot(x, wub_ref[:, cols], preferred_element_type=F32)
        o_ref[:, cols] = (_silu(g) * u).astype(o_ref.dtype)


def _wspec(k, tn, col_block0):
    return pl.BlockSpec((k, tn), lambda j, i: (0, j + col_block0), pipeline_mode=pl.Buffered(1))


def _mm_plain(x, wt, col0, n, tm, tn, out_dtype, name):
    m, k = x.shape
    nbytes = k * tn * 6 + 2 * tm * k * x.dtype.itemsize + 2 * tm * tn * 4 + 2 * MXU_COLS * k * 8
    row_block0 = col0 // tn
    return pl.pallas_call(
        _mm_plain_kernel,
        grid=(n // tn, m // tm),
        in_specs=[pl.BlockSpec((tm, k), lambda j, i: (i, 0)),
                  pl.BlockSpec((tn, k), lambda j, i: (j + row_block0, 0), pipeline_mode=pl.Buffered(1))],
        out_specs=pl.BlockSpec((tm, tn), lambda j, i: (i, j)),
        out_shape=jax.ShapeDtypeStruct((m, n), out_dtype),
        scratch_shapes=[pltpu.VMEM((k, tn), BF16)],
        compiler_params=_params(("arbitrary", "arbitrary"), nbytes),
        name=name,
    )(x, wt)


def _norm_in_proj_kernel(x_ref, g_ref, wz_ref, wdt_ref, a_ref, z_ref, dtr_ref, wzb_ref, wdtb_ref):
    @pl.when(pl.program_id(0) == 0)
    def _():
        _cast_transposed(wz_ref, wzb_ref)
        _cast_transposed(wdt_ref, wdtb_ref)

    a = _rms(x_ref[...], g_ref[...]).astype(BF16)
    a_ref[...] = a
    z_ref[...] = jnp.dot(a, wzb_ref[...], preferred_element_type=F32).astype(z_ref.dtype)
    dtr_ref[...] = jnp.dot(a, wdtb_ref[...], preferred_element_type=F32)


def _norm_in_proj(x, g, w_in_t, w_dt_t, tm):
    m, k = x.shape
    n = D_INNER
    row = lambda i: (i, 0)
    fixed = lambda i: (0, 0)
    nbytes = k * n * 6 + k * LANES * 6 + 2 * tm * k * 4 + 2 * tm * k * 2 + 2 * tm * n * 2 + 2 * tm * n * 4
    return pl.pallas_call(
        _norm_in_proj_kernel,
        grid=(m // tm,),
        in_specs=[pl.BlockSpec((tm, k), row), pl.BlockSpec((1, k), fixed),
                  pl.BlockSpec((n, k), fixed, pipeline_mode=pl.Buffered(1)),
                  pl.BlockSpec((LANES, k), fixed)],
        out_specs=[pl.BlockSpec((tm, k), row), pl.BlockSpec((tm, n), row), pl.BlockSpec((tm, LANES), row)],
        out_shape=[jax.ShapeDtypeStruct((m, k), BF16), jax.ShapeDtypeStruct((m, n), BF16),
                   jax.ShapeDtypeStruct((m, LANES), F32)],
        scratch_shapes=[pltpu.VMEM((k, n), BF16), pltpu.VMEM((k, LANES), BF16)],
        compiler_params=_params(("arbitrary",), nbytes),
        name="norm_in_proj",
    )(x, g.reshape(1, k), w_in_t, w_dt_t)


def _mm_bias_gelu(x, w, b, tm, tn, out_dtype, name):
    m, k = x.shape
    n = w.shape[1]
    nbytes = k * tn * 6 + 2 * tm * k * x.dtype.itemsize + 2 * tm * tn * 4
    return pl.pallas_call(
        _mm_bias_gelu_kernel,
        grid=(n // tn, m // tm),
        in_specs=[pl.BlockSpec((tm, k), lambda j, i: (i, 0)), _wspec(k, tn, 0),
                  pl.BlockSpec((1, tn), lambda j, i: (0, j))],
        out_specs=pl.BlockSpec((tm, tn), lambda j, i: (i, j)),
        out_shape=jax.ShapeDtypeStruct((m, n), out_dtype),
        scratch_shapes=[pltpu.VMEM((k, tn), BF16)],
        compiler_params=_params(("arbitrary", "arbitrary"), nbytes),
        name=name,
    )(x, w, b.reshape(1, n))


def _mm_swiglu(x, w_gu, d_ff, tm, tn, name):
    m, k = x.shape
    nbytes = 2 * k * tn * 6 + 2 * tm * k * x.dtype.itemsize + 2 * tm * tn * 2 + 3 * tm * tn * 4
    return pl.pallas_call(
        _mm_swiglu_kernel,
        grid=(d_ff // tn, m // tm),
        in_specs=[pl.BlockSpec((tm, k), lambda j, i: (i, 0)), _wspec(k, tn, 0), _wspec(k, tn, d_ff // tn)],
        out_specs=pl.BlockSpec((tm, tn), lambda j, i: (i, j)),
        out_shape=jax.ShapeDtypeStruct((m, d_ff), BF16),
        scratch_shapes=[pltpu.VMEM((k, tn), BF16), pltpu.VMEM((k, tn), BF16)],
        compiler_params=_params(("arbitrary", "arbitrary"), nbytes),
        name=name,
    )(x, w_gu, w_gu)


def _mm_resnorm_kernel(x_ref, w_ref, r_ref, g_ref, h_ref, a_ref, wb_ref):
    @pl.when(pl.program_id(0) == 0)
    def _():
        wb_ref[...] = w_ref[...].astype(BF16)

    h = r_ref[...] + _bdot(x_ref[...], wb_ref[...])
    h_ref[...] = h
    a_ref[...] = _rms(h, g_ref[...]).astype(a_ref.dtype)


def _mm_resnorm(x, w, resid, g, tm, name):
    m, k = x.shape
    d = w.shape[1]
    nbytes = k * d * 6 + 2 * tm * k * 2 + 8 * tm * d * 4
    return pl.pallas_call(
        _mm_resnorm_kernel,
        grid=(m // tm,),
        in_specs=[pl.BlockSpec((tm, k), lambda i: (i, 0)),
                  pl.BlockSpec((k, d), lambda i: (0, 0), pipeline_mode=pl.Buffered(1)),
                  pl.BlockSpec((tm, d), lambda i: (i, 0)),
                  pl.BlockSpec((1, d), lambda i: (0, 0))],
        out_specs=[pl.BlockSpec((tm, d), lambda i: (i, 0)), pl.BlockSpec((tm, d), lambda i: (i, 0))],
        out_shape=[jax.ShapeDtypeStruct((m, d), F32), jax.ShapeDtypeStruct((m, d), BF16)],
        scratch_shapes=[pltpu.VMEM((k, d), BF16)],
        compiler_params=_params(("arbitrary",), nbytes),
        name=name,
    )(x, w, resid, g.reshape(1, d))


def _sgu_out_kernel(u_ref, v_ref, xt_ref, lng_ref, lnb_ref, ws_ref, bst_ref, w_ref, rp_ref, rt_ref, g_ref, w3_ref,
                    h_ref, a_ref, lg_ref, wb_ref, wsb_ref, *, prompt_tiles):
    i = pl.program_id(0)
    q = CHUNK

    @pl.when(i == 0)
    def _():
        wb_ref[...] = w_ref[...].astype(BF16)
        causal = lax.broadcasted_iota(I32, (q, q), 0) >= lax.broadcasted_iota(I32, (q, q), 1)
        for g in range(SGU_HEADS):
            wsb_ref[g] = jnp.where(causal, ws_ref[g], 0.0).astype(BF16)

    def project(x, r_ref, rows):
        h = r_ref[rows, :] + jnp.dot(x, wb_ref[...], preferred_element_type=F32)
        h_ref[rows, :] = h
        a = _rms(h, g_ref[...])
        a_hi = a.astype(BF16)
        a_lo = (a - a_hi.astype(F32)).astype(BF16)
        a_ref[rows, :] = a_hi
        lg_ref[rows, :] = (jnp.dot(a_hi, w3_ref[...], preferred_element_type=F32)
                           + jnp.dot(a_lo, w3_ref[...], preferred_element_type=F32))

    chunks = [slice(c * q, (c + 1) * q) for c in range(u_ref.shape[0] // q)]

    @pl.when(i < prompt_tiles)
    def _():
        for rows in chunks:
            vn = _layernorm(v_ref[rows, :].astype(F32), lng_ref[...], lnb_ref[...]).astype(BF16)
            gated = []
            for g in range(SGU_HEADS):
                cols = slice(g * SGU_HEAD_DIM, (g + 1) * SGU_HEAD_DIM)
                s = jnp.dot(wsb_ref[g], vn[:, cols], preferred_element_type=F32) + bst_ref[:, g:g + 1]
                gated.append((u_ref[rows, cols].astype(F32) * s).astype(BF16))
            project(jnp.concatenate(gated, axis=1), rp_ref, rows)

    @pl.when(i == prompt_tiles)
    def _():
        for rows in chunks:
            project(xt_ref[rows, :], rt_ref, rows)


def _sgu_out(uv, xt, ln_g, ln_b, w_s, bs_t, w, rp, rt, g, w3, tm):
    mp = uv.shape[0]
    k, d = w.shape
    prompt_tiles = mp // tm
    last = prompt_tiles - 1
    rows = mp + tm
    prow = lambda i: (jnp.minimum(i, last), 0)
    fixed = lambda i: (0, 0)
    nbytes = k * d * 6 + 8 * tm * k * 2 + 12 * tm * d * 4 + 8 * CHUNK * k * 4
    return pl.pallas_call(
        functools.partial(_sgu_out_kernel, prompt_tiles=prompt_tiles),
        grid=(prompt_tiles + 1,),
        in_specs=[pl.BlockSpec((tm, k), prow),
                  pl.BlockSpec((tm, k), lambda i: (jnp.minimum(i, last), 1)),
                  pl.BlockSpec((tm, k), fixed),
                  pl.BlockSpec((1, k), fixed), pl.BlockSpec((1, k), fixed),
                  pl.BlockSpec((SGU_HEADS, CHUNK, CHUNK), lambda i: (0, 0, 0)),
                  pl.BlockSpec((CHUNK, LANES), fixed),
                  pl.BlockSpec((k, d), fixed, pipeline_mode=pl.Buffered(1)),
                  pl.BlockSpec((tm, d), prow),
                  pl.BlockSpec((tm, d), fixed),
                  pl.BlockSpec((1, d), fixed),
                  pl.BlockSpec((d, LANES), fixed)],
        out_specs=[pl.BlockSpec((tm, d), lambda i: (i, 0)), pl.BlockSpec((tm, d), lambda i: (i, 0)),
                   pl.BlockSpec((tm, LANES), lambda i: (i, 0))],
        out_shape=[jax.ShapeDtypeStruct((rows, d), F32), jax.ShapeDtypeStruct((rows, d), BF16),
                   jax.ShapeDtypeStruct((rows, LANES), F32)],
        scratch_shapes=[pltpu.VMEM((k, d), BF16), pltpu.VMEM((SGU_HEADS, CHUNK, CHUNK), BF16)],
        compiler_params=_params(("arbitrary",), nbytes),
        name="sgu_out",
    )(uv, uv, xt, ln_g, ln_b, w_s, bs_t, w, rp, rt, g.reshape(1, d), w3)


def _gate_and_groupnorm(y, xs, z, d_rep, gn_w):
    yg = (y + d_rep * xs) * _silu(z)
    gw = D_INNER // SSD_GROUPS
    outs = []
    for g in range(SSD_GROUPS):
        blk = yg[:, g * gw:(g + 1) * gw]
        outs.append(blk * lax.rsqrt(jnp.mean(blk * blk, axis=-1, keepdims=True) + EPS))
    return jnp.concatenate(outs, axis=1) * gn_w


def _softplus(x):
    return jnp.maximum(x, 0.0) + jnp.log(1.0 + jnp.exp(-jnp.abs(x)))


SSD_SEQS = 2


def _ssd_prompt_kernel(xbc_ref, z_ref, dtr_ref, cw_ref, cb_ref, dtb_ref, alog_ref, drep_ref, gnw_ref,
                       yg_ref, ssm_ref, conv_ref, ext_ref, st_ref, y_ref, xa_ref):
    for n in range(SSD_SEQS):
        _ssd_chunk(xbc_ref.at[n], z_ref.at[n], dtr_ref.at[n], cw_ref, cb_ref, dtb_ref, alog_ref, drep_ref, gnw_ref,
                   yg_ref.at[n], ssm_ref.at[n], conv_ref.at[n], ext_ref.at[n], st_ref.at[n], y_ref.at[n],
                   xa_ref.at[n])


def _ssd_chunk(xbc_ref, z_ref, dtr_ref, cw_ref, cb_ref, dtb_ref, alog_ref, drep_ref, gnw_ref,
               yg_ref, ssm_ref, conv_ref, ext_ref, st_ref, y_ref, xa_ref):
    c = pl.program_id(1)
    nc = pl.num_programs(1)
    q = CHUNK

    @pl.when(c == 0)
    def _():
        ext_ref[0:SUBLANES, :] = jnp.zeros((SUBLANES, CONV_DIM), F32)
        st_ref[...] = jnp.zeros_like(st_ref)

    ext_ref[SUBLANES:SUBLANES + q, :] = xbc_ref[...]
    acc = cb_ref[...] + cw_ref[CONV_W - 1:CONV_W, :] * ext_ref[SUBLANES:SUBLANES + q, :]
    for k in range(CONV_W - 1):
        lo = SUBLANES - (CONV_W - 1) + k
        acc = acc + cw_ref[k:k + 1, :] * ext_ref[lo:lo + q, :]
    xa_ref[...] = _silu(acc)
    tail = ext_ref[q:q + SUBLANES, :]
    ext_ref[0:SUBLANES, :] = tail

    dt = _softplus(dtr_ref[...] + dtb_ref[...])
    a_neg = -jnp.exp(alog_ref[...])
    causal = lax.broadcasted_iota(I32, (q, q), 0) >= lax.broadcasted_iota(I32, (q, q), 1)
    tril = jnp.where(causal, 1.0, 0.0)
    cs = jnp.dot(tril, dt * a_neg, precision=HIGHEST, preferred_element_type=F32) * LOG2E
    cs_t = cs.T
    dt_t = dt.T
    rowp = cs_t - jnp.log2(dt_t)
    w_t = dt_t * jnp.exp2(cs_t[:, q - 1:q] - cs_t)
    first = lax.broadcasted_iota(I32, (q, 2 * SSD_HEADDIM), 1) < SSD_HEADDIM

    def per_head(v):
        zero = jnp.zeros_like(v)
        return jnp.concatenate([jnp.where(first, v, zero), jnp.where(first, zero, v)], axis=0)

    for g in range(SSD_GROUPS):
        b_g = xa_ref[:, D_INNER + g * D_STATE:D_INNER + (g + 1) * D_STATE]
        c_g = xa_ref[:, D_INNER + (SSD_GROUPS + g) * D_STATE:D_INNER + (SSD_GROUPS + g + 1) * D_STATE]
        b_gt = b_g.T
        cb = _bdot(c_g, b_gt)
        for pr in range(4):
            hp = g * 4 + pr
            cols = slice(hp * 2 * SSD_HEADDIM, (hp + 1) * 2 * SSD_HEADDIM)
            x2 = per_head(xa_ref[:, cols].astype(BF16))
            st_p = st_ref[:, cols]
            st2 = per_head(st_p.astype(BF16))
            ms, cs_scaled, bws, cds = [], [], [], []
            for s in range(2):
                h = 2 * hp + s
                colb = jnp.broadcast_to(cs[:, h:h + 1], (q, q))
                rowb = jnp.broadcast_to(rowp[h:h + 1, :], (q, q))
                ecol = jnp.exp2(colb)
                ms.append((cb * jnp.where(causal, jnp.exp2(colb - rowb), 0.0)).astype(BF16))
                cs_scaled.append((c_g * ecol).astype(BF16))
                bws.append((b_gt * w_t[h:h + 1, :]).astype(BF16))
                cds.append(ecol[q - 1:q, :])
            y_ref[:, cols] = jnp.dot(jnp.concatenate(ms + cs_scaled, axis=1), jnp.concatenate([x2, st2], axis=0),
                                     preferred_element_type=F32)
            cd = jnp.where(first[0:1, :], cds[0], cds[1])
            st_ref[:, cols] = st_p * cd + jnp.dot(jnp.concatenate(bws, axis=1), x2, preferred_element_type=F32)

    out = _gate_and_groupnorm(y_ref[...], xa_ref[:, :D_INNER], z_ref[...].astype(F32), drep_ref[...],
                              gnw_ref[...])
    yg_ref[...] = out.astype(yg_ref.dtype)

    @pl.when(c == nc - 1)
    def _():
        conv_ref[...] = tail[SUBLANES - (CONV_W - 1):, :]
        for hp in range(SSD_HEADS // 2):
            blk = st_ref[:, hp * 2 * SSD_HEADDIM:(hp + 1) * 2 * SSD_HEADDIM].T
            ssm_ref[2 * hp:2 * hp + 2] = blk.reshape(2, SSD_HEADDIM, D_STATE)


def _ssd_prompt(xbc, z, dtr, conv_w, conv_b, dtb, alog, d_rep, gn_w):
    batch, seq, _ = xbc.shape
    nb = SSD_SEQS
    blk = lambda b, c: (b, c, 0)
    fixed = lambda b, c: (0, 0)
    nbytes = nb * (2 * CHUNK * (CONV_DIM * 4 + D_INNER * 4 + LANES * 4) + (CHUNK + SUBLANES) * CONV_DIM * 4
                   + 4 * D_STATE * D_INNER * 4 + 12 * CHUNK * CONV_DIM * 4)
    return pl.pallas_call(
        _ssd_prompt_kernel,
        grid=(batch // nb, seq // CHUNK),
        in_specs=[pl.BlockSpec((nb, CHUNK, CONV_DIM), blk), pl.BlockSpec((nb, CHUNK, D_INNER), blk),
                  pl.BlockSpec((nb, CHUNK, LANES), blk),
                  pl.BlockSpec((CONV_W, CONV_DIM), fixed), pl.BlockSpec((1, CONV_DIM), fixed),
                  pl.BlockSpec((1, LANES), fixed), pl.BlockSpec((1, LANES), fixed),
                  pl.BlockSpec((1, D_INNER), fixed), pl.BlockSpec((1, D_INNER), fixed)],
        out_specs=[pl.BlockSpec((nb, CHUNK, D_INNER), blk),
                   pl.BlockSpec((nb, SSD_HEADS, SSD_HEADDIM, D_STATE), lambda b, c: (b, 0, 0, 0)),
                   pl.BlockSpec((nb, CONV_W - 1, CONV_DIM), lambda b, c: (b, 0, 0))],
        out_shape=[jax.ShapeDtypeStruct((batch, seq, D_INNER), BF16),
                   jax.ShapeDtypeStruct((batch, SSD_HEADS, SSD_HEADDIM, D_STATE), F32),
                   jax.ShapeDtypeStruct((batch, CONV_W - 1, CONV_DIM), F32)],
        scratch_shapes=[pltpu.VMEM((nb, CHUNK + SUBLANES, CONV_DIM), F32), pltpu.VMEM((nb, D_STATE, D_INNER), F32),
                        pltpu.VMEM((nb, CHUNK, D_INNER), F32), pltpu.VMEM((nb, CHUNK, CONV_DIM), F32)],
        compiler_params=_params(("arbitrary", "arbitrary"), nbytes),
        name="ssd_prompt",
    )(xbc, z, dtr, conv_w, conv_b, dtb, alog, d_rep, gn_w)


def _ssd_out_kernel(y_ref, xs_ref, z_ref, drep_ref, gnw_ref, w_ref, r_ref, g_ref, h_ref, a_ref, wb_ref):
    @pl.when(pl.program_id(0) == 0)
    def _():
        wb_ref[...] = w_ref[...].astype(BF16)

    sub = min(ROW_SUB // 2, y_ref.shape[0])
    for s in range(y_ref.shape[0] // sub):
        rows = slice(s * sub, (s + 1) * sub)
        yg = _gate_and_groupnorm(y_ref[rows, :], xs_ref[rows, :], z_ref[rows, :].astype(F32), drep_ref[...],
                                 gnw_ref[...])
        h = r_ref[rows, :] + _bdot(yg, wb_ref[...])
        h_ref[rows, :] = h
        a_ref[rows, :] = _rms(h, g_ref[...]).astype(a_ref.dtype)


def _ssd_out(y, xa, z, d_rep, gn_w, w, resid, g, tm):
    m, k = y.shape
    d = w.shape[1]
    row = lambda i: (i, 0)
    fixed = lambda i: (0, 0)
    nbytes = k * d * 6 + 2 * tm * k * 10 + 8 * tm * d * 4 + 6 * tm * k * 4
    return pl.pallas_call(
        _ssd_out_kernel,
        grid=(m // tm,),
        in_specs=[pl.BlockSpec((tm, k), row), pl.BlockSpec((tm, k), row), pl.BlockSpec((tm, k), row),
                  pl.BlockSpec((1, k), fixed), pl.BlockSpec((1, k), fixed),
                  pl.BlockSpec((k, d), fixed, pipeline_mode=pl.Buffered(1)),
                  pl.BlockSpec((tm, d), row), pl.BlockSpec((1, d), fixed)],
        out_specs=[pl.BlockSpec((tm, d), row), pl.BlockSpec((tm, d), row)],
        out_shape=[jax.ShapeDtypeStruct((m, d), F32), jax.ShapeDtypeStruct((m, d), BF16)],
        scratch_shapes=[pltpu.VMEM((k, d), BF16)],
        compiler_params=_params(("arbitrary",), nbytes),
        name="ssd_out",
    )(y, xa, z, d_rep, gn_w, w, resid, g.reshape(1, d))


def _ssd_sample_pre_kernel(xbc_ref, conv_ref, dtr_ref, cw_ref, cb_ref, dtb_ref, alog_ref,
                           nconv_ref, xa_ref, xdt_ref, da_ref, bst_ref, cst_ref):
    xn = xbc_ref[...]
    acc = cb_ref[...] + cw_ref[CONV_W - 1:CONV_W, :] * xn
    for k in range(CONV_W - 1):
        acc = acc + cw_ref[k:k + 1, :] * conv_ref[:, k * CONV_DIM:(k + 1) * CONV_DIM]
    xa = _silu(acc)
    xa_ref[...] = xa
    nconv_ref[:, 0:CONV_DIM] = conv_ref[:, CONV_DIM:2 * CONV_DIM]
    nconv_ref[:, CONV_DIM:2 * CONV_DIM] = conv_ref[:, 2 * CONV_DIM:3 * CONV_DIM]
    nconv_ref[:, 2 * CONV_DIM:3 * CONV_DIM] = xn

    dt = _softplus(dtr_ref[...] + dtb_ref[...])
    a_neg = -jnp.exp(alog_ref[...])
    hh = lax.broadcasted_iota(I32, (LANES, D_INNER), 0)
    cc = lax.broadcasted_iota(I32, (LANES, D_INNER), 1)
    expand = jnp.where(lax.shift_right_logical(cc, 6) == hh, 1.0, 0.0)
    dt_rep = jnp.dot(dt, expand, precision=HIGHEST, preferred_element_type=F32)
    xdt_ref[...] = xa[:, :D_INNER] * dt_rep
    da_ref[...] = jnp.exp(dt * a_neg)

    n = xn.shape[0]
    gs = SSD_GROUPS * D_STATE
    b1, b2, b3 = _split3(xa[:, D_INNER:D_INNER + gs])
    cm = xa[:, D_INNER + gs:]
    c1 = cm.astype(BF16)
    c2 = (cm - c1.astype(F32)).astype(BF16)
    zero = jnp.zeros((n, D_STATE), BF16)
    b_rows, c_rows = [], []
    for g in range(SSD_GROUPS):
        sl = slice(g * D_STATE, (g + 1) * D_STATE)
        b_rows += [b1[:, sl], b2[:, sl], b1[:, sl], b3[:, sl], b2[:, sl], b1[:, sl]] + [zero] * (TERM_ROWS - 6)
        c_rows += [c1[:, sl], c2[:, sl]] + [zero] * (TERM_ROWS - 2)
    bst_ref[...] = jnp.concatenate(b_rows, axis=1)
    cst_ref[...] = jnp.concatenate(c_rows, axis=1)


TERM_ROWS = 16


def _split3(v):
    v1 = v.astype(BF16)
    r1 = v - v1.astype(F32)
    v2 = r1.astype(BF16)
    v3 = (r1 - v2.astype(F32)).astype(BF16)
    return v1, v2, v3


def _ssd_sample_pre(xbc, conv_flat, dtr, conv_w, conv_b, dtb, alog):
    n = xbc.shape[0]
    full = lambda shape: pl.BlockSpec(shape, lambda i: (0,) * len(shape))
    stack = SSD_GROUPS * TERM_ROWS * D_STATE
    return pl.pallas_call(
        _ssd_sample_pre_kernel,
        grid=(1,),
        in_specs=[full((n, CONV_DIM)), full((n, 3 * CONV_DIM)), full((n, LANES)), full((CONV_W, CONV_DIM)),
                  full((1, CONV_DIM)), full((1, LANES)), full((1, LANES))],
        out_specs=[full((n, 3 * CONV_DIM)), full((n, CONV_DIM)), full((n, D_INNER)), full((n, LANES)),
                   full((n, stack)), full((n, stack))],
        out_shape=[jax.ShapeDtypeStruct((n, 3 * CONV_DIM), F32), jax.ShapeDtypeStruct((n, CONV_DIM), F32),
                   jax.ShapeDtypeStruct((n, D_INNER), F32), jax.ShapeDtypeStruct((n, LANES), F32),
                   jax.ShapeDtypeStruct((n, stack), BF16), jax.ShapeDtypeStruct((n, stack), BF16)],
        compiler_params=_params(("arbitrary",), 32 * n * CONV_DIM * 4),
        name="ssd_sample_pre",
    )(xbc, conv_flat, dtr, conv_w, conv_b, dtb, alog)


STATE_SAMPLES = 8


def _ssd_sample_state_kernel(da_ref, st_ref, xdt_ref, bst_ref, cst_ref, nst_ref, y_ref):
    first = pl.program_id(0) * STATE_SAMPLES
    rows_g = D_INNER // SSD_GROUPS
    sub = lax.broadcasted_iota(I32, (TERM_ROWS, D_INNER), 0)
    for s in range(STATE_SAMPLES):
        x1, x2, x3 = [t.astype(F32) for t in _split3(xdt_ref[s])]
        terms = jnp.where(sub == 0, x1, jnp.where(sub == 1, x1, jnp.where(sub == 2, x2, jnp.where(
            sub == 3, x1, jnp.where(sub == 4, x2, jnp.where(sub == 5, x3, 0.0)))))).astype(BF16)
        ys = []
        for g in range(SSD_GROUPS):
            tile = slice(g * TERM_ROWS, (g + 1) * TERM_ROWS)
            upd = lax.dot_general(terms[:, g * rows_g:(g + 1) * rows_g], bst_ref[s, tile, :],
                                  (((0,), (0,)), ((), ())), preferred_element_type=F32)
            halves = []
            for hh in range(rows_g // SSD_HEADDIM):
                head = g * (rows_g // SSD_HEADDIM) + hh
                rows = slice(head * SSD_HEADDIM, (head + 1) * SSD_HEADDIM)
                h_new = (da_ref[first + s, head] * st_ref[s, rows, :]
                         + upd[hh * SSD_HEADDIM:(hh + 1) * SSD_HEADDIM, :])
                nst_ref[s, rows, :] = h_new
                halves.append(h_new)
            h_g = jnp.concatenate(halves, axis=0)
            h_hi = h_g.astype(BF16)
            h_lo = (h_g - h_hi.astype(F32)).astype(BF16)
            nt = (((1,), (1,)), ((), ()))
            o_hi = lax.dot_general(cst_ref[s, tile, :], h_hi, nt, preferred_element_type=F32)
            o_lo = lax.dot_general(cst_ref[s, tile, :], h_lo, nt, preferred_element_type=F32)
            ys.append(o_hi[0:1, :] + o_hi[1:2, :] + o_lo[0:1, :])
        y_ref[s] = jnp.concatenate(ys, axis=1)


def _ssd_sample_state(state, xdt, da, bst, cst):
    n = state.shape[0]
    ns = STATE_SAMPLES
    stack_rows = SSD_GROUPS * TERM_ROWS
    return pl.pallas_call(
        _ssd_sample_state_kernel,
        grid=(n // ns,),
        in_specs=[pl.BlockSpec(memory_space=pltpu.SMEM),
                  pl.BlockSpec((ns, D_INNER, D_STATE), lambda i: (i, 0, 0)),
                  pl.BlockSpec((ns, 1, D_INNER), lambda i: (i, 0, 0)),
                  pl.BlockSpec((ns, stack_rows, D_STATE), lambda i: (i, 0, 0)),
                  pl.BlockSpec((ns, stack_rows, D_STATE), lambda i: (i, 0, 0))],
        out_specs=[pl.BlockSpec((ns, D_INNER, D_STATE), lambda i: (i, 0, 0)),
                   pl.BlockSpec((ns, 1, D_INNER), lambda i: (i, 0, 0))],
        out_shape=[jax.ShapeDtypeStruct((n, D_INNER, D_STATE), F32),
                   jax.ShapeDtypeStruct((n, 1, D_INNER), F32)],
        compiler_params=_params(("arbitrary",), 6 * ns * D_INNER * D_STATE * 4),
        name="ssd_sample_state",
    )(da, state, xdt, bst, cst)


def _layernorm(v, g, b):
    mu = jnp.mean(v, axis=-1, keepdims=True)
    d = v - mu
    var = jnp.mean(d * d, axis=-1, keepdims=True)
    return d * lax.rsqrt(var + EPS) * g + b


def _sgu_sample_kernel(u_ref, v_ref, lng_ref, lnb_ref, w0_ref, b0_ref, o_ref, vn_ref):
    vn = _layernorm(v_ref[...], lng_ref[...], lnb_ref[...])
    vn_ref[...] = vn
    o_ref[...] = (u_ref[...] * (w0_ref[...] * vn + b0_ref[...])).astype(o_ref.dtype)


def _sgu_sample(uv, ln_g, ln_b, w0_rep, b0_rep):
    n = uv.shape[0]
    fixed2 = lambda i: (0, 0)
    return pl.pallas_call(
        _sgu_sample_kernel,
        grid=(1,),
        in_specs=[pl.BlockSpec((n, D_SGU), lambda i: (0, 0)), pl.BlockSpec((n, D_SGU), lambda i: (0, 1)),
                  pl.BlockSpec((1, D_SGU), fixed2), pl.BlockSpec((1, D_SGU), fixed2),
                  pl.BlockSpec((1, D_SGU), fixed2), pl.BlockSpec((1, D_SGU), fixed2)],
        out_specs=[pl.BlockSpec((n, D_SGU), fixed2), pl.BlockSpec((n, D_SGU), fixed2)],
        out_shape=[jax.ShapeDtypeStruct((n, D_SGU), BF16), jax.ShapeDtypeStruct((n, D_SGU), F32)],
        compiler_params=_params(("arbitrary",), 12 * n * D_SGU * 4),
        name="sgu_sample",
    )(uv, uv, ln_g, ln_b, w0_rep, b0_rep)


def _experts_to_lanes(col, sub, lane):
    return jnp.sum(jnp.where(sub == lane, col, 0.0), axis=0, keepdims=True)


def _route_kernel(lg_ref, loc_ref, gate_ref, bmeta_ref, tmeta_ref, *, tile, n_valid):
    nb, _, tb = lg_ref.shape
    sub = lax.broadcasted_iota(I32, (N_EXPERTS, LANES), 0)
    lane = lax.broadcasted_iota(I32, (N_EXPERTS, LANES), 1)
    subf = lax.broadcasted_iota(I32, (N_EXPERTS, tb), 0).astype(F32)
    tok = lax.broadcasted_iota(I32, (N_EXPERTS, tb), 1)
    incl = jnp.where(lax.broadcasted_iota(I32, (tb, tb), 0) <= lax.broadcasted_iota(I32, (tb, tb), 1), 1.0, 0.0)
    neg = jnp.float32(-jnp.inf)
    none = jnp.float32(N_EXPERTS)

    def select(k):
        blk = lg_ref[k]
        l = blk[0:N_EXPERTS] + blk[N_EXPERTS:2 * N_EXPERTS]
        m1 = jnp.max(l, axis=0, keepdims=True)
        i1 = jnp.min(jnp.where(l == m1, subf, none), axis=0, keepdims=True)
        l2 = jnp.where(subf == i1, neg, l)
        m2 = jnp.max(l2, axis=0, keepdims=True)
        i2 = jnp.min(jnp.where(l2 == m2, subf, none), axis=0, keepdims=True)
        valid = (k * tb + tok) < n_valid
        sel = jnp.where(valid, jnp.where(subf == i1, 1.0, jnp.where(subf == i2, 1.0, 0.0)), 0.0)
        return m1, i1, m2, i2, valid, sel

    def up8(rows):
        return jnp.ceil(rows / SUBLANES) * SUBLANES

    def count_body(k, carry):
        return carry + jnp.sum(select(k)[5], axis=1, keepdims=True)

    counts = lax.fori_loop(0, nb, count_body, jnp.zeros((N_EXPERTS, 1), F32))
    tiles = jnp.ceil(counts / tile)
    cum_incl = jnp.sum(jnp.where(lane <= sub, _experts_to_lanes(tiles, sub, lane), 0.0), axis=1, keepdims=True)
    offset = (cum_incl - tiles) * tile
    tile_expert = jnp.sum(jnp.where(lane.astype(F32) >= cum_incl, 1.0, 0.0), axis=0, keepdims=True)
    tile_expert = jnp.minimum(tile_expert, N_EXPERTS - 1.0)
    n_used = jnp.max(cum_incl, axis=0, keepdims=True)
    region_end = _experts_to_lanes(offset + up8(counts), sub, lane)
    region_pad = _experts_to_lanes(tiles * tile - up8(counts), sub, lane)
    tmeta_ref[...] = jnp.where(sub == 0, tile_expert, jnp.where(sub == 1, n_used, jnp.where(
        sub == 2, region_end, jnp.where(sub == 3, region_pad, 0.0)))).astype(I32)

    def place_body(k, before):
        m1, i1, m2, i2, valid, sel = select(k)
        run = jnp.dot(sel, incl, preferred_element_type=F32)
        cnt = run[:, tb - 1:tb]
        lead = before - jnp.floor(before / SUBLANES) * SUBLANES
        span = up8(lead + cnt)
        lstart = jnp.sum(jnp.where(lane < sub, _experts_to_lanes(span, sub, lane), 0.0), axis=1, keepdims=True)
        local = lstart + lead + run - sel
        loc1 = jnp.sum(jnp.where(subf == i1, local, 0.0), axis=0, keepdims=True)
        loc2 = jnp.sum(jnp.where(subf == i2, local, 0.0), axis=0, keepdims=True)
        live = (k * tb + tok[0:1, :]) < n_valid
        loc_ref[k] = jnp.where(subf == 0.0, jnp.where(live, loc1, -1.0),
                               jnp.where(subf == 1.0, jnp.where(live, loc2, -1.0), 0.0))
        e = jnp.exp(m2 - m1)
        gate_ref[k] = jnp.where(subf == 0.0, 1.0 / (1.0 + e), jnp.where(subf == 1.0, e / (1.0 + e), 0.0))
        moved = jnp.where(cnt > 0.0, span, 0.0)
        bmeta_ref[k] = jnp.where(lane == 0, offset + before - lead, jnp.where(lane == 1, moved, jnp.where(
            lane == 2, lstart, jnp.where(lane == 3, lead, jnp.where(lane == 4, lead + cnt, 0.0))))).astype(I32)
        return before + cnt

    lax.fori_loop(0, nb, place_body, jnp.zeros((N_EXPERTS, 1), F32))


def _route(logits3, tile, n_valid):
    nb, _, tb = logits3.shape
    tok_blk = pl.BlockSpec((nb, N_EXPERTS, tb), lambda i: (0, 0, 0))
    return pl.pallas_call(
        functools.partial(_route_kernel, tile=tile, n_valid=n_valid),
        grid=(1,),
        in_specs=[pl.BlockSpec((nb, 2 * N_EXPERTS, tb), lambda i: (0, 0, 0))],
        out_specs=[tok_blk, tok_blk, pl.BlockSpec((nb, N_EXPERTS, LANES), lambda i: (0, 0, 0)),
                   pl.BlockSpec((N_EXPERTS, LANES), lambda i: (0, 0))],
        out_shape=[jax.ShapeDtypeStruct((nb, N_EXPERTS, tb), F32),
                   jax.ShapeDtypeStruct((nb, N_EXPERTS, tb), F32),
                   jax.ShapeDtypeStruct((nb, N_EXPERTS, LANES), I32),
                   jax.ShapeDtypeStruct((N_EXPERTS, LANES), I32)],
        compiler_params=_params(("arbitrary",), 16 * nb * N_EXPERTS * tb * 4),
        name="moe_route",
    )(logits3)


RUN_FIELDS = 5


def _run_copies(bm_ref, blk, vmem_ref, hbm_ref, sem, *, tb, to_hbm, wait):
    for e in range(N_EXPERTS):
        base = (blk * N_EXPERTS + e) * RUN_FIELDS
        start, cnt, lstart = bm_ref[base], bm_ref[base + 1], bm_ref[base + 2]
        off = 0
        size = tb
        while size >= SUBLANES:
            @pl.when((cnt & size) != 0)
            def _(size=size, off=off):
                v = vmem_ref.at[pl.ds(pl.multiple_of(lstart + off, SUBLANES), size)]
                h = hbm_ref.at[pl.ds(pl.multiple_of(start + off, SUBLANES), size)]
                cp = pltpu.make_async_copy(v, h, sem) if to_hbm else pltpu.make_async_copy(h, v, sem)
                if wait:
                    cp.wait()
                else:
                    cp.start()

            off = off + (cnt & size)
            size //= 2


def _compact_rows(tb):
    return 2 * tb + N_EXPERTS * 2 * SUBLANES + 2 * SUBLANES


def _selection(loc_ref, rows, tb):
    r = lax.broadcasted_iota(I32, (rows, tb), 0).astype(F32)
    return r == loc_ref[0, 0:1, :], r == loc_ref[0, 1:2, :]


def _zero_fill(zm_ref, zbuf_ref, xs_ref, zsem, *, tm, n_tiles, wait):
    def fill(row0, size):
        cp = pltpu.make_async_copy(zbuf_ref.at[pl.ds(0, size)],
                                   xs_ref.at[pl.ds(pl.multiple_of(row0, SUBLANES), size)], zsem)
        if wait:
            cp.wait()
        else:
            cp.start()

    for e in range(N_EXPERTS):
        end, pad = zm_ref[1 + 2 * e], zm_ref[2 + 2 * e]
        off = 0
        size = tm // 2
        while size >= SUBLANES:
            @pl.when((pad & size) != 0)
            def _(size=size, off=off):
                fill(end + off, size)

            off = off + (pad & size)
            size //= 2
    for t in range(n_tiles):
        @pl.when(t >= zm_ref[0])
        def _(t=t):
            fill(t * tm, tm)


def _moe_scatter_kernel(bm_ref, zm_ref, a_ref, loc_ref, gate_ref, xs_ref,
                        buf_ref, zbuf_ref, carry_ref, sem, zsem, *, tb, tm, n_tiles):
    blk = pl.program_id(0)
    last = pl.num_programs(0) - 1
    slot = blk % 2
    d = a_ref.shape[1]

    @pl.when(blk == 0)
    def _():
        zbuf_ref[...] = jnp.zeros_like(zbuf_ref)
        carry_ref[...] = jnp.zeros_like(carry_ref)
        _zero_fill(zm_ref, zbuf_ref, xs_ref, zsem, tm=tm, n_tiles=n_tiles, wait=False)

    def copies(block, wait):
        s = block % 2
        _run_copies(bm_ref, block, buf_ref.at[s], xs_ref, sem.at[s], tb=tb, to_hbm=True, wait=wait)

    first, second = _selection(loc_ref, _compact_rows(tb), tb)
    pick = jnp.where(first, 1.0, jnp.where(second, 1.0, 0.0)).astype(BF16)
    buf_ref[slot, :, 0:d] = jnp.dot(pick, a_ref[...], preferred_element_type=F32)
    gsel = jnp.where(first, gate_ref[0, 0:1, :], jnp.where(second, gate_ref[0, 1:2, :], 0.0))
    buf_ref[slot, :, d:d + LANES] = jnp.broadcast_to(jnp.sum(gsel, axis=1, keepdims=True),
                                                    (buf_ref.shape[1], LANES))

    row = lax.broadcasted_iota(I32, (SUBLANES, buf_ref.shape[2]), 0)
    for e in range(N_EXPERTS):
        base = (blk * N_EXPERTS + e) * RUN_FIELDS
        lstart, total = bm_ref[base + 2], bm_ref[base + 4]
        head = pl.ds(pl.multiple_of(lstart, SUBLANES), SUBLANES)
        buf_ref[slot, head, :] = buf_ref[slot, head, :] + carry_ref[e]
        full = lax.shift_left(lax.shift_right_logical(total, 3), 3)
        tail = pl.ds(pl.multiple_of(lstart + full, SUBLANES), SUBLANES)
        carry_ref[e] = jnp.where(row < total - full, buf_ref[slot, tail, :], 0.0)

    @pl.when(blk >= 1)
    def _():
        copies(blk - 1, True)

    copies(blk, False)

    @pl.when(blk == last)
    def _():
        copies(blk, True)
        _zero_fill(zm_ref, zbuf_ref, xs_ref, zsem, tm=tm, n_tiles=n_tiles, wait=True)


def _moe_scatter(a, loc, gate, bmeta, zmeta, n_slots, tb, tm):
    m, d = a.shape
    width = d + LANES
    tok_spec = pl.BlockSpec((1, N_EXPERTS, tb), lambda i, bm, zm: (i, 0, 0))
    grid_spec = pltpu.PrefetchScalarGridSpec(
        num_scalar_prefetch=2,
        grid=(m // tb,),
        in_specs=[pl.BlockSpec((tb, d), lambda i, bm, zm: (i, 0)), tok_spec, tok_spec],
        out_specs=pl.BlockSpec(memory_space=pl.ANY),
        scratch_shapes=[pltpu.VMEM((2, _compact_rows(tb), width), F32), pltpu.VMEM((tm, width), F32),
                        pltpu.VMEM((N_EXPERTS, SUBLANES, width), F32),
                        pltpu.SemaphoreType.DMA((2,)), pltpu.SemaphoreType.DMA(())],
    )
    return pl.pallas_call(
        functools.partial(_moe_scatter_kernel, tb=tb, tm=tm, n_tiles=n_slots // tm),
        grid_spec=grid_spec,
        out_shape=jax.ShapeDtypeStruct((n_slots, width), F32),
        compiler_params=_params(("arbitrary",), 16 * tb * d * 4),
        name="moe_scatter",
    )(bmeta, zmeta, a, loc, gate)


def _new_weights(te_ref):
    i = pl.program_id(1)
    return (i == 0) | (te_ref[i] != te_ref[jnp.maximum(i - 1, 0)])


def _moe_gu_kernel(te_ref, nu_ref, x_ref, wg_ref, wu_ref, o_ref, wgb_ref, wub_ref):
    i = pl.program_id(1)

    @pl.when(_new_weights(te_ref))
    def _():
        wgb_ref[...] = wg_ref[0].astype(BF16)
        wub_ref[...] = wu_ref[0].astype(BF16)

    @pl.when(i < nu_ref[0])
    def _():
        x = x_ref[...].astype(BF16)
        for c in range(o_ref.shape[1] // MXU_COLS):
            cols = slice(c * MXU_COLS, (c + 1) * MXU_COLS)
            g = jnp.dot(x, wgb_ref[:, cols], preferred_element_type=F32)
            u = jnp.dot(x, wub_ref[:, cols], preferred_element_type=F32)
            o_ref[:, cols] = (_silu(g) * u).astype(o_ref.dtype)

    @pl.when(i >= nu_ref[0])
    def _():
        o_ref[...] = jnp.zeros_like(o_ref)


def _used_tile(i, nu):
    return jnp.minimum(i, nu[0] - 1)


def _moe_gu(xs, w_gu, te, nu, tm, tn):
    s = xs.shape[0]
    k = w_gu.shape[1]
    f = D_FF_EXPERT
    nbytes = 2 * (2 * k * tn * 4 + k * tn * 2) + 2 * tm * k * 4 + 2 * tm * tn * 2 + 4 * tm * MXU_COLS * 4
    grid_spec = pltpu.PrefetchScalarGridSpec(
        num_scalar_prefetch=2,
        grid=(f // tn, s // tm),
        in_specs=[pl.BlockSpec((tm, k), lambda j, i, te, nu: (_used_tile(i, nu), 0)),
                  pl.BlockSpec((1, k, tn), lambda j, i, te, nu: (te[i], 0, j)),
                  pl.BlockSpec((1, k, tn), lambda j, i, te, nu: (te[i], 0, j + f // tn))],
        out_specs=pl.BlockSpec((tm, tn), lambda j, i, te, nu: (i, j)),
        scratch_shapes=[pltpu.VMEM((k, tn), BF16), pltpu.VMEM((k, tn), BF16)],
    )
    return pl.pallas_call(
        _moe_gu_kernel,
        grid_spec=grid_spec,
        out_shape=jax.ShapeDtypeStruct((s, f), BF16),
        compiler_params=_params(("arbitrary", "arbitrary"), nbytes),
        name="moe_gate_up",
    )(te, nu, xs, w_gu, w_gu)


def _moe_down_kernel(te_ref, nu_ref, x_ref, w_ref, gs_ref, o_ref, wb_ref):
    i = pl.program_id(1)

    @pl.when(_new_weights(te_ref))
    def _():
        wb_ref[...] = w_ref[0].astype(BF16)

    @pl.when(i < nu_ref[0])
    def _():
        o_ref[...] = gs_ref[:, 0:1] * jnp.dot(x_ref[...], wb_ref[...], preferred_element_type=F32)

    @pl.when(i >= nu_ref[0])
    def _():
        o_ref[...] = jnp.zeros_like(o_ref)


def _moe_down(act, w_down, xs, te, nu, tm, tn):
    s, k = act.shape
    d = w_down.shape[2]
    gate_block = d // LANES
    nbytes = 2 * k * tn * 4 + k * tn * 2 + 2 * tm * k * 2 + 3 * tm * tn * 4
    grid_spec = pltpu.PrefetchScalarGridSpec(
        num_scalar_prefetch=2,
        grid=(d // tn, s // tm),
        in_specs=[pl.BlockSpec((tm, k), lambda j, i, te, nu: (_used_tile(i, nu), 0)),
                  pl.BlockSpec((1, k, tn), lambda j, i, te, nu: (te[i], 0, j)),
                  pl.BlockSpec((tm, LANES), lambda j, i, te, nu: (_used_tile(i, nu), gate_block))],
        out_specs=pl.BlockSpec((tm, tn), lambda j, i, te, nu: (i, j)),
        scratch_shapes=[pltpu.VMEM((k, tn), BF16)],
    )
    return pl.pallas_call(
        _moe_down_kernel,
        grid_spec=grid_spec,
        out_shape=jax.ShapeDtypeStruct((s, d), F32),
        compiler_params=_params(("arbitrary", "arbitrary"), nbytes),
        name="moe_down",
    )(te, nu, act, w_down, xs)


def _moe_combine_kernel(bm_ref, h_ref, locc_ref, g_ref, ys_ref, op_ref, os_ref, ybuf_ref, sem, *, tb, prompt_blocks):
    blk = pl.program_id(0)
    slot = blk % 2

    def copies(block, wait):
        s = block % 2
        _run_copies(bm_ref, block, ybuf_ref.at[s], ys_ref, sem.at[s], tb=tb, to_hbm=False, wait=wait)

    @pl.when(blk == 0)
    def _():
        ybuf_ref[...] = jnp.zeros_like(ybuf_ref)
        copies(blk, False)

    @pl.when(blk + 1 < pl.num_programs(0))
    def _():
        copies(blk + 1, False)

    r = lax.broadcasted_iota(I32, (tb, _compact_rows(tb)), 1).astype(F32)
    pick = jnp.where(r == locc_ref[:, 0:1], 1.0, jnp.where(r == locc_ref[:, 1:2], 1.0, 0.0)).astype(BF16)
    copies(blk, True)
    moe = jnp.dot(pick, ybuf_ref[slot].astype(BF16), preferred_element_type=F32)
    out = _rms(h_ref[...] + moe, g_ref[...])

    @pl.when(blk < prompt_blocks)
    def _():
        op_ref[...] = out

    @pl.when(blk == prompt_blocks)
    def _():
        os_ref[...] = out


def _moe_combine(h, locc, bmeta, ys, g, tb, n_prompt):
    m, d = h.shape
    prompt_blocks = n_prompt // tb
    any_spec = pl.BlockSpec(memory_space=pl.ANY)
    grid_spec = pltpu.PrefetchScalarGridSpec(
        num_scalar_prefetch=1,
        grid=(m // tb,),
        in_specs=[pl.BlockSpec((tb, d), lambda i, bm: (i, 0)),
                  pl.BlockSpec((tb, N_EXPERTS), lambda i, bm: (i, 0)),
                  pl.BlockSpec((1, d), lambda i, bm: (0, 0)),
                  any_spec],
        out_specs=[pl.BlockSpec((tb, d), lambda i, bm: (jnp.minimum(i, prompt_blocks - 1), 0)),
                   pl.BlockSpec((tb, d), lambda i, bm: (0, 0))],
        scratch_shapes=[pltpu.VMEM((2, _compact_rows(tb), d), F32), pltpu.SemaphoreType.DMA((2,))],
    )
    return pl.pallas_call(
        functools.partial(_moe_combine_kernel, tb=tb, prompt_blocks=prompt_blocks),
        grid_spec=grid_spec,
        out_shape=[jax.ShapeDtypeStruct((n_prompt, d), F32), jax.ShapeDtypeStruct((tb, d), F32)],
        compiler_params=_params(("arbitrary",), 16 * tb * d * 4),
        name="moe_combine",
    )(bmeta, h, locc, g.reshape(1, d), ys)


MOE_TOKEN_BLOCK = 256
MOE_ROW_TILE = 512


def _moe(h, a, lg_parts, n_valid, n_prompt, w_gu, w_down, g_final):
    m, d = h.shape
    tb, tm = MOE_TOKEN_BLOCK, MOE_ROW_TILE
    nb = m // tb
    lg3 = lg_parts[:, :2 * N_EXPERTS].reshape(nb, tb, 2 * N_EXPERTS).transpose(0, 2, 1)
    n_tiles = pl.cdiv(2 * n_valid, tm) + N_EXPERTS
    loc, gate, bmeta, tmeta = _route(lg3, tm, n_valid)
    te = tmeta[0, :n_tiles]
    nu = tmeta[1, 0:1]
    bm = bmeta[:, :, :RUN_FIELDS].reshape(-1)
    zm = jnp.concatenate([nu, tmeta[2:4, :N_EXPERTS].T.reshape(-1)])
    locc = loc.transpose(0, 2, 1).reshape(m, N_EXPERTS)
    xs = _moe_scatter(a, loc, gate, bm, zm, n_tiles * tm, tb, tm)
    act = _moe_gu(xs, w_gu, te, nu, tm, D_FF_EXPERT // 2)
    ys = _moe_down(act, w_down, xs, te, nu, tm, d)
    return _moe_combine(h, locc, bm, ys, g_final, tb, n_prompt)


def _pad_lanes(v):
    return jnp.pad(v.reshape(1, -1), ((0, 0), (0, LANES - v.shape[-1])))


def kernel(x_prompt, x_sample, state_ssm, state_conv, norm_mix, norm_ffn, norm_final, ssd_w_in, ssd_conv_w,
           ssd_conv_b, ssd_dt_bias, ssd_a_log, ssd_d, ssd_gnorm, ssd_w_out, sgu_w_in, sgu_b_in, sgu_ln_g,
           sgu_ln_b, sgu_w_s, sgu_b_s, sgu_w_out, ffn_w_gu, ffn_w_down, moe_w_router, moe_w_gu, moe_w_down):
    batch, seq, d = x_prompt.shape
    n_s = x_sample.shape[0]
    mp = batch * seq
    xp = x_prompt.reshape(mp, d)
    xs = x_sample.reshape(n_s, d)

    w_in = ssd_w_in[0].T
    w_dt = jnp.pad(w_in[D_INNER + CONV_DIM:], ((0, LANES - SSD_HEADS), (0, 0)))
    dtb = _pad_lanes(ssd_dt_bias[0])
    alog = _pad_lanes(ssd_a_log[0])
    d_rep = jnp.repeat(ssd_d[0], SSD_HEADDIM).reshape(1, D_INNER)
    gn_w = ssd_gnorm[0].reshape(1, D_INNER)
    conv_w = ssd_conv_w[0]
    conv_b = ssd_conv_b[0].reshape(1, CONV_DIM)
    ln_g = sgu_ln_g[0].reshape(1, D_SGU)
    ln_b = sgu_ln_b[0].reshape(1, D_SGU)
    bs_t = jnp.pad(sgu_b_s[0].T, ((0, 0), (0, LANES - SGU_HEADS)))
    w0_rep = jnp.repeat(sgu_w_s[0][:, 0, 0], SGU_HEAD_DIM).reshape(1, D_SGU)
    b0_rep = jnp.repeat(sgu_b_s[0][:, 0], SGU_HEAD_DIM).reshape(1, D_SGU)
    wr_hi = moe_w_router[0].astype(BF16)
    wr_lo = (moe_w_router[0] - wr_hi.astype(F32)).astype(BF16)
    w3 = jnp.concatenate([wr_hi, wr_lo, jnp.zeros((d, LANES - 2 * N_EXPERTS), BF16)], axis=1)

    def dense_ffn(h1, a1, tm):
        act = _mm_swiglu(a1, ffn_w_gu[0], D_FF_DENSE, min(2 * tm, a1.shape[0]), 1408, "ffn_gate_up")
        return _mm_resnorm(act, ffn_w_down[0], h1, norm_mix[1], tm, "ffn_down")

    a0p, zp, dtrp = _norm_in_proj(xp, norm_mix[0], w_in, w_dt, 1024)
    xbcp = _mm_plain(a0p, w_in, D_INNER, CONV_DIM, 1024, 1024, F32, "ssd_in_xbc")
    per_seq = lambda t: t.reshape(batch, seq, t.shape[-1])
    ygp, ssm_p, conv_p = _ssd_prompt(per_seq(xbcp), per_seq(zp), per_seq(dtrp), conv_w, conv_b, dtb, alog, d_rep,
                                     gn_w)
    h1p, a1p = _mm_resnorm(ygp.reshape(mp, D_INNER), ssd_w_out[0], xp, norm_ffn[0], 512, "ssd_out")
    h2p, a2p = dense_ffn(h1p, a1p, 512)

    a0s, zs, dtrs = _norm_in_proj(xs, norm_mix[0], w_in, w_dt, n_s)
    xbcs = _mm_plain(a0s, w_in, D_INNER, CONV_DIM, n_s, 1024, F32, "ssd_in_xbc")
    conv_s, xas, xdts, das, bsts, csts = _ssd_sample_pre(xbcs, state_conv[0].reshape(n_s, 3 * CONV_DIM), dtrs,
                                                         conv_w, conv_b, dtb, alog)
    stack_rows = SSD_GROUPS * TERM_ROWS
    ssm_s, ys3 = _ssd_sample_state(state_ssm[0].reshape(n_s, D_INNER, D_STATE), xdts.reshape(n_s, 1, D_INNER),
                                   das[:, :SSD_HEADS], bsts.reshape(n_s, stack_rows, D_STATE),
                                   csts.reshape(n_s, stack_rows, D_STATE))
    h1s, a1s = _ssd_out(ys3.reshape(n_s, D_INNER), xas, zs, d_rep, gn_w, ssd_w_out[0], xs, norm_ffn[0], n_s)
    h2s, a2s = dense_ffn(h1s, a1s, n_s)

    uvp = _mm_bias_gelu(a2p, sgu_w_in[0], sgu_b_in[0], 1024, 2048, BF16, "sgu_in")
    uvs = _mm_bias_gelu(a2s, sgu_w_in[0], sgu_b_in[0], n_s, 1024, F32, "sgu_in")
    ss, v_s = _sgu_sample(uvs, ln_g, ln_b, w0_rep, b0_rep)
    tm_out = 512
    pad_rows = ((0, tm_out - n_s), (0, 0))
    h3, a3, lg = _sgu_out(uvp, jnp.pad(ss, pad_rows), ln_g, ln_b, sgu_w_s[0], bs_t, sgu_w_out[0], h2p,
                          jnp.pad(h2s, pad_rows), norm_ffn[1], w3, tm_out)

    y_prompt, y_tail = _moe(h3, a3, lg, mp + n_s, mp, moe_w_gu[0], moe_w_down[0], norm_final)
    y_sample = y_tail[:n_s]

    return (y_prompt.reshape(batch, seq, d),
            y_sample.reshape(n_s, 1, d),
            ssm_p[None],
            conv_p[None],
            ssm_s.reshape(1, n_s, SSD_HEADS, SSD_HEADDIM, D_STATE),
            conv_s.reshape(1, n_s, CONV_W - 1, CONV_DIM),
            v_s.reshape(1, n_s, 1, D_SGU))
```

```python
import functools

import jax
import jax.numpy as jnp
from jax import lax
from jax.experimental import pallas as pl
from jax.experimental.pallas import tpu as pltpu

F32 = jnp.float32
BF16 = jnp.bfloat16
I32 = jnp.int32
HIGHEST = lax.Precision.HIGHEST

D_MODEL = 1024
D_INNER = 2048
SSD_HEADS = 32
SSD_HEADDIM = 64
SSD_GROUPS = 4
D_STATE = 128
CONV_W = 4
CONV_DIM = D_INNER + 2 * SSD_GROUPS * D_STATE
CHUNK = 128
D_SGU = 2048
SGU_HEADS = 8
SGU_HEAD_DIM = D_SGU // SGU_HEADS
D_FF_DENSE = 2816
N_EXPERTS = 8
D_FF_EXPERT = 3584
EPS = 1e-5

MXU_COLS = 256
ROW_SUB = 256
LANES = 128
SUBLANES = 8
VMEM_CAP = 60000 * 1024
VMEM_FLOOR = 32 * 1024 * 1024


def _vmem_limit(nbytes):
    return int(min(max(nbytes * 5 // 4 + (4 << 20), VMEM_FLOOR), VMEM_CAP))


def _params(sem, nbytes):
    return pltpu.CompilerParams(dimension_semantics=sem, vmem_limit_bytes=_vmem_limit(nbytes))


LOG2E = 1.4426950408889634


def _silu(x):
    return x / (1.0 + jnp.exp2(x * (-LOG2E)))


def _rms(x, g):
    return x * lax.rsqrt(jnp.mean(x * x, axis=-1, keepdims=True) + EPS) * g


def _bdot(a, b):
    return jnp.dot(a.astype(BF16), b.astype(BF16), preferred_element_type=F32)


def _cast_transposed(wt_ref, wb_ref):
    step = 2 * MXU_COLS
    for r0 in range(0, wt_ref.shape[0], step):
        r1 = min(r0 + step, wt_ref.shape[0])
        wb_ref[:, r0:r1] = wt_ref[r0:r1, :].T.astype(BF16)


def _mm_plain_kernel(x_ref, wt_ref, o_ref, wb_ref):
    @pl.when(pl.program_id(1) == 0)
    def _():
        _cast_transposed(wt_ref, wb_ref)

    o_ref[...] = _bdot(x_ref[...], wb_ref[...]).astype(o_ref.dtype)


def _mm_bias_gelu_kernel(x_ref, w_ref, b_ref, o_ref, wb_ref):
    @pl.when(pl.program_id(1) == 0)
    def _():
        wb_ref[...] = w_ref[...].astype(BF16)

    x = x_ref[...].astype(BF16)
    for c0 in range(0, o_ref.shape[1], 2 * MXU_COLS):
        cols = slice(c0, c0 + 2 * MXU_COLS)
        y = jnp.dot(x, wb_ref[:, cols], preferred_element_type=F32) + b_ref[:, cols]
        o_ref[:, cols] = (0.5 * y * (1.0 + lax.erf(y * (2.0 ** -0.5)))).astype(o_ref.dtype)


def _mm_swiglu_kernel(x_ref, wg_ref, wu_ref, o_ref, wgb_ref, wub_ref):
    @pl.when(pl.program_id(1) == 0)
    def _():
        wgb_ref[...] = wg_ref[...].astype(BF16)
        wub_ref[...] = wu_ref[...].astype(BF16)

    x = x_ref[...].astype(BF16)
    tn = o_ref.shape[1]
    for c0 in range(0, tn, MXU_COLS):
        cols = slice(c0, min(c0 + MXU_COLS, tn))
        g = jnp.dot(x, wgb_ref[:, cols], preferred_element_type=F32)
        u = jnp.dot(x, wub_ref[:, cols], preferred_element_type=F32)
        o_ref[:, cols] = (_silu(g) * u).astype(o_ref.dtype)


def _wspec(k, tn, col_block0):
    return pl.BlockSpec((k, tn), lambda j, i: (0, j + col_block0), pipeline_mode=pl.Buffered(1))


def _mm_plain(x, wt, col0, n, tm, tn, out_dtype, name):
    m, k = x.shape
    nbytes = k * tn * 6 + 2 * tm * k * x.dtype.itemsize + 2 * tm * tn * 4 + 2 * MXU_COLS * k * 8
    row_block0 = col0 // tn
    return pl.pallas_call(
        _mm_plain_kernel,
        grid=(n // tn, m // tm),
        in_specs=[pl.BlockSpec((tm, k), lambda j, i: (i, 0)),
                  pl.BlockSpec((tn, k), lambda j, i: (j + row_block0, 0), pipeline_mode=pl.Buffered(1))],
        out_specs=pl.BlockSpec((tm, tn), lambda j, i: (i, j)),
        out_shape=jax.ShapeDtypeStruct((m, n), out_dtype),
        scratch_shapes=[pltpu.VMEM((k, tn), BF16)],
        compiler_params=_params(("arbitrary", "arbitrary"), nbytes),
        name=name,
    )(x, wt)


def _norm_in_proj_kernel(x_ref, g_ref, wz_ref, wdt_ref, a_ref, z_ref, dtr_ref, wzb_ref, wdtb_ref):
    @pl.when(pl.program_id(0) == 0)
    def _():
        _cast_transposed(wz_ref, wzb_ref)
        _cast_transposed(wdt_ref, wdtb_ref)

    a = _rms(x_ref[...], g_ref[...]).astype(BF16)
    a_ref[...] = a
    for c0 in range(0, z_ref.shape[1], 2 * MXU_COLS):
        cols = slice(c0, c0 + 2 * MXU_COLS)
        z_ref[:, cols] = _silu(jnp.dot(a, wzb_ref[:, cols], preferred_element_type=F32)).astype(z_ref.dtype)
    dtr_ref[...] = jnp.dot(a, wdtb_ref[...], preferred_element_type=F32)


def _norm_in_proj(x, g, w_in_t, w_dt_t, tm):
    m, k = x.shape
    n = D_INNER
    row = lambda i: (i, 0)
    fixed = lambda i: (0, 0)
    nbytes = k * n * 6 + k * LANES * 6 + 2 * tm * k * 4 + 2 * tm * k * 2 + 2 * tm * n * 2 + 2 * tm * n * 4
    return pl.pallas_call(
        _norm_in_proj_kernel,
        grid=(m // tm,),
        in_specs=[pl.BlockSpec((tm, k), row), pl.BlockSpec((1, k), fixed),
                  pl.BlockSpec((n, k), fixed, pipeline_mode=pl.Buffered(1)),
                  pl.BlockSpec((LANES, k), fixed)],
        out_specs=[pl.BlockSpec((tm, k), row), pl.BlockSpec((tm, n), row), pl.BlockSpec((tm, LANES), row)],
        out_shape=[jax.ShapeDtypeStruct((m, k), BF16), jax.ShapeDtypeStruct((m, n), BF16),
                   jax.ShapeDtypeStruct((m, LANES), F32)],
        scratch_shapes=[pltpu.VMEM((k, n), BF16), pltpu.VMEM((k, LANES), BF16)],
        compiler_params=_params(("arbitrary",), nbytes),
        name="norm_in_proj",
    )(x, g.reshape(1, k), w_in_t, w_dt_t)


def _mm_bias_gelu(x, w, b, tm, tn, out_dtype, name):
    m, k = x.shape
    n = w.shape[1]
    nbytes = k * tn * 6 + 2 * tm * k * x.dtype.itemsize + 2 * tm * tn * 4
    return pl.pallas_call(
        _mm_bias_gelu_kernel,
        grid=(n // tn, m // tm),
        in_specs=[pl.BlockSpec((tm, k), lambda j, i: (i, 0)), _wspec(k, tn, 0),
                  pl.BlockSpec((1, tn), lambda j, i: (0, j))],
        out_specs=pl.BlockSpec((tm, tn), lambda j, i: (i, j)),
        out_shape=jax.ShapeDtypeStruct((m, n), out_dtype),
        scratch_shapes=[pltpu.VMEM((k, tn), BF16)],
        compiler_params=_params(("arbitrary", "arbitrary"), nbytes),
        name=name,
    )(x, w, b.reshape(1, n))


def _mm_swiglu(x, w_gu, d_ff, tm, tn, name):
    m, k = x.shape
    nbytes = 2 * k * tn * 6 + 2 * tm * k * x.dtype.itemsize + 2 * tm * tn * 2 + 3 * tm * tn * 4
    return pl.pallas_call(
        _mm_swiglu_kernel,
        grid=(d_ff // tn, m // tm),
        in_specs=[pl.BlockSpec((tm, k), lambda j, i: (i, 0)), _wspec(k, tn, 0), _wspec(k, tn, d_ff // tn)],
        out_specs=pl.BlockSpec((tm, tn), lambda j, i: (i, j)),
        out_shape=jax.ShapeDtypeStruct((m, d_ff), BF16),
        scratch_shapes=[pltpu.VMEM((k, tn), BF16), pltpu.VMEM((k, tn), BF16)],
        compiler_params=_params(("arbitrary", "arbitrary"), nbytes),
        name=name,
    )(x, w_gu, w_gu)


def _mm_resnorm_kernel(x_ref, w_ref, r_ref, g_ref, h_ref, a_ref, wb_ref):
    @pl.when(pl.program_id(0) == 0)
    def _():
        wb_ref[...] = w_ref[...].astype(BF16)

    h = r_ref[...] + _bdot(x_ref[...], wb_ref[...])
    h_ref[...] = h
    a_ref[...] = _rms(h, g_ref[...]).astype(a_ref.dtype)


def _mm_resnorm(x, w, resid, g, tm, name):
    m, k = x.shape
    d = w.shape[1]
    nbytes = k * d * 6 + 2 * tm * k * 2 + 8 * tm * d * 4
    return pl.pallas_call(
        _mm_resnorm_kernel,
        grid=(m // tm,),
        in_specs=[pl.BlockSpec((tm, k), lambda i: (i, 0)),
                  pl.BlockSpec((k, d), lambda i: (0, 0), pipeline_mode=pl.Buffered(1)),
                  pl.BlockSpec((tm, d), lambda i: (i, 0)),
                  pl.BlockSpec((1, d), lambda i: (0, 0))],
        out_specs=[pl.BlockSpec((tm, d), lambda i: (i, 0)), pl.BlockSpec((tm, d), lambda i: (i, 0))],
        out_shape=[jax.ShapeDtypeStruct((m, d), F32), jax.ShapeDtypeStruct((m, d), BF16)],
        scratch_shapes=[pltpu.VMEM((k, d), BF16)],
        compiler_params=_params(("arbitrary",), nbytes),
        name=name,
    )(x, w, resid, g.reshape(1, d))


def _sgu_out_kernel(u_ref, v_ref, xt_ref, lng_ref, lnb_ref, ws_ref, bst_ref, w_ref, rp_ref, rt_ref, g_ref, w3_ref,
                    h_ref, a_ref, lg_ref, wb_ref, wsb_ref, *, prompt_tiles):
    i = pl.program_id(0)
    q = CHUNK

    @pl.when(i == 0)
    def _():
        wb_ref[...] = w_ref[...].astype(BF16)
        causal = lax.broadcasted_iota(I32, (q, q), 0) >= lax.broadcasted_iota(I32, (q, q), 1)
        for g in range(SGU_HEADS):
            wsb_ref[g] = jnp.where(causal, ws_ref[g], 0.0).astype(BF16)

    def project(x, r_ref, rows):
        h = r_ref[rows, :] + jnp.dot(x, wb_ref[...], preferred_element_type=F32)
        h_ref[rows, :] = h
        a = _rms(h, g_ref[...])
        a_hi = a.astype(BF16)
        a_lo = (a - a_hi.astype(F32)).astype(BF16)
        a_ref[rows, :] = a_hi
        lg_ref[rows, :] = (jnp.dot(a_hi, w3_ref[...], preferred_element_type=F32)
                           + jnp.dot(a_lo, w3_ref[...], preferred_element_type=F32))

    chunks = [slice(c * q, (c + 1) * q) for c in range(u_ref.shape[0] // q)]

    @pl.when(i < prompt_tiles)
    def _():
        for rows in chunks:
            vn = _layernorm(v_ref[rows, :].astype(F32), lng_ref[...], lnb_ref[...]).astype(BF16)
            gated = []
            for g in range(SGU_HEADS):
                cols = slice(g * SGU_HEAD_DIM, (g + 1) * SGU_HEAD_DIM)
                s = jnp.dot(wsb_ref[g], vn[:, cols], preferred_element_type=F32) + bst_ref[:, g:g + 1]
                gated.append((u_ref[rows, cols].astype(F32) * s).astype(BF16))
            project(jnp.concatenate(gated, axis=1), rp_ref, rows)

    @pl.when(i == prompt_tiles)
    def _():
        for rows in chunks:
            project(xt_ref[rows, :], rt_ref, rows)


def _sgu_out(uv, xt, ln_g, ln_b, w_s, bs_t, w, rp, rt, g, w3, tm):
    mp = uv.shape[0]
    k, d = w.shape
    prompt_tiles = mp // tm
    last = prompt_tiles - 1
    rows = mp + tm
    prow = lambda i: (jnp.minimum(i, last), 0)
    fixed = lambda i: (0, 0)
    nbytes = k * d * 6 + 8 * tm * k * 2 + 12 * tm * d * 4 + 8 * CHUNK * k * 4
    return pl.pallas_call(
        functools.partial(_sgu_out_kernel, prompt_tiles=prompt_tiles),
        grid=(prompt_tiles + 1,),
        in_specs=[pl.BlockSpec((tm, k), prow),
                  pl.BlockSpec((tm, k), lambda i: (jnp.minimum(i, last), 1)),
                  pl.BlockSpec((tm, k), fixed),
                  pl.BlockSpec((1, k), fixed), pl.BlockSpec((1, k), fixed),
                  pl.BlockSpec((SGU_HEADS, CHUNK, CHUNK), lambda i: (0, 0, 0)),
                  pl.BlockSpec((CHUNK, LANES), fixed),
                  pl.BlockSpec((k, d), fixed, pipeline_mode=pl.Buffered(1)),
                  pl.BlockSpec((tm, d), prow),
                  pl.BlockSpec((tm, d), fixed),
                  pl.BlockSpec((1, d), fixed),
                  pl.BlockSpec((d, LANES), fixed)],
        out_specs=[pl.BlockSpec((tm, d), lambda i: (i, 0)), pl.BlockSpec((tm, d), lambda i: (i, 0)),
                   pl.BlockSpec((tm, LANES), lambda i: (i, 0))],
        out_shape=[jax.ShapeDtypeStruct((rows, d), F32), jax.ShapeDtypeStruct((rows, d), BF16),
                   jax.ShapeDtypeStruct((rows, LANES), F32)],
        scratch_shapes=[pltpu.VMEM((k, d), BF16), pltpu.VMEM((SGU_HEADS, CHUNK, CHUNK), BF16)],
        compiler_params=_params(("arbitrary",), nbytes),
        name="sgu_out",
    )(uv, uv, xt, ln_g, ln_b, w_s, bs_t, w, rp, rt, g.reshape(1, d), w3)


def _gate_and_groupnorm(y, xs, gate, d_rep, gn_w):
    yg = (y + d_rep * xs) * gate
    gw = D_INNER // SSD_GROUPS
    outs = []
    for g in range(SSD_GROUPS):
        blk = yg[:, g * gw:(g + 1) * gw]
        outs.append(blk * lax.rsqrt(jnp.mean(blk * blk, axis=-1, keepdims=True) + EPS))
    return jnp.concatenate(outs, axis=1) * gn_w


def _softplus(x):
    return jnp.maximum(x, 0.0) + jnp.log(1.0 + jnp.exp(-jnp.abs(x)))


SSD_SEQS = 2


def _ssd_prompt_kernel(xbc_ref, z_ref, dtr_ref, cw_ref, cb_ref, dtb_ref, alog_ref, drep_ref, gnw_ref,
                       yg_ref, ssm_ref, conv_ref, ext_ref, st_ref, y_ref, xa_ref):
    for n in range(SSD_SEQS):
        _ssd_chunk(xbc_ref.at[n], z_ref.at[n], dtr_ref.at[n], cw_ref, cb_ref, dtb_ref, alog_ref, drep_ref, gnw_ref,
                   yg_ref.at[n], ssm_ref.at[n], conv_ref.at[n], ext_ref.at[n], st_ref.at[n], y_ref.at[n],
                   xa_ref.at[n])


def _ssd_chunk(xbc_ref, z_ref, dtr_ref, cw_ref, cb_ref, dtb_ref, alog_ref, drep_ref, gnw_ref,
               yg_ref, ssm_ref, conv_ref, ext_ref, st_ref, y_ref, xa_ref):
    c = pl.program_id(1)
    nc = pl.num_programs(1)
    q = CHUNK

    @pl.when(c == 0)
    def _():
        ext_ref[0:SUBLANES, :] = jnp.zeros((SUBLANES, CONV_DIM), F32)
        st_ref[...] = jnp.zeros_like(st_ref)

    ext_ref[SUBLANES:SUBLANES + q, :] = xbc_ref[...]
    acc = cb_ref[...] + cw_ref[CONV_W - 1:CONV_W, :] * ext_ref[SUBLANES:SUBLANES + q, :]
    for k in range(CONV_W - 1):
        lo = SUBLANES - (CONV_W - 1) + k
        acc = acc + cw_ref[k:k + 1, :] * ext_ref[lo:lo + q, :]
    xa_ref[...] = _silu(acc)
    tail = ext_ref[q:q + SUBLANES, :]
    ext_ref[0:SUBLANES, :] = tail

    dt = _softplus(dtr_ref[...] + dtb_ref[...])
    a_neg = -jnp.exp(alog_ref[...])
    causal = lax.broadcasted_iota(I32, (q, q), 0) >= lax.broadcasted_iota(I32, (q, q), 1)
    tril = jnp.where(causal, 1.0, 0.0)
    cs = jnp.dot(tril, dt * a_neg, precision=HIGHEST, preferred_element_type=F32) * LOG2E
    cs_t = cs.T
    dt_t = dt.T
    rowp = cs_t - jnp.log2(dt_t)
    w_t = dt_t * jnp.exp2(cs_t[:, q - 1:q] - cs_t)
    first = lax.broadcasted_iota(I32, (q, 2 * SSD_HEADDIM), 1) < SSD_HEADDIM

    def per_head(v):
        zero = jnp.zeros_like(v)
        return jnp.concatenate([jnp.where(first, v, zero), jnp.where(first, zero, v)], axis=0)

    for g in range(SSD_GROUPS):
        b_g = xa_ref[:, D_INNER + g * D_STATE:D_INNER + (g + 1) * D_STATE]
        c_g = xa_ref[:, D_INNER + (SSD_GROUPS + g) * D_STATE:D_INNER + (SSD_GROUPS + g + 1) * D_STATE]
        b_gt = b_g.T
        cb = _bdot(c_g, b_gt)
        for pr in range(4):
            hp = g * 4 + pr
            cols = slice(hp * 2 * SSD_HEADDIM, (hp + 1) * 2 * SSD_HEADDIM)
            x2 = per_head(xa_ref[:, cols].astype(BF16))
            st_p = st_ref[:, cols]
            st2 = per_head(st_p.astype(BF16))
            ms, cs_scaled, bws, cds = [], [], [], []
            for s in range(2):
                h = 2 * hp + s
                colb = jnp.broadcast_to(cs[:, h:h + 1], (q, q))
                rowb = jnp.broadcast_to(rowp[h:h + 1, :], (q, q))
                ecol = jnp.exp2(colb)
                ms.append((cb * jnp.where(causal, jnp.exp2(colb - rowb), 0.0)).astype(BF16))
                cs_scaled.append((c_g * ecol).astype(BF16))
                bws.append((b_gt * w_t[h:h + 1, :]).astype(BF16))
                cds.append(ecol[q - 1:q, :])
            y_ref[:, cols] = jnp.dot(jnp.concatenate(ms + cs_scaled, axis=1), jnp.concatenate([x2, st2], axis=0),
                                     preferred_element_type=F32)
            cd = jnp.where(first[0:1, :], cds[0], cds[1])
            st_ref[:, cols] = st_p * cd + jnp.dot(jnp.concatenate(bws, axis=1), x2, preferred_element_type=F32)

    out = _gate_and_groupnorm(y_ref[...], xa_ref[:, :D_INNER], z_ref[...].astype(F32), drep_ref[...],
                              gnw_ref[...])
    yg_ref[...] = out.astype(yg_ref.dtype)

    @pl.when(c == nc - 1)
    def _():
        conv_ref[...] = tail[SUBLANES - (CONV_W - 1):, :]
        for hp in range(SSD_HEADS // 2):
            blk = st_ref[:, hp * 2 * SSD_HEADDIM:(hp + 1) * 2 * SSD_HEADDIM].T
            ssm_ref[2 * hp:2 * hp + 2] = blk.reshape(2, SSD_HEADDIM, D_STATE)


def _ssd_prompt(xbc, z, dtr, conv_w, conv_b, dtb, alog, d_rep, gn_w):
    batch, seq, _ = xbc.shape
    nb = SSD_SEQS
    blk = lambda b, c: (b, c, 0)
    fixed = lambda b, c: (0, 0)
    nbytes = nb * (2 * CHUNK * (CONV_DIM * 4 + D_INNER * 4 + LANES * 4) + (CHUNK + SUBLANES) * CONV_DIM * 4
                   + 4 * D_STATE * D_INNER * 4 + 12 * CHUNK * CONV_DIM * 4)
    return pl.pallas_call(
        _ssd_prompt_kernel,
        grid=(batch // nb, seq // CHUNK),
        in_specs=[pl.BlockSpec((nb, CHUNK, CONV_DIM), blk), pl.BlockSpec((nb, CHUNK, D_INNER), blk),
                  pl.BlockSpec((nb, CHUNK, LANES), blk),
                  pl.BlockSpec((CONV_W, CONV_DIM), fixed), pl.BlockSpec((1, CONV_DIM), fixed),
                  pl.BlockSpec((1, LANES), fixed), pl.BlockSpec((1, LANES), fixed),
                  pl.BlockSpec((1, D_INNER), fixed), pl.BlockSpec((1, D_INNER), fixed)],
        out_specs=[pl.BlockSpec((nb, CHUNK, D_INNER), blk),
                   pl.BlockSpec((nb, SSD_HEADS, SSD_HEADDIM, D_STATE), lambda b, c: (b, 0, 0, 0)),
                   pl.BlockSpec((nb, CONV_W - 1, CONV_DIM), lambda b, c: (b, 0, 0))],
        out_shape=[jax.ShapeDtypeStruct((batch, seq, D_INNER), BF16),
                   jax.ShapeDtypeStruct((batch, SSD_HEADS, SSD_HEADDIM, D_STATE), F32),
                   jax.ShapeDtypeStruct((batch, CONV_W - 1, CONV_DIM), F32)],
        scratch_shapes=[pltpu.VMEM((nb, CHUNK + SUBLANES, CONV_DIM), F32), pltpu.VMEM((nb, D_STATE, D_INNER), F32),
                        pltpu.VMEM((nb, CHUNK, D_INNER), F32), pltpu.VMEM((nb, CHUNK, CONV_DIM), F32)],
        compiler_params=_params(("arbitrary", "arbitrary"), nbytes),
        name="ssd_prompt",
    )(xbc, z, dtr, conv_w, conv_b, dtb, alog, d_rep, gn_w)


def _ssd_out_kernel(y_ref, xs_ref, z_ref, drep_ref, gnw_ref, w_ref, r_ref, g_ref, h_ref, a_ref, wb_ref):
    @pl.when(pl.program_id(0) == 0)
    def _():
        wb_ref[...] = w_ref[...].astype(BF16)

    sub = min(ROW_SUB // 2, y_ref.shape[0])
    for s in range(y_ref.shape[0] // sub):
        rows = slice(s * sub, (s + 1) * sub)
        yg = _gate_and_groupnorm(y_ref[rows, :], xs_ref[rows, :], z_ref[rows, :].astype(F32), drep_ref[...],
                                 gnw_ref[...])
        h = r_ref[rows, :] + _bdot(yg, wb_ref[...])
        h_ref[rows, :] = h
        a_ref[rows, :] = _rms(h, g_ref[...]).astype(a_ref.dtype)


def _ssd_out(y, xa, z, d_rep, gn_w, w, resid, g, tm):
    m, k = y.shape
    d = w.shape[1]
    row = lambda i: (i, 0)
    fixed = lambda i: (0, 0)
    nbytes = k * d * 6 + 2 * tm * k * 10 + 8 * tm * d * 4 + 6 * tm * k * 4
    return pl.pallas_call(
        _ssd_out_kernel,
        grid=(m // tm,),
        in_specs=[pl.BlockSpec((tm, k), row), pl.BlockSpec((tm, k), row), pl.BlockSpec((tm, k), row),
                  pl.BlockSpec((1, k), fixed), pl.BlockSpec((1, k), fixed),
                  pl.BlockSpec((k, d), fixed, pipeline_mode=pl.Buffered(1)),
                  pl.BlockSpec((tm, d), row), pl.BlockSpec((1, d), fixed)],
        out_specs=[pl.BlockSpec((tm, d), row), pl.BlockSpec((tm, d), row)],
        out_shape=[jax.ShapeDtypeStruct((m, d), F32), jax.ShapeDtypeStruct((m, d), BF16)],
        scratch_shapes=[pltpu.VMEM((k, d), BF16)],
        compiler_params=_params(("arbitrary",), nbytes),
        name="ssd_out",
    )(y, xa, z, d_rep, gn_w, w, resid, g.reshape(1, d))


def _ssd_sample_pre_kernel(xbc_ref, conv_ref, dtr_ref, cw_ref, cb_ref, dtb_ref, alog_ref,
                           nconv_ref, xa_ref, xdt_ref, da_ref, bst_ref, cst_ref):
    xn = xbc_ref[...]
    acc = cb_ref[...] + cw_ref[CONV_W - 1:CONV_W, :] * xn
    for k in range(CONV_W - 1):
        acc = acc + cw_ref[k:k + 1, :] * conv_ref[:, k * CONV_DIM:(k + 1) * CONV_DIM]
    xa = _silu(acc)
    xa_ref[...] = xa
    nconv_ref[:, 0:CONV_DIM] = conv_ref[:, CONV_DIM:2 * CONV_DIM]
    nconv_ref[:, CONV_DIM:2 * CONV_DIM] = conv_ref[:, 2 * CONV_DIM:3 * CONV_DIM]
    nconv_ref[:, 2 * CONV_DIM:3 * CONV_DIM] = xn

    dt = _softplus(dtr_ref[...] + dtb_ref[...])
    a_neg = -jnp.exp(alog_ref[...])
    hh = lax.broadcasted_iota(I32, (LANES, D_INNER), 0)
    cc = lax.broadcasted_iota(I32, (LANES, D_INNER), 1)
    expand = jnp.where(lax.shift_right_logical(cc, 6) == hh, 1.0, 0.0)
    dt_rep = jnp.dot(dt, expand, precision=HIGHEST, preferred_element_type=F32)
    xdt_ref[...] = xa[:, :D_INNER] * dt_rep
    da_ref[...] = jnp.exp(dt * a_neg)

    n = xn.shape[0]
    gs = SSD_GROUPS * D_STATE
    b1, b2, b3 = _split3(xa[:, D_INNER:D_INNER + gs])
    cm = xa[:, D_INNER + gs:]
    c1 = cm.astype(BF16)
    c2 = (cm - c1.astype(F32)).astype(BF16)
    zero = jnp.zeros((n, D_STATE), BF16)
    b_rows, c_rows = [], []
    for g in range(SSD_GROUPS):
        sl = slice(g * D_STATE, (g + 1) * D_STATE)
        b_rows += [b1[:, sl], b2[:, sl], b1[:, sl], b3[:, sl], b2[:, sl], b1[:, sl]] + [zero] * (TERM_ROWS - 6)
        c_rows += [c1[:, sl], c2[:, sl]] + [zero] * (TERM_ROWS - 2)
    bst_ref[...] = jnp.concatenate(b_rows, axis=1)
    cst_ref[...] = jnp.concatenate(c_rows, axis=1)


TERM_ROWS = 16


def _split3(v):
    v1 = v.astype(BF16)
    r1 = v - v1.astype(F32)
    v2 = r1.astype(BF16)
    v3 = (r1 - v2.astype(F32)).astype(BF16)
    return v1, v2, v3


def _ssd_sample_pre(xbc, conv_flat, dtr, conv_w, conv_b, dtb, alog):
    n = xbc.shape[0]
    full = lambda shape: pl.BlockSpec(shape, lambda i: (0,) * len(shape))
    stack = SSD_GROUPS * TERM_ROWS * D_STATE
    return pl.pallas_call(
        _ssd_sample_pre_kernel,
        grid=(1,),
        in_specs=[full((n, CONV_DIM)), full((n, 3 * CONV_DIM)), full((n, LANES)), full((CONV_W, CONV_DIM)),
                  full((1, CONV_DIM)), full((1, LANES)), full((1, LANES))],
        out_specs=[full((n, 3 * CONV_DIM)), full((n, CONV_DIM)), full((n, D_INNER)), full((n, LANES)),
                   full((n, stack)), full((n, stack))],
        out_shape=[jax.ShapeDtypeStruct((n, 3 * CONV_DIM), F32), jax.ShapeDtypeStruct((n, CONV_DIM), F32),
                   jax.ShapeDtypeStruct((n, D_INNER), F32), jax.ShapeDtypeStruct((n, LANES), F32),
                   jax.ShapeDtypeStruct((n, stack), BF16), jax.ShapeDtypeStruct((n, stack), BF16)],
        compiler_params=_params(("arbitrary",), 32 * n * CONV_DIM * 4),
        name="ssd_sample_pre",
    )(xbc, conv_flat, dtr, conv_w, conv_b, dtb, alog)


STATE_SAMPLES = 8


def _ssd_sample_state_kernel(da_ref, st_ref, xdt_ref, bst_ref, cst_ref, nst_ref, y_ref):
    first = pl.program_id(0) * STATE_SAMPLES
    rows_g = D_INNER // SSD_GROUPS
    sub = lax.broadcasted_iota(I32, (TERM_ROWS, D_INNER), 0)
    for s in range(STATE_SAMPLES):
        x1, x2, x3 = [t.astype(F32) for t in _split3(xdt_ref[s])]
        terms = jnp.where(sub == 0, x1, jnp.where(sub == 1, x1, jnp.where(sub == 2, x2, jnp.where(
            sub == 3, x1, jnp.where(sub == 4, x2, jnp.where(sub == 5, x3, 0.0)))))).astype(BF16)
        ys = []
        for g in range(SSD_GROUPS):
            tile = slice(g * TERM_ROWS, (g + 1) * TERM_ROWS)
            upd = lax.dot_general(terms[:, g * rows_g:(g + 1) * rows_g], bst_ref[s, tile, :],
                                  (((0,), (0,)), ((), ())), preferred_element_type=F32)
            halves = []
            for hh in range(rows_g // SSD_HEADDIM):
                head = g * (rows_g // SSD_HEADDIM) + hh
                rows = slice(head * SSD_HEADDIM, (head + 1) * SSD_HEADDIM)
                h_new = (da_ref[first + s, head] * st_ref[s, rows, :]
                         + upd[hh * SSD_HEADDIM:(hh + 1) * SSD_HEADDIM, :])
                nst_ref[s, rows, :] = h_new
                halves.append(h_new)
            h_g = jnp.concatenate(halves, axis=0)
            h_hi = h_g.astype(BF16)
            h_lo = (h_g - h_hi.astype(F32)).astype(BF16)
            nt = (((1,), (1,)), ((), ()))
            o_hi = lax.dot_general(cst_ref[s, tile, :], h_hi, nt, preferred_element_type=F32)
            o_lo = lax.dot_general(cst_ref[s, tile, :], h_lo, nt, preferred_element_type=F32)
            ys.append(o_hi[0:1, :] + o_hi[1:2, :] + o_lo[0:1, :])
        y_ref[s] = jnp.concatenate(ys, axis=1)


def _ssd_sample_state(state, xdt, da, bst, cst):
    n = state.shape[0]
    ns = STATE_SAMPLES
    stack_rows = SSD_GROUPS * TERM_ROWS
    return pl.pallas_call(
        _ssd_sample_state_kernel,
        grid=(n // ns,),
        in_specs=[pl.BlockSpec(memory_space=pltpu.SMEM),
                  pl.BlockSpec((ns, D_INNER, D_STATE), lambda i: (i, 0, 0)),
                  pl.BlockSpec((ns, 1, D_INNER), lambda i: (i, 0, 0)),
                  pl.BlockSpec((ns, stack_rows, D_STATE), lambda i: (i, 0, 0)),
                  pl.BlockSpec((ns, stack_rows, D_STATE), lambda i: (i, 0, 0))],
        out_specs=[pl.BlockSpec((ns, D_INNER, D_STATE), lambda i: (i, 0, 0)),
                   pl.BlockSpec((ns, 1, D_INNER), lambda i: (i, 0, 0))],
        out_shape=[jax.ShapeDtypeStruct((n, D_INNER, D_STATE), F32),
                   jax.ShapeDtypeStruct((n, 1, D_INNER), F32)],
        compiler_params=_params(("arbitrary",), 6 * ns * D_INNER * D_STATE * 4),
        name="ssd_sample_state",
    )(da, state, xdt, bst, cst)


def _layernorm(v, g, b):
    mu = jnp.mean(v, axis=-1, keepdims=True)
    d = v - mu
    var = jnp.mean(d * d, axis=-1, keepdims=True)
    return d * lax.rsqrt(var + EPS) * g + b


def _sgu_sample_kernel(u_ref, v_ref, lng_ref, lnb_ref, w0_ref, b0_ref, o_ref, vn_ref):
    vn = _layernorm(v_ref[...], lng_ref[...], lnb_ref[...])
    vn_ref[...] = vn
    o_ref[...] = (u_ref[...] * (w0_ref[...] * vn + b0_ref[...])).astype(o_ref.dtype)


def _sgu_sample(uv, ln_g, ln_b, w0_rep, b0_rep):
    n = uv.shape[0]
    fixed2 = lambda i: (0, 0)
    return pl.pallas_call(
        _sgu_sample_kernel,
        grid=(1,),
        in_specs=[pl.BlockSpec((n, D_SGU), lambda i: (0, 0)), pl.BlockSpec((n, D_SGU), lambda i: (0, 1)),
                  pl.BlockSpec((1, D_SGU), fixed2), pl.BlockSpec((1, D_SGU), fixed2),
                  pl.BlockSpec((1, D_SGU), fixed2), pl.BlockSpec((1, D_SGU), fixed2)],
        out_specs=[pl.BlockSpec((n, D_SGU), fixed2), pl.BlockSpec((n, D_SGU), fixed2)],
        out_shape=[jax.ShapeDtypeStruct((n, D_SGU), BF16), jax.ShapeDtypeStruct((n, D_SGU), F32)],
        compiler_params=_params(("arbitrary",), 12 * n * D_SGU * 4),
        name="sgu_sample",
    )(uv, uv, ln_g, ln_b, w0_rep, b0_rep)


def _experts_to_lanes(col, sub, lane):
    return jnp.sum(jnp.where(sub == lane, col, 0.0), axis=0, keepdims=True)


def _route_kernel(lg_ref, loc_ref, gate_ref, bmeta_ref, tmeta_ref, *, tile, n_valid):
    nb, _, tb = lg_ref.shape
    sub = lax.broadcasted_iota(I32, (N_EXPERTS, LANES), 0)
    lane = lax.broadcasted_iota(I32, (N_EXPERTS, LANES), 1)
    subf = lax.broadcasted_iota(I32, (N_EXPERTS, tb), 0).astype(F32)
    tok = lax.broadcasted_iota(I32, (N_EXPERTS, tb), 1)
    incl = jnp.where(lax.broadcasted_iota(I32, (tb, tb), 0) <= lax.broadcasted_iota(I32, (tb, tb), 1), 1.0, 0.0)
    neg = jnp.float32(-jnp.inf)
    none = jnp.float32(N_EXPERTS)

    def select(k):
        blk = lg_ref[k]
        l = blk[0:N_EXPERTS] + blk[N_EXPERTS:2 * N_EXPERTS]
        m1 = jnp.max(l, axis=0, keepdims=True)
        i1 = jnp.min(jnp.where(l == m1, subf, none), axis=0, keepdims=True)
        l2 = jnp.where(subf == i1, neg, l)
        m2 = jnp.max(l2, axis=0, keepdims=True)
        i2 = jnp.min(jnp.where(l2 == m2, subf, none), axis=0, keepdims=True)
        valid = (k * tb + tok) < n_valid
        sel = jnp.where(valid, jnp.where(subf == i1, 1.0, jnp.where(subf == i2, 1.0, 0.0)), 0.0)
        return m1, i1, m2, i2, valid, sel

    def up8(rows):
        return jnp.ceil(rows / SUBLANES) * SUBLANES

    def count_body(k, carry):
        return carry + jnp.sum(select(k)[5], axis=1, keepdims=True)

    counts = lax.fori_loop(0, nb, count_body, jnp.zeros((N_EXPERTS, 1), F32))
    tiles = jnp.ceil(counts / tile)
    cum_incl = jnp.sum(jnp.where(lane <= sub, _experts_to_lanes(tiles, sub, lane), 0.0), axis=1, keepdims=True)
    offset = (cum_incl - tiles) * tile
    tile_expert = jnp.sum(jnp.where(lane.astype(F32) >= cum_incl, 1.0, 0.0), axis=0, keepdims=True)
    tile_expert = jnp.minimum(tile_expert, N_EXPERTS - 1.0)
    n_used = jnp.max(cum_incl, axis=0, keepdims=True)
    region_end = _experts_to_lanes(offset + up8(counts), sub, lane)
    region_pad = _experts_to_lanes(tiles * tile - up8(counts), sub, lane)
    tmeta_ref[...] = jnp.where(sub == 0, tile_expert, jnp.where(sub == 1, n_used, jnp.where(
        sub == 2, region_end, jnp.where(sub == 3, region_pad, 0.0)))).astype(I32)

    def place_body(k, before):
        m1, i1, m2, i2, valid, sel = select(k)
        run = jnp.dot(sel, incl, preferred_element_type=F32)
        cnt = run[:, tb - 1:tb]
        lead = before - jnp.floor(before / SUBLANES) * SUBLANES
        span = up8(lead + cnt)
        lstart = jnp.sum(jnp.where(lane < sub, _experts_to_lanes(span, sub, lane), 0.0), axis=1, keepdims=True)
        local = lstart + lead + run - sel
        loc1 = jnp.sum(jnp.where(subf == i1, local, 0.0), axis=0, keepdims=True)
        loc2 = jnp.sum(jnp.where(subf == i2, local, 0.0), axis=0, keepdims=True)
        live = (k * tb + tok[0:1, :]) < n_valid
        loc_ref[k] = jnp.where(subf == 0.0, jnp.where(live, loc1, -1.0),
                               jnp.where(subf == 1.0, jnp.where(live, loc2, -1.0), 0.0))
        e = jnp.exp(m2 - m1)
        gate_ref[k] = jnp.where(subf == 0.0, 1.0 / (1.0 + e), jnp.where(subf == 1.0, e / (1.0 + e), 0.0))
        moved = jnp.where(cnt > 0.0, span, 0.0)
        bmeta_ref[k] = jnp.where(lane == 0, offset + before - lead, jnp.where(lane == 1, moved, jnp.where(
            lane == 2, lstart, jnp.where(lane == 3, lead, jnp.where(lane == 4, lead + cnt, 0.0))))).astype(I32)
        return before + cnt

    lax.fori_loop(0, nb, place_body, jnp.zeros((N_EXPERTS, 1), F32))


def _route(logits3, tile, n_valid):
    nb, _, tb = logits3.shape
    tok_blk = pl.BlockSpec((nb, N_EXPERTS, tb), lambda i: (0, 0, 0))
    return pl.pallas_call(
        functools.partial(_route_kernel, tile=tile, n_valid=n_valid),
        grid=(1,),
        in_specs=[pl.BlockSpec((nb, 2 * N_EXPERTS, tb), lambda i: (0, 0, 0))],
        out_specs=[tok_blk, tok_blk, pl.BlockSpec((nb, N_EXPERTS, LANES), lambda i: (0, 0, 0)),
                   pl.BlockSpec((N_EXPERTS, LANES), lambda i: (0, 0))],
        out_shape=[jax.ShapeDtypeStruct((nb, N_EXPERTS, tb), F32),
                   jax.ShapeDtypeStruct((nb, N_EXPERTS, tb), F32),
                   jax.ShapeDtypeStruct((nb, N_EXPERTS, LANES), I32),
                   jax.ShapeDtypeStruct((N_EXPERTS, LANES), I32)],
        compiler_params=_params(("arbitrary",), 16 * nb * N_EXPERTS * tb * 4),
        name="moe_route",
    )(logits3)


RUN_FIELDS = 5


def _run_copies(bm_ref, blk, vmem_ref, hbm_ref, sem, *, tb, to_hbm, wait):
    for e in range(N_EXPERTS):
        base = (blk * N_EXPERTS + e) * RUN_FIELDS
        start, cnt, lstart = bm_ref[base], bm_ref[base + 1], bm_ref[base + 2]
        off = 0
        size = tb
        while size >= SUBLANES:
            @pl.when((cnt & size) != 0)
            def _(size=size, off=off):
                v = vmem_ref.at[pl.ds(pl.multiple_of(lstart + off, SUBLANES), size)]
                h = hbm_ref.at[pl.ds(pl.multiple_of(start + off, SUBLANES), size)]
                cp = pltpu.make_async_copy(v, h, sem) if to_hbm else pltpu.make_async_copy(h, v, sem)
                if wait:
                    cp.wait()
                else:
                    cp.start()

            off = off + (cnt & size)
            size //= 2


def _compact_rows(tb):
    return 2 * tb + N_EXPERTS * 2 * SUBLANES + 2 * SUBLANES


def _selection(loc_ref, rows, tb):
    r = lax.broadcasted_iota(I32, (rows, tb), 0).astype(F32)
    return r == loc_ref[0, 0:1, :], r == loc_ref[0, 1:2, :]


def _zero_fill(zm_ref, zbuf_ref, xs_ref, zsem, *, tm, n_tiles, wait):
    def fill(row0, size):
        cp = pltpu.make_async_copy(zbuf_ref.at[pl.ds(0, size)],
                                   xs_ref.at[pl.ds(pl.multiple_of(row0, SUBLANES), size)], zsem)
        if wait:
            cp.wait()
        else:
            cp.start()

    for e in range(N_EXPERTS):
        end, pad = zm_ref[1 + 2 * e], zm_ref[2 + 2 * e]
        off = 0
        size = tm // 2
        while size >= SUBLANES:
            @pl.when((pad & size) != 0)
            def _(size=size, off=off):
                fill(end + off, size)

            off = off + (pad & size)
            size //= 2
    for t in range(n_tiles):
        @pl.when(t >= zm_ref[0])
        def _(t=t):
            fill(t * tm, tm)


def _moe_scatter_kernel(bm_ref, zm_ref, a_ref, loc_ref, gate_ref, xs_ref,
                        buf_ref, zbuf_ref, carry_ref, sem, zsem, *, tb, tm, n_tiles):
    blk = pl.program_id(0)
    last = pl.num_programs(0) - 1
    slot = blk % 2
    d = a_ref.shape[1]

    @pl.when(blk == 0)
    def _():
        zbuf_ref[...] = jnp.zeros_like(zbuf_ref)
        carry_ref[...] = jnp.zeros_like(carry_ref)
        _zero_fill(zm_ref, zbuf_ref, xs_ref, zsem, tm=tm, n_tiles=n_tiles, wait=False)

    def copies(block, wait):
        s = block % 2
        _run_copies(bm_ref, block, buf_ref.at[s], xs_ref, sem.at[s], tb=tb, to_hbm=True, wait=wait)

    first, second = _selection(loc_ref, _compact_rows(tb), tb)
    pick = jnp.where(first, 1.0, jnp.where(second, 1.0, 0.0)).astype(BF16)
    buf_ref[slot, :, 0:d] = jnp.dot(pick, a_ref[...], preferred_element_type=F32)
    gsel = jnp.where(first, gate_ref[0, 0:1, :], jnp.where(second, gate_ref[0, 1:2, :], 0.0))
    buf_ref[slot, :, d:d + LANES] = jnp.broadcast_to(jnp.sum(gsel, axis=1, keepdims=True),
                                                    (buf_ref.shape[1], LANES))

    row = lax.broadcasted_iota(I32, (SUBLANES, buf_ref.shape[2]), 0)
    for e in range(N_EXPERTS):
        base = (blk * N_EXPERTS + e) * RUN_FIELDS
        lstart, total = bm_ref[base + 2], bm_ref[base + 4]
        head = pl.ds(pl.multiple_of(lstart, SUBLANES), SUBLANES)
        buf_ref[slot, head, :] = buf_ref[slot, head, :] + carry_ref[e]
        full = lax.shift_left(lax.shift_right_logical(total, 3), 3)
        tail = pl.ds(pl.multiple_of(lstart + full, SUBLANES), SUBLANES)
        carry_ref[e] = jnp.where(row < total - full, buf_ref[slot, tail, :], 0.0)

    @pl.when(blk >= 1)
    def _():
        copies(blk - 1, True)

    copies(blk, False)

    @pl.when(blk == last)
    def _():
        copies(blk, True)
        _zero_fill(zm_ref, zbuf_ref, xs_ref, zsem, tm=tm, n_tiles=n_tiles, wait=True)


def _moe_scatter(a, loc, gate, bmeta, zmeta, n_slots, tb, tm):
    m, d = a.shape
    width = d + LANES
    tok_spec = pl.BlockSpec((1, N_EXPERTS, tb), lambda i, bm, zm: (i, 0, 0))
    grid_spec = pltpu.PrefetchScalarGridSpec(
        num_scalar_prefetch=2,
        grid=(m // tb,),
        in_specs=[pl.BlockSpec((tb, d), lambda i, bm, zm: (i, 0)), tok_spec, tok_spec],
        out_specs=pl.BlockSpec(memory_space=pl.ANY),
        scratch_shapes=[pltpu.VMEM((2, _compact_rows(tb), width), F32), pltpu.VMEM((tm, width), F32),
                        pltpu.VMEM((N_EXPERTS, SUBLANES, width), F32),
                        pltpu.SemaphoreType.DMA((2,)), pltpu.SemaphoreType.DMA(())],
    )
    return pl.pallas_call(
        functools.partial(_moe_scatter_kernel, tb=tb, tm=tm, n_tiles=n_slots // tm),
        grid_spec=grid_spec,
        out_shape=jax.ShapeDtypeStruct((n_slots, width), F32),
        compiler_params=_params(("arbitrary",), 16 * tb * d * 4),
        name="moe_scatter",
    )(bmeta, zmeta, a, loc, gate)


def _new_weights(te_ref):
    i = pl.program_id(1)
    return (i == 0) | (te_ref[i] != te_ref[jnp.maximum(i - 1, 0)])


def _moe_gu_kernel(te_ref, nu_ref, x_ref, wg_ref, wu_ref, o_ref, wgb_ref, wub_ref):
    i = pl.program_id(1)

    @pl.when(_new_weights(te_ref))
    def _():
        wgb_ref[...] = wg_ref[0].astype(BF16)
        wub_ref[...] = wu_ref[0].astype(BF16)

    @pl.when(i < nu_ref[0])
    def _():
        x = x_ref[...].astype(BF16)
        for c in range(o_ref.shape[1] // MXU_COLS):
            cols = slice(c * MXU_COLS, (c + 1) * MXU_COLS)
            g = jnp.dot(x, wgb_ref[:, cols], preferred_element_type=F32)
            u = jnp.dot(x, wub_ref[:, cols], preferred_element_type=F32)
            o_ref[:, cols] = (_silu(g) * u).astype(o_ref.dtype)

    @pl.when(i >= nu_ref[0])
    def _():
        o_ref[...] = jnp.zeros_like(o_ref)


def _used_tile(i, nu):
    return jnp.minimum(i, nu[0] - 1)


def _moe_gu(xs, w_gu, te, nu, tm, tn):
    s = xs.shape[0]
    k = w_gu.shape[1]
    f = D_FF_EXPERT
    nbytes = 2 * (2 * k * tn * 4 + k * tn * 2) + 2 * tm * k * 4 + 2 * tm * tn * 2 + 4 * tm * MXU_COLS * 4
    grid_spec = pltpu.PrefetchScalarGridSpec(
        num_scalar_prefetch=2,
        grid=(f // tn, s // tm),
        in_specs=[pl.BlockSpec((tm, k), lambda j, i, te, nu: (_used_tile(i, nu), 0)),
                  pl.BlockSpec((1, k, tn), lambda j, i, te, nu: (te[i], 0, j)),
                  pl.BlockSpec((1, k, tn), lambda j, i, te, nu: (te[i], 0, j + f // tn))],
        out_specs=pl.BlockSpec((tm, tn), lambda j, i, te, nu: (i, j)),
        scratch_shapes=[pltpu.VMEM((k, tn), BF16), pltpu.VMEM((k, tn), BF16)],
    )
    return pl.pallas_call(
        _moe_gu_kernel,
        grid_spec=grid_spec,
        out_shape=jax.ShapeDtypeStruct((s, f), BF16),
        compiler_params=_params(("arbitrary", "arbitrary"), nbytes),
        name="moe_gate_up",
    )(te, nu, xs, w_gu, w_gu)


def _moe_down_kernel(te_ref, nu_ref, x_ref, w_ref, gs_ref, o_ref, wb_ref):
    i = pl.program_id(1)

    @pl.when(_new_weights(te_ref))
    def _():
        wb_ref[...] = w_ref[0].astype(BF16)

    @pl.when(i < nu_ref[0])
    def _():
        o_ref[...] = gs_ref[:, 0:1] * jnp.dot(x_ref[...], wb_ref[...], preferred_element_type=F32)

    @pl.when(i >= nu_ref[0])
    def _():
        o_ref[...] = jnp.zeros_like(o_ref)


def _moe_down(act, w_down, xs, te, nu, tm, tn):
    s, k = act.shape
    d = w_down.shape[2]
    gate_block = d // LANES
    nbytes = 2 * k * tn * 4 + k * tn * 2 + 2 * tm * k * 2 + 3 * tm * tn * 4
    grid_spec = pltpu.PrefetchScalarGridSpec(
        num_scalar_prefetch=2,
        grid=(d // tn, s // tm),
        in_specs=[pl.BlockSpec((tm, k), lambda j, i, te, nu: (_used_tile(i, nu), 0)),
                  pl.BlockSpec((1, k, tn), lambda j, i, te, nu: (te[i], 0, j)),
                  pl.BlockSpec((tm, LANES), lambda j, i, te, nu: (_used_tile(i, nu), gate_block))],
        out_specs=pl.BlockSpec((tm, tn), lambda j, i, te, nu: (i, j)),
        scratch_shapes=[pltpu.VMEM((k, tn), BF16)],
    )
    return pl.pallas_call(
        _moe_down_kernel,
        grid_spec=grid_spec,
        out_shape=jax.ShapeDtypeStruct((s, d), F32),
        compiler_params=_params(("arbitrary", "arbitrary"), nbytes),
        name="moe_down",
    )(te, nu, act, w_down, xs)


def _moe_combine_kernel(bm_ref, h_ref, locc_ref, g_ref, ys_ref, op_ref, os_ref, ybuf_ref, sem, *, tb, prompt_blocks):
    blk = pl.program_id(0)
    slot = blk % 2

    def copies(block, wait):
        s = block % 2
        _run_copies(bm_ref, block, ybuf_ref.at[s], ys_ref, sem.at[s], tb=tb, to_hbm=False, wait=wait)

    @pl.when(blk == 0)
    def _():
        ybuf_ref[...] = jnp.zeros_like(ybuf_ref)
        copies(blk, False)

    @pl.when(blk + 1 < pl.num_programs(0))
    def _():
        copies(blk + 1, False)

    r = lax.broadcasted_iota(I32, (tb, _compact_rows(tb)), 1).astype(F32)
    pick = jnp.where(r == locc_ref[:, 0:1], 1.0, jnp.where(r == locc_ref[:, 1:2], 1.0, 0.0)).astype(BF16)
    copies(blk, True)
    moe = jnp.dot(pick, ybuf_ref[slot].astype(BF16), preferred_element_type=F32)
    out = _rms(h_ref[...] + moe, g_ref[...])

    @pl.when(blk < prompt_blocks)
    def _():
        op_ref[...] = out

    @pl.when(blk == prompt_blocks)
    def _():
        os_ref[...] = out


def _moe_combine(h, locc, bmeta, ys, g, tb, n_prompt):
    m, d = h.shape
    prompt_blocks = n_prompt // tb
    any_spec = pl.BlockSpec(memory_space=pl.ANY)
    grid_spec = pltpu.PrefetchScalarGridSpec(
        num_scalar_prefetch=1,
        grid=(m // tb,),
        in_specs=[pl.BlockSpec((tb, d), lambda i, bm: (i, 0)),
                  pl.BlockSpec((tb, N_EXPERTS), lambda i, bm: (i, 0)),
                  pl.BlockSpec((1, d), lambda i, bm: (0, 0)),
                  any_spec],
        out_specs=[pl.BlockSpec((tb, d), lambda i, bm: (jnp.minimum(i, prompt_blocks - 1), 0)),
                   pl.BlockSpec((tb, d), lambda i, bm: (0, 0))],
        scratch_shapes=[pltpu.VMEM((2, _compact_rows(tb), d), F32), pltpu.SemaphoreType.DMA((2,))],
    )
    return pl.pallas_call(
        functools.partial(_moe_combine_kernel, tb=tb, prompt_blocks=prompt_blocks),
        grid_spec=grid_spec,
        out_shape=[jax.ShapeDtypeStruct((n_prompt, d), F32), jax.ShapeDtypeStruct((tb, d), F32)],
        compiler_params=_params(("arbitrary",), 16 * tb * d * 4),
        name="moe_combine",
    )(bmeta, h, locc, g.reshape(1, d), ys)


MOE_TOKEN_BLOCK = 256
MOE_ROW_TILE = 512


def _moe(h, a, lg_parts, n_valid, n_prompt, w_gu, w_down, g_final):
    m, d = h.shape
    tb, tm = MOE_TOKEN_BLOCK, MOE_ROW_TILE
    nb = m // tb
    lg3 = lg_parts[:, :2 * N_EXPERTS].reshape(nb, tb, 2 * N_EXPERTS).transpose(0, 2, 1)
    n_tiles = pl.cdiv(2 * n_valid, tm) + N_EXPERTS
    loc, gate, bmeta, tmeta = _route(lg3, tm, n_valid)
    te = tmeta[0, :n_tiles]
    nu = tmeta[1, 0:1]
    bm = bmeta[:, :, :RUN_FIELDS].reshape(-1)
    zm = jnp.concatenate([nu, tmeta[2:4, :N_EXPERTS].T.reshape(-1)])
    locc = loc.transpose(0, 2, 1).reshape(m, N_EXPERTS)
    xs = _moe_scatter(a, loc, gate, bm, zm, n_tiles * tm, tb, tm)
    act = _moe_gu(xs, w_gu, te, nu, tm, D_FF_EXPERT // 2)
    ys = _moe_down(act, w_down, xs, te, nu, tm, d)
    return _moe_combine(h, locc, bm, ys, g_final, tb, n_prompt)


def _pad_lanes(v):
    return jnp.pad(v.reshape(1, -1), ((0, 0), (0, LANES - v.shape[-1])))


def kernel(x_prompt, x_sample, state_ssm, state_conv, norm_mix, norm_ffn, norm_final, ssd_w_in, ssd_conv_w,
           ssd_conv_b, ssd_dt_bias, ssd_a_log, ssd_d, ssd_gnorm, ssd_w_out, sgu_w_in, sgu_b_in, sgu_ln_g,
           sgu_ln_b, sgu_w_s, sgu_b_s, sgu_w_out, ffn_w_gu, ffn_w_down, moe_w_router, moe_w_gu, moe_w_down):
    batch, seq, d = x_prompt.shape
    n_s = x_sample.shape[0]
    mp = batch * seq
    xp = x_prompt.reshape(mp, d)
    xs = x_sample.reshape(n_s, d)

    w_in = ssd_w_in[0].T
    w_dt = jnp.pad(w_in[D_INNER + CONV_DIM:], ((0, LANES - SSD_HEADS), (0, 0)))
    dtb = _pad_lanes(ssd_dt_bias[0])
    alog = _pad_lanes(ssd_a_log[0])
    d_rep = jnp.repeat(ssd_d[0], SSD_HEADDIM).reshape(1, D_INNER)
    gn_w = ssd_gnorm[0].reshape(1, D_INNER)
    conv_w = ssd_conv_w[0]
    conv_b = ssd_conv_b[0].reshape(1, CONV_DIM)
    ln_g = sgu_ln_g[0].reshape(1, D_SGU)
    ln_b = sgu_ln_b[0].reshape(1, D_SGU)
    bs_t = jnp.pad(sgu_b_s[0].T, ((0, 0), (0, LANES - SGU_HEADS)))
    w0_rep = jnp.repeat(sgu_w_s[0][:, 0, 0], SGU_HEAD_DIM).reshape(1, D_SGU)
    b0_rep = jnp.repeat(sgu_b_s[0][:, 0], SGU_HEAD_DIM).reshape(1, D_SGU)
    wr_hi = moe_w_router[0].astype(BF16)
    wr_lo = (moe_w_router[0] - wr_hi.astype(F32)).astype(BF16)
    w3 = jnp.concatenate([wr_hi, wr_lo, jnp.zeros((d, LANES - 2 * N_EXPERTS), BF16)], axis=1)

    def dense_ffn(h1, a1, tm):
        act = _mm_swiglu(a1, ffn_w_gu[0], D_FF_DENSE, min(2 * tm, a1.shape[0]), 1408, "ffn_gate_up")
        return _mm_resnorm(act, ffn_w_down[0], h1, norm_mix[1], tm, "ffn_down")

    a0p, zp, dtrp = _norm_in_proj(xp, norm_mix[0], w_in, w_dt, 1024)
    xbcp = _mm_plain(a0p, w_in, D_INNER, CONV_DIM, 1024, 1024, F32, "ssd_in_xbc")
    per_seq = lambda t: t.reshape(batch, seq, t.shape[-1])
    ygp, ssm_p, conv_p = _ssd_prompt(per_seq(xbcp), per_seq(zp), per_seq(dtrp), conv_w, conv_b, dtb, alog, d_rep,
                                     gn_w)
    h1p, a1p = _mm_resnorm(ygp.reshape(mp, D_INNER), ssd_w_out[0], xp, norm_ffn[0], 512, "ssd_out")
    h2p, a2p = dense_ffn(h1p, a1p, 512)

    a0s, zs, dtrs = _norm_in_proj(xs, norm_mix[0], w_in, w_dt, n_s)
    xbcs = _mm_plain(a0s, w_in, D_INNER, CONV_DIM, n_s, 1024, F32, "ssd_in_xbc")
    conv_s, xas, xdts, das, bsts, csts = _ssd_sample_pre(xbcs, state_conv[0].reshape(n_s, 3 * CONV_DIM), dtrs,
                                                         conv_w, conv_b, dtb, alog)
    stack_rows = SSD_GROUPS * TERM_ROWS
    ssm_s, ys3 = _ssd_sample_state(state_ssm[0].reshape(n_s, D_INNER, D_STATE), xdts.reshape(n_s, 1, D_INNER),
                                   das[:, :SSD_HEADS], bsts.reshape(n_s, stack_rows, D_STATE),
                                   csts.reshape(n_s, stack_rows, D_STATE))
    h1s, a1s = _ssd_out(ys3.reshape(n_s, D_INNER), xas, zs, d_rep, gn_w, ssd_w_out[0], xs, norm_ffn[0], n_s)
    h2s, a2s = dense_ffn(h1s, a1s, n_s)

    uvp = _mm_bias_gelu(a2p, sgu_w_in[0], sgu_b_in[0], 1024, 2048, BF16, "sgu_in")
    uvs = _mm_bias_gelu(a2s, sgu_w_in[0], sgu_b_in[0], n_s, 1024, F32, "sgu_in")
    ss, v_s = _sgu_sample(uvs, ln_g, ln_b, w0_rep, b0_rep)
    tm_out = 512
    pad_rows = ((0, tm_out - n_s), (0, 0))
    h3, a3, lg = _sgu_out(uvp, jnp.pad(ss, pad_rows), ln_g, ln_b, sgu_w_s[0], bs_t, sgu_w_out[0], h2p,
                          jnp.pad(h2s, pad_rows), norm_ffn[1], w3, tm_out)

    y_prompt, y_tail = _moe(h3, a3, lg, mp + n_s, mp, moe_w_gu[0], moe_w_down[0], norm_final)
    y_sample = y_tail[:n_s]

    return (y_prompt.reshape(batch, seq, d),
            y_sample.reshape(n_s, 1, d),
            ssm_p[None],
            conv_p[None],
            ssm_s.reshape(1, n_s, SSD_HEADS, SSD_HEADDIM, D_STATE),
            conv_s.reshape(1, n_s, CONV_W - 1, CONV_DIM),
            v_s.reshape(1, n_s, 1, D_SGU))
```

```python
import functools

import jax
import jax.numpy as jnp
from jax import lax
from jax.experimental import pallas as pl
from jax.experimental.pallas import tpu as pltpu

F32 = jnp.float32
BF16 = jnp.bfloat16
I32 = jnp.int32
HIGHEST = lax.Precision.HIGHEST

D_MODEL = 1024
D_INNER = 2048
SSD_HEADS = 32
SSD_HEADDIM = 64
SSD_GROUPS = 4
D_STATE = 128
CONV_W = 4
CONV_DIM = D_INNER + 2 * SSD_GROUPS * D_STATE
CHUNK = 128
D_SGU = 2048
SGU_HEADS = 8
SGU_HEAD_DIM = D_SGU // SGU_HEADS
D_FF_DENSE = 2816
N_EXPERTS = 8
D_FF_EXPERT = 3584
EPS = 1e-5

MXU_COLS = 256
ROW_SUB = 256
LANES = 128
SUBLANES = 8
VMEM_CAP = 60000 * 1024
VMEM_FLOOR = 32 * 1024 * 1024


def _vmem_limit(nbytes):
    return int(min(max(nbytes * 5 // 4 + (4 << 20), VMEM_FLOOR), VMEM_CAP))


def _params(sem, nbytes):
    return pltpu.CompilerParams(dimension_semantics=sem, vmem_limit_bytes=_vmem_limit(nbytes))


LOG2E = 1.4426950408889634


def _silu(x):
    return x / (1.0 + jnp.exp2(x * (-LOG2E)))


def _rms(x, g):
    return x * lax.rsqrt(jnp.mean(x * x, axis=-1, keepdims=True) + EPS) * g


def _bdot(a, b):
    return jnp.dot(a.astype(BF16), b.astype(BF16), preferred_element_type=F32)


def _cast_transposed(wt_ref, wb_ref):
    step = 2 * MXU_COLS
    for r0 in range(0, wt_ref.shape[0], step):
        r1 = min(r0 + step, wt_ref.shape[0])
        wb_ref[:, r0:r1] = wt_ref[r0:r1, :].T.astype(BF16)


def _mm_plain_kernel(x_ref, wt_ref, o_ref, wb_ref):
    @pl.when(pl.program_id(1) == 0)
    def _():
        _cast_transposed(wt_ref, wb_ref)

    o_ref[...] = _bdot(x_ref[...], wb_ref[...]).astype(o_ref.dtype)


def _mm_bias_gelu_kernel(x_ref, w_ref, b_ref, o_ref, wb_ref):
    @pl.when(pl.program_id(1) == 0)
    def _():
        wb_ref[...] = w_ref[...].astype(BF16)

    x = x_ref[...].astype(BF16)
    for c0 in range(0, o_ref.shape[1], 2 * MXU_COLS):
        cols = slice(c0, c0 + 2 * MXU_COLS)
        y = jnp.dot(x, wb_ref[:, cols], preferred_element_type=F32) + b_ref[:, cols]
        o_ref[:, cols] = (0.5 * y * (1.0 + lax.erf(y * (2.0 ** -0.5)))).astype(o_ref.dtype)


def _mm_swiglu_kernel(x_ref, wg_ref, wu_ref, o_ref, wgb_ref, wub_ref):
    @pl.when(pl.program_id(1) == 0)
    def _():
        wgb_ref[...] = wg_ref[...].astype(BF16)
        wub_ref[...] = wu_ref[...].astype(BF16)

    x = x_ref[...].astype(BF16)
    tn = o_ref.shape[1]
    for c0 in range(0, tn, MXU_COLS):
        cols = slice(c0, min(c0 + MXU_COLS, tn))
        g = jnp.dot(x, wgb_ref[:, cols], preferred_element_type=F32)
        u = jnp.dot(x, wub_ref[:, cols], preferred_element_type=F32)
        o_ref[:, cols] = (_silu(g) * u).astype(o_ref.dtype)


def _wspec(k, tn, col_block0):
    return pl.BlockSpec((k, tn), lambda j, i: (0, j + col_block0), pipeline_mode=pl.Buffered(1))


def _mm_plain(x, wt, col0, n, tm, tn, out_dtype, name):
    m, k = x.shape
    nbytes = k * tn * 6 + 2 * tm * k * x.dtype.itemsize + 2 * tm * tn * 4 + 2 * MXU_COLS * k * 8
    row_block0 = col0 // tn
    return pl.pallas_call(
        _mm_plain_kernel,
        grid=(n // tn, m // tm),
        in_specs=[pl.BlockSpec((tm, k), lambda j, i: (i, 0)),
                  pl.BlockSpec((tn, k), lambda j, i: (j + row_block0, 0), pipeline_mode=pl.Buffered(1))],
        out_specs=pl.BlockSpec((tm, tn), lambda j, i: (i, j)),
        out_shape=jax.ShapeDtypeStruct((m, n), out_dtype),
        scratch_shapes=[pltpu.VMEM((k, tn), BF16)],
        compiler_params=_params(("arbitrary", "arbitrary"), nbytes),
        name=name,
    )(x, wt)


def _norm_in_proj_kernel(x_ref, g_ref, wz_ref, wdt_ref, a_ref, z_ref, dtr_ref, wzb_ref, wdtb_ref):
    @pl.when(pl.program_id(0) == 0)
    def _():
        _cast_transposed(wz_ref, wzb_ref)
        _cast_transposed(wdt_ref, wdtb_ref)

    a = _rms(x_ref[...], g_ref[...]).astype(BF16)
    a_ref[...] = a
    for c0 in range(0, z_ref.shape[1], 2 * MXU_COLS):
        cols = slice(c0, c0 + 2 * MXU_COLS)
        z_ref[:, cols] = _silu(jnp.dot(a, wzb_ref[:, cols], preferred_element_type=F32)).astype(z_ref.dtype)
    dtr_ref[...] = jnp.dot(a, wdtb_ref[...], preferred_element_type=F32)


def _norm_in_proj(x, g, w_in_t, w_dt_t, tm):
    m, k = x.shape
    n = D_INNER
    row = lambda i: (i, 0)
    fixed = lambda i: (0, 0)
    nbytes = k * n * 6 + k * LANES * 6 + 2 * tm * k * 4 + 2 * tm * k * 2 + 2 * tm * n * 2 + 2 * tm * n * 4
    return pl.pallas_call(
        _norm_in_proj_kernel,
        grid=(m // tm,),
        in_specs=[pl.BlockSpec((tm, k), row), pl.BlockSpec((1, k), fixed),
                  pl.BlockSpec((n, k), fixed, pipeline_mode=pl.Buffered(1)),
                  pl.BlockSpec((LANES, k), fixed)],
        out_specs=[pl.BlockSpec((tm, k), row), pl.BlockSpec((tm, n), row), pl.BlockSpec((tm, LANES), row)],
        out_shape=[jax.ShapeDtypeStruct((m, k), BF16), jax.ShapeDtypeStruct((m, n), BF16),
                   jax.ShapeDtypeStruct((m, LANES), F32)],
        scratch_shapes=[pltpu.VMEM((k, n), BF16), pltpu.VMEM((k, LANES), BF16)],
        compiler_params=_params(("arbitrary",), nbytes),
        name="norm_in_proj",
    )(x, g.reshape(1, k), w_in_t, w_dt_t)


def _mm_bias_gelu(x, w, b, tm, tn, out_dtype, name):
    m, k = x.shape
    n = w.shape[1]
    nbytes = k * tn * 6 + 2 * tm * k * x.dtype.itemsize + 2 * tm * tn * 4
    return pl.pallas_call(
        _mm_bias_gelu_kernel,
        grid=(n // tn, m // tm),
        in_specs=[pl.BlockSpec((tm, k), lambda j, i: (i, 0)), _wspec(k, tn, 0),
                  pl.BlockSpec((1, tn), lambda j, i: (0, j))],
        out_specs=pl.BlockSpec((tm, tn), lambda j, i: (i, j)),
        out_shape=jax.ShapeDtypeStruct((m, n), out_dtype),
        scratch_shapes=[pltpu.VMEM((k, tn), BF16)],
        compiler_params=_params(("arbitrary", "arbitrary"), nbytes),
        name=name,
    )(x, w, b.reshape(1, n))


def _mm_swiglu(x, w_gu, d_ff, tm, tn, name):
    m, k = x.shape
    nbytes = 2 * k * tn * 6 + 2 * tm * k * x.dtype.itemsize + 2 * tm * tn * 2 + 3 * tm * tn * 4
    return pl.pallas_call(
        _mm_swiglu_kernel,
        grid=(d_ff // tn, m // tm),
        in_specs=[pl.BlockSpec((tm, k), lambda j, i: (i, 0)), _wspec(k, tn, 0), _wspec(k, tn, d_ff // tn)],
        out_specs=pl.BlockSpec((tm, tn), lambda j, i: (i, j)),
        out_shape=jax.ShapeDtypeStruct((m, d_ff), BF16),
        scratch_shapes=[pltpu.VMEM((k, tn), BF16), pltpu.VMEM((k, tn), BF16)],
        compiler_params=_params(("arbitrary", "arbitrary"), nbytes),
        name=name,
    )(x, w_gu, w_gu)


def _mm_resnorm_kernel(x_ref, w_ref, r_ref, g_ref, h_ref, a_ref, wb_ref):
    @pl.when(pl.program_id(0) == 0)
    def _():
        wb_ref[...] = w_ref[...].astype(BF16)

    h = r_ref[...] + _bdot(x_ref[...], wb_ref[...])
    h_ref[...] = h
    a_ref[...] = _rms(h, g_ref[...]).astype(a_ref.dtype)


def _mm_resnorm(x, w, resid, g, tm, name):
    m, k = x.shape
    d = w.shape[1]
    nbytes = k * d * 6 + 2 * tm * k * 2 + 8 * tm * d * 4
    return pl.pallas_call(
        _mm_resnorm_kernel,
        grid=(m // tm,),
        in_specs=[pl.BlockSpec((tm, k), lambda i: (i, 0)),
                  pl.BlockSpec((k, d), lambda i: (0, 0), pipeline_mode=pl.Buffered(1)),
                  pl.BlockSpec((tm, d), lambda i: (i, 0)),
                  pl.BlockSpec((1, d), lambda i: (0, 0))],
        out_specs=[pl.BlockSpec((tm, d), lambda i: (i, 0)), pl.BlockSpec((tm, d), lambda i: (i, 0))],
        out_shape=[jax.ShapeDtypeStruct((m, d), F32), jax.ShapeDtypeStruct((m, d), BF16)],
        scratch_shapes=[pltpu.VMEM((k, d), BF16)],
        compiler_params=_params(("arbitrary",), nbytes),
        name=name,
    )(x, w, resid, g.reshape(1, d))


def _sgu_out_kernel(u_ref, v_ref, xt_ref, lng_ref, lnb_ref, ws_ref, bst_ref, w_ref, rp_ref, rt_ref, g_ref, w3_ref,
                    h_ref, a_ref, lg_ref, wb_ref, wsb_ref, *, prompt_tiles):
    i = pl.program_id(0)
    q = CHUNK

    @pl.when(i == 0)
    def _():
        wb_ref[...] = w_ref[...].astype(BF16)
        causal = lax.broadcasted_iota(I32, (q, q), 0) >= lax.broadcasted_iota(I32, (q, q), 1)
        for g in range(SGU_HEADS):
            wsb_ref[g] = jnp.where(causal, ws_ref[g], 0.0).astype(BF16)

    def project(x, r_ref, rows):
        h = r_ref[rows, :] + jnp.dot(x, wb_ref[...], preferred_element_type=F32)
        h_ref[rows, :] = h
        a = _rms(h, g_ref[...])
        a_hi = a.astype(BF16)
        a_lo = (a - a_hi.astype(F32)).astype(BF16)
        a_ref[rows, :] = a_hi
        lg_ref[rows, :] = (jnp.dot(a_hi, w3_ref[...], preferred_element_type=F32)
                           + jnp.dot(a_lo, w3_ref[...], preferred_element_type=F32))

    chunks = [slice(c * q, (c + 1) * q) for c in range(u_ref.shape[0] // q)]

    @pl.when(i < prompt_tiles)
    def _():
        for rows in chunks:
            vn = _layernorm(v_ref[rows, :].astype(F32), lng_ref[...], lnb_ref[...]).astype(BF16)
            gated = []
            for g in range(SGU_HEADS):
                cols = slice(g * SGU_HEAD_DIM, (g + 1) * SGU_HEAD_DIM)
                s = jnp.dot(wsb_ref[g], vn[:, cols], preferred_element_type=F32) + bst_ref[:, g:g + 1]
                gated.append((u_ref[rows, cols].astype(F32) * s).astype(BF16))
            project(jnp.concatenate(gated, axis=1), rp_ref, rows)

    @pl.when(i == prompt_tiles)
    def _():
        for rows in chunks:
            project(xt_ref[rows, :], rt_ref, rows)


def _sgu_out(uv, xt, ln_g, ln_b, w_s, bs_t, w, rp, rt, g, w3, tm):
    mp = uv.shape[0]
    k, d = w.shape
    prompt_tiles = mp // tm
    last = prompt_tiles - 1
    rows = mp + tm
    prow = lambda i: (jnp.minimum(i, last), 0)
    fixed = lambda i: (0, 0)
    nbytes = k * d * 6 + 8 * tm * k * 2 + 12 * tm * d * 4 + 8 * CHUNK * k * 4
    return pl.pallas_call(
        functools.partial(_sgu_out_kernel, prompt_tiles=prompt_tiles),
        grid=(prompt_tiles + 1,),
        in_specs=[pl.BlockSpec((tm, k), prow),
                  pl.BlockSpec((tm, k), lambda i: (jnp.minimum(i, last), 1)),
                  pl.BlockSpec((tm, k), fixed),
                  pl.BlockSpec((1, k), fixed), pl.BlockSpec((1, k), fixed),
                  pl.BlockSpec((SGU_HEADS, CHUNK, CHUNK), lambda i: (0, 0, 0)),
                  pl.BlockSpec((CHUNK, LANES), fixed),
                  pl.BlockSpec((k, d), fixed, pipeline_mode=pl.Buffered(1)),
                  pl.BlockSpec((tm, d), prow),
                  pl.BlockSpec((tm, d), fixed),
                  pl.BlockSpec((1, d), fixed),
                  pl.BlockSpec((d, LANES), fixed)],
        out_specs=[pl.BlockSpec((tm, d), lambda i: (i, 0)), pl.BlockSpec((tm, d), lambda i: (i, 0)),
                   pl.BlockSpec((tm, LANES), lambda i: (i, 0))],
        out_shape=[jax.ShapeDtypeStruct((rows, d), F32), jax.ShapeDtypeStruct((rows, d), BF16),
                   jax.ShapeDtypeStruct((rows, LANES), F32)],
        scratch_shapes=[pltpu.VMEM((k, d), BF16), pltpu.VMEM((SGU_HEADS, CHUNK, CHUNK), BF16)],
        compiler_params=_params(("arbitrary",), nbytes),
        name="sgu_out",
    )(uv, uv, xt, ln_g, ln_b, w_s, bs_t, w, rp, rt, g.reshape(1, d), w3)


def _gate_and_groupnorm(y, xs, gate, d_rep, gn_w):
    yg = (y + d_rep * xs) * gate
    gw = D_INNER // SSD_GROUPS
    outs = []
    for g in range(SSD_GROUPS):
        blk = yg[:, g * gw:(g + 1) * gw]
        outs.append(blk * lax.rsqrt(jnp.mean(blk * blk, axis=-1, keepdims=True) + EPS))
    return jnp.concatenate(outs, axis=1) * gn_w


def _softplus(x):
    return jnp.maximum(x, 0.0) + jnp.log(1.0 + jnp.exp(-jnp.abs(x)))


SSD_SEQS = 2


def _ssd_prompt_kernel(xbc_ref, z_ref, dtr_ref, cw_ref, cb_ref, dtb_ref, alog_ref, drep_ref, gnw_ref,
                       yg_ref, ssm_ref, conv_ref, ext_ref, st_ref, y_ref, xa_ref):
    for n in range(SSD_SEQS):
        _ssd_chunk(xbc_ref.at[n], z_ref.at[n], dtr_ref.at[n], cw_ref, cb_ref, dtb_ref, alog_ref, drep_ref, gnw_ref,
                   yg_ref.at[n], ssm_ref.at[n], conv_ref.at[n], ext_ref.at[n], st_ref.at[n], y_ref.at[n],
                   xa_ref.at[n])


def _ssd_chunk(xbc_ref, z_ref, dtr_ref, cw_ref, cb_ref, dtb_ref, alog_ref, drep_ref, gnw_ref,
               yg_ref, ssm_ref, conv_ref, ext_ref, st_ref, y_ref, xa_ref):
    c = pl.program_id(1)
    nc = pl.num_programs(1)
    q = CHUNK

    @pl.when(c == 0)
    def _():
        ext_ref[0:SUBLANES, :] = jnp.zeros((SUBLANES, CONV_DIM), F32)
        st_ref[...] = jnp.zeros_like(st_ref)

    ext_ref[SUBLANES:SUBLANES + q, :] = xbc_ref[...]
    acc = cb_ref[...] + cw_ref[CONV_W - 1:CONV_W, :] * ext_ref[SUBLANES:SUBLANES + q, :]
    for k in range(CONV_W - 1):
        lo = SUBLANES - (CONV_W - 1) + k
        acc = acc + cw_ref[k:k + 1, :] * ext_ref[lo:lo + q, :]
    xa_ref[...] = _silu(acc)
    tail = ext_ref[q:q + SUBLANES, :]
    ext_ref[0:SUBLANES, :] = tail

    dt = _softplus(dtr_ref[...] + dtb_ref[...])
    a_neg = -jnp.exp(alog_ref[...])
    causal = lax.broadcasted_iota(I32, (q, q), 0) >= lax.broadcasted_iota(I32, (q, q), 1)
    tril = jnp.where(causal, 1.0, 0.0)
    cs = jnp.dot(tril, dt * a_neg, precision=HIGHEST, preferred_element_type=F32) * LOG2E
    cs_t = cs.T
    dt_t = dt.T
    rowp = cs_t - jnp.log2(dt_t)
    w_t = dt_t * jnp.exp2(cs_t[:, q - 1:q] - cs_t)
    first = lax.broadcasted_iota(I32, (q, 2 * SSD_HEADDIM), 1) < SSD_HEADDIM

    def per_head(v):
        zero = jnp.zeros_like(v)
        return jnp.concatenate([jnp.where(first, v, zero), jnp.where(first, zero, v)], axis=0)

    for g in range(SSD_GROUPS):
        b_g = xa_ref[:, D_INNER + g * D_STATE:D_INNER + (g + 1) * D_STATE]
        c_g = xa_ref[:, D_INNER + (SSD_GROUPS + g) * D_STATE:D_INNER + (SSD_GROUPS + g + 1) * D_STATE]
        b_gt = b_g.T
        cb = _bdot(c_g, b_gt)
        for pr in range(4):
            hp = g * 4 + pr
            cols = slice(hp * 2 * SSD_HEADDIM, (hp + 1) * 2 * SSD_HEADDIM)
            x2 = per_head(xa_ref[:, cols].astype(BF16))
            st_p = st_ref[:, cols]
            st2 = per_head(st_p.astype(BF16))
            ms, cs_scaled, bws, cds = [], [], [], []
            for s in range(2):
                h = 2 * hp + s
                colb = jnp.broadcast_to(cs[:, h:h + 1], (q, q))
                rowb = jnp.broadcast_to(rowp[h:h + 1, :], (q, q))
                ecol = jnp.exp2(colb)
                ms.append((cb * jnp.where(causal, jnp.exp2(colb - rowb), 0.0)).astype(BF16))
                cs_scaled.append((c_g * ecol).astype(BF16))
                bws.append((b_gt * w_t[h:h + 1, :]).astype(BF16))
                cds.append(ecol[q - 1:q, :])
            y_ref[:, cols] = jnp.dot(jnp.concatenate(ms + cs_scaled, axis=1), jnp.concatenate([x2, st2], axis=0),
                                     preferred_element_type=F32)
            cd = jnp.where(first[0:1, :], cds[0], cds[1])
            st_ref[:, cols] = st_p * cd + jnp.dot(jnp.concatenate(bws, axis=1), x2, preferred_element_type=F32)

    out = _gate_and_groupnorm(y_ref[...], xa_ref[:, :D_INNER], z_ref[...].astype(F32), drep_ref[...],
                              gnw_ref[...])
    yg_ref[...] = out.astype(yg_ref.dtype)

    @pl.when(c == nc - 1)
    def _():
        conv_ref[...] = tail[SUBLANES - (CONV_W - 1):, :]
        for hp in range(SSD_HEADS // 2):
            blk = st_ref[:, hp * 2 * SSD_HEADDIM:(hp + 1) * 2 * SSD_HEADDIM].T
            ssm_ref[2 * hp:2 * hp + 2] = blk.reshape(2, SSD_HEADDIM, D_STATE)


def _ssd_prompt(xbc, z, dtr, conv_w, conv_b, dtb, alog, d_rep, gn_w):
    batch, seq, _ = xbc.shape
    nb = SSD_SEQS
    blk = lambda b, c: (b, c, 0)
    fixed = lambda b, c: (0, 0)
    nbytes = nb * (2 * CHUNK * (CONV_DIM * 4 + D_INNER * 4 + LANES * 4) + (CHUNK + SUBLANES) * CONV_DIM * 4
                   + 4 * D_STATE * D_INNER * 4 + 12 * CHUNK * CONV_DIM * 4)
    return pl.pallas_call(
        _ssd_prompt_kernel,
        grid=(batch // nb, seq // CHUNK),
        in_specs=[pl.BlockSpec((nb, CHUNK, CONV_DIM), blk), pl.BlockSpec((nb, CHUNK, D_INNER), blk),
                  pl.BlockSpec((nb, CHUNK, LANES), blk),
                  pl.BlockSpec((CONV_W, CONV_DIM), fixed), pl.BlockSpec((1, CONV_DIM), fixed),
                  pl.BlockSpec((1, LANES), fixed), pl.BlockSpec((1, LANES), fixed),
                  pl.BlockSpec((1, D_INNER), fixed), pl.BlockSpec((1, D_INNER), fixed)],
        out_specs=[pl.BlockSpec((nb, CHUNK, D_INNER), blk),
                   pl.BlockSpec((nb, SSD_HEADS, SSD_HEADDIM, D_STATE), lambda b, c: (b, 0, 0, 0)),
                   pl.BlockSpec((nb, CONV_W - 1, CONV_DIM), lambda b, c: (b, 0, 0))],
        out_shape=[jax.ShapeDtypeStruct((batch, seq, D_INNER), BF16),
                   jax.ShapeDtypeStruct((batch, SSD_HEADS, SSD_HEADDIM, D_STATE), F32),
                   jax.ShapeDtypeStruct((batch, CONV_W - 1, CONV_DIM), F32)],
        scratch_shapes=[pltpu.VMEM((nb, CHUNK + SUBLANES, CONV_DIM), F32), pltpu.VMEM((nb, D_STATE, D_INNER), F32),
                        pltpu.VMEM((nb, CHUNK, D_INNER), F32), pltpu.VMEM((nb, CHUNK, CONV_DIM), F32)],
        compiler_params=_params(("arbitrary", "arbitrary"), nbytes),
        name="ssd_prompt",
    )(xbc, z, dtr, conv_w, conv_b, dtb, alog, d_rep, gn_w)


def _ssd_out_kernel(y_ref, xs_ref, z_ref, drep_ref, gnw_ref, w_ref, r_ref, g_ref, h_ref, a_ref, wb_ref):
    @pl.when(pl.program_id(0) == 0)
    def _():
        wb_ref[...] = w_ref[...].astype(BF16)

    sub = min(ROW_SUB // 2, y_ref.shape[0])
    for s in range(y_ref.shape[0] // sub):
        rows = slice(s * sub, (s + 1) * sub)
        yg = _gate_and_groupnorm(y_ref[rows, :], xs_ref[rows, :], z_ref[rows, :].astype(F32), drep_ref[...],
                                 gnw_ref[...])
        h = r_ref[rows, :] + _bdot(yg, wb_ref[...])
        h_ref[rows, :] = h
        a_ref[rows, :] = _rms(h, g_ref[...]).astype(a_ref.dtype)


def _ssd_out(y, xa, z, d_rep, gn_w, w, resid, g, tm):
    m, k = y.shape
    d = w.shape[1]
    row = lambda i: (i, 0)
    fixed = lambda i: (0, 0)
    nbytes = k * d * 6 + 2 * tm * k * 10 + 8 * tm * d * 4 + 6 * tm * k * 4
    return pl.pallas_call(
        _ssd_out_kernel,
        grid=(m // tm,),
        in_specs=[pl.BlockSpec((tm, k), row), pl.BlockSpec((tm, k), row), pl.BlockSpec((tm, k), row),
                  pl.BlockSpec((1, k), fixed), pl.BlockSpec((1, k), fixed),
                  pl.BlockSpec((k, d), fixed, pipeline_mode=pl.Buffered(1)),
                  pl.BlockSpec((tm, d), row), pl.BlockSpec((1, d), fixed)],
        out_specs=[pl.BlockSpec((tm, d), row), pl.BlockSpec((tm, d), row)],
        out_shape=[jax.ShapeDtypeStruct((m, d), F32), jax.ShapeDtypeStruct((m, d), BF16)],
        scratch_shapes=[pltpu.VMEM((k, d), BF16)],
        compiler_params=_params(("arbitrary",), nbytes),
        name="ssd_out",
    )(y, xa, z, d_rep, gn_w, w, resid, g.reshape(1, d))


def _ssd_sample_pre_kernel(xbc_ref, conv_ref, dtr_ref, cw_ref, cb_ref, dtb_ref, alog_ref,
                           nconv_ref, xa_ref, xdt_ref, da_ref, bst_ref, cst_ref):
    xn = xbc_ref[...]
    acc = cb_ref[...] + cw_ref[CONV_W - 1:CONV_W, :] * xn
    for k in range(CONV_W - 1):
        acc = acc + cw_ref[k:k + 1, :] * conv_ref[:, k * CONV_DIM:(k + 1) * CONV_DIM]
    xa = _silu(acc)
    xa_ref[...] = xa
    nconv_ref[:, 0:CONV_DIM] = conv_ref[:, CONV_DIM:2 * CONV_DIM]
    nconv_ref[:, CONV_DIM:2 * CONV_DIM] = conv_ref[:, 2 * CONV_DIM:3 * CONV_DIM]
    nconv_ref[:, 2 * CONV_DIM:3 * CONV_DIM] = xn

    dt = _softplus(dtr_ref[...] + dtb_ref[...])
    a_neg = -jnp.exp(alog_ref[...])
    hh = lax.broadcasted_iota(I32, (LANES, D_INNER), 0)
    cc = lax.broadcasted_iota(I32, (LANES, D_INNER), 1)
    expand = jnp.where(lax.shift_right_logical(cc, 6) == hh, 1.0, 0.0)
    dt_rep = jnp.dot(dt, expand, precision=HIGHEST, preferred_element_type=F32)
    xdt_ref[...] = xa[:, :D_INNER] * dt_rep
    da_ref[...] = jnp.exp(dt * a_neg)

    n = xn.shape[0]
    gs = SSD_GROUPS * D_STATE
    b1, b2, b3 = _split3(xa[:, D_INNER:D_INNER + gs])
    cm = xa[:, D_INNER + gs:]
    c1 = cm.astype(BF16)
    c2 = (cm - c1.astype(F32)).astype(BF16)
    zero = jnp.zeros((n, D_STATE), BF16)
    b_rows, c_rows = [], []
    for g in range(SSD_GROUPS):
        sl = slice(g * D_STATE, (g + 1) * D_STATE)
        b_rows += [b1[:, sl], b2[:, sl], b1[:, sl], b3[:, sl], b2[:, sl], b1[:, sl]] + [zero] * (TERM_ROWS - 6)
        c_rows += [c1[:, sl], c2[:, sl]] + [zero] * (TERM_ROWS - 2)
    bst_ref[...] = jnp.concatenate(b_rows, axis=1)
    cst_ref[...] = jnp.concatenate(c_rows, axis=1)


TERM_ROWS = 16


def _split3(v):
    v1 = v.astype(BF16)
    r1 = v - v1.astype(F32)
    v2 = r1.astype(BF16)
    v3 = (r1 - v2.astype(F32)).astype(BF16)
    return v1, v2, v3


def _ssd_sample_pre(xbc, conv_flat, dtr, conv_w, conv_b, dtb, alog):
    n = xbc.shape[0]
    full = lambda shape: pl.BlockSpec(shape, lambda i: (0,) * len(shape))
    stack = SSD_GROUPS * TERM_ROWS * D_STATE
    return pl.pallas_call(
        _ssd_sample_pre_kernel,
        grid=(1,),
        in_specs=[full((n, CONV_DIM)), full((n, 3 * CONV_DIM)), full((n, LANES)), full((CONV_W, CONV_DIM)),
                  full((1, CONV_DIM)), full((1, LANES)), full((1, LANES))],
        out_specs=[full((n, 3 * CONV_DIM)), full((n, CONV_DIM)), full((n, D_INNER)), full((n, LANES)),
                   full((n, stack)), full((n, stack))],
        out_shape=[jax.ShapeDtypeStruct((n, 3 * CONV_DIM), F32), jax.ShapeDtypeStruct((n, CONV_DIM), F32),
                   jax.ShapeDtypeStruct((n, D_INNER), F32), jax.ShapeDtypeStruct((n, LANES), F32),
                   jax.ShapeDtypeStruct((n, stack), BF16), jax.ShapeDtypeStruct((n, stack), BF16)],
        compiler_params=_params(("arbitrary",), 32 * n * CONV_DIM * 4),
        name="ssd_sample_pre",
    )(xbc, conv_flat, dtr, conv_w, conv_b, dtb, alog)


STATE_SAMPLES = 8


def _ssd_sample_state_kernel(da_ref, st_ref, xdt_ref, bst_ref, cst_ref, nst_ref, y_ref):
    first = pl.program_id(0) * STATE_SAMPLES
    rows_g = D_INNER // SSD_GROUPS
    sub = lax.broadcasted_iota(I32, (TERM_ROWS, D_INNER), 0)
    for s in range(STATE_SAMPLES):
        x1, x2, x3 = [t.astype(F32) for t in _split3(xdt_ref[s])]
        terms = jnp.where(sub == 0, x1, jnp.where(sub == 1, x1, jnp.where(sub == 2, x2, jnp.where(
            sub == 3, x1, jnp.where(sub == 4, x2, jnp.where(sub == 5, x3, 0.0)))))).astype(BF16)
        ys = []
        for g in range(SSD_GROUPS):
            tile = slice(g * TERM_ROWS, (g + 1) * TERM_ROWS)
            upd = lax.dot_general(terms[:, g * rows_g:(g + 1) * rows_g], bst_ref[s, tile, :],
                                  (((0,), (0,)), ((), ())), preferred_element_type=F32)
            halves = []
            for hh in range(rows_g // SSD_HEADDIM):
                head = g * (rows_g // SSD_HEADDIM) + hh
                rows = slice(head * SSD_HEADDIM, (head + 1) * SSD_HEADDIM)
                h_new = (da_ref[first + s, head] * st_ref[s, rows, :]
                         + upd[hh * SSD_HEADDIM:(hh + 1) * SSD_HEADDIM, :])
                nst_ref[s, rows, :] = h_new
                halves.append(h_new)
            h_g = jnp.concatenate(halves, axis=0)
            h_hi = h_g.astype(BF16)
            h_lo = (h_g - h_hi.astype(F32)).astype(BF16)
            nt = (((1,), (1,)), ((), ()))
            o_hi = lax.dot_general(cst_ref[s, tile, :], h_hi, nt, preferred_element_type=F32)
            o_lo = lax.dot_general(cst_ref[s, tile, :], h_lo, nt, preferred_element_type=F32)
            ys.append(o_hi[0:1, :] + o_hi[1:2, :] + o_lo[0:1, :])
        y_ref[s] = jnp.concatenate(ys, axis=1)


def _ssd_sample_state(state, xdt, da, bst, cst):
    n = state.shape[0]
    ns = STATE_SAMPLES
    stack_rows = SSD_GROUPS * TERM_ROWS
    return pl.pallas_call(
        _ssd_sample_state_kernel,
        grid=(n // ns,),
        in_specs=[pl.BlockSpec(memory_space=pltpu.SMEM),
                  pl.BlockSpec((ns, D_INNER, D_STATE), lambda i: (i, 0, 0)),
                  pl.BlockSpec((ns, 1, D_INNER), lambda i: (i, 0, 0)),
                  pl.BlockSpec((ns, stack_rows, D_STATE), lambda i: (i, 0, 0)),
                  pl.BlockSpec((ns, stack_rows, D_STATE), lambda i: (i, 0, 0))],
        out_specs=[pl.BlockSpec((ns, D_INNER, D_STATE), lambda i: (i, 0, 0)),
                   pl.BlockSpec((ns, 1, D_INNER), lambda i: (i, 0, 0))],
        out_shape=[jax.ShapeDtypeStruct((n, D_INNER, D_STATE), F32),
                   jax.ShapeDtypeStruct((n, 1, D_INNER), F32)],
        compiler_params=_params(("arbitrary",), 6 * ns * D_INNER * D_STATE * 4),
        name="ssd_sample_state",
    )(da, state, xdt, bst, cst)


def _layernorm(v, g, b):
    mu = jnp.mean(v, axis=-1, keepdims=True)
    d = v - mu
    var = jnp.mean(d * d, axis=-1, keepdims=True)
    return d * lax.rsqrt(var + EPS) * g + b


def _sgu_sample_kernel(u_ref, v_ref, lng_ref, lnb_ref, w0_ref, b0_ref, o_ref, vn_ref):
    vn = _layernorm(v_ref[...], lng_ref[...], lnb_ref[...])
    vn_ref[...] = vn
    o_ref[...] = (u_ref[...] * (w0_ref[...] * vn + b0_ref[...])).astype(o_ref.dtype)


def _sgu_sample(uv, ln_g, ln_b, w0_rep, b0_rep):
    n = uv.shape[0]
    fixed2 = lambda i: (0, 0)
    return pl.pallas_call(
        _sgu_sample_kernel,
        grid=(1,),
        in_specs=[pl.BlockSpec((n, D_SGU), lambda i: (0, 0)), pl.BlockSpec((n, D_SGU), lambda i: (0, 1)),
                  pl.BlockSpec((1, D_SGU), fixed2), pl.BlockSpec((1, D_SGU), fixed2),
                  pl.BlockSpec((1, D_SGU), fixed2), pl.BlockSpec((1, D_SGU), fixed2)],
        out_specs=[pl.BlockSpec((n, D_SGU), fixed2), pl.BlockSpec((n, D_SGU), fixed2)],
        out_shape=[jax.ShapeDtypeStruct((n, D_SGU), BF16), jax.ShapeDtypeStruct((n, D_SGU), F32)],
        compiler_params=_params(("arbitrary",), 12 * n * D_SGU * 4),
        name="sgu_sample",
    )(uv, uv, ln_g, ln_b, w0_rep, b0_rep)


def _experts_to_lanes(col, sub, lane):
    return jnp.sum(jnp.where(sub == lane, col, 0.0), axis=0, keepdims=True)


def _route_kernel(lg_ref, loc_ref, gate_ref, bmeta_ref, tmeta_ref, *, tile, n_valid):
    nb, _, tb = lg_ref.shape
    sub = lax.broadcasted_iota(I32, (N_EXPERTS, LANES), 0)
    lane = lax.broadcasted_iota(I32, (N_EXPERTS, LANES), 1)
    subf = lax.broadcasted_iota(I32, (N_EXPERTS, tb), 0).astype(F32)
    tok = lax.broadcasted_iota(I32, (N_EXPERTS, tb), 1)
    incl = jnp.where(lax.broadcasted_iota(I32, (tb, tb), 0) <= lax.broadcasted_iota(I32, (tb, tb), 1), 1.0, 0.0)
    neg = jnp.float32(-jnp.inf)
    none = jnp.float32(N_EXPERTS)

    def select(k):
        blk = lg_ref[k]
        l = blk[0:N_EXPERTS] + blk[N_EXPERTS:2 * N_EXPERTS]
        m1 = jnp.max(l, axis=0, keepdims=True)
        i1 = jnp.min(jnp.where(l == m1, subf, none), axis=0, keepdims=True)
        l2 = jnp.where(subf == i1, neg, l)
        m2 = jnp.max(l2, axis=0, keepdims=True)
        i2 = jnp.min(jnp.where(l2 == m2, subf, none), axis=0, keepdims=True)
        valid = (k * tb + tok) < n_valid
        sel = jnp.where(valid, jnp.where(subf == i1, 1.0, jnp.where(subf == i2, 1.0, 0.0)), 0.0)
        return m1, i1, m2, i2, valid, sel

    def up8(rows):
        return jnp.ceil(rows / SUBLANES) * SUBLANES

    def count_body(k, carry):
        return carry + jnp.sum(select(k)[5], axis=1, keepdims=True)

    counts = lax.fori_loop(0, nb, count_body, jnp.zeros((N_EXPERTS, 1), F32))
    tiles = jnp.ceil(counts / tile)
    cum_incl = jnp.sum(jnp.where(lane <= sub, _experts_to_lanes(tiles, sub, lane), 0.0), axis=1, keepdims=True)
    offset = (cum_incl - tiles) * tile
    tile_expert = jnp.sum(jnp.where(lane.astype(F32) >= cum_incl, 1.0, 0.0), axis=0, keepdims=True)
    tile_expert = jnp.minimum(tile_expert, N_EXPERTS - 1.0)
    n_used = jnp.max(cum_incl, axis=0, keepdims=True)
    region_end = _experts_to_lanes(offset + up8(counts), sub, lane)
    region_pad = _experts_to_lanes(tiles * tile - up8(counts), sub, lane)
    tmeta_ref[...] = jnp.where(sub == 0, tile_expert, jnp.where(sub == 1, n_used, jnp.where(
        sub == 2, region_end, jnp.where(sub == 3, region_pad, 0.0)))).astype(I32)

    def place_body(k, before):
        m1, i1, m2, i2, valid, sel = select(k)
        run = jnp.dot(sel, incl, preferred_element_type=F32)
        cnt = run[:, tb - 1:tb]
        lead = before - jnp.floor(before / SUBLANES) * SUBLANES
        span = up8(lead + cnt)
        lstart = jnp.sum(jnp.where(lane < sub, _experts_to_lanes(span, sub, lane), 0.0), axis=1, keepdims=True)
        local = lstart + lead + run - sel
        loc1 = jnp.sum(jnp.where(subf == i1, local, 0.0), axis=0, keepdims=True)
        loc2 = jnp.sum(jnp.where(subf == i2, local, 0.0), axis=0, keepdims=True)
        live = (k * tb + tok[0:1, :]) < n_valid
        loc_ref[k] = jnp.where(subf == 0.0, jnp.where(live, loc1, -1.0),
                               jnp.where(subf == 1.0, jnp.where(live, loc2, -1.0), 0.0))
        e = jnp.exp(m2 - m1)
        gate_ref[k] = jnp.where(subf == 0.0, 1.0 / (1.0 + e), jnp.where(subf == 1.0, e / (1.0 + e), 0.0))
        moved = jnp.where(cnt > 0.0, span, 0.0)
        bmeta_ref[k] = jnp.where(lane == 0, offset + before - lead, jnp.where(lane == 1, moved, jnp.where(
            lane == 2, lstart, jnp.where(lane == 3, lead, jnp.where(lane == 4, lead + cnt, 0.0))))).astype(I32)
        return before + cnt

    lax.fori_loop(0, nb, place_body, jnp.zeros((N_EXPERTS, 1), F32))


def _route(logits3, tile, n_valid):
    nb, _, tb = logits3.shape
    tok_blk = pl.BlockSpec((nb, N_EXPERTS, tb), lambda i: (0, 0, 0))
    return pl.pallas_call(
        functools.partial(_route_kernel, tile=tile, n_valid=n_valid),
        grid=(1,),
        in_specs=[pl.BlockSpec((nb, 2 * N_EXPERTS, tb), lambda i: (0, 0, 0))],
        out_specs=[tok_blk, tok_blk, pl.BlockSpec((nb, N_EXPERTS, LANES), lambda i: (0, 0, 0)),
                   pl.BlockSpec((N_EXPERTS, LANES), lambda i: (0, 0))],
        out_shape=[jax.ShapeDtypeStruct((nb, N_EXPERTS, tb), F32),
                   jax.ShapeDtypeStruct((nb, N_EXPERTS, tb), F32),
                   jax.ShapeDtypeStruct((nb, N_EXPERTS, LANES), I32),
                   jax.ShapeDtypeStruct((N_EXPERTS, LANES), I32)],
        compiler_params=_params(("arbitrary",), 16 * nb * N_EXPERTS * tb * 4),
        name="moe_route",
    )(logits3)


RUN_FIELDS = 5


def _run_copies(bm_ref, blk, vmem_ref, hbm_ref, sem, *, tb, to_hbm, wait):
    for e in range(N_EXPERTS):
        base = (blk * N_EXPERTS + e) * RUN_FIELDS
        start, cnt, lstart = bm_ref[base], bm_ref[base + 1], bm_ref[base + 2]
        off = 0
        size = tb
        while size >= SUBLANES:
            @pl.when((cnt & size) != 0)
            def _(size=size, off=off):
                v = vmem_ref.at[pl.ds(pl.multiple_of(lstart + off, SUBLANES), size)]
                h = hbm_ref.at[pl.ds(pl.multiple_of(start + off, SUBLANES), size)]
                cp = pltpu.make_async_copy(v, h, sem) if to_hbm else pltpu.make_async_copy(h, v, sem)
                if wait:
                    cp.wait()
                else:
                    cp.start()

            off = off + (cnt & size)
            size //= 2


def _compact_rows(tb):
    return 2 * tb + N_EXPERTS * 2 * SUBLANES + 2 * SUBLANES


def _selection(loc_ref, rows, tb):
    r = lax.broadcasted_iota(I32, (rows, tb), 0).astype(F32)
    return r == loc_ref[0, 0:1, :], r == loc_ref[0, 1:2, :]


def _zero_fill(zm_ref, zbuf_ref, xs_ref, zsem, *, tm, n_tiles, wait):
    def fill(row0, size):
        cp = pltpu.make_async_copy(zbuf_ref.at[pl.ds(0, size)],
                                   xs_ref.at[pl.ds(pl.multiple_of(row0, SUBLANES), size)], zsem)
        if wait:
            cp.wait()
        else:
            cp.start()

    for e in range(N_EXPERTS):
        end, pad = zm_ref[1 + 2 * e], zm_ref[2 + 2 * e]
        off = 0
        size = tm // 2
        while size >= SUBLANES:
            @pl.when((pad & size) != 0)
            def _(size=size, off=off):
                fill(end + off, size)

            off = off + (pad & size)
            size //= 2
    for t in range(n_tiles):
        @pl.when(t >= zm_ref[0])
        def _(t=t):
            fill(t * tm, tm)


def _moe_scatter_kernel(bm_ref, zm_ref, a_ref, loc_ref, gate_ref, xs_ref,
                        buf_ref, zbuf_ref, carry_ref, sem, zsem, *, tb, tm, n_tiles):
    blk = pl.program_id(0)
    last = pl.num_programs(0) - 1
    slot = blk % 2
    d = a_ref.shape[1]

    @pl.when(blk == 0)
    def _():
        zbuf_ref[...] = jnp.zeros_like(zbuf_ref)
        carry_ref[...] = jnp.zeros_like(carry_ref)
        _zero_fill(zm_ref, zbuf_ref, xs_ref, zsem, tm=tm, n_tiles=n_tiles, wait=False)

    def copies(block, wait):
        s = block % 2
        _run_copies(bm_ref, block, buf_ref.at[s], xs_ref, sem.at[s], tb=tb, to_hbm=True, wait=wait)

    first, second = _selection(loc_ref, _compact_rows(tb), tb)
    pick = jnp.where(first, 1.0, jnp.where(second, 1.0, 0.0)).astype(BF16)
    buf_ref[slot, :, 0:d] = jnp.dot(pick, a_ref[...], preferred_element_type=F32)
    gsel = jnp.where(first, gate_ref[0, 0:1, :], jnp.where(second, gate_ref[0, 1:2, :], 0.0))
    buf_ref[slot, :, d:d + LANES] = jnp.broadcast_to(jnp.sum(gsel, axis=1, keepdims=True),
                                                    (buf_ref.shape[1], LANES))

    row = lax.broadcasted_iota(I32, (SUBLANES, buf_ref.shape[2]), 0)
    for e in range(N_EXPERTS):
        base = (blk * N_EXPERTS + e) * RUN_FIELDS
        lstart, total = bm_ref[base + 2], bm_ref[base + 4]
        head = pl.ds(pl.multiple_of(lstart, SUBLANES), SUBLANES)
        buf_ref[slot, head, :] = buf_ref[slot, head, :] + carry_ref[e]
        full = lax.shift_left(lax.shift_right_logical(total, 3), 3)
        tail = pl.ds(pl.multiple_of(lstart + full, SUBLANES), SUBLANES)
        carry_ref[e] = jnp.where(row < total - full, buf_ref[slot, tail, :], 0.0)

    @pl.when(blk >= 1)
    def _():
        copies(blk - 1, True)

    copies(blk, False)

    @pl.when(blk == last)
    def _():
        copies(blk, True)
        _zero_fill(zm_ref, zbuf_ref, xs_ref, zsem, tm=tm, n_tiles=n_tiles, wait=True)


def _moe_scatter(a, loc, gate, bmeta, zmeta, n_slots, tb, tm):
    m, d = a.shape
    width = d + LANES
    tok_spec = pl.BlockSpec((1, N_EXPERTS, tb), lambda i, bm, zm: (i, 0, 0))
    grid_spec = pltpu.PrefetchScalarGridSpec(
        num_scalar_prefetch=2,
        grid=(m // tb,),
        in_specs=[pl.BlockSpec((tb, d), lambda i, bm, zm: (i, 0)), tok_spec, tok_spec],
        out_specs=pl.BlockSpec(memory_space=pl.ANY),
        scratch_shapes=[pltpu.VMEM((2, _compact_rows(tb), width), F32), pltpu.VMEM((tm, width), F32),
                        pltpu.VMEM((N_EXPERTS, SUBLANES, width), F32),
                        pltpu.SemaphoreType.DMA((2,)), pltpu.SemaphoreType.DMA(())],
    )
    return pl.pallas_call(
        functools.partial(_moe_scatter_kernel, tb=tb, tm=tm, n_tiles=n_slots // tm),
        grid_spec=grid_spec,
        out_shape=jax.ShapeDtypeStruct((n_slots, width), F32),
        compiler_params=_params(("arbitrary",), 16 * tb * d * 4),
        name="moe_scatter",
    )(bmeta, zmeta, a, loc, gate)


def _new_weights(te_ref):
    i = pl.program_id(1)
    return (i == 0) | (te_ref[i] != te_ref[jnp.maximum(i - 1, 0)])


def _moe_gu_kernel(te_ref, nu_ref, x_ref, wg_ref, wu_ref, o_ref, wgb_ref, wub_ref):
    i = pl.program_id(1)

    @pl.when(_new_weights(te_ref))
    def _():
        wgb_ref[...] = wg_ref[0].astype(BF16)
        wub_ref[...] = wu_ref[0].astype(BF16)

    @pl.when(i < nu_ref[0])
    def _():
        x = x_ref[...].astype(BF16)
        for c in range(o_ref.shape[1] // MXU_COLS):
            cols = slice(c * MXU_COLS, (c + 1) * MXU_COLS)
            g = jnp.dot(x, wgb_ref[:, cols], preferred_element_type=F32)
            u = jnp.dot(x, wub_ref[:, cols], preferred_element_type=F32)
            o_ref[:, cols] = (_silu(g) * u).astype(o_ref.dtype)

    @pl.when(i >= nu_ref[0])
    def _():
        o_ref[...] = jnp.zeros_like(o_ref)


def _used_tile(i, nu):
    return jnp.minimum(i, nu[0] - 1)


def _moe_gu(xs, w_gu, te, nu, tm, tn):
    s = xs.shape[0]
    k = w_gu.shape[1]
    f = D_FF_EXPERT
    nbytes = 2 * (2 * k * tn * 4 + k * tn * 2) + 2 * tm * k * 4 + 2 * tm * tn * 2 + 4 * tm * MXU_COLS * 4
    grid_spec = pltpu.PrefetchScalarGridSpec(
        num_scalar_prefetch=2,
        grid=(f // tn, s // tm),
        in_specs=[pl.BlockSpec((tm, k), lambda j, i, te, nu: (_used_tile(i, nu), 0)),
                  pl.BlockSpec((1, k, tn), lambda j, i, te, nu: (te[i], 0, j)),
                  pl.BlockSpec((1, k, tn), lambda j, i, te, nu: (te[i], 0, j + f // tn))],
        out_specs=pl.BlockSpec((tm, tn), lambda j, i, te, nu: (i, j)),
        scratch_shapes=[pltpu.VMEM((k, tn), BF16), pltpu.VMEM((k, tn), BF16)],
    )
    return pl.pallas_call(
        _moe_gu_kernel,
        grid_spec=grid_spec,
        out_shape=jax.ShapeDtypeStruct((s, f), BF16),
        compiler_params=_params(("arbitrary", "arbitrary"), nbytes),
        name="moe_gate_up",
    )(te, nu, xs, w_gu, w_gu)


def _moe_down_kernel(te_ref, nu_ref, x_ref, w_ref, gs_ref, o_ref, wb_ref):
    i = pl.program_id(1)

    @pl.when(_new_weights(te_ref))
    def _():
        wb_ref[...] = w_ref[0].astype(BF16)

    @pl.when(i < nu_ref[0])
    def _():
        o_ref[...] = gs_ref[:, 0:1] * jnp.dot(x_ref[...], wb_ref[...], preferred_element_type=F32)

    @pl.when(i >= nu_ref[0])
    def _():
        o_ref[...] = jnp.zeros_like(o_ref)


def _moe_down(act, w_down, xs, te, nu, tm, tn):
    s, k = act.shape
    d = w_down.shape[2]
    gate_block = d // LANES
    nbytes = 2 * k * tn * 4 + k * tn * 2 + 2 * tm * k * 2 + 3 * tm * tn * 4
    grid_spec = pltpu.PrefetchScalarGridSpec(
        num_scalar_prefetch=2,
        grid=(d // tn, s // tm),
        in_specs=[pl.BlockSpec((tm, k), lambda j, i, te, nu: (_used_tile(i, nu), 0)),
                  pl.BlockSpec((1, k, tn), lambda j, i, te, nu: (te[i], 0, j)),
                  pl.BlockSpec((tm, LANES), lambda j, i, te, nu: (_used_tile(i, nu), gate_block))],
        out_specs=pl.BlockSpec((tm, tn), lambda j, i, te, nu: (i, j)),
        scratch_shapes=[pltpu.VMEM((k, tn), BF16)],
    )
    return pl.pallas_call(
        _moe_down_kernel,
        grid_spec=grid_spec,
        out_shape=jax.ShapeDtypeStruct((s, d), F32),
        compiler_params=_params(("arbitrary", "arbitrary"), nbytes),
        name="moe_down",
    )(te, nu, act, w_down, xs)


def _moe_combine_kernel(bm_ref, h_ref, locc_ref, g_ref, ys_ref, op_ref, os_ref, ybuf_ref, sem, *, tb, prompt_blocks):
    blk = pl.program_id(0)
    slot = blk % 2

    def copies(block, wait):
        s = block % 2
        _run_copies(bm_ref, block, ybuf_ref.at[s], ys_ref, sem.at[s], tb=tb, to_hbm=False, wait=wait)

    @pl.when(blk == 0)
    def _():
        ybuf_ref[...] = jnp.zeros_like(ybuf_ref)
        copies(blk, False)

    @pl.when(blk + 1 < pl.num_programs(0))
    def _():
        copies(blk + 1, False)

    r = lax.broadcasted_iota(I32, (tb, _compact_rows(tb)), 1).astype(F32)
    pick = jnp.where(r == locc_ref[:, 0:1], 1.0, jnp.where(r == locc_ref[:, 1:2], 1.0, 0.0)).astype(BF16)
    copies(blk, True)
    moe = jnp.dot(pick, ybuf_ref[slot].astype(BF16), preferred_element_type=F32)
    out = _rms(h_ref[...] + moe, g_ref[...])

    @pl.when(blk < prompt_blocks)
    def _():
        op_ref[...] = out

    @pl.when(blk == prompt_blocks)
    def _():
        os_ref[...] = out


def _moe_combine(h, locc, bmeta, ys, g, tb, n_prompt):
    m, d = h.shape
    prompt_blocks = n_prompt // tb
    any_spec = pl.BlockSpec(memory_space=pl.ANY)
    grid_spec = pltpu.PrefetchScalarGridSpec(
        num_scalar_prefetch=1,
        grid=(m // tb,),
        in_specs=[pl.BlockSpec((tb, d), lambda i, bm: (i, 0)),
                  pl.BlockSpec((tb, N_EXPERTS), lambda i, bm: (i, 0)),
                  pl.BlockSpec((1, d), lambda i, bm: (0, 0)),
                  any_spec],
        out_specs=[pl.BlockSpec((tb, d), lambda i, bm: (jnp.minimum(i, prompt_blocks - 1), 0)),
                   pl.BlockSpec((tb, d), lambda i, bm: (0, 0))],
        scratch_shapes=[pltpu.VMEM((2, _compact_rows(tb), d), F32), pltpu.SemaphoreType.DMA((2,))],
    )
    return pl.pallas_call(
        functools.partial(_moe_combine_kernel, tb=tb, prompt_blocks=prompt_blocks),
        grid_spec=grid_spec,
        out_shape=[jax.ShapeDtypeStruct((n_prompt, d), F32), jax.ShapeDtypeStruct((tb, d), F32)],
        compiler_params=_params(("arbitrary",), 16 * tb * d * 4),
        name="moe_combine",
    )(bmeta, h, locc, g.reshape(1, d), ys)


MOE_TOKEN_BLOCK = 256
MOE_ROW_TILE = 512


def _moe(h, a, lg_parts, n_valid, n_prompt, w_gu, w_down, g_final):
    m, d = h.shape
    tb, tm = MOE_TOKEN_BLOCK, MOE_ROW_TILE
    nb = m // tb
    lg3 = lg_parts[:, :2 * N_EXPERTS].reshape(nb, tb, 2 * N_EXPERTS).transpose(0, 2, 1)
    n_tiles = pl.cdiv(2 * n_valid, tm) + N_EXPERTS
    loc, gate, bmeta, tmeta = _route(lg3, tm, n_valid)
    te = tmeta[0, :n_tiles]
    nu = tmeta[1, 0:1]
    bm = bmeta[:, :, :RUN_FIELDS].reshape(-1)
    zm = jnp.concatenate([nu, tmeta[2:4, :N_EXPERTS].T.reshape(-1)])
    locc = loc.transpose(0, 2, 1).reshape(m, N_EXPERTS)
    xs = _moe_scatter(a, loc, gate, bm, zm, n_tiles * tm, tb, tm)
    act = _moe_gu(xs, w_gu, te, nu, tm, D_FF_EXPERT // 2)
    ys = _moe_down(act, w_down, xs, te, nu, tm, d)
    return _moe_combine(h, locc, bm, ys, g_final, tb, n_prompt)


def _pad_lanes(v):
    return jnp.pad(v.reshape(1, -1), ((0, 0), (0, LANES - v.shape[-1])))


def kernel(x_prompt, x_sample, state_ssm, state_conv, norm_mix, norm_ffn, norm_final, ssd_w_in, ssd_conv_w,
           ssd_conv_b, ssd_dt_bias, ssd_a_log, ssd_d, ssd_gnorm, ssd_w_out, sgu_w_in, sgu_b_in, sgu_ln_g,
           sgu_ln_b, sgu_w_s, sgu_b_s, sgu_w_out, ffn_w_gu, ffn_w_down, moe_w_router, moe_w_gu, moe_w_down):
    batch, seq, d = x_prompt.shape
    n_s = x_sample.shape[0]
    mp = batch * seq
    xp = x_prompt.reshape(mp, d)
    xs = x_sample.reshape(n_s, d)

    w_in = ssd_w_in[0].T
    w_dt = jnp.pad(w_in[D_INNER + CONV_DIM:], ((0, LANES - SSD_HEADS), (0, 0)))
    dtb = _pad_lanes(ssd_dt_bias[0])
    alog = _pad_lanes(ssd_a_log[0])
    d_rep = jnp.repeat(ssd_d[0], SSD_HEADDIM).reshape(1, D_INNER)
    gn_w = ssd_gnorm[0].reshape(1, D_INNER)
    conv_w = ssd_conv_w[0]
    conv_b = ssd_conv_b[0].reshape(1, CONV_DIM)
    ln_g = sgu_ln_g[0].reshape(1, D_SGU)
    ln_b = sgu_ln_b[0].reshape(1, D_SGU)
    bs_t = jnp.pad(sgu_b_s[0].T, ((0, 0), (0, LANES - SGU_HEADS)))
    w0_rep = jnp.repeat(sgu_w_s[0][:, 0, 0], SGU_HEAD_DIM).reshape(1, D_SGU)
    b0_rep = jnp.repeat(sgu_b_s[0][:, 0], SGU_HEAD_DIM).reshape(1, D_SGU)
    wr_hi = moe_w_router[0].astype(BF16)
    wr_lo = (moe_w_router[0] - wr_hi.astype(F32)).astype(BF16)
    w3 = jnp.concatenate([wr_hi, wr_lo, jnp.zeros((d, LANES - 2 * N_EXPERTS), BF16)], axis=1)

    def dense_ffn(h1, a1, tm):
        act = _mm_swiglu(a1, ffn_w_gu[0], D_FF_DENSE, min(4 * tm, a1.shape[0]), 1408, "ffn_gate_up")
        return _mm_resnorm(act, ffn_w_down[0], h1, norm_mix[1], tm, "ffn_down")

    a0p, zp, dtrp = _norm_in_proj(xp, norm_mix[0], w_in, w_dt, 1024)
    xbcp = _mm_plain(a0p, w_in, D_INNER, CONV_DIM, 2048, 1024, F32, "ssd_in_xbc")
    per_seq = lambda t: t.reshape(batch, seq, t.shape[-1])
    ygp, ssm_p, conv_p = _ssd_prompt(per_seq(xbcp), per_seq(zp), per_seq(dtrp), conv_w, conv_b, dtb, alog, d_rep,
                                     gn_w)
    h1p, a1p = _mm_resnorm(ygp.reshape(mp, D_INNER), ssd_w_out[0], xp, norm_ffn[0], 512, "ssd_out")
    h2p, a2p = dense_ffn(h1p, a1p, 512)

    a0s, zs, dtrs = _norm_in_proj(xs, norm_mix[0], w_in, w_dt, n_s)
    xbcs = _mm_plain(a0s, w_in, D_INNER, CONV_DIM, n_s, 1024, F32, "ssd_in_xbc")
    conv_s, xas, xdts, das, bsts, csts = _ssd_sample_pre(xbcs, state_conv[0].reshape(n_s, 3 * CONV_DIM), dtrs,
                                                         conv_w, conv_b, dtb, alog)
    stack_rows = SSD_GROUPS * TERM_ROWS
    ssm_s, ys3 = _ssd_sample_state(state_ssm[0].reshape(n_s, D_INNER, D_STATE), xdts.reshape(n_s, 1, D_INNER),
                                   das[:, :SSD_HEADS], bsts.reshape(n_s, stack_rows, D_STATE),
                                   csts.reshape(n_s, stack_rows, D_STATE))
    h1s, a1s = _ssd_out(ys3.reshape(n_s, D_INNER), xas, zs, d_rep, gn_w, ssd_w_out[0], xs, norm_ffn[0], n_s)
    h2s, a2s = dense_ffn(h1s, a1s, n_s)

    uvp = _mm_bias_gelu(a2p, sgu_w_in[0], sgu_b_in[0], 2048, 2048, BF16, "sgu_in")
    uvs = _mm_bias_gelu(a2s, sgu_w_in[0], sgu_b_in[0], n_s, 1024, F32, "sgu_in")
    ss, v_s = _sgu_sample(uvs, ln_g, ln_b, w0_rep, b0_rep)
    tm_out = 512
    pad_rows = ((0, tm_out - n_s), (0, 0))
    h3, a3, lg = _sgu_out(uvp, jnp.pad(ss, pad_rows), ln_g, ln_b, sgu_w_s[0], bs_t, sgu_w_out[0], h2p,
                          jnp.pad(h2s, pad_rows), norm_ffn[1], w3, tm_out)

    y_prompt, y_tail = _moe(h3, a3, lg, mp + n_s, mp, moe_w_gu[0], moe_w_down[0], norm_final)
    y_sample = y_tail[:n_s]

    return (y_prompt.reshape(batch, seq, d),
            y_sample.reshape(n_s, 1, d),
            ssm_p[None],
            conv_p[None],
            ssm_s.reshape(1, n_s, SSD_HEADS, SSD_HEADDIM, D_STATE),
            conv_s.reshape(1, n_s, CONV_W - 1, CONV_DIM),
            v_s.reshape(1, n_s, 1, D_SGU))
```

```python
import functools

import jax
import jax.numpy as jnp
from jax import lax
from jax.experimental import pallas as pl
from jax.experimental.pallas import tpu as pltpu

F32 = jnp.float32
BF16 = jnp.bfloat16
I32 = jnp.int32
HIGHEST = lax.Precision.HIGHEST

D_MODEL = 1024
D_INNER = 2048
SSD_HEADS = 32
SSD_HEADDIM = 64
SSD_GROUPS = 4
D_STATE = 128
CONV_W = 4
CONV_DIM = D_INNER + 2 * SSD_GROUPS * D_STATE
CHUNK = 128
D_SGU = 2048
SGU_HEADS = 8
SGU_HEAD_DIM = D_SGU // SGU_HEADS
D_FF_DENSE = 2816
N_EXPERTS = 8
D_FF_EXPERT = 3584
EPS = 1e-5

MXU_COLS = 256
ROW_SUB = 256
LANES = 128
SUBLANES = 8
VMEM_CAP = 60000 * 1024
VMEM_FLOOR = 32 * 1024 * 1024


def _vmem_limit(nbytes):
    return int(min(max(nbytes * 5 // 4 + (4 << 20), VMEM_FLOOR), VMEM_CAP))


def _params(sem, nbytes):
    return pltpu.CompilerParams(dimension_semantics=sem, vmem_limit_bytes=_vmem_limit(nbytes))


LOG2E = 1.4426950408889634


def _silu(x):
    return x / (1.0 + jnp.exp2(x * (-LOG2E)))


def _rms(x, g):
    return x * lax.rsqrt(jnp.mean(x * x, axis=-1, keepdims=True) + EPS) * g


def _bdot(a, b):
    return jnp.dot(a.astype(BF16), b.astype(BF16), preferred_element_type=F32)


def _cast_transposed(wt_ref, wb_ref):
    step = 2 * MXU_COLS
    for r0 in range(0, wt_ref.shape[0], step):
        r1 = min(r0 + step, wt_ref.shape[0])
        wb_ref[:, r0:r1] = wt_ref[r0:r1, :].T.astype(BF16)


def _mm_plain_kernel(x_ref, wt_ref, o_ref, wb_ref):
    @pl.when(pl.program_id(1) == 0)
    def _():
        _cast_transposed(wt_ref, wb_ref)

    o_ref[...] = _bdot(x_ref[...], wb_ref[...]).astype(o_ref.dtype)


def _mm_bias_gelu_kernel(x_ref, w_ref, b_ref, o_ref, wb_ref):
    @pl.when(pl.program_id(1) == 0)
    def _():
        wb_ref[...] = w_ref[...].astype(BF16)

    x = x_ref[...].astype(BF16)
    for c0 in range(0, o_ref.shape[1], 2 * MXU_COLS):
        cols = slice(c0, c0 + 2 * MXU_COLS)
        y = jnp.dot(x, wb_ref[:, cols], preferred_element_type=F32) + b_ref[:, cols]
        o_ref[:, cols] = (0.5 * y * (1.0 + lax.erf(y * (2.0 ** -0.5)))).astype(o_ref.dtype)


def _mm_swiglu_kernel(x_ref, wg_ref, wu_ref, o_ref, wgb_ref, wub_ref):
    @pl.when(pl.program_id(1) == 0)
    def _():
        wgb_ref[...] = wg_ref[...].astype(BF16)
        wub_ref[...] = wu_ref[...].astype(BF16)

    x = x_ref[...].astype(BF16)
    tn = o_ref.shape[1]
    for c0 in range(0, tn, MXU_COLS):
        cols = slice(c0, min(c0 + MXU_COLS, tn))
        g = jnp.dot(x, wgb_ref[:, cols], preferred_element_type=F32)
        u = jnp.dot(x, wub_ref[:, cols], preferred_element_type=F32)
        o_ref[:, cols] = (_silu(g) * u).astype(o_ref.dtype)


def _wspec(k, tn, col_block0):
    return pl.BlockSpec((k, tn), lambda j, i: (0, j + col_block0), pipeline_mode=pl.Buffered(1))


def _mm_plain(x, wt, col0, n, tm, tn, out_dtype, name):
    m, k = x.shape
    nbytes = k * tn * 6 + 2 * tm * k * x.dtype.itemsize + 2 * tm * tn * 4 + 2 * MXU_COLS * k * 8
    row_block0 = col0 // tn
    return pl.pallas_call(
        _mm_plain_kernel,
        grid=(n // tn, m // tm),
        in_specs=[pl.BlockSpec((tm, k), lambda j, i: (i, 0)),
                  pl.BlockSpec((tn, k), lambda j, i: (j + row_block0, 0), pipeline_mode=pl.Buffered(1))],
        out_specs=pl.BlockSpec((tm, tn), lambda j, i: (i, j)),
        out_shape=jax.ShapeDtypeStruct((m, n), out_dtype),
        scratch_shapes=[pltpu.VMEM((k, tn), BF16)],
        compiler_params=_params(("arbitrary", "arbitrary"), nbytes),
        name=name,
    )(x, wt)


def _norm_in_proj_kernel(x_ref, g_ref, wz_ref, wdt_ref, a_ref, z_ref, dtr_ref, wzb_ref, wdtb_ref):
    @pl.when(pl.program_id(0) == 0)
    def _():
        _cast_transposed(wz_ref, wzb_ref)
        _cast_transposed(wdt_ref, wdtb_ref)

    a = _rms(x_ref[...], g_ref[...]).astype(BF16)
    a_ref[...] = a
    for c0 in range(0, z_ref.shape[1], 2 * MXU_COLS):
        cols = slice(c0, c0 + 2 * MXU_COLS)
        z_ref[:, cols] = _silu(jnp.dot(a, wzb_ref[:, cols], preferred_element_type=F32)).astype(z_ref.dtype)
    dtr_ref[...] = jnp.dot(a, wdtb_ref[...], preferred_element_type=F32)


def _norm_in_proj(x, g, w_in_t, w_dt_t, tm):
    m, k = x.shape
    n = D_INNER
    row = lambda i: (i, 0)
    fixed = lambda i: (0, 0)
    nbytes = k * n * 6 + k * LANES * 6 + 2 * tm * k * 4 + 2 * tm * k * 2 + 2 * tm * n * 2 + 2 * tm * n * 4
    return pl.pallas_call(
        _norm_in_proj_kernel,
        grid=(m // tm,),
        in_specs=[pl.BlockSpec((tm, k), row), pl.BlockSpec((1, k), fixed),
                  pl.BlockSpec((n, k), fixed, pipeline_mode=pl.Buffered(1)),
                  pl.BlockSpec((LANES, k), fixed)],
        out_specs=[pl.BlockSpec((tm, k), row), pl.BlockSpec((tm, n), row), pl.BlockSpec((tm, LANES), row)],
        out_shape=[jax.ShapeDtypeStruct((m, k), BF16), jax.ShapeDtypeStruct((m, n), BF16),
                   jax.ShapeDtypeStruct((m, LANES), F32)],
        scratch_shapes=[pltpu.VMEM((k, n), BF16), pltpu.VMEM((k, LANES), BF16)],
        compiler_params=_params(("arbitrary",), nbytes),
        name="norm_in_proj",
    )(x, g.reshape(1, k), w_in_t, w_dt_t)


def _mm_bias_gelu(x, w, b, tm, tn, out_dtype, name):
    m, k = x.shape
    n = w.shape[1]
    nbytes = k * tn * 6 + 2 * tm * k * x.dtype.itemsize + 2 * tm * tn * 4
    return pl.pallas_call(
        _mm_bias_gelu_kernel,
        grid=(n // tn, m // tm),
        in_specs=[pl.BlockSpec((tm, k), lambda j, i: (i, 0)), _wspec(k, tn, 0),
                  pl.BlockSpec((1, tn), lambda j, i: (0, j))],
        out_specs=pl.BlockSpec((tm, tn), lambda j, i: (i, j)),
        out_shape=jax.ShapeDtypeStruct((m, n), out_dtype),
        scratch_shapes=[pltpu.VMEM((k, tn), BF16)],
        compiler_params=_params(("arbitrary", "arbitrary"), nbytes),
        name=name,
    )(x, w, b.reshape(1, n))


def _mm_swiglu(x, w_gu, d_ff, tm, tn, name):
    m, k = x.shape
    nbytes = 2 * k * tn * 6 + 2 * tm * k * x.dtype.itemsize + 2 * tm * tn * 2 + 3 * tm * tn * 4
    return pl.pallas_call(
        _mm_swiglu_kernel,
        grid=(d_ff // tn, m // tm),
        in_specs=[pl.BlockSpec((tm, k), lambda j, i: (i, 0)), _wspec(k, tn, 0), _wspec(k, tn, d_ff // tn)],
        out_specs=pl.BlockSpec((tm, tn), lambda j, i: (i, j)),
        out_shape=jax.ShapeDtypeStruct((m, d_ff), BF16),
        scratch_shapes=[pltpu.VMEM((k, tn), BF16), pltpu.VMEM((k, tn), BF16)],
        compiler_params=_params(("arbitrary", "arbitrary"), nbytes),
        name=name,
    )(x, w_gu, w_gu)


def _mm_resnorm_kernel(x_ref, w_ref, r_ref, g_ref, h_ref, a_ref, wb_ref):
    @pl.when(pl.program_id(0) == 0)
    def _():
        wb_ref[...] = w_ref[...].astype(BF16)

    h = r_ref[...] + _bdot(x_ref[...], wb_ref[...])
    h_ref[...] = h
    a_ref[...] = _rms(h, g_ref[...]).astype(a_ref.dtype)


def _mm_resnorm(x, w, resid, g, tm, name):
    m, k = x.shape
    d = w.shape[1]
    nbytes = k * d * 6 + 2 * tm * k * 2 + 8 * tm * d * 4
    return pl.pallas_call(
        _mm_resnorm_kernel,
        grid=(m // tm,),
        in_specs=[pl.BlockSpec((tm, k), lambda i: (i, 0)),
                  pl.BlockSpec((k, d), lambda i: (0, 0), pipeline_mode=pl.Buffered(1)),
                  pl.BlockSpec((tm, d), lambda i: (i, 0)),
                  pl.BlockSpec((1, d), lambda i: (0, 0))],
        out_specs=[pl.BlockSpec((tm, d), lambda i: (i, 0)), pl.BlockSpec((tm, d), lambda i: (i, 0))],
        out_shape=[jax.ShapeDtypeStruct((m, d), F32), jax.ShapeDtypeStruct((m, d), BF16)],
        scratch_shapes=[pltpu.VMEM((k, d), BF16)],
        compiler_params=_params(("arbitrary",), nbytes),
        name=name,
    )(x, w, resid, g.reshape(1, d))


def _sgu_out_kernel(u_ref, v_ref, xt_ref, lng_ref, lnb_ref, ws_ref, bst_ref, w_ref, rp_ref, rt_ref, g_ref, w3_ref,
                    h_ref, a_ref, lg_ref, wb_ref, wsb_ref, *, prompt_tiles):
    i = pl.program_id(0)
    q = CHUNK

    @pl.when(i == 0)
    def _():
        wb_ref[...] = w_ref[...].astype(BF16)
        causal = lax.broadcasted_iota(I32, (q, q), 0) >= lax.broadcasted_iota(I32, (q, q), 1)
        for g in range(SGU_HEADS):
            wsb_ref[g] = jnp.where(causal, ws_ref[g], 0.0).astype(BF16)

    def project(x, r_ref, rows):
        h = r_ref[rows, :] + jnp.dot(x, wb_ref[...], preferred_element_type=F32)
        h_ref[rows, :] = h
        a = _rms(h, g_ref[...])
        a_hi = a.astype(BF16)
        a_lo = (a - a_hi.astype(F32)).astype(BF16)
        a_ref[rows, :] = a_hi
        lg_ref[rows, :] = (jnp.dot(a_hi, w3_ref[...], preferred_element_type=F32)
                           + jnp.dot(a_lo, w3_ref[...], preferred_element_type=F32))

    chunks = [slice(c * q, (c + 1) * q) for c in range(u_ref.shape[0] // q)]

    @pl.when(i < prompt_tiles)
    def _():
        for rows in chunks:
            vn = _layernorm(v_ref[rows, :].astype(F32), lng_ref[...], lnb_ref[...]).astype(BF16)
            gated = []
            for g in range(SGU_HEADS):
                cols = slice(g * SGU_HEAD_DIM, (g + 1) * SGU_HEAD_DIM)
                s = jnp.dot(wsb_ref[g], vn[:, cols], preferred_element_type=F32) + bst_ref[:, g:g + 1]
                gated.append((u_ref[rows, cols].astype(F32) * s).astype(BF16))
            project(jnp.concatenate(gated, axis=1), rp_ref, rows)

    @pl.when(i == prompt_tiles)
    def _():
        for rows in chunks:
            project(xt_ref[rows, :], rt_ref, rows)


def _sgu_out(uv, xt, ln_g, ln_b, w_s, bs_t, w, rp, rt, g, w3, tm):
    mp = uv.shape[0]
    k, d = w.shape
    prompt_tiles = mp // tm
    last = prompt_tiles - 1
    rows = mp + tm
    prow = lambda i: (jnp.minimum(i, last), 0)
    fixed = lambda i: (0, 0)
    nbytes = k * d * 6 + 8 * tm * k * 2 + 12 * tm * d * 4 + 8 * CHUNK * k * 4
    return pl.pallas_call(
        functools.partial(_sgu_out_kernel, prompt_tiles=prompt_tiles),
        grid=(prompt_tiles + 1,),
        in_specs=[pl.BlockSpec((tm, k), prow),
                  pl.BlockSpec((tm, k), lambda i: (jnp.minimum(i, last), 1)),
                  pl.BlockSpec((tm, k), fixed),
                  pl.BlockSpec((1, k), fixed), pl.BlockSpec((1, k), fixed),
                  pl.BlockSpec((SGU_HEADS, CHUNK, CHUNK), lambda i: (0, 0, 0)),
                  pl.BlockSpec((CHUNK, LANES), fixed),
                  pl.BlockSpec((k, d), fixed, pipeline_mode=pl.Buffered(1)),
                  pl.BlockSpec((tm, d), prow),
                  pl.BlockSpec((tm, d), fixed),
                  pl.BlockSpec((1, d), fixed),
                  pl.BlockSpec((d, LANES), fixed)],
        out_specs=[pl.BlockSpec((tm, d), lambda i: (i, 0)), pl.BlockSpec((tm, d), lambda i: (i, 0)),
                   pl.BlockSpec((tm, LANES), lambda i: (i, 0))],
        out_shape=[jax.ShapeDtypeStruct((rows, d), F32), jax.ShapeDtypeStruct((rows, d), BF16),
                   jax.ShapeDtypeStruct((rows, LANES), F32)],
        scratch_shapes=[pltpu.VMEM((k, d), BF16), pltpu.VMEM((SGU_HEADS, CHUNK, CHUNK), BF16)],
        compiler_params=_params(("arbitrary",), nbytes),
        name="sgu_out",
    )(uv, uv, xt, ln_g, ln_b, w_s, bs_t, w, rp, rt, g.reshape(1, d), w3)


def _gate_and_groupnorm(y, xs, gate, d_rep, gn_w):
    yg = (y + d_rep * xs) * gate
    gw = D_INNER // SSD_GROUPS
    outs = []
    for g in range(SSD_GROUPS):
        blk = yg[:, g * gw:(g + 1) * gw]
        outs.append(blk * lax.rsqrt(jnp.mean(blk * blk, axis=-1, keepdims=True) + EPS))
    return jnp.concatenate(outs, axis=1) * gn_w


def _softplus(x):
    return jnp.maximum(x, 0.0) + jnp.log(1.0 + jnp.exp(-jnp.abs(x)))


SSD_SEQS = 2


def _ssd_prompt_kernel(xbc_ref, z_ref, dtr_ref, cw_ref, cb_ref, dtb_ref, alog_ref, drep_ref, gnw_ref,
                       yg_ref, ssm_ref, conv_ref, ext_ref, st_ref, y_ref, xa_ref):
    for n in range(SSD_SEQS):
        _ssd_chunk(xbc_ref.at[n], z_ref.at[n], dtr_ref.at[n], cw_ref, cb_ref, dtb_ref, alog_ref, drep_ref, gnw_ref,
                   yg_ref.at[n], ssm_ref.at[n], conv_ref.at[n], ext_ref.at[n], st_ref.at[n], y_ref.at[n],
                   xa_ref.at[n])


def _ssd_chunk(xbc_ref, z_ref, dtr_ref, cw_ref, cb_ref, dtb_ref, alog_ref, drep_ref, gnw_ref,
               yg_ref, ssm_ref, conv_ref, ext_ref, st_ref, y_ref, xa_ref):
    c = pl.program_id(1)
    nc = pl.num_programs(1)
    q = CHUNK

    @pl.when(c == 0)
    def _():
        ext_ref[0:SUBLANES, :] = jnp.zeros((SUBLANES, CONV_DIM), F32)
        st_ref[...] = jnp.zeros_like(st_ref)

    ext_ref[SUBLANES:SUBLANES + q, :] = xbc_ref[...]
    acc = cb_ref[...] + cw_ref[CONV_W - 1:CONV_W, :] * ext_ref[SUBLANES:SUBLANES + q, :]
    for k in range(CONV_W - 1):
        lo = SUBLANES - (CONV_W - 1) + k
        acc = acc + cw_ref[k:k + 1, :] * ext_ref[lo:lo + q, :]
    xa_ref[...] = _silu(acc)
    tail = ext_ref[q:q + SUBLANES, :]
    ext_ref[0:SUBLANES, :] = tail

    dt = _softplus(dtr_ref[...] + dtb_ref[...])
    a_neg = -jnp.exp(alog_ref[...])
    causal = lax.broadcasted_iota(I32, (q, q), 0) >= lax.broadcasted_iota(I32, (q, q), 1)
    tril = jnp.where(causal, 1.0, 0.0)
    cs = jnp.dot(tril, dt * a_neg, precision=HIGHEST, preferred_element_type=F32) * LOG2E
    cs_t = cs.T
    dt_t = dt.T
    rowp = cs_t - jnp.log2(dt_t)
    w_t = dt_t * jnp.exp2(cs_t[:, q - 1:q] - cs_t)
    first = lax.broadcasted_iota(I32, (q, 2 * SSD_HEADDIM), 1) < SSD_HEADDIM

    def per_head(v):
        zero = jnp.zeros_like(v)
        return jnp.concatenate([jnp.where(first, v, zero), jnp.where(first, zero, v)], axis=0)

    for g in range(SSD_GROUPS):
        b_g = xa_ref[:, D_INNER + g * D_STATE:D_INNER + (g + 1) * D_STATE]
        c_g = xa_ref[:, D_INNER + (SSD_GROUPS + g) * D_STATE:D_INNER + (SSD_GROUPS + g + 1) * D_STATE]
        b_gt = b_g.T
        cb = _bdot(c_g, b_gt)
        for pr in range(4):
            hp = g * 4 + pr
            cols = slice(hp * 2 * SSD_HEADDIM, (hp + 1) * 2 * SSD_HEADDIM)
            x2 = per_head(xa_ref[:, cols].astype(BF16))
            st_p = st_ref[:, cols]
            st2 = per_head(st_p.astype(BF16))
            ms, cs_scaled, bws, cds = [], [], [], []
            for s in range(2):
                h = 2 * hp + s
                colb = jnp.broadcast_to(cs[:, h:h + 1], (q, q))
                rowb = jnp.broadcast_to(rowp[h:h + 1, :], (q, q))
                ecol = jnp.exp2(colb)
                ms.append((cb * jnp.where(causal, jnp.exp2(colb - rowb), 0.0)).astype(BF16))
                cs_scaled.append((c_g * ecol).astype(BF16))
                bws.append((b_gt * w_t[h:h + 1, :]).astype(BF16))
                cds.append(ecol[q - 1:q, :])
            y_ref[:, cols] = jnp.dot(jnp.concatenate(ms + cs_scaled, axis=1), jnp.concatenate([x2, st2], axis=0),
                                     preferred_element_type=F32)
            cd = jnp.where(first[0:1, :], cds[0], cds[1])
            st_ref[:, cols] = st_p * cd + jnp.dot(jnp.concatenate(bws, axis=1), x2, preferred_element_type=F32)

    out = _gate_and_groupnorm(y_ref[...], xa_ref[:, :D_INNER], z_ref[...].astype(F32), drep_ref[...],
                              gnw_ref[...])
    yg_ref[...] = out.astype(yg_ref.dtype)

    @pl.when(c == nc - 1)
    def _():
        conv_ref[...] = tail[SUBLANES - (CONV_W - 1):, :]
        for hp in range(SSD_HEADS // 2):
            blk = st_ref[:, hp * 2 * SSD_HEADDIM:(hp + 1) * 2 * SSD_HEADDIM].T
            ssm_ref[2 * hp:2 * hp + 2] = blk.reshape(2, SSD_HEADDIM, D_STATE)


def _ssd_prompt(xbc, z, dtr, conv_w, conv_b, dtb, alog, d_rep, gn_w):
    batch, seq, _ = xbc.shape
    nb = SSD_SEQS
    blk = lambda b, c: (b, c, 0)
    fixed = lambda b, c: (0, 0)
    nbytes = nb * (2 * CHUNK * (CONV_DIM * 4 + D_INNER * 4 + LANES * 4) + (CHUNK + SUBLANES) * CONV_DIM * 4
                   + 4 * D_STATE * D_INNER * 4 + 12 * CHUNK * CONV_DIM * 4)
    return pl.pallas_call(
        _ssd_prompt_kernel,
        grid=(batch // nb, seq // CHUNK),
        in_specs=[pl.BlockSpec((nb, CHUNK, CONV_DIM), blk), pl.BlockSpec((nb, CHUNK, D_INNER), blk),
                  pl.BlockSpec((nb, CHUNK, LANES), blk),
                  pl.BlockSpec((CONV_W, CONV_DIM), fixed), pl.BlockSpec((1, CONV_DIM), fixed),
                  pl.BlockSpec((1, LANES), fixed), pl.BlockSpec((1, LANES), fixed),
                  pl.BlockSpec((1, D_INNER), fixed), pl.BlockSpec((1, D_INNER), fixed)],
        out_specs=[pl.BlockSpec((nb, CHUNK, D_INNER), blk),
                   pl.BlockSpec((nb, SSD_HEADS, SSD_HEADDIM, D_STATE), lambda b, c: (b, 0, 0, 0)),
                   pl.BlockSpec((nb, CONV_W - 1, CONV_DIM), lambda b, c: (b, 0, 0))],
        out_shape=[jax.ShapeDtypeStruct((batch, seq, D_INNER), BF16),
                   jax.ShapeDtypeStruct((batch, SSD_HEADS, SSD_HEADDIM, D_STATE), F32),
                   jax.ShapeDtypeStruct((batch, CONV_W - 1, CONV_DIM), F32)],
        scratch_shapes=[pltpu.VMEM((nb, CHUNK + SUBLANES, CONV_DIM), F32), pltpu.VMEM((nb, D_STATE, D_INNER), F32),
                        pltpu.VMEM((nb, CHUNK, D_INNER), F32), pltpu.VMEM((nb, CHUNK, CONV_DIM), F32)],
        compiler_params=_params(("arbitrary", "arbitrary"), nbytes),
        name="ssd_prompt",
    )(xbc, z, dtr, conv_w, conv_b, dtb, alog, d_rep, gn_w)


def _ssd_out_kernel(y_ref, xs_ref, z_ref, drep_ref, gnw_ref, w_ref, r_ref, g_ref, h_ref, a_ref, wb_ref):
    @pl.when(pl.program_id(0) == 0)
    def _():
        wb_ref[...] = w_ref[...].astype(BF16)

    sub = min(ROW_SUB // 2, y_ref.shape[0])
    for s in range(y_ref.shape[0] // sub):
        rows = slice(s * sub, (s + 1) * sub)
        yg = _gate_and_groupnorm(y_ref[rows, :], xs_ref[rows, :], z_ref[rows, :].astype(F32), drep_ref[...],
                                 gnw_ref[...])
        h = r_ref[rows, :] + _bdot(yg, wb_ref[...])
        h_ref[rows, :] = h
        a_ref[rows, :] = _rms(h, g_ref[...]).astype(a_ref.dtype)


def _ssd_out(y, xa, z, d_rep, gn_w, w, resid, g, tm):
    m, k = y.shape
    d = w.shape[1]
    row = lambda i: (i, 0)
    fixed = lambda i: (0, 0)
    nbytes = k * d * 6 + 2 * tm * k * 10 + 8 * tm * d * 4 + 6 * tm * k * 4
    return pl.pallas_call(
        _ssd_out_kernel,
        grid=(m // tm,),
        in_specs=[pl.BlockSpec((tm, k), row), pl.BlockSpec((tm, k), row), pl.BlockSpec((tm, k), row),
                  pl.BlockSpec((1, k), fixed), pl.BlockSpec((1, k), fixed),
                  pl.BlockSpec((k, d), fixed, pipeline_mode=pl.Buffered(1)),
                  pl.BlockSpec((tm, d), row), pl.BlockSpec((1, d), fixed)],
        out_specs=[pl.BlockSpec((tm, d), row), pl.BlockSpec((tm, d), row)],
        out_shape=[jax.ShapeDtypeStruct((m, d), F32), jax.ShapeDtypeStruct((m, d), BF16)],
        scratch_shapes=[pltpu.VMEM((k, d), BF16)],
        compiler_params=_params(("arbitrary",), nbytes),
        name="ssd_out",
    )(y, xa, z, d_rep, gn_w, w, resid, g.reshape(1, d))


def _ssd_sample_pre_kernel(xbc_ref, conv_ref, dtr_ref, cw_ref, cb_ref, dtb_ref, alog_ref,
                           nconv_ref, xa_ref, xdt_ref, da_ref, bst_ref, cst_ref):
    xn = xbc_ref[...]
    acc = cb_ref[...] + cw_ref[CONV_W - 1:CONV_W, :] * xn
    for k in range(CONV_W - 1):
        acc = acc + cw_ref[k:k + 1, :] * conv_ref[:, k * CONV_DIM:(k + 1) * CONV_DIM]
    xa = _silu(acc)
    xa_ref[...] = xa
    nconv_ref[:, 0:CONV_DIM] = conv_ref[:, CONV_DIM:2 * CONV_DIM]
    nconv_ref[:, CONV_DIM:2 * CONV_DIM] = conv_ref[:, 2 * CONV_DIM:3 * CONV_DIM]
    nconv_ref[:, 2 * CONV_DIM:3 * CONV_DIM] = xn

    dt = _softplus(dtr_ref[...] + dtb_ref[...])
    a_neg = -jnp.exp(alog_ref[...])
    hh = lax.broadcasted_iota(I32, (LANES, D_INNER), 0)
    cc = lax.broadcasted_iota(I32, (LANES, D_INNER), 1)
    expand = jnp.where(lax.shift_right_logical(cc, 6) == hh, 1.0, 0.0)
    dt_rep = jnp.dot(dt, expand, precision=HIGHEST, preferred_element_type=F32)
    xdt_ref[...] = xa[:, :D_INNER] * dt_rep
    da_ref[...] = jnp.exp(dt * a_neg)

    n = xn.shape[0]
    gs = SSD_GROUPS * D_STATE
    b1, b2, b3 = _split3(xa[:, D_INNER:D_INNER + gs])
    cm = xa[:, D_INNER + gs:]
    c1 = cm.astype(BF16)
    c2 = (cm - c1.astype(F32)).astype(BF16)
    zero = jnp.zeros((n, D_STATE), BF16)
    b_rows, c_rows = [], []
    for g in range(SSD_GROUPS):
        sl = slice(g * D_STATE, (g + 1) * D_STATE)
        b_rows += [b1[:, sl], b2[:, sl], b1[:, sl], b3[:, sl], b2[:, sl], b1[:, sl]] + [zero] * (TERM_ROWS - 6)
        c_rows += [c1[:, sl], c2[:, sl]] + [zero] * (TERM_ROWS - 2)
    bst_ref[...] = jnp.concatenate(b_rows, axis=1)
    cst_ref[...] = jnp.concatenate(c_rows, axis=1)


TERM_ROWS = 16


def _split3(v):
    v1 = v.astype(BF16)
    r1 = v - v1.astype(F32)
    v2 = r1.astype(BF16)
    v3 = (r1 - v2.astype(F32)).astype(BF16)
    return v1, v2, v3


def _ssd_sample_pre(xbc, conv_flat, dtr, conv_w, conv_b, dtb, alog):
    n = xbc.shape[0]
    full = lambda shape: pl.BlockSpec(shape, lambda i: (0,) * len(shape))
    stack = SSD_GROUPS * TERM_ROWS * D_STATE
    return pl.pallas_call(
        _ssd_sample_pre_kernel,
        grid=(1,),
        in_specs=[full((n, CONV_DIM)), full((n, 3 * CONV_DIM)), full((n, LANES)), full((CONV_W, CONV_DIM)),
                  full((1, CONV_DIM)), full((1, LANES)), full((1, LANES))],
        out_specs=[full((n, 3 * CONV_DIM)), full((n, CONV_DIM)), full((n, D_INNER)), full((n, LANES)),
                   full((n, stack)), full((n, stack))],
        out_shape=[jax.ShapeDtypeStruct((n, 3 * CONV_DIM), F32), jax.ShapeDtypeStruct((n, CONV_DIM), F32),
                   jax.ShapeDtypeStruct((n, D_INNER), F32), jax.ShapeDtypeStruct((n, LANES), F32),
                   jax.ShapeDtypeStruct((n, stack), BF16), jax.ShapeDtypeStruct((n, stack), BF16)],
        compiler_params=_params(("arbitrary",), 32 * n * CONV_DIM * 4),
        name="ssd_sample_pre",
    )(xbc, conv_flat, dtr, conv_w, conv_b, dtb, alog)


STATE_SAMPLES = 8


def _ssd_sample_state_kernel(da_ref, st_ref, xdt_ref, bst_ref, cst_ref, nst_ref, y_ref):
    first = pl.program_id(0) * STATE_SAMPLES
    rows_g = D_INNER // SSD_GROUPS
    sub = lax.broadcasted_iota(I32, (TERM_ROWS, D_INNER), 0)
    for s in range(STATE_SAMPLES):
        x1, x2, x3 = [t.astype(F32) for t in _split3(xdt_ref[s])]
        terms = jnp.where(sub == 0, x1, jnp.where(sub == 1, x1, jnp.where(sub == 2, x2, jnp.where(
            sub == 3, x1, jnp.where(sub == 4, x2, jnp.where(sub == 5, x3, 0.0)))))).astype(BF16)
        ys = []
        for g in range(SSD_GROUPS):
            tile = slice(g * TERM_ROWS, (g + 1) * TERM_ROWS)
            upd = lax.dot_general(terms[:, g * rows_g:(g + 1) * rows_g], bst_ref[s, tile, :],
                                  (((0,), (0,)), ((), ())), preferred_element_type=F32)
            halves = []
            for hh in range(rows_g // SSD_HEADDIM):
                head = g * (rows_g // SSD_HEADDIM) + hh
                rows = slice(head * SSD_HEADDIM, (head + 1) * SSD_HEADDIM)
                h_new = (da_ref[first + s, head] * st_ref[s, rows, :]
                         + upd[hh * SSD_HEADDIM:(hh + 1) * SSD_HEADDIM, :])
                nst_ref[s, rows, :] = h_new
                halves.append(h_new)
            h_g = jnp.concatenate(halves, axis=0)
            h_hi = h_g.astype(BF16)
            h_lo = (h_g - h_hi.astype(F32)).astype(BF16)
            nt = (((1,), (1,)), ((), ()))
            o_hi = lax.dot_general(cst_ref[s, tile, :], h_hi, nt, preferred_element_type=F32)
            o_lo = lax.dot_general(cst_ref[s, tile, :], h_lo, nt, preferred_element_type=F32)
            ys.append(o_hi[0:1, :] + o_hi[1:2, :] + o_lo[0:1, :])
        y_ref[s] = jnp.concatenate(ys, axis=1)


def _ssd_sample_state(state, xdt, da, bst, cst):
    n = state.shape[0]
    ns = STATE_SAMPLES
    stack_rows = SSD_GROUPS * TERM_ROWS
    return pl.pallas_call(
        _ssd_sample_state_kernel,
        grid=(n // ns,),
        in_specs=[pl.BlockSpec(memory_space=pltpu.SMEM),
                  pl.BlockSpec((ns, D_INNER, D_STATE), lambda i: (i, 0, 0)),
                  pl.BlockSpec((ns, 1, D_INNER), lambda i: (i, 0, 0)),
                  pl.BlockSpec((ns, stack_rows, D_STATE), lambda i: (i, 0, 0)),
                  pl.BlockSpec((ns, stack_rows, D_STATE), lambda i: (i, 0, 0))],
        out_specs=[pl.BlockSpec((ns, D_INNER, D_STATE), lambda i: (i, 0, 0)),
                   pl.BlockSpec((ns, 1, D_INNER), lambda i: (i, 0, 0))],
        out_shape=[jax.ShapeDtypeStruct((n, D_INNER, D_STATE), F32),
                   jax.ShapeDtypeStruct((n, 1, D_INNER), F32)],
        compiler_params=_params(("arbitrary",), 6 * ns * D_INNER * D_STATE * 4),
        name="ssd_sample_state",
    )(da, state, xdt, bst, cst)


def _layernorm(v, g, b):
    mu = jnp.mean(v, axis=-1, keepdims=True)
    d = v - mu
    var = jnp.mean(d * d, axis=-1, keepdims=True)
    return d * lax.rsqrt(var + EPS) * g + b


def _sgu_sample_kernel(u_ref, v_ref, lng_ref, lnb_ref, w0_ref, b0_ref, o_ref, vn_ref):
    vn = _layernorm(v_ref[...], lng_ref[...], lnb_ref[...])
    vn_ref[...] = vn
    o_ref[...] = (u_ref[...] * (w0_ref[...] * vn + b0_ref[...])).astype(o_ref.dtype)


def _sgu_sample(uv, ln_g, ln_b, w0_rep, b0_rep):
    n = uv.shape[0]
    fixed2 = lambda i: (0, 0)
    return pl.pallas_call(
        _sgu_sample_kernel,
        grid=(1,),
        in_specs=[pl.BlockSpec((n, D_SGU), lambda i: (0, 0)), pl.BlockSpec((n, D_SGU), lambda i: (0, 1)),
                  pl.BlockSpec((1, D_SGU), fixed2), pl.BlockSpec((1, D_SGU), fixed2),
                  pl.BlockSpec((1, D_SGU), fixed2), pl.BlockSpec((1, D_SGU), fixed2)],
        out_specs=[pl.BlockSpec((n, D_SGU), fixed2), pl.BlockSpec((n, D_SGU), fixed2)],
        out_shape=[jax.ShapeDtypeStruct((n, D_SGU), BF16), jax.ShapeDtypeStruct((n, D_SGU), F32)],
        compiler_params=_params(("arbitrary",), 12 * n * D_SGU * 4),
        name="sgu_sample",
    )(uv, uv, ln_g, ln_b, w0_rep, b0_rep)


def _experts_to_lanes(col, sub, lane):
    return jnp.sum(jnp.where(sub == lane, col, 0.0), axis=0, keepdims=True)


def _route_kernel(lg_ref, loc_ref, gate_ref, bmeta_ref, tmeta_ref, *, tile, n_valid):
    nb, _, tb = lg_ref.shape
    sub = lax.broadcasted_iota(I32, (N_EXPERTS, LANES), 0)
    lane = lax.broadcasted_iota(I32, (N_EXPERTS, LANES), 1)
    subf = lax.broadcasted_iota(I32, (N_EXPERTS, tb), 0).astype(F32)
    tok = lax.broadcasted_iota(I32, (N_EXPERTS, tb), 1)
    incl = jnp.where(lax.broadcasted_iota(I32, (tb, tb), 0) <= lax.broadcasted_iota(I32, (tb, tb), 1), 1.0, 0.0)
    neg = jnp.float32(-jnp.inf)
    none = jnp.float32(N_EXPERTS)

    def select(k):
        blk = lg_ref[k]
        l = blk[0:N_EXPERTS] + blk[N_EXPERTS:2 * N_EXPERTS]
        m1 = jnp.max(l, axis=0, keepdims=True)
        i1 = jnp.min(jnp.where(l == m1, subf, none), axis=0, keepdims=True)
        l2 = jnp.where(subf == i1, neg, l)
        m2 = jnp.max(l2, axis=0, keepdims=True)
        i2 = jnp.min(jnp.where(l2 == m2, subf, none), axis=0, keepdims=True)
        valid = (k * tb + tok) < n_valid
        sel = jnp.where(valid, jnp.where(subf == i1, 1.0, jnp.where(subf == i2, 1.0, 0.0)), 0.0)
        return m1, i1, m2, i2, valid, sel

    def up8(rows):
        return jnp.ceil(rows / SUBLANES) * SUBLANES

    def count_body(k, carry):
        return carry + jnp.sum(select(k)[5], axis=1, keepdims=True)

    counts = lax.fori_loop(0, nb, count_body, jnp.zeros((N_EXPERTS, 1), F32))
    tiles = jnp.ceil(counts / tile)
    cum_incl = jnp.sum(jnp.where(lane <= sub, _experts_to_lanes(tiles, sub, lane), 0.0), axis=1, keepdims=True)
    offset = (cum_incl - tiles) * tile
    tile_expert = jnp.sum(jnp.where(lane.astype(F32) >= cum_incl, 1.0, 0.0), axis=0, keepdims=True)
    tile_expert = jnp.minimum(tile_expert, N_EXPERTS - 1.0)
    n_used = jnp.max(cum_incl, axis=0, keepdims=True)
    region_end = _experts_to_lanes(offset + up8(counts), sub, lane)
    region_pad = _experts_to_lanes(tiles * tile - up8(counts), sub, lane)
    tmeta_ref[...] = jnp.where(sub == 0, tile_expert, jnp.where(sub == 1, n_used, jnp.where(
        sub == 2, region_end, jnp.where(sub == 3, region_pad, 0.0)))).astype(I32)

    def place_body(k, before):
        m1, i1, m2, i2, valid, sel = select(k)
        run = jnp.dot(sel, incl, preferred_element_type=F32)
        cnt = run[:, tb - 1:tb]
        lead = before - jnp.floor(before / SUBLANES) * SUBLANES
        span = up8(lead + cnt)
        lstart = jnp.sum(jnp.where(lane < sub, _experts_to_lanes(span, sub, lane), 0.0), axis=1, keepdims=True)
        local = lstart + lead + run - sel
        loc1 = jnp.sum(jnp.where(subf == i1, local, 0.0), axis=0, keepdims=True)
        loc2 = jnp.sum(jnp.where(subf == i2, local, 0.0), axis=0, keepdims=True)
        live = (k * tb + tok[0:1, :]) < n_valid
        loc_ref[k] = jnp.where(subf == 0.0, jnp.where(live, loc1, -1.0),
                               jnp.where(subf == 1.0, jnp.where(live, loc2, -1.0), 0.0))
        e = jnp.exp(m2 - m1)
        gate_ref[k] = jnp.where(subf == 0.0, 1.0 / (1.0 + e), jnp.where(subf == 1.0, e / (1.0 + e), 0.0))
        moved = jnp.where(cnt > 0.0, span, 0.0)
        bmeta_ref[k] = jnp.where(lane == 0, offset + before - lead, jnp.where(lane == 1, moved, jnp.where(
            lane == 2, lstart, jnp.where(lane == 3, lead, jnp.where(lane == 4, lead + cnt, 0.0))))).astype(I32)
        return before + cnt

    lax.fori_loop(0, nb, place_body, jnp.zeros((N_EXPERTS, 1), F32))


def _route(logits3, tile, n_valid):
    nb, _, tb = logits3.shape
    tok_blk = pl.BlockSpec((nb, N_EXPERTS, tb), lambda i: (0, 0, 0))
    return pl.pallas_call(
        functools.partial(_route_kernel, tile=tile, n_valid=n_valid),
        grid=(1,),
        in_specs=[pl.BlockSpec((nb, 2 * N_EXPERTS, tb), lambda i: (0, 0, 0))],
        out_specs=[tok_blk, tok_blk, pl.BlockSpec((nb, N_EXPERTS, LANES), lambda i: (0, 0, 0)),
                   pl.BlockSpec((N_EXPERTS, LANES), lambda i: (0, 0))],
        out_shape=[jax.ShapeDtypeStruct((nb, N_EXPERTS, tb), F32),
                   jax.ShapeDtypeStruct((nb, N_EXPERTS, tb), F32),
                   jax.ShapeDtypeStruct((nb, N_EXPERTS, LANES), I32),
                   jax.ShapeDtypeStruct((N_EXPERTS, LANES), I32)],
        compiler_params=_params(("arbitrary",), 16 * nb * N_EXPERTS * tb * 4),
        name="moe_route",
    )(logits3)


RUN_FIELDS = 5


def _run_copies(bm_ref, blk, vmem_ref, hbm_ref, sem, *, tb, to_hbm, wait):
    for e in range(N_EXPERTS):
        base = (blk * N_EXPERTS + e) * RUN_FIELDS
        start, cnt, lstart = bm_ref[base], bm_ref[base + 1], bm_ref[base + 2]
        off = 0
        size = tb
        while size >= SUBLANES:
            @pl.when((cnt & size) != 0)
            def _(size=size, off=off):
                v = vmem_ref.at[pl.ds(pl.multiple_of(lstart + off, SUBLANES), size)]
                h = hbm_ref.at[pl.ds(pl.multiple_of(start + off, SUBLANES), size)]
                cp = pltpu.make_async_copy(v, h, sem) if to_hbm else pltpu.make_async_copy(h, v, sem)
                if wait:
                    cp.wait()
                else:
                    cp.start()

            off = off + (cnt & size)
            size //= 2


def _compact_rows(tb):
    return 2 * tb + N_EXPERTS * 2 * SUBLANES + 2 * SUBLANES


def _selection(loc_ref, rows, tb):
    r = lax.broadcasted_iota(I32, (rows, tb), 0).astype(F32)
    return r == loc_ref[0, 0:1, :], r == loc_ref[0, 1:2, :]


def _zero_fill(zm_ref, zbuf_ref, xs_ref, zsem, *, tm, n_tiles, wait):
    def fill(row0, size):
        cp = pltpu.make_async_copy(zbuf_ref.at[pl.ds(0, size)],
                                   xs_ref.at[pl.ds(pl.multiple_of(row0, SUBLANES), size)], zsem)
        if wait:
            cp.wait()
        else:
            cp.start()

    for e in range(N_EXPERTS):
        end, pad = zm_ref[1 + 2 * e], zm_ref[2 + 2 * e]
        off = 0
        size = tm // 2
        while size >= SUBLANES:
            @pl.when((pad & size) != 0)
            def _(size=size, off=off):
                fill(end + off, size)

            off = off + (pad & size)
            size //= 2
    for t in range(n_tiles):
        @pl.when(t >= zm_ref[0])
        def _(t=t):
            fill(t * tm, tm)


def _moe_scatter_kernel(bm_ref, zm_ref, a_ref, loc_ref, gate_ref, xs_ref,
                        buf_ref, zbuf_ref, carry_ref, sem, zsem, *, tb, tm, n_tiles):
    blk = pl.program_id(0)
    last = pl.num_programs(0) - 1
    slot = blk % 2
    d = a_ref.shape[1]

    @pl.when(blk == 0)
    def _():
        zbuf_ref[...] = jnp.zeros_like(zbuf_ref)
        carry_ref[...] = jnp.zeros_like(carry_ref)
        _zero_fill(zm_ref, zbuf_ref, xs_ref, zsem, tm=tm, n_tiles=n_tiles, wait=False)

    def copies(block, wait):
        s = block % 2
        _run_copies(bm_ref, block, buf_ref.at[s], xs_ref, sem.at[s], tb=tb, to_hbm=True, wait=wait)

    first, second = _selection(loc_ref, _compact_rows(tb), tb)
    pick = jnp.where(first, 1.0, jnp.where(second, 1.0, 0.0)).astype(BF16)
    buf_ref[slot, :, 0:d] = jnp.dot(pick, a_ref[...], preferred_element_type=F32)
    gsel = jnp.where(first, gate_ref[0, 0:1, :], jnp.where(second, gate_ref[0, 1:2, :], 0.0))
    buf_ref[slot, :, d:d + LANES] = jnp.broadcast_to(jnp.sum(gsel, axis=1, keepdims=True),
                                                    (buf_ref.shape[1], LANES))

    row = lax.broadcasted_iota(I32, (SUBLANES, buf_ref.shape[2]), 0)
    for e in range(N_EXPERTS):
        base = (blk * N_EXPERTS + e) * RUN_FIELDS
        lstart, total = bm_ref[base + 2], bm_ref[base + 4]
        head = pl.ds(pl.multiple_of(lstart, SUBLANES), SUBLANES)
        buf_ref[slot, head, :] = buf_ref[slot, head, :] + carry_ref[e]
        full = lax.shift_left(lax.shift_right_logical(total, 3), 3)
        tail = pl.ds(pl.multiple_of(lstart + full, SUBLANES), SUBLANES)
        carry_ref[e] = jnp.where(row < total - full, buf_ref[slot, tail, :], 0.0)

    @pl.when(blk >= 1)
    def _():
        copies(blk - 1, True)

    copies(blk, False)

    @pl.when(blk == last)
    def _():
        copies(blk, True)
        _zero_fill(zm_ref, zbuf_ref, xs_ref, zsem, tm=tm, n_tiles=n_tiles, wait=True)


def _moe_scatter(a, loc, gate, bmeta, zmeta, n_slots, tb, tm):
    m, d = a.shape
    width = d + LANES
    tok_spec = pl.BlockSpec((1, N_EXPERTS, tb), lambda i, bm, zm: (i, 0, 0))
    grid_spec = pltpu.PrefetchScalarGridSpec(
        num_scalar_prefetch=2,
        grid=(m // tb,),
        in_specs=[pl.BlockSpec((tb, d), lambda i, bm, zm: (i, 0)), tok_spec, tok_spec],
        out_specs=pl.BlockSpec(memory_space=pl.ANY),
        scratch_shapes=[pltpu.VMEM((2, _compact_rows(tb), width), F32), pltpu.VMEM((tm, width), F32),
                        pltpu.VMEM((N_EXPERTS, SUBLANES, width), F32),
                        pltpu.SemaphoreType.DMA((2,)), pltpu.SemaphoreType.DMA(())],
    )
    return pl.pallas_call(
        functools.partial(_moe_scatter_kernel, tb=tb, tm=tm, n_tiles=n_slots // tm),
        grid_spec=grid_spec,
        out_shape=jax.ShapeDtypeStruct((n_slots, width), F32),
        compiler_params=_params(("arbitrary",), 16 * tb * d * 4),
        name="moe_scatter",
    )(bmeta, zmeta, a, loc, gate)


def _new_weights(te_ref):
    i = pl.program_id(1)
    return (i == 0) | (te_ref[i] != te_ref[jnp.maximum(i - 1, 0)])


def _moe_gu_kernel(te_ref, nu_ref, x_ref, wg_ref, wu_ref, o_ref, wgb_ref, wub_ref):
    i = pl.program_id(1)

    @pl.when(_new_weights(te_ref))
    def _():
        wgb_ref[...] = wg_ref[0].astype(BF16)
        wub_ref[...] = wu_ref[0].astype(BF16)

    @pl.when(i < nu_ref[0])
    def _():
        x = x_ref[...].astype(BF16)
        for c in range(o_ref.shape[1] // MXU_COLS):
            cols = slice(c * MXU_COLS, (c + 1) * MXU_COLS)
            g = jnp.dot(x, wgb_ref[:, cols], preferred_element_type=F32)
            u = jnp.dot(x, wub_ref[:, cols], preferred_element_type=F32)
            o_ref[:, cols] = (_silu(g) * u).astype(o_ref.dtype)

    @pl.when(i >= nu_ref[0])
    def _():
        o_ref[...] = jnp.zeros_like(o_ref)


def _used_tile(i, nu):
    return jnp.minimum(i, nu[0] - 1)


def _moe_gu(xs, w_gu, te, nu, tm, tn):
    s = xs.shape[0]
    k = w_gu.shape[1]
    f = D_FF_EXPERT
    nbytes = 2 * (2 * k * tn * 4 + k * tn * 2) + 2 * tm * k * 4 + 2 * tm * tn * 2 + 4 * tm * MXU_COLS * 4
    grid_spec = pltpu.PrefetchScalarGridSpec(
        num_scalar_prefetch=2,
        grid=(f // tn, s // tm),
        in_specs=[pl.BlockSpec((tm, k), lambda j, i, te, nu: (_used_tile(i, nu), 0)),
                  pl.BlockSpec((1, k, tn), lambda j, i, te, nu: (te[i], 0, j)),
                  pl.BlockSpec((1, k, tn), lambda j, i, te, nu: (te[i], 0, j + f // tn))],
        out_specs=pl.BlockSpec((tm, tn), lambda j, i, te, nu: (i, j)),
        scratch_shapes=[pltpu.VMEM((k, tn), BF16), pltpu.VMEM((k, tn), BF16)],
    )
    return pl.pallas_call(
        _moe_gu_kernel,
        grid_spec=grid_spec,
        out_shape=jax.ShapeDtypeStruct((s, f), BF16),
        compiler_params=_params(("arbitrary", "arbitrary"), nbytes),
        name="moe_gate_up",
    )(te, nu, xs, w_gu, w_gu)


def _moe_down_kernel(te_ref, nu_ref, x_ref, w_ref, gs_ref, o_ref, wb_ref):
    i = pl.program_id(1)

    @pl.when(_new_weights(te_ref))
    def _():
        wb_ref[...] = w_ref[0].astype(BF16)

    @pl.when(i < nu_ref[0])
    def _():
        o_ref[...] = gs_ref[:, 0:1] * jnp.dot(x_ref[...], wb_ref[...], preferred_element_type=F32)

    @pl.when(i >= nu_ref[0])
    def _():
        o_ref[...] = jnp.zeros_like(o_ref)


def _moe_down(act, w_down, xs, te, nu, tm, tn):
    s, k = act.shape
    d = w_down.shape[2]
    gate_block = d // LANES
    nbytes = 2 * k * tn * 4 + k * tn * 2 + 2 * tm * k * 2 + 3 * tm * tn * 4
    grid_spec = pltpu.PrefetchScalarGridSpec(
        num_scalar_prefetch=2,
        grid=(d // tn, s // tm),
        in_specs=[pl.BlockSpec((tm, k), lambda j, i, te, nu: (_used_tile(i, nu), 0)),
                  pl.BlockSpec((1, k, tn), lambda j, i, te, nu: (te[i], 0, j)),
                  pl.BlockSpec((tm, LANES), lambda j, i, te, nu: (_used_tile(i, nu), gate_block))],
        out_specs=pl.BlockSpec((tm, tn), lambda j, i, te, nu: (i, j)),
        scratch_shapes=[pltpu.VMEM((k, tn), BF16)],
    )
    return pl.pallas_call(
        _moe_down_kernel,
        grid_spec=grid_spec,
        out_shape=jax.ShapeDtypeStruct((s, d), F32),
        compiler_params=_params(("arbitrary", "arbitrary"), nbytes),
        name="moe_down",
    )(te, nu, act, w_down, xs)


def _moe_combine_kernel(bm_ref, h_ref, locc_ref, g_ref, ys_ref, op_ref, os_ref, ybuf_ref, sem, *, tb, prompt_blocks):
    blk = pl.program_id(0)
    slot = blk % 2

    def copies(block, wait):
        s = block % 2
        _run_copies(bm_ref, block, ybuf_ref.at[s], ys_ref, sem.at[s], tb=tb, to_hbm=False, wait=wait)

    @pl.when(blk == 0)
    def _():
        ybuf_ref[...] = jnp.zeros_like(ybuf_ref)
        copies(blk, False)

    @pl.when(blk + 1 < pl.num_programs(0))
    def _():
        copies(blk + 1, False)

    r = lax.broadcasted_iota(I32, (tb, _compact_rows(tb)), 1).astype(F32)
    pick = jnp.where(r == locc_ref[:, 0:1], 1.0, jnp.where(r == locc_ref[:, 1:2], 1.0, 0.0)).astype(BF16)
    copies(blk, True)
    moe = jnp.dot(pick, ybuf_ref[slot].astype(BF16), preferred_element_type=F32)
    out = _rms(h_ref[...] + moe, g_ref[...])

    @pl.when(blk < prompt_blocks)
    def _():
        op_ref[...] = out

    @pl.when(blk == prompt_blocks)
    def _():
        os_ref[...] = out


def _moe_combine(h, locc, bmeta, ys, g, tb, n_prompt):
    m, d = h.shape
    prompt_blocks = n_prompt // tb
    any_spec = pl.BlockSpec(memory_space=pl.ANY)
    grid_spec = pltpu.PrefetchScalarGridSpec(
        num_scalar_prefetch=1,
        grid=(m // tb,),
        in_specs=[pl.BlockSpec((tb, d), lambda i, bm: (i, 0)),
                  pl.BlockSpec((tb, N_EXPERTS), lambda i, bm: (i, 0)),
                  pl.BlockSpec((1, d), lambda i, bm: (0, 0)),
                  any_spec],
        out_specs=[pl.BlockSpec((tb, d), lambda i, bm: (jnp.minimum(i, prompt_blocks - 1), 0)),
                   pl.BlockSpec((tb, d), lambda i, bm: (0, 0))],
        scratch_shapes=[pltpu.VMEM((2, _compact_rows(tb), d), F32), pltpu.SemaphoreType.DMA((2,))],
    )
    return pl.pallas_call(
        functools.partial(_moe_combine_kernel, tb=tb, prompt_blocks=prompt_blocks),
        grid_spec=grid_spec,
        out_shape=[jax.ShapeDtypeStruct((n_prompt, d), F32), jax.ShapeDtypeStruct((tb, d), F32)],
        compiler_params=_params(("arbitrary",), 16 * tb * d * 4),
        name="moe_combine",
    )(bmeta, h, locc, g.reshape(1, d), ys)


MOE_TOKEN_BLOCK = 256
MOE_ROW_TILE = 512


def _moe(h, a, lg_parts, n_valid, n_prompt, w_gu, w_down, g_final):
    m, d = h.shape
    tb, tm = MOE_TOKEN_BLOCK, MOE_ROW_TILE
    nb = m // tb
    lg3 = lg_parts[:, :2 * N_EXPERTS].reshape(nb, tb, 2 * N_EXPERTS).transpose(0, 2, 1)
    n_tiles = pl.cdiv(2 * n_valid, tm) + N_EXPERTS
    loc, gate, bmeta, tmeta = _route(lg3, tm, n_valid)
    te = tmeta[0, :n_tiles]
    nu = tmeta[1, 0:1]
    bm = bmeta[:, :, :RUN_FIELDS].reshape(-1)
    zm = jnp.concatenate([nu, tmeta[2:4, :N_EXPERTS].T.reshape(-1)])
    locc = loc.transpose(0, 2, 1).reshape(m, N_EXPERTS)
    xs = _moe_scatter(a, loc, gate, bm, zm, n_tiles * tm, tb, tm)
    act = _moe_gu(xs, w_gu, te, nu, tm, D_FF_EXPERT // 2)
    ys = _moe_down(act, w_down, xs, te, nu, tm, d)
    return _moe_combine(h, locc, bm, ys, g_final, tb, n_prompt)


def _pad_lanes(v):
    return jnp.pad(v.reshape(1, -1), ((0, 0), (0, LANES - v.shape[-1])))


def kernel(x_prompt, x_sample, state_ssm, state_conv, norm_mix, norm_ffn, norm_final, ssd_w_in, ssd_conv_w,
           ssd_conv_b, ssd_dt_bias, ssd_a_log, ssd_d, ssd_gnorm, ssd_w_out, sgu_w_in, sgu_b_in, sgu_ln_g,
           sgu_ln_b, sgu_w_s, sgu_b_s, sgu_w_out, ffn_w_gu, ffn_w_down, moe_w_router, moe_w_gu, moe_w_down):
    batch, seq, d = x_prompt.shape
    n_s = x_sample.shape[0]
    mp = batch * seq
    xp = x_prompt.reshape(mp, d)
    xs = x_sample.reshape(n_s, d)

    in_hbm = lambda w: pltpu.with_memory_space_constraint(w, pltpu.HBM)
    w_in = in_hbm(ssd_w_in[0].T)
    sgu_w_in_hbm = in_hbm(sgu_w_in[0])
    ssd_w_out_hbm = in_hbm(ssd_w_out[0])
    sgu_w_out_hbm = in_hbm(sgu_w_out[0])
    w_dt = jnp.pad(w_in[D_INNER + CONV_DIM:], ((0, LANES - SSD_HEADS), (0, 0)))
    dtb = _pad_lanes(ssd_dt_bias[0])
    alog = _pad_lanes(ssd_a_log[0])
    d_rep = jnp.repeat(ssd_d[0], SSD_HEADDIM).reshape(1, D_INNER)
    gn_w = ssd_gnorm[0].reshape(1, D_INNER)
    conv_w = ssd_conv_w[0]
    conv_b = ssd_conv_b[0].reshape(1, CONV_DIM)
    ln_g = sgu_ln_g[0].reshape(1, D_SGU)
    ln_b = sgu_ln_b[0].reshape(1, D_SGU)
    bs_t = jnp.pad(sgu_b_s[0].T, ((0, 0), (0, LANES - SGU_HEADS)))
    w0_rep = jnp.repeat(sgu_w_s[0][:, 0, 0], SGU_HEAD_DIM).reshape(1, D_SGU)
    b0_rep = jnp.repeat(sgu_b_s[0][:, 0], SGU_HEAD_DIM).reshape(1, D_SGU)
    wr_hi = moe_w_router[0].astype(BF16)
    wr_lo = (moe_w_router[0] - wr_hi.astype(F32)).astype(BF16)
    w3 = jnp.concatenate([wr_hi, wr_lo, jnp.zeros((d, LANES - 2 * N_EXPERTS), BF16)], axis=1)

    def dense_ffn(h1, a1, tm):
        act = _mm_swiglu(a1, ffn_w_gu[0], D_FF_DENSE, min(2 * tm, a1.shape[0]), 1408, "ffn_gate_up")
        return _mm_resnorm(act, ffn_w_down[0], h1, norm_mix[1], tm, "ffn_down")

    a0p, zp, dtrp = _norm_in_proj(xp, norm_mix[0], w_in, w_dt, 1024)
    xbcp = _mm_plain(a0p, w_in, D_INNER, CONV_DIM, 1024, 1024, F32, "ssd_in_xbc")
    per_seq = lambda t: t.reshape(batch, seq, t.shape[-1])
    ygp, ssm_p, conv_p = _ssd_prompt(per_seq(xbcp), per_seq(zp), per_seq(dtrp), conv_w, conv_b, dtb, alog, d_rep,
                                     gn_w)
    h1p, a1p = _mm_resnorm(ygp.reshape(mp, D_INNER), ssd_w_out_hbm, xp, norm_ffn[0], 512, "ssd_out")
    h2p, a2p = dense_ffn(h1p, a1p, 512)

    a0s, zs, dtrs = _norm_in_proj(xs, norm_mix[0], w_in, w_dt, n_s)
    xbcs = _mm_plain(a0s, w_in, D_INNER, CONV_DIM, n_s, 1024, F32, "ssd_in_xbc")
    conv_s, xas, xdts, das, bsts, csts = _ssd_sample_pre(xbcs, state_conv[0].reshape(n_s, 3 * CONV_DIM), dtrs,
                                                         conv_w, conv_b, dtb, alog)
    stack_rows = SSD_GROUPS * TERM_ROWS
    ssm_s, ys3 = _ssd_sample_state(state_ssm[0].reshape(n_s, D_INNER, D_STATE), xdts.reshape(n_s, 1, D_INNER),
                                   das[:, :SSD_HEADS], bsts.reshape(n_s, stack_rows, D_STATE),
                                   csts.reshape(n_s, stack_rows, D_STATE))
    h1s, a1s = _ssd_out(ys3.reshape(n_s, D_INNER), xas, zs, d_rep, gn_w, ssd_w_out_hbm, xs, norm_ffn[0], n_s)
    h2s, a2s = dense_ffn(h1s, a1s, n_s)

    uvp = _mm_bias_gelu(a2p, sgu_w_in_hbm, sgu_b_in[0], 1024, 2048, BF16, "sgu_in")
    uvs = _mm_bias_gelu(a2s, sgu_w_in_hbm, sgu_b_in[0], n_s, 1024, F32, "sgu_in")
    ss, v_s = _sgu_sample(uvs, ln_g, ln_b, w0_rep, b0_rep)
    tm_out = 512
    pad_rows = ((0, tm_out - n_s), (0, 0))
    h3, a3, lg = _sgu_out(uvp, jnp.pad(ss, pad_rows), ln_g, ln_b, sgu_w_s[0], bs_t, sgu_w_out_hbm, h2p,
                          jnp.pad(h2s, pad_rows), norm_ffn[1], w3, tm_out)

    y_prompt, y_tail = _moe(h3, a3, lg, mp + n_s, mp, moe_w_gu[0], moe_w_down[0], norm_final)
    y_sample = y_tail[:n_s]

    return (y_prompt.reshape(batch, seq, d),
            y_sample.reshape(n_s, 1, d),
            ssm_p[None],
            conv_p[None],
            ssm_s.reshape(1, n_s, SSD_HEADS, SSD_HEADDIM, D_STATE),
            conv_s.reshape(1, n_s, CONV_W - 1, CONV_DIM),
            v_s.reshape(1, n_s, 1, D_SGU))
```

```python
import functools

import jax
import jax.numpy as jnp
from jax import lax
from jax.experimental import pallas as pl
from jax.experimental.pallas import tpu as pltpu

F32 = jnp.float32
BF16 = jnp.bfloat16
I32 = jnp.int32
HIGHEST = lax.Precision.HIGHEST

D_MODEL = 1024
D_INNER = 2048
SSD_HEADS = 32
SSD_HEADDIM = 64
SSD_GROUPS = 4
D_STATE = 128
CONV_W = 4
CONV_DIM = D_INNER + 2 * SSD_GROUPS * D_STATE
CHUNK = 128
D_SGU = 2048
SGU_HEADS = 8
SGU_HEAD_DIM = D_SGU // SGU_HEADS
D_FF_DENSE = 2816
N_EXPERTS = 8
D_FF_EXPERT = 3584
EPS = 1e-5

MXU_COLS = 256
ROW_SUB = 256
LANES = 128
SUBLANES = 8
VMEM_CAP = 60000 * 1024
VMEM_FLOOR = 32 * 1024 * 1024


def _vmem_limit(nbytes):
    return int(min(max(nbytes * 5 // 4 + (4 << 20), VMEM_FLOOR), VMEM_CAP))


def _params(sem, nbytes):
    return pltpu.CompilerParams(dimension_semantics=sem, vmem_limit_bytes=_vmem_limit(nbytes))


LOG2E = 1.4426950408889634


def _silu(x):
    return x / (1.0 + jnp.exp2(x * (-LOG2E)))


def _rms(x, g):
    return x * lax.rsqrt(jnp.mean(x * x, axis=-1, keepdims=True) + EPS) * g


def _bdot(a, b):
    return jnp.dot(a.astype(BF16), b.astype(BF16), preferred_element_type=F32)


def _cast_transposed(wt_ref, wb_ref):
    step = 2 * MXU_COLS
    for r0 in range(0, wt_ref.shape[0], step):
        r1 = min(r0 + step, wt_ref.shape[0])
        wb_ref[:, r0:r1] = wt_ref[r0:r1, :].T.astype(BF16)


def _mm_plain_kernel(x_ref, wt_ref, o_ref, wb_ref):
    @pl.when(pl.program_id(1) == 0)
    def _():
        _cast_transposed(wt_ref, wb_ref)

    o_ref[...] = _bdot(x_ref[...], wb_ref[...]).astype(o_ref.dtype)


def _mm_bias_gelu_kernel(x_ref, w_ref, b_ref, o_ref, wb_ref):
    @pl.when(pl.program_id(1) == 0)
    def _():
        wb_ref[...] = w_ref[...].astype(BF16)

    x = x_ref[...].astype(BF16)
    for c0 in range(0, o_ref.shape[1], 2 * MXU_COLS):
        cols = slice(c0, c0 + 2 * MXU_COLS)
        y = jnp.dot(x, wb_ref[:, cols], preferred_element_type=F32) + b_ref[:, cols]
        o_ref[:, cols] = (0.5 * y * (1.0 + lax.erf(y * (2.0 ** -0.5)))).astype(o_ref.dtype)


def _mm_swiglu_kernel(x_ref, wg_ref, wu_ref, o_ref, wgb_ref, wub_ref):
    @pl.when(pl.program_id(1) == 0)
    def _():
        wgb_ref[...] = wg_ref[...].astype(BF16)
        wub_ref[...] = wu_ref[...].astype(BF16)

    x = x_ref[...].astype(BF16)
    tn = o_ref.shape[1]
    for c0 in range(0, tn, MXU_COLS):
        cols = slice(c0, min(c0 + MXU_COLS, tn))
        g = jnp.dot(x, wgb_ref[:, cols], preferred_element_type=F32)
        u = jnp.dot(x, wub_ref[:, cols], preferred_element_type=F32)
        o_ref[:, cols] = (_silu(g) * u).astype(o_ref.dtype)


def _wspec(k, tn, col_block0):
    return pl.BlockSpec((k, tn), lambda j, i: (0, j + col_block0), pipeline_mode=pl.Buffered(1))


def _mm_plain(x, wt, col0, n, tm, tn, out_dtype, name):
    m, k = x.shape
    nbytes = k * tn * 6 + 2 * tm * k * x.dtype.itemsize + 2 * tm * tn * 4 + 2 * MXU_COLS * k * 8
    row_block0 = col0 // tn
    return pl.pallas_call(
        _mm_plain_kernel,
        grid=(n // tn, m // tm),
        in_specs=[pl.BlockSpec((tm, k), lambda j, i: (i, 0)),
                  pl.BlockSpec((tn, k), lambda j, i: (j + row_block0, 0), pipeline_mode=pl.Buffered(1))],
        out_specs=pl.BlockSpec((tm, tn), lambda j, i: (i, j)),
        out_shape=jax.ShapeDtypeStruct((m, n), out_dtype),
        scratch_shapes=[pltpu.VMEM((k, tn), BF16)],
        compiler_params=_params(("arbitrary", "arbitrary"), nbytes),
        name=name,
    )(x, wt)


def _norm_in_proj_kernel(x_ref, g_ref, wz_ref, wdt_ref, a_ref, z_ref, dtr_ref, wzb_ref, wdtb_ref):
    @pl.when(pl.program_id(0) == 0)
    def _():
        _cast_transposed(wz_ref, wzb_ref)
        _cast_transposed(wdt_ref, wdtb_ref)

    a = _rms(x_ref[...], g_ref[...]).astype(BF16)
    a_ref[...] = a
    for c0 in range(0, z_ref.shape[1], 2 * MXU_COLS):
        cols = slice(c0, c0 + 2 * MXU_COLS)
        z_ref[:, cols] = _silu(jnp.dot(a, wzb_ref[:, cols], preferred_element_type=F32)).astype(z_ref.dtype)
    dtr_ref[...] = jnp.dot(a, wdtb_ref[...], preferred_element_type=F32)


def _norm_in_proj(x, g, w_in_t, w_dt_t, tm):
    m, k = x.shape
    n = D_INNER
    row = lambda i: (i, 0)
    fixed = lambda i: (0, 0)
    nbytes = k * n * 6 + k * LANES * 6 + 2 * tm * k * 4 + 2 * tm * k * 2 + 2 * tm * n * 2 + 2 * tm * n * 4
    return pl.pallas_call(
        _norm_in_proj_kernel,
        grid=(m // tm,),
        in_specs=[pl.BlockSpec((tm, k), row), pl.BlockSpec((1, k), fixed),
                  pl.BlockSpec((n, k), fixed, pipeline_mode=pl.Buffered(1)),
                  pl.BlockSpec((LANES, k), fixed)],
        out_specs=[pl.BlockSpec((tm, k), row), pl.BlockSpec((tm, n), row), pl.BlockSpec((tm, LANES), row)],
        out_shape=[jax.ShapeDtypeStruct((m, k), BF16), jax.ShapeDtypeStruct((m, n), BF16),
                   jax.ShapeDtypeStruct((m, LANES), F32)],
        scratch_shapes=[pltpu.VMEM((k, n), BF16), pltpu.VMEM((k, LANES), BF16)],
        compiler_params=_params(("arbitrary",), nbytes),
        name="norm_in_proj",
    )(x, g.reshape(1, k), w_in_t, w_dt_t)


def _mm_bias_gelu(x, w, b, tm, tn, out_dtype, name):
    m, k = x.shape
    n = w.shape[1]
    nbytes = k * tn * 6 + 2 * tm * k * x.dtype.itemsize + 2 * tm * tn * 4
    return pl.pallas_call(
        _mm_bias_gelu_kernel,
        grid=(n // tn, m // tm),
        in_specs=[pl.BlockSpec((tm, k), lambda j, i: (i, 0)), _wspec(k, tn, 0),
                  pl.BlockSpec((1, tn), lambda j, i: (0, j))],
        out_specs=pl.BlockSpec((tm, tn), lambda j, i: (i, j)),
        out_shape=jax.ShapeDtypeStruct((m, n), out_dtype),
        scratch_shapes=[pltpu.VMEM((k, tn), BF16)],
        compiler_params=_params(("arbitrary", "arbitrary"), nbytes),
        name=name,
    )(x, w, b.reshape(1, n))


def _mm_swiglu(x, w_gu, d_ff, tm, tn, name):
    m, k = x.shape
    nbytes = 2 * k * tn * 6 + 2 * tm * k * x.dtype.itemsize + 2 * tm * tn * 2 + 3 * tm * tn * 4
    return pl.pallas_call(
        _mm_swiglu_kernel,
        grid=(d_ff // tn, m // tm),
        in_specs=[pl.BlockSpec((tm, k), lambda j, i: (i, 0)), _wspec(k, tn, 0), _wspec(k, tn, d_ff // tn)],
        out_specs=pl.BlockSpec((tm, tn), lambda j, i: (i, j)),
        out_shape=jax.ShapeDtypeStruct((m, d_ff), BF16),
        scratch_shapes=[pltpu.VMEM((k, tn), BF16), pltpu.VMEM((k, tn), BF16)],
        compiler_params=_params(("arbitrary", "arbitrary"), nbytes),
        name=name,
    )(x, w_gu, w_gu)


def _mm_resnorm_kernel(x_ref, w_ref, r_ref, g_ref, h_ref, a_ref, wb_ref):
    @pl.when(pl.program_id(0) == 0)
    def _():
        wb_ref[...] = w_ref[...].astype(BF16)

    h = r_ref[...] + _bdot(x_ref[...], wb_ref[...])
    h_ref[...] = h
    a_ref[...] = _rms(h, g_ref[...]).astype(a_ref.dtype)


def _mm_resnorm(x, w, resid, g, tm, name):
    m, k = x.shape
    d = w.shape[1]
    nbytes = k * d * 6 + 2 * tm * k * 2 + 8 * tm * d * 4
    return pl.pallas_call(
        _mm_resnorm_kernel,
        grid=(m // tm,),
        in_specs=[pl.BlockSpec((tm, k), lambda i: (i, 0)),
                  pl.BlockSpec((k, d), lambda i: (0, 0), pipeline_mode=pl.Buffered(1)),
                  pl.BlockSpec((tm, d), lambda i: (i, 0)),
                  pl.BlockSpec((1, d), lambda i: (0, 0))],
        out_specs=[pl.BlockSpec((tm, d), lambda i: (i, 0)), pl.BlockSpec((tm, d), lambda i: (i, 0))],
        out_shape=[jax.ShapeDtypeStruct((m, d), F32), jax.ShapeDtypeStruct((m, d), BF16)],
        scratch_shapes=[pltpu.VMEM((k, d), BF16)],
        compiler_params=_params(("arbitrary",), nbytes),
        name=name,
    )(x, w, resid, g.reshape(1, d))


def _sgu_out_kernel(u_ref, v_ref, xt_ref, lng_ref, lnb_ref, ws_ref, bst_ref, w_ref, rp_ref, rt_ref, g_ref, w3_ref,
                    h_ref, a_ref, lg_ref, wb_ref, wsb_ref, *, prompt_tiles):
    i = pl.program_id(0)
    q = CHUNK

    @pl.when(i == 0)
    def _():
        wb_ref[...] = w_ref[...].astype(BF16)
        causal = lax.broadcasted_iota(I32, (q, q), 0) >= lax.broadcasted_iota(I32, (q, q), 1)
        for g in range(SGU_HEADS):
            wsb_ref[g] = jnp.where(causal, ws_ref[g], 0.0).astype(BF16)

    def project(x, r_ref, rows):
        h = r_ref[rows, :] + jnp.dot(x, wb_ref[...], preferred_element_type=F32)
        h_ref[rows, :] = h
        a = _rms(h, g_ref[...])
        a_hi = a.astype(BF16)
        a_lo = (a - a_hi.astype(F32)).astype(BF16)
        a_ref[rows, :] = a_hi
        lg_ref[rows, :] = (jnp.dot(a_hi, w3_ref[...], preferred_element_type=F32)
                           + jnp.dot(a_lo, w3_ref[...], preferred_element_type=F32))

    chunks = [slice(c * q, (c + 1) * q) for c in range(u_ref.shape[0] // q)]

    @pl.when(i < prompt_tiles)
    def _():
        for rows in chunks:
            heads = [slice(g * SGU_HEAD_DIM, (g + 1) * SGU_HEAD_DIM) for g in range(SGU_HEADS)]
            n_inv = 1.0 / v_ref.shape[1]
            mu = sum(jnp.sum(v_ref[rows, c].astype(F32), axis=-1, keepdims=True) for c in heads) * n_inv
            var = sum(jnp.sum(jnp.square(v_ref[rows, c].astype(F32) - mu), axis=-1, keepdims=True)
                      for c in heads) * n_inv
            rstd = lax.rsqrt(var + EPS)
            gated = []
            for g in range(SGU_HEADS):
                cols = heads[g]
                vn_g = ((v_ref[rows, cols].astype(F32) - mu) * rstd * lng_ref[:, cols] + lnb_ref[:, cols]).astype(BF16)
                s = jnp.dot(wsb_ref[g], vn_g, preferred_element_type=F32) + bst_ref[:, g:g + 1]
                gated.append((u_ref[rows, cols].astype(F32) * s).astype(BF16))
            project(jnp.concatenate(gated, axis=1), rp_ref, rows)

    @pl.when(i == prompt_tiles)
    def _():
        for rows in chunks:
            project(xt_ref[rows, :], rt_ref, rows)


def _sgu_out(uv, xt, ln_g, ln_b, w_s, bs_t, w, rp, rt, g, w3, tm):
    mp = uv.shape[0]
    k, d = w.shape
    prompt_tiles = mp // tm
    last = prompt_tiles - 1
    rows = mp + tm
    prow = lambda i: (jnp.minimum(i, last), 0)
    fixed = lambda i: (0, 0)
    nbytes = k * d * 6 + 8 * tm * k * 2 + 12 * tm * d * 4 + 8 * CHUNK * k * 4
    return pl.pallas_call(
        functools.partial(_sgu_out_kernel, prompt_tiles=prompt_tiles),
        grid=(prompt_tiles + 1,),
        in_specs=[pl.BlockSpec((tm, k), prow),
                  pl.BlockSpec((tm, k), lambda i: (jnp.minimum(i, last), 1)),
                  pl.BlockSpec((tm, k), fixed),
                  pl.BlockSpec((1, k), fixed), pl.BlockSpec((1, k), fixed),
                  pl.BlockSpec((SGU_HEADS, CHUNK, CHUNK), lambda i: (0, 0, 0)),
                  pl.BlockSpec((CHUNK, LANES), fixed),
                  pl.BlockSpec((k, d), fixed, pipeline_mode=pl.Buffered(1)),
                  pl.BlockSpec((tm, d), prow),
                  pl.BlockSpec((tm, d), fixed),
                  pl.BlockSpec((1, d), fixed),
                  pl.BlockSpec((d, LANES), fixed)],
        out_specs=[pl.BlockSpec((tm, d), lambda i: (i, 0)), pl.BlockSpec((tm, d), lambda i: (i, 0)),
                   pl.BlockSpec((tm, LANES), lambda i: (i, 0))],
        out_shape=[jax.ShapeDtypeStruct((rows, d), F32), jax.ShapeDtypeStruct((rows, d), BF16),
                   jax.ShapeDtypeStruct((rows, LANES), F32)],
        scratch_shapes=[pltpu.VMEM((k, d), BF16), pltpu.VMEM((SGU_HEADS, CHUNK, CHUNK), BF16)],
        compiler_params=_params(("arbitrary",), nbytes),
        name="sgu_out",
    )(uv, uv, xt, ln_g, ln_b, w_s, bs_t, w, rp, rt, g.reshape(1, d), w3)


def _gate_and_groupnorm(y, xs, gate, d_rep, gn_w):
    yg = (y + d_rep * xs) * gate
    gw = D_INNER // SSD_GROUPS
    outs = []
    for g in range(SSD_GROUPS):
        blk = yg[:, g * gw:(g + 1) * gw]
        outs.append(blk * lax.rsqrt(jnp.mean(blk * blk, axis=-1, keepdims=True) + EPS))
    return jnp.concatenate(outs, axis=1) * gn_w


def _softplus(x):
    return jnp.maximum(x, 0.0) + jnp.log(1.0 + jnp.exp(-jnp.abs(x)))


SSD_SEQS = 2


def _ssd_prompt_kernel(xbc_ref, z_ref, dtr_ref, cw_ref, cb_ref, dtb_ref, alog_ref, drep_ref, gnw_ref,
                       yg_ref, ssm_ref, conv_ref, ext_ref, st_ref, y_ref, xa_ref):
    for n in range(SSD_SEQS):
        _ssd_chunk(xbc_ref.at[n], z_ref.at[n], dtr_ref.at[n], cw_ref, cb_ref, dtb_ref, alog_ref, drep_ref, gnw_ref,
                   yg_ref.at[n], ssm_ref.at[n], conv_ref.at[n], ext_ref.at[n], st_ref.at[n], y_ref.at[n],
                   xa_ref.at[n])


def _ssd_chunk(xbc_ref, z_ref, dtr_ref, cw_ref, cb_ref, dtb_ref, alog_ref, drep_ref, gnw_ref,
               yg_ref, ssm_ref, conv_ref, ext_ref, st_ref, y_ref, xa_ref):
    c = pl.program_id(1)
    nc = pl.num_programs(1)
    q = CHUNK

    @pl.when(c == 0)
    def _():
        ext_ref[0:SUBLANES, :] = jnp.zeros((SUBLANES, CONV_DIM), F32)
        st_ref[...] = jnp.zeros_like(st_ref)

    ext_ref[SUBLANES:SUBLANES + q, :] = xbc_ref[...]
    acc = cb_ref[...] + cw_ref[CONV_W - 1:CONV_W, :] * ext_ref[SUBLANES:SUBLANES + q, :]
    for k in range(CONV_W - 1):
        lo = SUBLANES - (CONV_W - 1) + k
        acc = acc + cw_ref[k:k + 1, :] * ext_ref[lo:lo + q, :]
    xa_ref[...] = _silu(acc)
    tail = ext_ref[q:q + SUBLANES, :]
    ext_ref[0:SUBLANES, :] = tail

    dt = _softplus(dtr_ref[...] + dtb_ref[...])
    a_neg = -jnp.exp(alog_ref[...])
    causal = lax.broadcasted_iota(I32, (q, q), 0) >= lax.broadcasted_iota(I32, (q, q), 1)
    tril = jnp.where(causal, 1.0, 0.0)
    cs = jnp.dot(tril, dt * a_neg, precision=HIGHEST, preferred_element_type=F32) * LOG2E
    cs_t = cs.T
    dt_t = dt.T
    rowp = cs_t - jnp.log2(dt_t)
    w_t = dt_t * jnp.exp2(cs_t[:, q - 1:q] - cs_t)
    first = lax.broadcasted_iota(I32, (q, 2 * SSD_HEADDIM), 1) < SSD_HEADDIM

    def per_head(v):
        zero = jnp.zeros_like(v)
        return jnp.concatenate([jnp.where(first, v, zero), jnp.where(first, zero, v)], axis=0)

    for g in range(SSD_GROUPS):
        b_g = xa_ref[:, D_INNER + g * D_STATE:D_INNER + (g + 1) * D_STATE]
        c_g = xa_ref[:, D_INNER + (SSD_GROUPS + g) * D_STATE:D_INNER + (SSD_GROUPS + g + 1) * D_STATE]
        b_gt = b_g.T
        cb = _bdot(c_g, b_gt)
        for pr in range(4):
            hp = g * 4 + pr
            cols = slice(hp * 2 * SSD_HEADDIM, (hp + 1) * 2 * SSD_HEADDIM)
            x2 = per_head(xa_ref[:, cols].astype(BF16))
            st_p = st_ref[:, cols]
            st2 = per_head(st_p.astype(BF16))
            ms, cs_scaled, bws, cds = [], [], [], []
            for s in range(2):
                h = 2 * hp + s
                colb = jnp.broadcast_to(cs[:, h:h + 1], (q, q))
                rowb = jnp.broadcast_to(rowp[h:h + 1, :], (q, q))
                ecol = jnp.exp2(colb)
                ms.append((cb * jnp.where(causal, jnp.exp2(colb - rowb), 0.0)).astype(BF16))
                cs_scaled.append((c_g * ecol).astype(BF16))
                bws.append((b_gt * w_t[h:h + 1, :]).astype(BF16))
                cds.append(ecol[q - 1:q, :])
            y_ref[:, cols] = jnp.dot(jnp.concatenate(ms + cs_scaled, axis=1), jnp.concatenate([x2, st2], axis=0),
                                     preferred_element_type=F32)
            cd = jnp.where(first[0:1, :], cds[0], cds[1])
            st_ref[:, cols] = st_p * cd + jnp.dot(jnp.concatenate(bws, axis=1), x2, preferred_element_type=F32)

    out = _gate_and_groupnorm(y_ref[...], xa_ref[:, :D_INNER], z_ref[...].astype(F32), drep_ref[...],
                              gnw_ref[...])
    yg_ref[...] = out.astype(yg_ref.dtype)

    @pl.when(c == nc - 1)
    def _():
        conv_ref[...] = tail[SUBLANES - (CONV_W - 1):, :]
        for hp in range(SSD_HEADS // 2):
            blk = st_ref[:, hp * 2 * SSD_HEADDIM:(hp + 1) * 2 * SSD_HEADDIM].T
            ssm_ref[2 * hp:2 * hp + 2] = blk.reshape(2, SSD_HEADDIM, D_STATE)


def _ssd_prompt(xbc, z, dtr, conv_w, conv_b, dtb, alog, d_rep, gn_w):
    batch, seq, _ = xbc.shape
    nb = SSD_SEQS
    blk = lambda b, c: (b, c, 0)
    fixed = lambda b, c: (0, 0)
    nbytes = nb * (2 * CHUNK * (CONV_DIM * 4 + D_INNER * 4 + LANES * 4) + (CHUNK + SUBLANES) * CONV_DIM * 4
                   + 4 * D_STATE * D_INNER * 4 + 12 * CHUNK * CONV_DIM * 4)
    return pl.pallas_call(
        _ssd_prompt_kernel,
        grid=(batch // nb, seq // CHUNK),
        in_specs=[pl.BlockSpec((nb, CHUNK, CONV_DIM), blk), pl.BlockSpec((nb, CHUNK, D_INNER), blk),
                  pl.BlockSpec((nb, CHUNK, LANES), blk),
                  pl.BlockSpec((CONV_W, CONV_DIM), fixed), pl.BlockSpec((1, CONV_DIM), fixed),
                  pl.BlockSpec((1, LANES), fixed), pl.BlockSpec((1, LANES), fixed),
                  pl.BlockSpec((1, D_INNER), fixed), pl.BlockSpec((1, D_INNER), fixed)],
        out_specs=[pl.BlockSpec((nb, CHUNK, D_INNER), blk),
                   pl.BlockSpec((nb, SSD_HEADS, SSD_HEADDIM, D_STATE), lambda b, c: (b, 0, 0, 0)),
                   pl.BlockSpec((nb, CONV_W - 1, CONV_DIM), lambda b, c: (b, 0, 0))],
        out_shape=[jax.ShapeDtypeStruct((batch, seq, D_INNER), BF16),
                   jax.ShapeDtypeStruct((batch, SSD_HEADS, SSD_HEADDIM, D_STATE), F32),
                   jax.ShapeDtypeStruct((batch, CONV_W - 1, CONV_DIM), F32)],
        scratch_shapes=[pltpu.VMEM((nb, CHUNK + SUBLANES, CONV_DIM), F32), pltpu.VMEM((nb, D_STATE, D_INNER), F32),
                        pltpu.VMEM((nb, CHUNK, D_INNER), F32), pltpu.VMEM((nb, CHUNK, CONV_DIM), F32)],
        compiler_params=_params(("arbitrary", "arbitrary"), nbytes),
        name="ssd_prompt",
    )(xbc, z, dtr, conv_w, conv_b, dtb, alog, d_rep, gn_w)


def _ssd_out_kernel(y_ref, xs_ref, z_ref, drep_ref, gnw_ref, w_ref, r_ref, g_ref, h_ref, a_ref, wb_ref):
    @pl.when(pl.program_id(0) == 0)
    def _():
        wb_ref[...] = w_ref[...].astype(BF16)

    sub = min(ROW_SUB // 2, y_ref.shape[0])
    for s in range(y_ref.shape[0] // sub):
        rows = slice(s * sub, (s + 1) * sub)
        yg = _gate_and_groupnorm(y_ref[rows, :], xs_ref[rows, :], z_ref[rows, :].astype(F32), drep_ref[...],
                                 gnw_ref[...])
        h = r_ref[rows, :] + _bdot(yg, wb_ref[...])
        h_ref[rows, :] = h
        a_ref[rows, :] = _rms(h, g_ref[...]).astype(a_ref.dtype)


def _ssd_out(y, xa, z, d_rep, gn_w, w, resid, g, tm):
    m, k = y.shape
    d = w.shape[1]
    row = lambda i: (i, 0)
    fixed = lambda i: (0, 0)
    nbytes = k * d * 6 + 2 * tm * k * 10 + 8 * tm * d * 4 + 6 * tm * k * 4
    return pl.pallas_call(
        _ssd_out_kernel,
        grid=(m // tm,),
        in_specs=[pl.BlockSpec((tm, k), row), pl.BlockSpec((tm, k), row), pl.BlockSpec((tm, k), row),
                  pl.BlockSpec((1, k), fixed), pl.BlockSpec((1, k), fixed),
                  pl.BlockSpec((k, d), fixed, pipeline_mode=pl.Buffered(1)),
                  pl.BlockSpec((tm, d), row), pl.BlockSpec((1, d), fixed)],
        out_specs=[pl.BlockSpec((tm, d), row), pl.BlockSpec((tm, d), row)],
        out_shape=[jax.ShapeDtypeStruct((m, d), F32), jax.ShapeDtypeStruct((m, d), BF16)],
        scratch_shapes=[pltpu.VMEM((k, d), BF16)],
        compiler_params=_params(("arbitrary",), nbytes),
        name="ssd_out",
    )(y, xa, z, d_rep, gn_w, w, resid, g.reshape(1, d))


def _ssd_sample_pre_kernel(xbc_ref, conv_ref, dtr_ref, cw_ref, cb_ref, dtb_ref, alog_ref,
                           nconv_ref, xa_ref, xdt_ref, da_ref, bst_ref, cst_ref):
    xn = xbc_ref[...]
    acc = cb_ref[...] + cw_ref[CONV_W - 1:CONV_W, :] * xn
    for k in range(CONV_W - 1):
        acc = acc + cw_ref[k:k + 1, :] * conv_ref[:, k * CONV_DIM:(k + 1) * CONV_DIM]
    xa = _silu(acc)
    xa_ref[...] = xa
    nconv_ref[:, 0:CONV_DIM] = conv_ref[:, CONV_DIM:2 * CONV_DIM]
    nconv_ref[:, CONV_DIM:2 * CONV_DIM] = conv_ref[:, 2 * CONV_DIM:3 * CONV_DIM]
    nconv_ref[:, 2 * CONV_DIM:3 * CONV_DIM] = xn

    dt = _softplus(dtr_ref[...] + dtb_ref[...])
    a_neg = -jnp.exp(alog_ref[...])
    hh = lax.broadcasted_iota(I32, (LANES, D_INNER), 0)
    cc = lax.broadcasted_iota(I32, (LANES, D_INNER), 1)
    expand = jnp.where(lax.shift_right_logical(cc, 6) == hh, 1.0, 0.0)
    dt_rep = jnp.dot(dt, expand, precision=HIGHEST, preferred_element_type=F32)
    xdt_ref[...] = xa[:, :D_INNER] * dt_rep
    da_ref[...] = jnp.exp(dt * a_neg)

    n = xn.shape[0]
    gs = SSD_GROUPS * D_STATE
    b1, b2, b3 = _split3(xa[:, D_INNER:D_INNER + gs])
    cm = xa[:, D_INNER + gs:]
    c1 = cm.astype(BF16)
    c2 = (cm - c1.astype(F32)).astype(BF16)
    zero = jnp.zeros((n, D_STATE), BF16)
    b_rows, c_rows = [], []
    for g in range(SSD_GROUPS):
        sl = slice(g * D_STATE, (g + 1) * D_STATE)
        b_rows += [b1[:, sl], b2[:, sl], b1[:, sl], b3[:, sl], b2[:, sl], b1[:, sl]] + [zero] * (TERM_ROWS - 6)
        c_rows += [c1[:, sl], c2[:, sl]] + [zero] * (TERM_ROWS - 2)
    bst_ref[...] = jnp.concatenate(b_rows, axis=1)
    cst_ref[...] = jnp.concatenate(c_rows, axis=1)


TERM_ROWS = 16


def _split3(v):
    v1 = v.astype(BF16)
    r1 = v - v1.astype(F32)
    v2 = r1.astype(BF16)
    v3 = (r1 - v2.astype(F32)).astype(BF16)
    return v1, v2, v3


def _ssd_sample_pre(xbc, conv_flat, dtr, conv_w, conv_b, dtb, alog):
    n = xbc.shape[0]
    full = lambda shape: pl.BlockSpec(shape, lambda i: (0,) * len(shape))
    stack = SSD_GROUPS * TERM_ROWS * D_STATE
    return pl.pallas_call(
        _ssd_sample_pre_kernel,
        grid=(1,),
        in_specs=[full((n, CONV_DIM)), full((n, 3 * CONV_DIM)), full((n, LANES)), full((CONV_W, CONV_DIM)),
                  full((1, CONV_DIM)), full((1, LANES)), full((1, LANES))],
        out_specs=[full((n, 3 * CONV_DIM)), full((n, CONV_DIM)), full((n, D_INNER)), full((n, LANES)),
                   full((n, stack)), full((n, stack))],
        out_shape=[jax.ShapeDtypeStruct((n, 3 * CONV_DIM), F32), jax.ShapeDtypeStruct((n, CONV_DIM), F32),
                   jax.ShapeDtypeStruct((n, D_INNER), F32), jax.ShapeDtypeStruct((n, LANES), F32),
                   jax.ShapeDtypeStruct((n, stack), BF16), jax.ShapeDtypeStruct((n, stack), BF16)],
        compiler_params=_params(("arbitrary",), 32 * n * CONV_DIM * 4),
        name="ssd_sample_pre",
    )(xbc, conv_flat, dtr, conv_w, conv_b, dtb, alog)


STATE_SAMPLES = 8


def _ssd_sample_state_kernel(da_ref, st_ref, xdt_ref, bst_ref, cst_ref, nst_ref, y_ref):
    first = pl.program_id(0) * STATE_SAMPLES
    rows_g = D_INNER // SSD_GROUPS
    sub = lax.broadcasted_iota(I32, (TERM_ROWS, D_INNER), 0)
    for s in range(STATE_SAMPLES):
        x1, x2, x3 = [t.astype(F32) for t in _split3(xdt_ref[s])]
        terms = jnp.where(sub == 0, x1, jnp.where(sub == 1, x1, jnp.where(sub == 2, x2, jnp.where(
            sub == 3, x1, jnp.where(sub == 4, x2, jnp.where(sub == 5, x3, 0.0)))))).astype(BF16)
        ys = []
        for g in range(SSD_GROUPS):
            tile = slice(g * TERM_ROWS, (g + 1) * TERM_ROWS)
            upd = lax.dot_general(terms[:, g * rows_g:(g + 1) * rows_g], bst_ref[s, tile, :],
                                  (((0,), (0,)), ((), ())), preferred_element_type=F32)
            halves = []
            for hh in range(rows_g // SSD_HEADDIM):
                head = g * (rows_g // SSD_HEADDIM) + hh
                rows = slice(head * SSD_HEADDIM, (head + 1) * SSD_HEADDIM)
                h_new = (da_ref[first + s, head] * st_ref[s, rows, :]
                         + upd[hh * SSD_HEADDIM:(hh + 1) * SSD_HEADDIM, :])
                nst_ref[s, rows, :] = h_new
                halves.append(h_new)
            h_g = jnp.concatenate(halves, axis=0)
            h_hi = h_g.astype(BF16)
            h_lo = (h_g - h_hi.astype(F32)).astype(BF16)
            nt = (((1,), (1,)), ((), ()))
            o_hi = lax.dot_general(cst_ref[s, tile, :], h_hi, nt, preferred_element_type=F32)
            o_lo = lax.dot_general(cst_ref[s, tile, :], h_lo, nt, preferred_element_type=F32)
            ys.append(o_hi[0:1, :] + o_hi[1:2, :] + o_lo[0:1, :])
        y_ref[s] = jnp.concatenate(ys, axis=1)


def _ssd_sample_state(state, xdt, da, bst, cst):
    n = state.shape[0]
    ns = STATE_SAMPLES
    stack_rows = SSD_GROUPS * TERM_ROWS
    return pl.pallas_call(
        _ssd_sample_state_kernel,
        grid=(n // ns,),
        in_specs=[pl.BlockSpec(memory_space=pltpu.SMEM),
                  pl.BlockSpec((ns, D_INNER, D_STATE), lambda i: (i, 0, 0)),
                  pl.BlockSpec((ns, 1, D_INNER), lambda i: (i, 0, 0)),
                  pl.BlockSpec((ns, stack_rows, D_STATE), lambda i: (i, 0, 0)),
                  pl.BlockSpec((ns, stack_rows, D_STATE), lambda i: (i, 0, 0))],
        out_specs=[pl.BlockSpec((ns, D_INNER, D_STATE), lambda i: (i, 0, 0)),
                   pl.BlockSpec((ns, 1, D_INNER), lambda i: (i, 0, 0))],
        out_shape=[jax.ShapeDtypeStruct((n, D_INNER, D_STATE), F32),
                   jax.ShapeDtypeStruct((n, 1, D_INNER), F32)],
        compiler_params=_params(("arbitrary",), 6 * ns * D_INNER * D_STATE * 4),
        name="ssd_sample_state",
    )(da, state, xdt, bst, cst)


def _layernorm(v, g, b):
    mu = jnp.mean(v, axis=-1, keepdims=True)
    d = v - mu
    var = jnp.mean(d * d, axis=-1, keepdims=True)
    return d * lax.rsqrt(var + EPS) * g + b


def _sgu_sample_kernel(u_ref, v_ref, lng_ref, lnb_ref, w0_ref, b0_ref, o_ref, vn_ref):
    vn = _layernorm(v_ref[...], lng_ref[...], lnb_ref[...])
    vn_ref[...] = vn
    o_ref[...] = (u_ref[...] * (w0_ref[...] * vn + b0_ref[...])).astype(o_ref.dtype)


def _sgu_sample(uv, ln_g, ln_b, w0_rep, b0_rep):
    n = uv.shape[0]
    fixed2 = lambda i: (0, 0)
    return pl.pallas_call(
        _sgu_sample_kernel,
        grid=(1,),
        in_specs=[pl.BlockSpec((n, D_SGU), lambda i: (0, 0)), pl.BlockSpec((n, D_SGU), lambda i: (0, 1)),
                  pl.BlockSpec((1, D_SGU), fixed2), pl.BlockSpec((1, D_SGU), fixed2),
                  pl.BlockSpec((1, D_SGU), fixed2), pl.BlockSpec((1, D_SGU), fixed2)],
        out_specs=[pl.BlockSpec((n, D_SGU), fixed2), pl.BlockSpec((n, D_SGU), fixed2)],
        out_shape=[jax.ShapeDtypeStruct((n, D_SGU), BF16), jax.ShapeDtypeStruct((n, D_SGU), F32)],
        compiler_params=_params(("arbitrary",), 12 * n * D_SGU * 4),
        name="sgu_sample",
    )(uv, uv, ln_g, ln_b, w0_rep, b0_rep)


def _experts_to_lanes(col, sub, lane):
    return jnp.sum(jnp.where(sub == lane, col, 0.0), axis=0, keepdims=True)


def _route_kernel(lg_ref, loc_ref, gate_ref, bmeta_ref, tmeta_ref, *, tile, n_valid):
    nb, _, tb = lg_ref.shape
    sub = lax.broadcasted_iota(I32, (N_EXPERTS, LANES), 0)
    lane = lax.broadcasted_iota(I32, (N_EXPERTS, LANES), 1)
    subf = lax.broadcasted_iota(I32, (N_EXPERTS, tb), 0).astype(F32)
    tok = lax.broadcasted_iota(I32, (N_EXPERTS, tb), 1)
    incl = jnp.where(lax.broadcasted_iota(I32, (tb, tb), 0) <= lax.broadcasted_iota(I32, (tb, tb), 1), 1.0, 0.0)
    neg = jnp.float32(-jnp.inf)
    none = jnp.float32(N_EXPERTS)

    def select(k):
        blk = lg_ref[k]
        l = blk[0:N_EXPERTS] + blk[N_EXPERTS:2 * N_EXPERTS]
        m1 = jnp.max(l, axis=0, keepdims=True)
        i1 = jnp.min(jnp.where(l == m1, subf, none), axis=0, keepdims=True)
        l2 = jnp.where(subf == i1, neg, l)
        m2 = jnp.max(l2, axis=0, keepdims=True)
        i2 = jnp.min(jnp.where(l2 == m2, subf, none), axis=0, keepdims=True)
        valid = (k * tb + tok) < n_valid
        sel = jnp.where(valid, jnp.where(subf == i1, 1.0, jnp.where(subf == i2, 1.0, 0.0)), 0.0)
        return m1, i1, m2, i2, valid, sel

    def up8(rows):
        return jnp.ceil(rows / SUBLANES) * SUBLANES

    def count_body(k, carry):
        return carry + jnp.sum(select(k)[5], axis=1, keepdims=True)

    counts = lax.fori_loop(0, nb, count_body, jnp.zeros((N_EXPERTS, 1), F32))
    tiles = jnp.ceil(counts / tile)
    cum_incl = jnp.sum(jnp.where(lane <= sub, _experts_to_lanes(tiles, sub, lane), 0.0), axis=1, keepdims=True)
    offset = (cum_incl - tiles) * tile
    tile_expert = jnp.sum(jnp.where(lane.astype(F32) >= cum_incl, 1.0, 0.0), axis=0, keepdims=True)
    tile_expert = jnp.minimum(tile_expert, N_EXPERTS - 1.0)
    n_used = jnp.max(cum_incl, axis=0, keepdims=True)
    region_end = _experts_to_lanes(offset + up8(counts), sub, lane)
    region_pad = _experts_to_lanes(tiles * tile - up8(counts), sub, lane)
    tmeta_ref[...] = jnp.where(sub == 0, tile_expert, jnp.where(sub == 1, n_used, jnp.where(
        sub == 2, region_end, jnp.where(sub == 3, region_pad, 0.0)))).astype(I32)

    def place_body(k, before):
        m1, i1, m2, i2, valid, sel = select(k)
        run = jnp.dot(sel, incl, preferred_element_type=F32)
        cnt = run[:, tb - 1:tb]
        lead = before - jnp.floor(before / SUBLANES) * SUBLANES
        span = up8(lead + cnt)
        lstart = jnp.sum(jnp.where(lane < sub, _experts_to_lanes(span, sub, lane), 0.0), axis=1, keepdims=True)
        local = lstart + lead + run - sel
        loc1 = jnp.sum(jnp.where(subf == i1, local, 0.0), axis=0, keepdims=True)
        loc2 = jnp.sum(jnp.where(subf == i2, local, 0.0), axis=0, keepdims=True)
        live = (k * tb + tok[0:1, :]) < n_valid
        loc_ref[k] = jnp.where(subf == 0.0, jnp.where(live, loc1, -1.0),
                               jnp.where(subf == 1.0, jnp.where(live, loc2, -1.0), 0.0))
        e = jnp.exp(m2 - m1)
        gate_ref[k] = jnp.where(subf == 0.0, 1.0 / (1.0 + e), jnp.where(subf == 1.0, e / (1.0 + e), 0.0))
        moved = jnp.where(cnt > 0.0, span, 0.0)
        bmeta_ref[k] = jnp.where(lane == 0, offset + before - lead, jnp.where(lane == 1, moved, jnp.where(
            lane == 2, lstart, jnp.where(lane == 3, lead, jnp.where(lane == 4, lead + cnt, 0.0))))).astype(I32)
        return before + cnt

    lax.fori_loop(0, nb, place_body, jnp.zeros((N_EXPERTS, 1), F32))


def _route(logits3, tile, n_valid):
    nb, _, tb = logits3.shape
    tok_blk = pl.BlockSpec((nb, N_EXPERTS, tb), lambda i: (0, 0, 0))
    return pl.pallas_call(
        functools.partial(_route_kernel, tile=tile, n_valid=n_valid),
        grid=(1,),
        in_specs=[pl.BlockSpec((nb, 2 * N_EXPERTS, tb), lambda i: (0, 0, 0))],
        out_specs=[tok_blk, tok_blk, pl.BlockSpec((nb, N_EXPERTS, LANES), lambda i: (0, 0, 0)),
                   pl.BlockSpec((N_EXPERTS, LANES), lambda i: (0, 0))],
        out_shape=[jax.ShapeDtypeStruct((nb, N_EXPERTS, tb), F32),
                   jax.ShapeDtypeStruct((nb, N_EXPERTS, tb), F32),
                   jax.ShapeDtypeStruct((nb, N_EXPERTS, LANES), I32),
                   jax.ShapeDtypeStruct((N_EXPERTS, LANES), I32)],
        compiler_params=_params(("arbitrary",), 16 * nb * N_EXPERTS * tb * 4),
        name="moe_route",
    )(logits3)


RUN_FIELDS = 5


def _run_copies(bm_ref, blk, vmem_ref, hbm_ref, sem, *, tb, to_hbm, wait):
    for e in range(N_EXPERTS):
        base = (blk * N_EXPERTS + e) * RUN_FIELDS
        start, cnt, lstart = bm_ref[base], bm_ref[base + 1], bm_ref[base + 2]
        off = 0
        size = tb
        while size >= SUBLANES:
            @pl.when((cnt & size) != 0)
            def _(size=size, off=off):
                v = vmem_ref.at[pl.ds(pl.multiple_of(lstart + off, SUBLANES), size)]
                h = hbm_ref.at[pl.ds(pl.multiple_of(start + off, SUBLANES), size)]
                cp = pltpu.make_async_copy(v, h, sem) if to_hbm else pltpu.make_async_copy(h, v, sem)
                if wait:
                    cp.wait()
                else:
                    cp.start()

            off = off + (cnt & size)
            size //= 2


def _compact_rows(tb):
    return 2 * tb + N_EXPERTS * 2 * SUBLANES + 2 * SUBLANES


def _selection(loc_ref, rows, tb):
    r = lax.broadcasted_iota(I32, (rows, tb), 0).astype(F32)
    return r == loc_ref[0, 0:1, :], r == loc_ref[0, 1:2, :]


def _zero_fill(zm_ref, zbuf_ref, xs_ref, zsem, *, tm, n_tiles, wait):
    def fill(row0, size):
        cp = pltpu.make_async_copy(zbuf_ref.at[pl.ds(0, size)],
                                   xs_ref.at[pl.ds(pl.multiple_of(row0, SUBLANES), size)], zsem)
        if wait:
            cp.wait()
        else:
            cp.start()

    for e in range(N_EXPERTS):
        end, pad = zm_ref[1 + 2 * e], zm_ref[2 + 2 * e]
        off = 0
        size = tm // 2
        while size >= SUBLANES:
            @pl.when((pad & size) != 0)
            def _(size=size, off=off):
                fill(end + off, size)

            off = off + (pad & size)
            size //= 2
    for t in range(n_tiles):
        @pl.when(t >= zm_ref[0])
        def _(t=t):
            fill(t * tm, tm)


def _moe_scatter_kernel(bm_ref, zm_ref, a_ref, loc_ref, gate_ref, xs_ref,
                        buf_ref, zbuf_ref, carry_ref, sem, zsem, *, tb, tm, n_tiles):
    blk = pl.program_id(0)
    last = pl.num_programs(0) - 1
    slot = blk % 2
    d = a_ref.shape[1]

    @pl.when(blk == 0)
    def _():
        zbuf_ref[...] = jnp.zeros_like(zbuf_ref)
        carry_ref[...] = jnp.zeros_like(carry_ref)
        _zero_fill(zm_ref, zbuf_ref, xs_ref, zsem, tm=tm, n_tiles=n_tiles, wait=False)

    def copies(block, wait):
        s = block % 2
        _run_copies(bm_ref, block, buf_ref.at[s], xs_ref, sem.at[s], tb=tb, to_hbm=True, wait=wait)

    first, second = _selection(loc_ref, _compact_rows(tb), tb)
    pick = jnp.where(first, 1.0, jnp.where(second, 1.0, 0.0)).astype(BF16)
    buf_ref[slot, :, 0:d] = jnp.dot(pick, a_ref[...], preferred_element_type=F32)
    gsel = jnp.where(first, gate_ref[0, 0:1, :], jnp.where(second, gate_ref[0, 1:2, :], 0.0))
    buf_ref[slot, :, d:d + LANES] = jnp.broadcast_to(jnp.sum(gsel, axis=1, keepdims=True),
                                                    (buf_ref.shape[1], LANES))

    row = lax.broadcasted_iota(I32, (SUBLANES, buf_ref.shape[2]), 0)
    for e in range(N_EXPERTS):
        base = (blk * N_EXPERTS + e) * RUN_FIELDS
        lstart, total = bm_ref[base + 2], bm_ref[base + 4]
        head = pl.ds(pl.multiple_of(lstart, SUBLANES), SUBLANES)
        buf_ref[slot, head, :] = buf_ref[slot, head, :] + carry_ref[e]
        full = lax.shift_left(lax.shift_right_logical(total, 3), 3)
        tail = pl.ds(pl.multiple_of(lstart + full, SUBLANES), SUBLANES)
        carry_ref[e] = jnp.where(row < total - full, buf_ref[slot, tail, :], 0.0)

    @pl.when(blk >= 1)
    def _():
        copies(blk - 1, True)

    copies(blk, False)

    @pl.when(blk == last)
    def _():
        copies(blk, True)
        _zero_fill(zm_ref, zbuf_ref, xs_ref, zsem, tm=tm, n_tiles=n_tiles, wait=True)


def _moe_scatter(a, loc, gate, bmeta, zmeta, n_slots, tb, tm):
    m, d = a.shape
    width = d + LANES
    tok_spec = pl.BlockSpec((1, N_EXPERTS, tb), lambda i, bm, zm: (i, 0, 0))
    grid_spec = pltpu.PrefetchScalarGridSpec(
        num_scalar_prefetch=2,
        grid=(m // tb,),
        in_specs=[pl.BlockSpec((tb, d), lambda i, bm, zm: (i, 0)), tok_spec, tok_spec],
        out_specs=pl.BlockSpec(memory_space=pl.ANY),
        scratch_shapes=[pltpu.VMEM((2, _compact_rows(tb), width), F32), pltpu.VMEM((tm, width), F32),
                        pltpu.VMEM((N_EXPERTS, SUBLANES, width), F32),
                        pltpu.SemaphoreType.DMA((2,)), pltpu.SemaphoreType.DMA(())],
    )
    return pl.pallas_call(
        functools.partial(_moe_scatter_kernel, tb=tb, tm=tm, n_tiles=n_slots // tm),
        grid_spec=grid_spec,
        out_shape=jax.ShapeDtypeStruct((n_slots, width), F32),
        compiler_params=_params(("arbitrary",), 16 * tb * d * 4),
        name="moe_scatter",
    )(bmeta, zmeta, a, loc, gate)


def _new_weights(te_ref):
    i = pl.program_id(1)
    return (i == 0) | (te_ref[i] != te_ref[jnp.maximum(i - 1, 0)])


def _moe_gu_kernel(te_ref, nu_ref, x_ref, wg_ref, wu_ref, o_ref, wgb_ref, wub_ref):
    i = pl.program_id(1)

    @pl.when(_new_weights(te_ref))
    def _():
        wgb_ref[...] = wg_ref[0].astype(BF16)
        wub_ref[...] = wu_ref[0].astype(BF16)

    @pl.when(i < nu_ref[0])
    def _():
        x = x_ref[...].astype(BF16)
        for c in range(o_ref.shape[1] // MXU_COLS):
            cols = slice(c * MXU_COLS, (c + 1) * MXU_COLS)
            g = jnp.dot(x, wgb_ref[:, cols], preferred_element_type=F32)
            u = jnp.dot(x, wub_ref[:, cols], preferred_element_type=F32)
            o_ref[:, cols] = (_silu(g) * u).astype(o_ref.dtype)

    @pl.when(i >= nu_ref[0])
    def _():
        o_ref[...] = jnp.zeros_like(o_ref)


def _used_tile(i, nu):
    return jnp.minimum(i, nu[0] - 1)


def _moe_gu(xs, w_gu, te, nu, tm, tn):
    s = xs.shape[0]
    k = w_gu.shape[1]
    f = D_FF_EXPERT
    nbytes = 2 * (2 * k * tn * 4 + k * tn * 2) + 2 * tm * k * 4 + 2 * tm * tn * 2 + 4 * tm * MXU_COLS * 4
    grid_spec = pltpu.PrefetchScalarGridSpec(
        num_scalar_prefetch=2,
        grid=(f // tn, s // tm),
        in_specs=[pl.BlockSpec((tm, k), lambda j, i, te, nu: (_used_tile(i, nu), 0)),
                  pl.BlockSpec((1, k, tn), lambda j, i, te, nu: (te[i], 0, j)),
                  pl.BlockSpec((1, k, tn), lambda j, i, te, nu: (te[i], 0, j + f // tn))],
        out_specs=pl.BlockSpec((tm, tn), lambda j, i, te, nu: (i, j)),
        scratch_shapes=[pltpu.VMEM((k, tn), BF16), pltpu.VMEM((k, tn), BF16)],
    )
    return pl.pallas_call(
        _moe_gu_kernel,
        grid_spec=grid_spec,
        out_shape=jax.ShapeDtypeStruct((s, f), BF16),
        compiler_params=_params(("arbitrary", "arbitrary"), nbytes),
        name="moe_gate_up",
    )(te, nu, xs, w_gu, w_gu)


def _moe_down_kernel(te_ref, nu_ref, x_ref, w_ref, gs_ref, o_ref, wb_ref):
    i = pl.program_id(1)

    @pl.when(_new_weights(te_ref))
    def _():
        wb_ref[...] = w_ref[0].astype(BF16)

    @pl.when(i < nu_ref[0])
    def _():
        o_ref[...] = gs_ref[:, 0:1] * jnp.dot(x_ref[...], wb_ref[...], preferred_element_type=F32)

    @pl.when(i >= nu_ref[0])
    def _():
        o_ref[...] = jnp.zeros_like(o_ref)


def _moe_down(act, w_down, xs, te, nu, tm, tn):
    s, k = act.shape
    d = w_down.shape[2]
    gate_block = d // LANES
    nbytes = 2 * k * tn * 4 + k * tn * 2 + 2 * tm * k * 2 + 3 * tm * tn * 4
    grid_spec = pltpu.PrefetchScalarGridSpec(
        num_scalar_prefetch=2,
        grid=(d // tn, s // tm),
        in_specs=[pl.BlockSpec((tm, k), lambda j, i, te, nu: (_used_tile(i, nu), 0)),
                  pl.BlockSpec((1, k, tn), lambda j, i, te, nu: (te[i], 0, j)),
                  pl.BlockSpec((tm, LANES), lambda j, i, te, nu: (_used_tile(i, nu), gate_block))],
        out_specs=pl.BlockSpec((tm, tn), lambda j, i, te, nu: (i, j)),
        scratch_shapes=[pltpu.VMEM((k, tn), BF16)],
    )
    return pl.pallas_call(
        _moe_down_kernel,
        grid_spec=grid_spec,
        out_shape=jax.ShapeDtypeStruct((s, d), F32),
        compiler_params=_params(("arbitrary", "arbitrary"), nbytes),
        name="moe_down",
    )(te, nu, act, w_down, xs)


def _moe_combine_kernel(bm_ref, h_ref, locc_ref, g_ref, ys_ref, op_ref, os_ref, ybuf_ref, sem, *, tb, prompt_blocks):
    blk = pl.program_id(0)
    slot = blk % 2

    def copies(block, wait):
        s = block % 2
        _run_copies(bm_ref, block, ybuf_ref.at[s], ys_ref, sem.at[s], tb=tb, to_hbm=False, wait=wait)

    @pl.when(blk == 0)
    def _():
        ybuf_ref[...] = jnp.zeros_like(ybuf_ref)
        copies(blk, False)

    @pl.when(blk + 1 < pl.num_programs(0))
    def _():
        copies(blk + 1, False)

    r = lax.broadcasted_iota(I32, (tb, _compact_rows(tb)), 1).astype(F32)
    pick = jnp.where(r == locc_ref[:, 0:1], 1.0, jnp.where(r == locc_ref[:, 1:2], 1.0, 0.0)).astype(BF16)
    copies(blk, True)
    moe = jnp.dot(pick, ybuf_ref[slot].astype(BF16), preferred_element_type=F32)
    out = _rms(h_ref[...] + moe, g_ref[...])

    @pl.when(blk < prompt_blocks)
    def _():
        op_ref[...] = out

    @pl.when(blk == prompt_blocks)
    def _():
        os_ref[...] = out


def _moe_combine(h, locc, bmeta, ys, g, tb, n_prompt):
    m, d = h.shape
    prompt_blocks = n_prompt // tb
    any_spec = pl.BlockSpec(memory_space=pl.ANY)
    grid_spec = pltpu.PrefetchScalarGridSpec(
        num_scalar_prefetch=1,
        grid=(m // tb,),
        in_specs=[pl.BlockSpec((tb, d), lambda i, bm: (i, 0)),
                  pl.BlockSpec((tb, N_EXPERTS), lambda i, bm: (i, 0)),
                  pl.BlockSpec((1, d), lambda i, bm: (0, 0)),
                  any_spec],
        out_specs=[pl.BlockSpec((tb, d), lambda i, bm: (jnp.minimum(i, prompt_blocks - 1), 0)),
                   pl.BlockSpec((tb, d), lambda i, bm: (0, 0))],
        scratch_shapes=[pltpu.VMEM((2, _compact_rows(tb), d), F32), pltpu.SemaphoreType.DMA((2,))],
    )
    return pl.pallas_call(
        functools.partial(_moe_combine_kernel, tb=tb, prompt_blocks=prompt_blocks),
        grid_spec=grid_spec,
        out_shape=[jax.ShapeDtypeStruct((n_prompt, d), F32), jax.ShapeDtypeStruct((tb, d), F32)],
        compiler_params=_params(("arbitrary",), 16 * tb * d * 4),
        name="moe_combine",
    )(bmeta, h, locc, g.reshape(1, d), ys)


MOE_TOKEN_BLOCK = 256
MOE_ROW_TILE = 512


def _moe(h, a, lg_parts, n_valid, n_prompt, w_gu, w_down, g_final):
    m, d = h.shape
    tb, tm = MOE_TOKEN_BLOCK, MOE_ROW_TILE
    nb = m // tb
    lg3 = lg_parts[:, :2 * N_EXPERTS].reshape(nb, tb, 2 * N_EXPERTS).transpose(0, 2, 1)
    n_tiles = pl.cdiv(2 * n_valid, tm) + N_EXPERTS
    loc, gate, bmeta, tmeta = _route(lg3, tm, n_valid)
    te = tmeta[0, :n_tiles]
    nu = tmeta[1, 0:1]
    bm = bmeta[:, :, :RUN_FIELDS].reshape(-1)
    zm = jnp.concatenate([nu, tmeta[2:4, :N_EXPERTS].T.reshape(-1)])
    locc = loc.transpose(0, 2, 1).reshape(m, N_EXPERTS)
    xs = _moe_scatter(a, loc, gate, bm, zm, n_tiles * tm, tb, tm)
    act = _moe_gu(xs, w_gu, te, nu, tm, D_FF_EXPERT // 2)
    ys = _moe_down(act, w_down, xs, te, nu, tm, d)
    return _moe_combine(h, locc, bm, ys, g_final, tb, n_prompt)


def _pad_lanes(v):
    return jnp.pad(v.reshape(1, -1), ((0, 0), (0, LANES - v.shape[-1])))


def kernel(x_prompt, x_sample, state_ssm, state_conv, norm_mix, norm_ffn, norm_final, ssd_w_in, ssd_conv_w,
           ssd_conv_b, ssd_dt_bias, ssd_a_log, ssd_d, ssd_gnorm, ssd_w_out, sgu_w_in, sgu_b_in, sgu_ln_g,
           sgu_ln_b, sgu_w_s, sgu_b_s, sgu_w_out, ffn_w_gu, ffn_w_down, moe_w_router, moe_w_gu, moe_w_down):
    batch, seq, d = x_prompt.shape
    n_s = x_sample.shape[0]
    mp = batch * seq
    xp = x_prompt.reshape(mp, d)
    xs = x_sample.reshape(n_s, d)

    w_in = ssd_w_in[0].T
    w_dt = jnp.pad(w_in[D_INNER + CONV_DIM:], ((0, LANES - SSD_HEADS), (0, 0)))
    dtb = _pad_lanes(ssd_dt_bias[0])
    alog = _pad_lanes(ssd_a_log[0])
    d_rep = jnp.repeat(ssd_d[0], SSD_HEADDIM).reshape(1, D_INNER)
    gn_w = ssd_gnorm[0].reshape(1, D_INNER)
    conv_w = ssd_conv_w[0]
    conv_b = ssd_conv_b[0].reshape(1, CONV_DIM)
    ln_g = sgu_ln_g[0].reshape(1, D_SGU)
    ln_b = sgu_ln_b[0].reshape(1, D_SGU)
    bs_t = jnp.pad(sgu_b_s[0].T, ((0, 0), (0, LANES - SGU_HEADS)))
    w0_rep = jnp.repeat(sgu_w_s[0][:, 0, 0], SGU_HEAD_DIM).reshape(1, D_SGU)
    b0_rep = jnp.repeat(sgu_b_s[0][:, 0], SGU_HEAD_DIM).reshape(1, D_SGU)
    wr_hi = moe_w_router[0].astype(BF16)
    wr_lo = (moe_w_router[0] - wr_hi.astype(F32)).astype(BF16)
    w3 = jnp.concatenate([wr_hi, wr_lo, jnp.zeros((d, LANES - 2 * N_EXPERTS), BF16)], axis=1)

    def dense_ffn(h1, a1, tm):
        act = _mm_swiglu(a1, ffn_w_gu[0], D_FF_DENSE, min(2 * tm, a1.shape[0]), 1408, "ffn_gate_up")
        return _mm_resnorm(act, ffn_w_down[0], h1, norm_mix[1], tm, "ffn_down")

    a0p, zp, dtrp = _norm_in_proj(xp, norm_mix[0], w_in, w_dt, 1024)
    xbcp = _mm_plain(a0p, w_in, D_INNER, CONV_DIM, 1024, 1024, F32, "ssd_in_xbc")
    per_seq = lambda t: t.reshape(batch, seq, t.shape[-1])
    ygp, ssm_p, conv_p = _ssd_prompt(per_seq(xbcp), per_seq(zp), per_seq(dtrp), conv_w, conv_b, dtb, alog, d_rep,
                                     gn_w)
    h1p, a1p = _mm_resnorm(ygp.reshape(mp, D_INNER), ssd_w_out[0], xp, norm_ffn[0], 512, "ssd_out")
    h2p, a2p = dense_ffn(h1p, a1p, 512)

    a0s, zs, dtrs = _norm_in_proj(xs, norm_mix[0], w_in, w_dt, n_s)
    xbcs = _mm_plain(a0s, w_in, D_INNER, CONV_DIM, n_s, 1024, F32, "ssd_in_xbc")
    conv_s, xas, xdts, das, bsts, csts = _ssd_sample_pre(xbcs, state_conv[0].reshape(n_s, 3 * CONV_DIM), dtrs,
                                                         conv_w, conv_b, dtb, alog)
    stack_rows = SSD_GROUPS * TERM_ROWS
    ssm_s, ys3 = _ssd_sample_state(state_ssm[0].reshape(n_s, D_INNER, D_STATE), xdts.reshape(n_s, 1, D_INNER),
                                   das[:, :SSD_HEADS], bsts.reshape(n_s, stack_rows, D_STATE),
                                   csts.reshape(n_s, stack_rows, D_STATE))
    h1s, a1s = _ssd_out(ys3.reshape(n_s, D_INNER), xas, zs, d_rep, gn_w, ssd_w_out[0], xs, norm_ffn[0], n_s)
    h2s, a2s = dense_ffn(h1s, a1s, n_s)

    uvp = _mm_bias_gelu(a2p, sgu_w_in[0], sgu_b_in[0], 1024, 2048, BF16, "sgu_in")
    uvs = _mm_bias_gelu(a2s, sgu_w_in[0], sgu_b_in[0], n_s, 1024, F32, "sgu_in")
    ss, v_s = _sgu_sample(uvs, ln_g, ln_b, w0_rep, b0_rep)
    tm_out = 512
    pad_rows = ((0, tm_out - n_s), (0, 0))
    h3, a3, lg = _sgu_out(uvp, jnp.pad(ss, pad_rows), ln_g, ln_b, sgu_w_s[0], bs_t, sgu_w_out[0], h2p,
                          jnp.pad(h2s, pad_rows), norm_ffn[1], w3, tm_out)

    y_prompt, y_tail = _moe(h3, a3, lg, mp + n_s, mp, moe_w_gu[0], moe_w_down[0], norm_final)
    y_sample = y_tail[:n_s]

    return (y_prompt.reshape(batch, seq, d),
            y_sample.reshape(n_s, 1, d),
            ssm_p[None],
            conv_p[None],
            ssm_s.reshape(1, n_s, SSD_HEADS, SSD_HEADDIM, D_STATE),
            conv_s.reshape(1, n_s, CONV_W - 1, CONV_DIM),
            v_s.reshape(1, n_s, 1, D_SGU))
```
